```python
import math
import jax, jax.numpy as jnp
from jax import lax
import numpy as np

D_MODEL = 1024
BATCH = 16
SEQ = 2048
DEPTH = 1

MEM_LEN = 256
D_S5 = D_MODEL // 2
D_LRU = D_MODEL - D_S5
S5_GROUP = 16
S5_GROUPS = D_S5 // S5_GROUP
S5_STATE = 64
LRU_HEADS = 8
LRU_HEAD_DIM = D_LRU // LRU_HEADS
CONV_WIDTH = 4
LRU_C = 8.0
D_FF = 4 * D_MODEL
CA_HEADS = 4
CA_HEAD_DIM = D_MODEL // CA_HEADS
D_IN = D_S5 + 2 * D_LRU
ALPHA = (2 * DEPTH) ** 0.25
BETA = (8 * DEPTH) ** -0.25
LN_EPS = 1e-5
RMS_EPS = 1e-6

kernel_name = "hymba_s5_rglru_deepnorm_memxattn"


def layer_norm(x, g, b):
    xf = x.astype(jnp.float32)
    mu = jnp.mean(xf, axis=-1, keepdims=True)
    var = jnp.mean(jnp.square(xf - mu), axis=-1, keepdims=True)
    return ((xf - mu) * lax.rsqrt(var + LN_EPS) * g + b).astype(x.dtype)


def rms_norm(x, g):
    xf = x.astype(jnp.float32)
    ms = jnp.mean(jnp.square(xf), axis=-1, keepdims=True)
    return (xf * lax.rsqrt(ms + RMS_EPS) * g).astype(x.dtype)


def _complex_diag_combine(left, right):
    a1r, a1i, b1r, b1i = left
    a2r, a2i, b2r, b2i = right
    ar = a2r * a1r - a2i * a1i
    ai = a2r * a1i + a2i * a1r
    br = a2r * b1r - a2i * b1i + b2r
    bi = a2r * b1i + a2i * b1r + b2i
    return ar, ai, br, bi


def s5_mixer(u, a_re, a_im, log_dt, b_re, b_im, c_re, c_im, d, w_glu, b_glu):
    f32 = jnp.float32
    bsz, seq, _ = u.shape
    uf = u.astype(f32).reshape(bsz, seq, S5_GROUPS, S5_GROUP)
    a_re = a_re.astype(f32)
    a_im = a_im.astype(f32)
    dt = jnp.exp(log_dt.astype(f32))[:, None]
    mag = jnp.exp(dt * a_re)
    abar_re = mag * jnp.cos(dt * a_im)
    abar_im = mag * jnp.sin(dt * a_im)
    den = a_re * a_re + a_im * a_im
    q_re = ((abar_re - 1.0) * a_re + abar_im * a_im) / den
    q_im = (abar_im * a_re - (abar_re - 1.0) * a_im) / den
    b_re = b_re.astype(f32)
    b_im = b_im.astype(f32)
    bbar_re = q_re[..., None] * b_re - q_im[..., None] * b_im
    bbar_im = q_re[..., None] * b_im + q_im[..., None] * b_re
    bu_re = jnp.einsum('bsgh,gph->bsgp', uf, bbar_re)
    bu_im = jnp.einsum('bsgh,gph->bsgp', uf, bbar_im)
    lam_re = jnp.broadcast_to(abar_re, (1, seq) + abar_re.shape)
    lam_im = jnp.broadcast_to(abar_im, (1, seq) + abar_im.shape)
    _, _, h_re, h_im = lax.associative_scan(
        _complex_diag_combine, (lam_re, lam_im, bu_re, bu_im), axis=1)
    y = (jnp.einsum('bsgp,ghp->bsgh', h_re, c_re.astype(f32))
         - jnp.einsum('bsgp,ghp->bsgh', h_im, c_im.astype(f32)))
    y = y + d.astype(f32) * uf
    y = jax.nn.gelu(y.reshape(bsz, seq, D_S5))
    y = y * jax.nn.sigmoid(y @ w_glu.astype(f32) + b_glu.astype(f32))
    return y.astype(u.dtype)


def causal_depthwise_conv(x, w, b):
    seq = x.shape[1]
    xp = jnp.pad(x, ((0, 0), (CONV_WIDTH - 1, 0), (0, 0)))
    out = b
    for k in range(CONV_WIDTH):
        out = out + xp[:, k:k + seq] * w[k]
    return out


def _linear_recurrence_combine(left, right):
    a1, b1 = left
    a2, b2 = right
    return a1 * a2, a2 * b1 + b2


def rglru_mixer(xb, gate, conv_w, conv_b, w_a, b_a, w_x, b_x, lam):
    f32 = jnp.float32
    bsz, seq, _ = xb.shape
    xc = causal_depthwise_conv(xb.astype(f32), conv_w.astype(f32), conv_b.astype(f32))
    xh = xc.reshape(bsz, seq, LRU_HEADS, LRU_HEAD_DIM)
    r = jax.nn.sigmoid(jnp.einsum('bshi,hij->bshj', xh, w_a.astype(f32)) + b_a.astype(f32))
    ig = jax.nn.sigmoid(jnp.einsum('bshi,hij->bshj', xh, w_x.astype(f32)) + b_x.astype(f32))
    r = r.reshape(bsz, seq, D_LRU)
    ig = ig.reshape(bsz, seq, D_LRU)
    log_a = -LRU_C * r * jax.nn.softplus(-lam.astype(f32))
    a = jnp.exp(log_a)
    mult = jnp.sqrt(-jnp.expm1(2.0 * log_a))
    _, h = lax.associative_scan(_linear_recurrence_combine, (a, mult * (ig * xc)), axis=1)
    y = h * jax.nn.gelu(gate.astype(f32))
    return y.astype(xb.dtype)


def memory_cross_attention(h, mem_n, w_q, w_k, w_v, w_o):
    bsz, seq, _ = h.shape
    mlen = mem_n.shape[1]
    q = (h @ w_q).reshape(bsz, seq, CA_HEADS, CA_HEAD_DIM)
    k = (mem_n @ w_k).reshape(bsz, mlen, CA_HEADS, CA_HEAD_DIM)
    v = (mem_n @ w_v).reshape(bsz, mlen, CA_HEADS, CA_HEAD_DIM)
    scores = jnp.einsum('bshd,bmhd->bhsm', q.astype(jnp.float32), k.astype(jnp.float32))
    probs = jax.nn.softmax(scores * (CA_HEAD_DIM ** -0.5), axis=-1)
    out = jnp.einsum('bhsm,bmhd->bshd', probs, v.astype(jnp.float32))
    out = out.reshape(bsz, seq, D_MODEL).astype(h.dtype)
    return out @ w_o


def setup_inputs(seed: int = 0) -> dict:
    key = jax.random.key(seed)
    ks = iter(jax.random.split(key, 48))
    nrm = lambda shape, s: s * jax.random.normal(next(ks), shape, jnp.float32)
    L = DEPTH
    x = jax.random.normal(next(ks), (BATCH, SEQ, D_MODEL), jnp.float32)
    mem = jax.random.normal(next(ks), (BATCH, MEM_LEN, D_MODEL), jnp.float32)
    n = jnp.arange(S5_STATE, dtype=jnp.float32)
    s5_a_re = -0.5 + nrm((L, S5_GROUPS, S5_STATE), 0.01)
    s5_a_im = jnp.pi * n + nrm((L, S5_GROUPS, S5_STATE), 0.01)
    s5_log_dt = jax.random.uniform(next(ks), (L, S5_GROUPS), jnp.float32,
                                   math.log(1e-3), math.log(1e-1))
    u_a = jax.random.uniform(next(ks), (L, D_LRU), jnp.float32, 0.9, 0.999)
    a0 = u_a ** (1.0 / LRU_C)
    lru_lambda = jnp.log(a0) - jnp.log1p(-a0)
    return {
        "x": x,
        "mem": mem,
        "ln_in_g": 1.0 + nrm((D_MODEL,), 0.02),
        "ln_in_b": nrm((D_MODEL,), 0.02),
        "w_in": nrm((L, D_MODEL, D_IN), D_MODEL ** -0.5),
        "s5_a_re": s5_a_re,
        "s5_a_im": s5_a_im,
        "s5_log_dt": s5_log_dt,
        "s5_b_re": nrm((L, S5_GROUPS, S5_STATE, S5_GROUP), (2.0 * S5_GROUP) ** -0.5),
        "s5_b_im": nrm((L, S5_GROUPS, S5_STATE, S5_GROUP), (2.0 * S5_GROUP) ** -0.5),
        "s5_c_re": nrm((L, S5_GROUPS, S5_GROUP, S5_STATE), (2.0 * S5_STATE) ** -0.5),
        "s5_c_im": nrm((L, S5_GROUPS, S5_GROUP, S5_STATE), (2.0 * S5_STATE) ** -0.5),
        "s5_d": nrm((L, S5_GROUPS, S5_GROUP), 1.0),
        "s5_w_glu": nrm((L, D_S5, D_S5), D_S5 ** -0.5),
        "s5_b_glu": nrm((L, D_S5), 0.02),
        "conv_w": nrm((L, CONV_WIDTH, D_LRU), CONV_WIDTH ** -0.5),
        "conv_b": nrm((L, D_LRU), 0.02),
        "lru_w_a": nrm((L, LRU_HEADS, LRU_HEAD_DIM, LRU_HEAD_DIM), LRU_HEAD_DIM ** -0.5),
        "lru_b_a": nrm((L, LRU_HEADS, LRU_HEAD_DIM), 0.02),
        "lru_w_x": nrm((L, LRU_HEADS, LRU_HEAD_DIM, LRU_HEAD_DIM), LRU_HEAD_DIM ** -0.5),
        "lru_b_x": nrm((L, LRU_HEADS, LRU_HEAD_DIM), 0.02),
        "lru_lambda": lru_lambda,
        "g_s5": 1.0 + nrm((L, D_S5), 0.02),
        "g_lru": 1.0 + nrm((L, D_LRU), 0.02),
        "w_mix_out": nrm((L, D_MODEL, D_MODEL), BETA * D_MODEL ** -0.5),
        "ln1_g": 1.0 + nrm((L, D_MODEL), 0.02),
        "ln1_b": nrm((L, D_MODEL), 0.02),
        "mem_ln_g": 1.0 + nrm((L, D_MODEL), 0.02),
        "mem_ln_b": nrm((L, D_MODEL), 0.02),
        "w_q": nrm((L, D_MODEL, D_MODEL), D_MODEL ** -0.5),
        "w_k": nrm((L, D_MODEL, D_MODEL), D_MODEL ** -0.5),
        "w_v": nrm((L, D_MODEL, D_MODEL), BETA * D_MODEL ** -0.5),
        "w_o": nrm((L, D_MODEL, D_MODEL), BETA * D_MODEL ** -0.5),
        "ln2_g": 1.0 + nrm((L, D_MODEL), 0.02),
        "ln2_b": nrm((L, D_MODEL), 0.02),
        "w_ff1": nrm((L, D_MODEL, D_FF), BETA * D_MODEL ** -0.5),
        "w_ff2": nrm((L, D_FF, D_MODEL), BETA * D_FF ** -0.5),
        "ln3_g": 1.0 + nrm((L, D_MODEL), 0.02),
        "ln3_b": nrm((L, D_MODEL), 0.02),
    }


def reference(x, mem, ln_in_g, ln_in_b, w_in, s5_a_re, s5_a_im, s5_log_dt, s5_b_re, s5_b_im,
              s5_c_re, s5_c_im, s5_d, s5_w_glu, s5_b_glu, conv_w, conv_b, lru_w_a, lru_b_a,
              lru_w_x, lru_b_x, lru_lambda, g_s5, g_lru, w_mix_out, ln1_g, ln1_b,
              mem_ln_g, mem_ln_b, w_q, w_k, w_v, w_o, ln2_g, ln2_b, w_ff1, w_ff2,
              ln3_g, ln3_b):
    h = layer_norm(x, ln_in_g, ln_in_b)
    for l in range(DEPTH):
        z = h @ w_in[l]
        u_s5 = z[..., :D_S5]
        x_lru = z[..., D_S5:D_S5 + D_LRU]
        gate_lru = z[..., D_S5 + D_LRU:]
        y_s5 = s5_mixer(u_s5, s5_a_re[l], s5_a_im[l], s5_log_dt[l], s5_b_re[l], s5_b_im[l],
                        s5_c_re[l], s5_c_im[l], s5_d[l], s5_w_glu[l], s5_b_glu[l])
        y_lru = rglru_mixer(x_lru, gate_lru, conv_w[l], conv_b[l], lru_w_a[l], lru_b_a[l],
                            lru_w_x[l], lru_b_x[l], lru_lambda[l])
        y = jnp.concatenate([rms_norm(y_s5, g_s5[l]), rms_norm(y_lru, g_lru[l])], axis=-1)
        h = layer_norm(ALPHA * h + y @ w_mix_out[l], ln1_g[l], ln1_b[l])
        mem_n = layer_norm(mem, mem_ln_g[l], mem_ln_b[l])
        ca = memory_cross_attention(h, mem_n, w_q[l], w_k[l], w_v[l], w_o[l])
        h = layer_norm(ALPHA * h + ca, ln2_g[l], ln2_b[l])
        ff = jnp.square(jax.nn.relu(h @ w_ff1[l])) @ w_ff2[l]
        h = layer_norm(ALPHA * h + ff, ln3_g[l], ln3_b[l])
    return h
```

```python
import functools

import jax
import jax.numpy as jnp
from jax import lax
from jax.experimental import pallas as pl
from jax.experimental.pallas import tpu as pltpu

F32 = jnp.float32
BF16 = jnp.bfloat16

D_MODEL = 1024
D_S5 = 512
D_LRU = 512
D_IN = D_S5 + 2 * D_LRU
S5_GROUP = 16
S5_GROUPS = 32
S5_STATE = 64
LRU_HEADS = 8
LRU_HEAD_DIM = 64
CONV_WIDTH = 4
LRU_C = 8.0
D_FF = 4096
CA_HEADS = 4
CA_HEAD_DIM = 256
DEPTH = 1
ALPHA = (2 * DEPTH) ** 0.25
LN_EPS = 1e-5
RMS_EPS = 1e-6

S5_BLOCKS = 4
S5_BLOCK_CH = D_S5 // S5_BLOCKS
S5_BLOCK_STATES = 8 * S5_STATE
LRU_BLOCKS = 2
LRU_BLOCK_CH = D_LRU // LRU_BLOCKS

V7X_SCOPED_VMEM_BYTES = 56 * 1024 * 1024

MIXER_STEPS = 32
ATTN_ROWS = 512
MLP_ROWS = 512
MLP_CHUNK = 1024


def _layer_norm(x, g, b):
    mu = jnp.mean(x, axis=-1, keepdims=True)
    xc = x - mu
    var = jnp.mean(xc * xc, axis=-1, keepdims=True)
    return xc * lax.rsqrt(var + LN_EPS) * g + b


def _rms_norm(x, g):
    ms = jnp.mean(x * x, axis=-1, keepdims=True)
    return x * lax.rsqrt(ms + RMS_EPS) * g


def _gelu_tanh(x):
    c = 0.7978845608028654
    return 0.5 * x * (1.0 + jnp.tanh(c * (x + 0.044715 * (x * x * x))))


def _dot(a, b):
    return jnp.dot(a, b, preferred_element_type=F32)


def _s5_prep_body(are_ref, aim_ref, ldt_ref, bre_ref, bim_ref, abr_ref, abi_ref, bbr_ref, bbi_ref):
    a_re = are_ref[...]
    a_im = aim_ref[...]
    dt = jnp.exp(ldt_ref[...])
    mag = jnp.exp(dt * a_re)
    abar_re = mag * jnp.cos(dt * a_im)
    abar_im = mag * jnp.sin(dt * a_im)
    den = a_re * a_re + a_im * a_im
    q_re = ((abar_re - 1.0) * a_re + abar_im * a_im) / den
    q_im = (abar_im * a_re - (abar_re - 1.0) * a_im) / den
    b_re = bre_ref[...]
    b_im = bim_ref[...]
    abr_ref[...] = abar_re
    abi_ref[...] = abar_im
    bbr_ref[...] = q_re * b_re - q_im * b_im
    bbi_ref[...] = q_re * b_im + q_im * b_re


def _s5_prep(a_re, a_im, log_dt, b_re, b_im):
    g, p, h = b_re.shape
    f = jax.ShapeDtypeStruct
    return pl.pallas_call(
        _s5_prep_body,
        out_shape=(f((g, 1, p), F32), f((g, 1, p), F32), f((g, h, p), F32), f((g, h, p), F32)),
        name="s5_prep",
    )(a_re.reshape(g, 1, p), a_im.reshape(g, 1, p), log_dt.reshape(g, 1, 1),
      jnp.swapaxes(b_re, 1, 2), jnp.swapaxes(b_im, 1, 2))


def _mixer_body(x_ref, lng_ref, lnb_ref, win_ref, bblk_ref, are_ref, aim_ref, cblk_ref, d_ref, wglu_ref,
                bglu_ref, gs5_ref, cw_ref, cb_ref, wg_ref, ba_ref, bx_ref, lam_ref, glru_ref,
                y_ref,
                z_ref, bu_ref, hb_ref, st_ref, xl_ref, a_ref, hx_ref, hl_ref):
    rows = x_ref.shape[0]
    batch = st_ref.shape[1]
    steps = rows // batch
    halo = (CONV_WIDTH - 1) * batch

    @pl.when(pl.program_id(0) == 0)
    def _():
        st_ref[...] = jnp.zeros_like(st_ref)
        hl_ref[...] = jnp.zeros_like(hl_ref)
        xl_ref[0:halo, :] = jnp.zeros((halo, D_LRU), F32)

    h0 = _layer_norm(x_ref[...], lng_ref[...], lnb_ref[...])
    z_ref[...] = _dot(h0.astype(BF16), win_ref[...])

    u = z_ref[:, 0:D_S5]
    ub = u.astype(BF16)
    ys = []
    for k in range(S5_BLOCKS):
        bu_ref[...] = _dot(ub[:, k * S5_BLOCK_CH:(k + 1) * S5_BLOCK_CH], bblk_ref[k])
        ar = jnp.broadcast_to(are_ref[k], (batch, S5_BLOCK_STATES))
        ai = jnp.broadcast_to(aim_ref[k], (batch, S5_BLOCK_STATES))

        def s5_step(t, carry, ar=ar, ai=ai):
            hr, hi = carry
            r0 = pl.multiple_of(t * batch, batch)
            br = bu_ref[pl.ds(r0, batch), 0:S5_BLOCK_STATES]
            bi = bu_ref[pl.ds(r0, batch), S5_BLOCK_STATES:2 * S5_BLOCK_STATES]
            nr = ar * hr - ai * hi + br
            ni = ar * hi + ai * hr + bi
            hb_ref[pl.ds(r0, batch), 0:S5_BLOCK_STATES] = nr.astype(BF16)
            hb_ref[pl.ds(r0, batch), S5_BLOCK_STATES:2 * S5_BLOCK_STATES] = ni.astype(BF16)
            return nr, ni

        hr, hi = lax.fori_loop(
            0, steps, s5_step,
            (st_ref[k, :, 0:S5_BLOCK_STATES], st_ref[k, :, S5_BLOCK_STATES:2 * S5_BLOCK_STATES]),
            unroll=4)
        st_ref[k, :, 0:S5_BLOCK_STATES] = hr
        st_ref[k, :, S5_BLOCK_STATES:2 * S5_BLOCK_STATES] = hi
        ys.append(_dot(hb_ref[...], cblk_ref[k]))
    y = jnp.concatenate(ys, axis=1) + d_ref[...] * u
    y = _gelu_tanh(y)
    y = y * jax.nn.sigmoid(_dot(y.astype(BF16), wglu_ref[...]) + bglu_ref[...])
    y_ref[:, 0:D_S5] = _rms_norm(y, gs5_ref[...]).astype(y_ref.dtype)

    xl_ref[halo:halo + rows, :] = z_ref[:, D_S5:D_S5 + D_LRU]
    xc = cb_ref[...]
    for j in range(CONV_WIDTH):
        xc = xc + xl_ref[j * batch:j * batch + rows, :] * cw_ref[j:j + 1, :]
    xl_ref[0:halo, :] = xl_ref[rows:rows + halo, :]
    xcb = xc.astype(BF16)
    rs, igs = [], []
    for kb in range(LRU_BLOCKS):
        lo, hi_ = kb * LRU_BLOCK_CH, (kb + 1) * LRU_BLOCK_CH
        pre = _dot(xcb[:, lo:hi_], wg_ref[kb])
        rs.append(jax.nn.sigmoid(pre[:, 0:LRU_BLOCK_CH] + ba_ref[:, lo:hi_]))
        igs.append(jax.nn.sigmoid(pre[:, LRU_BLOCK_CH:2 * LRU_BLOCK_CH] + bx_ref[:, lo:hi_]))
    r = jnp.concatenate(rs, axis=1)
    ig = jnp.concatenate(igs, axis=1)
    lam = lam_ref[...]
    softplus_neg_lam = jnp.maximum(-lam, 0.0) + jnp.log1p(jnp.exp(-jnp.abs(lam)))
    log_a = -LRU_C * r * softplus_neg_lam
    a = jnp.exp(log_a)
    mult = jnp.sqrt(-jnp.tanh(log_a) * (a * a + 1.0))
    a_ref[...] = a
    hx_ref[...] = mult * (ig * xc)

    def lru_step(t, h):
        r0 = pl.multiple_of(t * batch, batch)
        h = a_ref[pl.ds(r0, batch), :] * h + hx_ref[pl.ds(r0, batch), :]
        hx_ref[pl.ds(r0, batch), :] = h
        return h

    hl_ref[...] = lax.fori_loop(0, steps, lru_step, hl_ref[...], unroll=4)
    ylru = hx_ref[...] * _gelu_tanh(z_ref[:, D_S5 + D_LRU:D_IN])
    y_ref[:, D_S5:D_MODEL] = _rms_norm(ylru, glru_ref[...]).astype(y_ref.dtype)


def _const_spec(shape):
    zeros = (0,) * len(shape)
    return pl.BlockSpec(shape, lambda *_: zeros)


def _mixer(x_tm, batch, ln_g, ln_b, w_in, bblk, abar_re, abar_im, cblk, d, w_glu, b_glu, g_s5,
           conv_w, conv_b, wg, b_a, b_x, lam, g_lru):
    n_rows = x_tm.shape[0]
    rows = MIXER_STEPS * batch
    halo = (CONV_WIDTH - 1) * batch
    consts = (ln_g, ln_b, w_in, bblk, abar_re, abar_im, cblk, d, w_glu, b_glu, g_s5,
              conv_w, conv_b, wg, b_a, b_x, lam, g_lru)
    return pl.pallas_call(
        _mixer_body,
        grid=(n_rows // rows,),
        in_specs=[pl.BlockSpec((rows, D_MODEL), lambda i: (i, 0))] + [_const_spec(c.shape) for c in consts],
        out_specs=pl.BlockSpec((rows, D_MODEL), lambda i: (i, 0)),
        out_shape=jax.ShapeDtypeStruct((n_rows, D_MODEL), BF16),
        scratch_shapes=[
            pltpu.VMEM((rows, D_IN), F32),
            pltpu.VMEM((rows, 2 * S5_BLOCK_STATES), F32),
            pltpu.VMEM((rows, 2 * S5_BLOCK_STATES), BF16),
            pltpu.VMEM((S5_BLOCKS, batch, 2 * S5_BLOCK_STATES), F32),
            pltpu.VMEM((rows + halo, D_LRU), F32),
            pltpu.VMEM((rows, D_LRU), F32),
            pltpu.VMEM((rows, D_LRU), F32),
            pltpu.VMEM((batch, D_LRU), F32),
        ],
        compiler_params=pltpu.CompilerParams(
            dimension_semantics=("arbitrary",), vmem_limit_bytes=V7X_SCOPED_VMEM_BYTES),
        name="mixer",
    )(x_tm, *consts)


def _kv_body(mem_ref, g_ref, b_ref, wk_ref, wv_ref, k_ref, v_ref):
    mn = _layer_norm(mem_ref[...], g_ref[...], b_ref[...]).astype(BF16)
    k_ref[...] = (_dot(mn, wk_ref[...]) * (CA_HEAD_DIM ** -0.5)).astype(k_ref.dtype)
    v_ref[...] = _dot(mn, wv_ref[...]).astype(v_ref.dtype)


def _kv(mem, g, b, w_k, w_v):
    bsz, mlen, _ = mem.shape
    blk = pl.BlockSpec((None, mlen, D_MODEL), lambda i: (i, 0, 0))
    return pl.pallas_call(
        _kv_body,
        grid=(bsz,),
        in_specs=[blk, _const_spec(g.shape), _const_spec(b.shape), _const_spec(w_k.shape), _const_spec(w_v.shape)],
        out_specs=(blk, blk),
        out_shape=(jax.ShapeDtypeStruct(mem.shape, BF16), jax.ShapeDtypeStruct(mem.shape, BF16)),
        compiler_params=pltpu.CompilerParams(dimension_semantics=("arbitrary",)),
        name="kv",
    )(mem, g, b, w_k, w_v)


def _attn_body(x_ref, y_ref, k_ref, v_ref, wmix_ref, wq_ref, wo_ref, lng_ref, lnb_ref, g1_ref, b1_ref,
               g2_ref, b2_ref, o_ref):
    h0 = _layer_norm(x_ref[...], lng_ref[...], lnb_ref[...])
    h1 = _layer_norm(ALPHA * h0 + _dot(y_ref[...], wmix_ref[...]), g1_ref[...], b1_ref[...])
    q = _dot(h1.astype(BF16), wq_ref[...]).astype(BF16)
    heads = []
    for hd in range(CA_HEADS):
        lo, hi = hd * CA_HEAD_DIM, (hd + 1) * CA_HEAD_DIM
        s = lax.dot_general(q[:, lo:hi], k_ref[:, lo:hi], (((1,), (1,)), ((), ())),
                            preferred_element_type=F32)
        p = jnp.exp(s - jnp.max(s, axis=-1, keepdims=True))
        denom = jnp.sum(p, axis=-1, keepdims=True)
        heads.append(_dot(p.astype(BF16), v_ref[:, lo:hi]) / denom)
    att = jnp.concatenate(heads, axis=1).astype(BF16)
    o_ref[...] = _layer_norm(ALPHA * h1 + _dot(att, wo_ref[...]), g2_ref[...], b2_ref[...])


def _attn(x, y, k, v, w_mix, w_q, w_o, ln_g, ln_b, g1, b1, g2, b2):
    bsz, seq, _ = x.shape
    mlen = k.shape[1]
    row_blk = pl.BlockSpec((None, ATTN_ROWS, D_MODEL), lambda b, i: (b, i, 0))
    mem_blk = pl.BlockSpec((None, mlen, D_MODEL), lambda b, i: (b, 0, 0))
    consts = (w_mix, w_q, w_o, ln_g, ln_b, g1, b1, g2, b2)
    return pl.pallas_call(
        _attn_body,
        grid=(bsz, seq // ATTN_ROWS),
        in_specs=[row_blk, row_blk, mem_blk, mem_blk] + [_const_spec(c.shape) for c in consts],
        out_specs=row_blk,
        out_shape=jax.ShapeDtypeStruct(x.shape, F32),
        compiler_params=pltpu.CompilerParams(
            dimension_semantics=("arbitrary", "arbitrary"), vmem_limit_bytes=V7X_SCOPED_VMEM_BYTES),
        name="attn",
    )(x, y, k, v, *consts)


def _mlp_body(h_ref, w1_ref, w2_ref, g_ref, b_ref, o_ref):
    h = h_ref[...]
    hb = h.astype(BF16)
    ff = jnp.zeros(h.shape, F32)
    for j in range(D_FF // MLP_CHUNK):
        lo, hi = j * MLP_CHUNK, (j + 1) * MLP_CHUNK
        t = jnp.maximum(_dot(hb, w1_ref[:, lo:hi]), 0.0)
        ff = ff + _dot((t * t).astype(BF16), w2_ref[lo:hi, :])
    o_ref[...] = _layer_norm(ALPHA * h + ff, g_ref[...], b_ref[...])


def _mlp(h, w1, w2, g, b):
    n_rows = h.shape[0]
    row_blk = pl.BlockSpec((MLP_ROWS, D_MODEL), lambda i: (i, 0))
    single = pl.Buffered(1)
    return pl.pallas_call(
        _mlp_body,
        grid=(n_rows // MLP_ROWS,),
        in_specs=[row_blk,
                  pl.BlockSpec(w1.shape, lambda i: (0, 0), pipeline_mode=single),
                  pl.BlockSpec(w2.shape, lambda i: (0, 0), pipeline_mode=single),
                  _const_spec(g.shape), _const_spec(b.shape)],
        out_specs=row_blk,
        out_shape=jax.ShapeDtypeStruct(h.shape, F32),
        compiler_params=pltpu.CompilerParams(
            dimension_semantics=("arbitrary",), vmem_limit_bytes=V7X_SCOPED_VMEM_BYTES),
        name="mlp",
    )(h, w1, w2, g, b)


def _s5_block_matrices(bbar_re, bbar_im, c_re, c_im):
    eye = jnp.eye(8, dtype=F32)

    def expand_b(bb):
        bb = bb.reshape(S5_BLOCKS, 8, S5_GROUP, S5_STATE)
        return jnp.einsum("kghp,gj->kghjp", bb, eye).reshape(S5_BLOCKS, S5_BLOCK_CH, S5_BLOCK_STATES)

    def expand_c(cc):
        cc = cc.reshape(S5_BLOCKS, 8, S5_GROUP, S5_STATE)
        return jnp.einsum("kghp,gj->kgpjh", cc, eye).reshape(S5_BLOCKS, S5_BLOCK_STATES, S5_BLOCK_CH)

    bblk = jnp.concatenate([expand_b(bbar_re), expand_b(bbar_im)], axis=2).astype(BF16)
    cblk = jnp.concatenate([expand_c(c_re), -expand_c(c_im)], axis=1).astype(BF16)
    return bblk, cblk


def _lru_gate_matrices(w_a, w_x):
    eye = jnp.eye(4, dtype=F32)

    def expand(w):
        w = w.reshape(LRU_BLOCKS, 4, LRU_HEAD_DIM, LRU_HEAD_DIM)
        return jnp.einsum("khij,hl->khilj", w, eye).reshape(LRU_BLOCKS, LRU_BLOCK_CH, LRU_BLOCK_CH)

    return jnp.concatenate([expand(w_a), expand(w_x)], axis=2).astype(BF16)


def kernel(x, mem, ln_in_g, ln_in_b, w_in, s5_a_re, s5_a_im, s5_log_dt, s5_b_re, s5_b_im, s5_c_re, s5_c_im, s5_d, s5_w_glu, s5_b_glu, conv_w, conv_b, lru_w_a, lru_b_a, lru_w_x, lru_b_x, lru_lambda, g_s5, g_lru, w_mix_out, ln1_g, ln1_b, mem_ln_g, mem_ln_b, w_q, w_k, w_v, w_o, ln2_g, ln2_b, w_ff1, w_ff2, ln3_g, ln3_b):
    bsz, seq, d_model = x.shape
    assert d_model == D_MODEL and w_in.shape == (DEPTH, D_MODEL, D_IN)
    assert seq % MIXER_STEPS == 0 and seq % ATTN_ROWS == 0 and (bsz * seq) % MLP_ROWS == 0
    assert bsz % 16 == 0, "time-major rows of one step must fill whole bf16 tiles"
    row = lambda v: v.reshape(1, -1).astype(F32)
    l = 0

    abar_re, abar_im, bbar_re, bbar_im = _s5_prep(s5_a_re[l], s5_a_im[l], s5_log_dt[l], s5_b_re[l], s5_b_im[l])
    bblk, cblk = _s5_block_matrices(bbar_re, bbar_im, s5_c_re[l], s5_c_im[l])
    abar_re = abar_re.reshape(S5_BLOCKS, 1, S5_BLOCK_STATES)
    abar_im = abar_im.reshape(S5_BLOCKS, 1, S5_BLOCK_STATES)
    wg = _lru_gate_matrices(lru_w_a[l], lru_w_x[l])

    x_tm = jnp.swapaxes(x, 0, 1).reshape(seq * bsz, D_MODEL)
    y_tm = _mixer(x_tm, bsz, row(ln_in_g), row(ln_in_b), w_in[l].astype(BF16), bblk, abar_re, abar_im, cblk,
                  row(s5_d[l]), s5_w_glu[l].astype(BF16), row(s5_b_glu[l]), row(g_s5[l]),
                  conv_w[l].astype(F32), row(conv_b[l]), wg, row(lru_b_a[l]), row(lru_b_x[l]),
                  row(lru_lambda[l]), row(g_lru[l]))
    y = jnp.swapaxes(y_tm.reshape(seq, bsz, D_MODEL), 0, 1)

    k, v = _kv(mem, row(mem_ln_g[l]), row(mem_ln_b[l]), w_k[l].astype(BF16), w_v[l].astype(BF16))
    h2 = _attn(x, y, k, v, w_mix_out[l].astype(BF16), w_q[l].astype(BF16), w_o[l].astype(BF16),
               row(ln_in_g), row(ln_in_b), row(ln1_g[l]), row(ln1_b[l]), row(ln2_g[l]), row(ln2_b[l]))
    out = _mlp(h2.reshape(bsz * seq, D_MODEL), w_ff1[l].astype(BF16), w_ff2[l].astype(BF16),
               row(ln3_g[l]), row(ln3_b[l]))
    return out.reshape(bsz, seq, D_MODEL)
```

```python
import jax
import jax.numpy as jnp
from jax import lax
from jax.experimental import pallas as pl
from jax.experimental.pallas import tpu as pltpu

F32 = jnp.float32
BF16 = jnp.bfloat16

D_MODEL = 1024
D_S5 = 512
D_LRU = 512
D_IN = D_S5 + 2 * D_LRU
S5_GROUP = 16
S5_GROUPS = 32
S5_STATE = 64
LRU_HEADS = 8
LRU_HEAD_DIM = 64
CONV_WIDTH = 4
LRU_C = 8.0
D_FF = 4096
CA_HEADS = 4
CA_HEAD_DIM = 256
DEPTH = 1
ALPHA = (2 * DEPTH) ** 0.25
LN_EPS = 1e-5
RMS_EPS = 1e-6

S5_BLOCKS = 4
S5_BLOCK_CH = D_S5 // S5_BLOCKS
S5_BLOCK_STATES = 8 * S5_STATE
LRU_BLOCKS = 2
LRU_BLOCK_CH = D_LRU // LRU_BLOCKS

V7X_SCOPED_VMEM_BYTES = 56 * 1024 * 1024

MIXER_STEPS = 32
ATTN_ROWS = 512
MLP_ROWS = 512
KV_ROWS = 1024
MLP_CHUNK = 1024
VEC_ROWS = 16


def _layer_norm(x, g, b):
    mu = jnp.mean(x, axis=-1, keepdims=True)
    xc = x - mu
    var = jnp.mean(xc * xc, axis=-1, keepdims=True)
    return xc * lax.rsqrt(var + LN_EPS) * g + b


def _rms_norm(x, g):
    ms = jnp.mean(x * x, axis=-1, keepdims=True)
    return x * lax.rsqrt(ms + RMS_EPS) * g


def _gelu_tanh(x):
    c = 0.7978845608028654
    return 0.5 * x * (1.0 + jnp.tanh(c * (x + 0.044715 * (x * x * x))))


def _dot(a, b):
    return jnp.dot(a, b, preferred_element_type=F32)


def _const_spec(shape):
    zeros = (0,) * len(shape)
    return pl.BlockSpec(shape, lambda *_: zeros)


def _resident_spec(shape, block=None):
    zeros = (0,) * len(shape)
    index = zeros if block is None else block
    return pl.BlockSpec(shape, lambda *_: index, pipeline_mode=pl.Buffered(1))


def _row_chunks(rows, chunk=None):
    chunk = chunk or VEC_ROWS
    return [slice(r, r + chunk) for r in range(0, rows, chunk)]


def _s5_prep_body(are_ref, aim_ref, ldt_ref, bre_ref, bim_ref, abr_ref, abi_ref, bbr_ref, bbi_ref):
    a_re = are_ref[...]
    a_im = aim_ref[...]
    dt = jnp.exp(ldt_ref[...])
    mag = jnp.exp(dt * a_re)
    abar_re = mag * jnp.cos(dt * a_im)
    abar_im = mag * jnp.sin(dt * a_im)
    den = a_re * a_re + a_im * a_im
    q_re = ((abar_re - 1.0) * a_re + abar_im * a_im) / den
    q_im = (abar_im * a_re - (abar_re - 1.0) * a_im) / den
    b_re = bre_ref[...]
    b_im = bim_ref[...]
    abr_ref[...] = abar_re
    abi_ref[...] = abar_im
    bbr_ref[...] = q_re * b_re - q_im * b_im
    bbi_ref[...] = q_re * b_im + q_im * b_re


def _s5_prep(a_re, a_im, log_dt, b_re, b_im):
    g, p, h = b_re.shape
    f = jax.ShapeDtypeStruct
    return pl.pallas_call(
        _s5_prep_body,
        out_shape=(f((g, 1, p), F32), f((g, 1, p), F32), f((g, h, p), F32), f((g, h, p), F32)),
        name="s5_prep",
    )(a_re.reshape(g, 1, p), a_im.reshape(g, 1, p), log_dt.reshape(g, 1, 1),
      jnp.swapaxes(b_re, 1, 2), jnp.swapaxes(b_im, 1, 2))


def _mixer_norm(x_ref, lng_ref, lnb_ref, h0b_ref):
    for r in _row_chunks(x_ref.shape[0]):
        h0b_ref[r, :] = _layer_norm(x_ref[r, :], lng_ref[...], lnb_ref[...]).astype(BF16)


def _mixer_project_in(h0b_ref, win_ref, z_ref, ub_ref):
    for piece in range(D_IN // D_S5):
        cols = slice(piece * D_S5, (piece + 1) * D_S5)
        z = _dot(h0b_ref[...], win_ref[:, cols])
        z_ref[:, cols] = z
        if piece == 0:
            ub_ref[...] = z.astype(BF16)


def _mixer_conv(z_ref, cw_ref, cb_ref, d_ref, xl_ref, xc_ref, xcb_ref, gz_ref, du_ref, batch):
    rows = z_ref.shape[0]
    halo = (CONV_WIDTH - 1) * batch
    xl_ref[halo:halo + rows, :] = z_ref[:, D_S5:D_S5 + D_LRU]
    for r in _row_chunks(rows, 2 * VEC_ROWS):
        xc = cb_ref[...]
        for j in range(CONV_WIDTH):
            xc = xc + xl_ref[j * batch + r.start:j * batch + r.stop, :] * cw_ref[j:j + 1, :]
        xc_ref[r, :] = xc
        xcb_ref[r, :] = xc.astype(BF16)
        gz_ref[r, :] = _gelu_tanh(z_ref[r, D_S5 + D_LRU:D_IN])
        du_ref[r, :] = d_ref[...] * z_ref[r, 0:D_S5]
    xl_ref[0:halo, :] = xl_ref[rows:rows + halo, :]


def _mixer_step(xnext_ref, lng_ref, lnb_ref, win_ref, bblk_ref, are_ref, aim_ref, cblk_ref, d_ref,
                wglu_ref, bglu_ref, gs5_ref, cw_ref, cb_ref, wg_ref, ba_ref, bx_ref, lam_ref, glru_ref, y_ref,
                z_ref, ub_ref, h0b_ref, bu_ref, hb_ref, st_ref, xl_ref, xc_ref, xcb_ref, pre_ref, a_ref, hx_ref,
                hl_ref, ys_ref, y32_ref, yb_ref, gate_ref, gz_ref, du_ref):
    rows = y_ref.shape[0]
    batch = st_ref.shape[1]
    steps = rows // batch
    wide = _row_chunks(rows, 2 * VEC_ROWS)

    for k in range(S5_BLOCKS):
        bu_ref[k] = _dot(ub_ref[:, k * S5_BLOCK_CH:(k + 1) * S5_BLOCK_CH], bblk_ref[k])
    for kb in range(LRU_BLOCKS):
        lo, hi_ = kb * LRU_BLOCK_CH, (kb + 1) * LRU_BLOCK_CH
        pre = _dot(xcb_ref[:, lo:hi_], wg_ref[kb])
        pre_ref[:, lo:hi_] = pre[:, 0:LRU_BLOCK_CH]
        pre_ref[:, D_LRU + lo:D_LRU + hi_] = pre[:, LRU_BLOCK_CH:2 * LRU_BLOCK_CH]
    _mixer_project_in(h0b_ref, win_ref, z_ref, ub_ref)

    for k in range(S5_BLOCKS):
        ar = jnp.broadcast_to(are_ref[k], (batch, S5_BLOCK_STATES))
        ai = jnp.broadcast_to(aim_ref[k], (batch, S5_BLOCK_STATES))
        hr = st_ref[k, :, 0:S5_BLOCK_STATES]
        hi = st_ref[k, :, S5_BLOCK_STATES:2 * S5_BLOCK_STATES]
        for t in range(steps):
            r = slice(t * batch, (t + 1) * batch)
            nr = ar * hr - ai * hi + bu_ref[k, r, 0:S5_BLOCK_STATES]
            ni = ar * hi + ai * hr + bu_ref[k, r, S5_BLOCK_STATES:2 * S5_BLOCK_STATES]
            hb_ref[k, r, 0:S5_BLOCK_STATES] = nr.astype(BF16)
            hb_ref[k, r, S5_BLOCK_STATES:2 * S5_BLOCK_STATES] = ni.astype(BF16)
            hr, hi = nr, ni
        st_ref[k, :, 0:S5_BLOCK_STATES] = hr
        st_ref[k, :, S5_BLOCK_STATES:2 * S5_BLOCK_STATES] = hi
        ys_ref[:, k * S5_BLOCK_CH:(k + 1) * S5_BLOCK_CH] = _dot(hb_ref[k], cblk_ref[k])
        if k == 1:
            lam = lam_ref[...]
            softplus_neg_lam = jnp.maximum(-lam, 0.0) + jnp.log1p(jnp.exp(-jnp.abs(lam)))
            for r in wide:
                gate_a = jax.nn.sigmoid(pre_ref[r, 0:D_LRU] + ba_ref[...])
                gate_x = jax.nn.sigmoid(pre_ref[r, D_LRU:2 * D_LRU] + bx_ref[...])
                log_a = -LRU_C * gate_a * softplus_neg_lam
                a = jnp.exp(log_a)
                m2 = -jnp.tanh(log_a) * (a * a + 1.0)
                mult = jnp.where(m2 > 0.0, m2 * lax.rsqrt(m2), 0.0)
                a_ref[r, :] = a
                hx_ref[r, :] = mult * (gate_x * xc_ref[r, :])
    hl = hl_ref[...]
    for t in range(steps):
        r = slice(t * batch, (t + 1) * batch)
        hl = a_ref[r, :] * hl + hx_ref[r, :]
        hx_ref[r, :] = hl
    hl_ref[...] = hl

    for r in wide:
        y = _gelu_tanh(ys_ref[r, :] + du_ref[r, :])
        y32_ref[r, :] = y
        yb_ref[r, :] = y.astype(BF16)
    gate_ref[...] = _dot(yb_ref[...], wglu_ref[...])
    for r in wide:
        y = y32_ref[r, :] * jax.nn.sigmoid(gate_ref[r, :] + bglu_ref[...])
        y_ref[r, 0:D_S5] = _rms_norm(y, gs5_ref[...]).astype(y_ref.dtype)
        ylru = hx_ref[r, :] * gz_ref[r, :]
        y_ref[r, D_S5:D_MODEL] = _rms_norm(ylru, glru_ref[...]).astype(y_ref.dtype)

    _mixer_conv(z_ref, cw_ref, cb_ref, d_ref, xl_ref, xc_ref, xcb_ref, gz_ref, du_ref, batch)
    _mixer_norm(xnext_ref, lng_ref, lnb_ref, h0b_ref)


def _mixer_body(xfirst_ref, xsecond_ref, xnext_ref, lng_ref, lnb_ref, win_ref, *rest):
    back_consts = rest[:15]
    d_ref, cw_ref, cb_ref = back_consts[4], back_consts[8], back_consts[9]
    y_ref = rest[15]
    scratch = rest[16:]
    z_ref, ub_ref, h0b_ref, _, _, st_ref, xl_ref, xc_ref, xcb_ref = scratch[:9]
    hl_ref, gz_ref, du_ref = scratch[12], scratch[17], scratch[18]
    batch = st_ref.shape[1]
    halo = (CONV_WIDTH - 1) * batch

    @pl.when(pl.program_id(0) == 0)
    def _():
        st_ref[...] = jnp.zeros_like(st_ref)
        hl_ref[...] = jnp.zeros_like(hl_ref)
        xl_ref[0:halo, :] = jnp.zeros((halo, D_LRU), F32)
        _mixer_norm(xfirst_ref, lng_ref, lnb_ref, h0b_ref)
        _mixer_project_in(h0b_ref, win_ref, z_ref, ub_ref)
        _mixer_conv(z_ref, cw_ref, cb_ref, d_ref, xl_ref, xc_ref, xcb_ref, gz_ref, du_ref, batch)
        _mixer_norm(xsecond_ref, lng_ref, lnb_ref, h0b_ref)

    _mixer_step(xnext_ref, lng_ref, lnb_ref, win_ref, *back_consts, y_ref, *scratch)


def _mixer(x_tm, batch, ln_g, ln_b, w_in, bblk, abar_re, abar_im, cblk, d, w_glu, b_glu, g_s5,
           conv_w, conv_b, wg, b_a, b_x, lam, g_lru):
    n_rows = x_tm.shape[0]
    rows = MIXER_STEPS * batch
    n_blocks = n_rows // rows
    halo = (CONV_WIDTH - 1) * batch
    consts = (ln_g, ln_b, w_in, bblk, abar_re, abar_im, cblk, d, w_glu, b_glu, g_s5,
              conv_w, conv_b, wg, b_a, b_x, lam, g_lru)
    return pl.pallas_call(
        _mixer_body,
        grid=(n_blocks,),
        in_specs=[_resident_spec((rows, D_MODEL), (0, 0)),
                  _resident_spec((rows, D_MODEL), (min(1, n_blocks - 1), 0)),
                  pl.BlockSpec((rows, D_MODEL), lambda i: (jnp.minimum(i + 2, n_blocks - 1), 0))]
                 + [_resident_spec(c.shape) for c in consts],
        out_specs=pl.BlockSpec((rows, D_MODEL), lambda i: (i, 0)),
        out_shape=jax.ShapeDtypeStruct((n_rows, D_MODEL), BF16),
        scratch_shapes=[
            pltpu.VMEM((rows, D_IN), F32),
            pltpu.VMEM((rows, D_S5), BF16),
            pltpu.VMEM((rows, D_MODEL), BF16),
            pltpu.VMEM((S5_BLOCKS, rows, 2 * S5_BLOCK_STATES), F32),
            pltpu.VMEM((S5_BLOCKS, rows, 2 * S5_BLOCK_STATES), BF16),
            pltpu.VMEM((S5_BLOCKS, batch, 2 * S5_BLOCK_STATES), F32),
            pltpu.VMEM((rows + halo, D_LRU), F32),
            pltpu.VMEM((rows, D_LRU), F32),
            pltpu.VMEM((rows, D_LRU), BF16),
            pltpu.VMEM((rows, 2 * D_LRU), F32),
            pltpu.VMEM((rows, D_LRU), F32),
            pltpu.VMEM((rows, D_LRU), F32),
            pltpu.VMEM((batch, D_LRU), F32),
            pltpu.VMEM((rows, D_S5), F32),
            pltpu.VMEM((rows, D_S5), F32),
            pltpu.VMEM((rows, D_S5), BF16),
            pltpu.VMEM((rows, D_S5), F32),
            pltpu.VMEM((rows, D_LRU), F32),
            pltpu.VMEM((rows, D_S5), F32),
        ],
        compiler_params=pltpu.CompilerParams(
            dimension_semantics=("arbitrary",), vmem_limit_bytes=V7X_SCOPED_VMEM_BYTES),
        name="mixer",
    )(x_tm, x_tm, x_tm, *consts)


def _kv_body(mem_ref, g_ref, b_ref, wk_ref, wv_ref, k_ref, v_ref):
    mn = _layer_norm(mem_ref[...], g_ref[...], b_ref[...]).astype(BF16)
    k_ref[...] = (_dot(mn, wk_ref[...]) * (CA_HEAD_DIM ** -0.5)).astype(k_ref.dtype)
    v_ref[...] = _dot(mn, wv_ref[...]).astype(v_ref.dtype)


def _kv(mem, g, b, w_k, w_v):
    bsz, mlen, _ = mem.shape
    n_rows = bsz * mlen
    blk = pl.BlockSpec((KV_ROWS, D_MODEL), lambda i: (i, 0))
    out = jax.ShapeDtypeStruct((n_rows, D_MODEL), BF16)
    k, v = pl.pallas_call(
        _kv_body,
        grid=(n_rows // KV_ROWS,),
        in_specs=[blk, _const_spec(g.shape), _const_spec(b.shape), _const_spec(w_k.shape), _const_spec(w_v.shape)],
        out_specs=(blk, blk),
        out_shape=(out, out),
        compiler_params=pltpu.CompilerParams(
            dimension_semantics=("arbitrary",), vmem_limit_bytes=V7X_SCOPED_VMEM_BYTES),
        name="kv",
    )(mem.reshape(n_rows, D_MODEL), g, b, w_k, w_v)
    return k.reshape(mem.shape), v.reshape(mem.shape)


def _attn_norms(x_ref, mix_ref, lng_ref, lnb_ref, g1_ref, b1_ref, h1_ref, h1b_ref, slot):
    for r in _row_chunks(x_ref.shape[0]):
        h0 = _layer_norm(x_ref[r, :], lng_ref[...], lnb_ref[...])
        h1 = _layer_norm(ALPHA * h0 + mix_ref[r, :], g1_ref[...], b1_ref[...])
        h1_ref[slot, r, :] = h1
        h1b_ref[r, :] = h1.astype(BF16)


def _attn_body(x0_ref, y0_ref, x1_ref, y1_ref, xnext_ref, ynext_ref, k_ref, v_ref, wmix_ref, wq_ref, wo_ref,
               lng_ref, lnb_ref, g1_ref, b1_ref, g2_ref, b2_ref, o_ref,
               h1_ref, q_ref, mix_ref, h1b_ref, s_ref, p_ref, att_ref, ca_ref):
    i = pl.program_id(0)
    slot = i % 2
    rows = o_ref.shape[0]
    norm_refs = (lng_ref, lnb_ref, g1_ref, b1_ref, h1_ref, h1b_ref)

    @pl.when(i == 0)
    def _():
        mix_ref[...] = _dot(y0_ref[...], wmix_ref[...])
        _attn_norms(x0_ref, mix_ref, *norm_refs, 0)
        q_ref[...] = _dot(h1b_ref[...], wq_ref[...]).astype(BF16)
        mix_ref[...] = _dot(y1_ref[...], wmix_ref[...])
        _attn_norms(x1_ref, mix_ref, *norm_refs, 1)

    head_slices = [slice(hd * CA_HEAD_DIM, (hd + 1) * CA_HEAD_DIM) for hd in range(CA_HEADS)]
    for hd, hs in enumerate(head_slices):
        s_ref[hd] = lax.dot_general(q_ref[:, hs], k_ref[:, hs], (((1,), (1,)), ((), ())),
                                    preferred_element_type=F32)
    q_ref[...] = _dot(h1b_ref[...], wq_ref[...]).astype(BF16)
    mix_ref[...] = _dot(ynext_ref[...], wmix_ref[...])
    for r in _row_chunks(rows):
        for hd in range(CA_HEADS):
            s = s_ref[hd, r, :]
            e = jnp.exp(s - jnp.max(s, axis=-1, keepdims=True))
            p_ref[hd, r, :] = (e * (1.0 / jnp.sum(e, axis=-1, keepdims=True))).astype(BF16)
    for hd, hs in enumerate(head_slices):
        att_ref[:, hs] = _dot(p_ref[hd], v_ref[:, hs]).astype(BF16)
    ca_ref[...] = _dot(att_ref[...], wo_ref[...])
    for r in _row_chunks(rows):
        o_ref[r, :] = _layer_norm(ALPHA * h1_ref[slot, r, :] + ca_ref[r, :], g2_ref[...], b2_ref[...])
    _attn_norms(xnext_ref, mix_ref, *norm_refs, slot)


def _attn(x, y, k, v, w_mix, w_q, w_o, ln_g, ln_b, g1, b1, g2, b2):
    bsz, seq, _ = x.shape
    mlen = k.shape[1]
    n_rows = bsz * seq
    n_blocks = n_rows // ATTN_ROWS
    blocks_per_batch = seq // ATTN_ROWS
    blk = (ATTN_ROWS, D_MODEL)
    first_blk = _resident_spec(blk, (0, 0))
    second_blk = _resident_spec(blk, (min(1, n_blocks - 1), 0))
    next_blk = pl.BlockSpec(blk, lambda i: (jnp.minimum(i + 2, n_blocks - 1), 0))
    mem_blk = pl.BlockSpec((None, mlen, D_MODEL), lambda i: (i // blocks_per_batch, 0, 0))
    consts = (w_mix, w_q, w_o, ln_g, ln_b, g1, b1, g2, b2)
    x2 = x.reshape(n_rows, D_MODEL)
    y2 = y.reshape(n_rows, D_MODEL)
    return pl.pallas_call(
        _attn_body,
        grid=(n_blocks,),
        in_specs=[first_blk, first_blk, second_blk, second_blk, next_blk, next_blk, mem_blk, mem_blk]
                 + [_resident_spec(c.shape) for c in consts],
        out_specs=pl.BlockSpec(blk, lambda i: (i, 0)),
        out_shape=jax.ShapeDtypeStruct((n_rows, D_MODEL), F32),
        scratch_shapes=[pltpu.VMEM((2, ATTN_ROWS, D_MODEL), F32),
                        pltpu.VMEM((ATTN_ROWS, D_MODEL), BF16),
                        pltpu.VMEM((ATTN_ROWS, D_MODEL), F32),
                        pltpu.VMEM((ATTN_ROWS, D_MODEL), BF16),
                        pltpu.VMEM((CA_HEADS, ATTN_ROWS, mlen), F32),
                        pltpu.VMEM((CA_HEADS, ATTN_ROWS, mlen), BF16),
                        pltpu.VMEM((ATTN_ROWS, D_MODEL), BF16),
                        pltpu.VMEM((ATTN_ROWS, D_MODEL), F32)],
        compiler_params=pltpu.CompilerParams(
            dimension_semantics=("arbitrary",), vmem_limit_bytes=V7X_SCOPED_VMEM_BYTES),
        name="attn",
    )(x2, y2, x2, y2, x2, y2, k, v, *consts)


def _mlp_body(h_ref, w1_ref, w2_ref, g_ref, b_ref, o_ref):
    h = h_ref[...]
    hb = h.astype(BF16)
    ff = jnp.zeros(h.shape, F32)
    for j in range(D_FF // MLP_CHUNK):
        lo, hi = j * MLP_CHUNK, (j + 1) * MLP_CHUNK
        t = jnp.maximum(_dot(hb, w1_ref[:, lo:hi]), 0.0)
        ff = ff + _dot((t * t).astype(BF16), w2_ref[lo:hi, :])
    o_ref[...] = _layer_norm(ALPHA * h + ff, g_ref[...], b_ref[...])


def _mlp(h, w1, w2, g, b):
    n_rows = h.shape[0]
    row_blk = pl.BlockSpec((MLP_ROWS, D_MODEL), lambda i: (i, 0))
    return pl.pallas_call(
        _mlp_body,
        grid=(n_rows // MLP_ROWS,),
        in_specs=[row_blk, _resident_spec(w1.shape), _resident_spec(w2.shape),
                  _const_spec(g.shape), _const_spec(b.shape)],
        out_specs=row_blk,
        out_shape=jax.ShapeDtypeStruct(h.shape, F32),
        compiler_params=pltpu.CompilerParams(
            dimension_semantics=("arbitrary",), vmem_limit_bytes=V7X_SCOPED_VMEM_BYTES),
        name="mlp",
    )(h, w1, w2, g, b)


def _s5_block_matrices(bbar_re, bbar_im, c_re, c_im):
    eye = jnp.eye(8, dtype=F32)

    def expand_b(bb):
        bb = bb.reshape(S5_BLOCKS, 8, S5_GROUP, S5_STATE)
        return jnp.einsum("kghp,gj->kghjp", bb, eye).reshape(S5_BLOCKS, S5_BLOCK_CH, S5_BLOCK_STATES)

    def expand_c(cc):
        cc = cc.reshape(S5_BLOCKS, 8, S5_GROUP, S5_STATE)
        return jnp.einsum("kghp,gj->kgpjh", cc, eye).reshape(S5_BLOCKS, S5_BLOCK_STATES, S5_BLOCK_CH)

    bblk = jnp.concatenate([expand_b(bbar_re), expand_b(bbar_im)], axis=2).astype(BF16)
    cblk = jnp.concatenate([expand_c(c_re), -expand_c(c_im)], axis=1).astype(BF16)
    return bblk, cblk


def _lru_gate_matrices(w_a, w_x):
    eye = jnp.eye(4, dtype=F32)

    def expand(w):
        w = w.reshape(LRU_BLOCKS, 4, LRU_HEAD_DIM, LRU_HEAD_DIM)
        return jnp.einsum("khij,hl->khilj", w, eye).reshape(LRU_BLOCKS, LRU_BLOCK_CH, LRU_BLOCK_CH)

    return jnp.concatenate([expand(w_a), expand(w_x)], axis=2).astype(BF16)


def kernel(x, mem, ln_in_g, ln_in_b, w_in, s5_a_re, s5_a_im, s5_log_dt, s5_b_re, s5_b_im, s5_c_re, s5_c_im, s5_d, s5_w_glu, s5_b_glu, conv_w, conv_b, lru_w_a, lru_b_a, lru_w_x, lru_b_x, lru_lambda, g_s5, g_lru, w_mix_out, ln1_g, ln1_b, mem_ln_g, mem_ln_b, w_q, w_k, w_v, w_o, ln2_g, ln2_b, w_ff1, w_ff2, ln3_g, ln3_b):
    bsz, seq, d_model = x.shape
    assert d_model == D_MODEL and w_in.shape == (DEPTH, D_MODEL, D_IN)
    assert seq % MIXER_STEPS == 0 and seq % ATTN_ROWS == 0 and (bsz * seq) % MLP_ROWS == 0
    assert bsz % 16 == 0, "time-major rows of one step must fill whole bf16 tiles"
    row = lambda v: v.reshape(1, -1).astype(F32)
    l = 0

    abar_re, abar_im, bbar_re, bbar_im = _s5_prep(s5_a_re[l], s5_a_im[l], s5_log_dt[l], s5_b_re[l], s5_b_im[l])
    bblk, cblk = _s5_block_matrices(bbar_re, bbar_im, s5_c_re[l], s5_c_im[l])
    abar_re = abar_re.reshape(S5_BLOCKS, 1, S5_BLOCK_STATES)
    abar_im = abar_im.reshape(S5_BLOCKS, 1, S5_BLOCK_STATES)
    wg = _lru_gate_matrices(lru_w_a[l], lru_w_x[l])

    x_tm = jnp.swapaxes(x, 0, 1).reshape(seq * bsz, D_MODEL)
    y_tm = _mixer(x_tm, bsz, row(ln_in_g), row(ln_in_b), w_in[l].astype(BF16), bblk, abar_re, abar_im, cblk,
                  row(s5_d[l]), s5_w_glu[l].astype(BF16), row(s5_b_glu[l]), row(g_s5[l]),
                  conv_w[l].astype(F32), row(conv_b[l]), wg, row(lru_b_a[l]), row(lru_b_x[l]),
                  row(lru_lambda[l]), row(g_lru[l]))
    y = jnp.swapaxes(y_tm.reshape(seq, bsz, D_MODEL), 0, 1)

    k, v = _kv(mem, row(mem_ln_g[l]), row(mem_ln_b[l]), w_k[l].astype(BF16), w_v[l].astype(BF16))
    h2 = _attn(x, y, k, v, w_mix_out[l].astype(BF16), w_q[l].astype(BF16), w_o[l].astype(BF16),
               row(ln_in_g), row(ln_in_b), row(ln1_g[l]), row(ln1_b[l]), row(ln2_g[l]), row(ln2_b[l]))
    out = _mlp(h2, w_ff1[l].astype(BF16), w_ff2[l].astype(BF16), row(ln3_g[l]), row(ln3_b[l]))
    return out.reshape(bsz, seq, D_MODEL)
```

```python
import jax
import jax.numpy as jnp
from jax import lax
from jax.experimental import pallas as pl
from jax.experimental.pallas import tpu as pltpu

F32 = jnp.float32
BF16 = jnp.bfloat16

D_MODEL = 1024
D_S5 = 512
D_LRU = 512
D_IN = D_S5 + 2 * D_LRU
S5_GROUP = 16
S5_GROUPS = 32
S5_STATE = 64
LRU_HEADS = 8
LRU_HEAD_DIM = 64
CONV_WIDTH = 4
LRU_C = 8.0
D_FF = 4096
CA_HEADS = 4
CA_HEAD_DIM = 256
DEPTH = 1
ALPHA = (2 * DEPTH) ** 0.25
LN_EPS = 1e-5
RMS_EPS = 1e-6

S5_BLOCKS = 4
S5_BLOCK_CH = D_S5 // S5_BLOCKS
S5_BLOCK_STATES = 8 * S5_STATE
S5_CHUNK = 4
LRU_BLOCKS = 2
LRU_BLOCK_CH = D_LRU // LRU_BLOCKS

V7X_SCOPED_VMEM_BYTES = 56 * 1024 * 1024

MIXER_STEPS = 32
ATTN_ROWS = 512
MLP_ROWS = 1024
MLP_GROUP_ROWS = 512
KV_ROWS = 1024
MLP_CHUNK = 1024
VEC_ROWS = 16


def _layer_norm(x, g, b):
    mu = jnp.mean(x, axis=-1, keepdims=True)
    xc = x - mu
    var = jnp.mean(xc * xc, axis=-1, keepdims=True)
    return xc * lax.rsqrt(var + LN_EPS) * g + b


def _rms_norm(x, g):
    ms = jnp.mean(x * x, axis=-1, keepdims=True)
    return x * lax.rsqrt(ms + RMS_EPS) * g


def _gelu_tanh(x):
    c = 0.7978845608028654
    return 0.5 * x * (1.0 + jnp.tanh(c * (x + 0.044715 * (x * x * x))))


def _dot(a, b):
    return jnp.dot(a, b, preferred_element_type=F32)


def _const_spec(shape):
    zeros = (0,) * len(shape)
    return pl.BlockSpec(shape, lambda *_: zeros)


def _resident_spec(shape, block=None):
    zeros = (0,) * len(shape)
    index = zeros if block is None else block
    return pl.BlockSpec(shape, lambda *_: index, pipeline_mode=pl.Buffered(1))


def _row_chunks(rows, chunk=None):
    chunk = chunk or VEC_ROWS
    return [slice(r, r + chunk) for r in range(0, rows, chunk)]


def _cmul(a_re, a_im, b_re, b_im):
    return a_re * b_re - a_im * b_im, a_re * b_im + a_im * b_re


def _s5_prep_body(are_ref, aim_ref, ldt_ref, bre_ref, bim_ref, cre_ref, cim_ref,
                  apow_re_ref, apow_im_ref, bt_re_ref, bt_im_ref, e_re_ref, e_im_ref, lag_ref):
    a_re = are_ref[...]
    a_im = aim_ref[...]
    dt = jnp.exp(ldt_ref[...])
    mag = jnp.exp(dt * a_re)
    abar_re = mag * jnp.cos(dt * a_im)
    abar_im = mag * jnp.sin(dt * a_im)
    den = a_re * a_re + a_im * a_im
    q_re = ((abar_re - 1.0) * a_re + abar_im * a_im) / den
    q_im = (abar_im * a_re - (abar_re - 1.0) * a_im) / den
    bbar_re, bbar_im = _cmul(q_re, q_im, bre_ref[...], bim_ref[...])
    c_re = cre_ref[...]
    c_im = cim_ref[...]

    pows = [(jnp.ones_like(abar_re), jnp.zeros_like(abar_im))]
    for _ in range(S5_CHUNK):
        pows.append(_cmul(pows[-1][0], pows[-1][1], abar_re, abar_im))
    apow_re_ref[...] = pows[S5_CHUNK][0]
    apow_im_ref[...] = pows[S5_CHUNK][1]
    lane = lax.broadcasted_iota(jnp.int32, lag_ref.shape[1:], 2)
    for j in range(S5_CHUNK):
        p_re, p_im = pows[S5_CHUNK - 1 - j]
        bt_re_ref[j], bt_im_ref[j] = _cmul(p_re, p_im, bbar_re, bbar_im)
        p_re, p_im = pows[j + 1]
        e_re_ref[j], e_im_ref[j] = _cmul(p_re, p_im, c_re, c_im)
        ce_re, ce_im = _cmul(pows[j][0], pows[j][1], c_re, c_im)
        lag = jnp.zeros(lag_ref.shape[1:], F32)
        for hi in range(S5_GROUP):
            col = jnp.sum(ce_re * bbar_re[:, hi:hi + 1, :] - ce_im * bbar_im[:, hi:hi + 1, :],
                          axis=-1, keepdims=True)
            lag = jnp.where(lane == hi, col, lag)
        lag_ref[j] = lag


def _s5_prep(a_re, a_im, log_dt, b_re, b_im, c_re, c_im):
    g, p, h = b_re.shape
    f = jax.ShapeDtypeStruct
    t = S5_CHUNK
    return pl.pallas_call(
        _s5_prep_body,
        out_shape=(f((g, 1, p), F32), f((g, 1, p), F32),
                   f((t, g, h, p), F32), f((t, g, h, p), F32),
                   f((t, g, h, p), F32), f((t, g, h, p), F32),
                   f((t, g, h, h), F32)),
        name="s5_prep",
    )(a_re.reshape(g, 1, p), a_im.reshape(g, 1, p), log_dt.reshape(g, 1, 1),
      jnp.swapaxes(b_re, 1, 2), jnp.swapaxes(b_im, 1, 2), c_re, c_im)


def _mixer_norm(x_ref, lng_ref, lnb_ref, h0b_ref):
    for r in _row_chunks(x_ref.shape[0]):
        h0b_ref[r, :] = _layer_norm(x_ref[r, :], lng_ref[...], lnb_ref[...]).astype(BF16)


def _mixer_project_in(h0b_ref, win_ref, z_ref, uc_ref, batch):
    for piece in range(D_IN // D_S5):
        cols = slice(piece * D_S5, (piece + 1) * D_S5)
        z = _dot(h0b_ref[...], win_ref[:, cols])
        z_ref[:, cols] = z
        if piece == 0:
            zb = z.astype(BF16)
            for step in range(z.shape[0] // batch):
                c, t = divmod(step, S5_CHUNK)
                for k in range(S5_BLOCKS):
                    uc_ref[k, c * batch:(c + 1) * batch, t * S5_BLOCK_CH:(t + 1) * S5_BLOCK_CH] = (
                        zb[step * batch:(step + 1) * batch, k * S5_BLOCK_CH:(k + 1) * S5_BLOCK_CH])


def _mixer_conv(z_ref, cw_ref, cb_ref, d_ref, xl_ref, xc_ref, xcb_ref, gz_ref, du_ref, batch):
    rows = z_ref.shape[0]
    halo = (CONV_WIDTH - 1) * batch
    xl_ref[halo:halo + rows, :] = z_ref[:, D_S5:D_S5 + D_LRU]
    for r in _row_chunks(rows, 2 * VEC_ROWS):
        xc = cb_ref[...]
        for j in range(CONV_WIDTH):
            xc = xc + xl_ref[j * batch + r.start:j * batch + r.stop, :] * cw_ref[j:j + 1, :]
        xc_ref[r, :] = xc
        xcb_ref[r, :] = xc.astype(BF16)
        gz_ref[r, :] = _gelu_tanh(z_ref[r, D_S5 + D_LRU:D_IN])
        du_ref[r, :] = d_ref[...] * z_ref[r, 0:D_S5]
    xl_ref[0:halo, :] = xl_ref[rows:rows + halo, :]


def _mixer_step(xnext_ref, lng_ref, lnb_ref, win_ref, wic_ref, apr_ref, api_ref, woc_ref, d_ref,
                wglu_ref, bglu_ref, gs5_ref, cw_ref, cb_ref, wg_ref, ba_ref, bx_ref, lam_ref, glru_ref, y_ref,
                z_ref, uc_ref, h0b_ref, inc_ref, hp_ref, st_ref, xl_ref, xc_ref, xcb_ref, pre_ref, a_ref, hx_ref,
                hl_ref, yo_ref, y32_ref, yb_ref, gate_ref, gz_ref, du_ref):
    rows = y_ref.shape[0]
    batch = st_ref.shape[1]
    steps = rows // batch
    chunks = steps // S5_CHUNK
    wide = _row_chunks(rows, 2 * VEC_ROWS)
    sre = slice(0, S5_BLOCK_STATES)
    sim = slice(S5_BLOCK_STATES, 2 * S5_BLOCK_STATES)

    for k in range(S5_BLOCKS):
        inc_ref[k] = _dot(uc_ref[k], wic_ref[k])
    for kb in range(LRU_BLOCKS):
        lo, hi_ = kb * LRU_BLOCK_CH, (kb + 1) * LRU_BLOCK_CH
        pre = _dot(xcb_ref[:, lo:hi_], wg_ref[kb])
        pre_ref[:, lo:hi_] = pre[:, 0:LRU_BLOCK_CH]
        pre_ref[:, D_LRU + lo:D_LRU + hi_] = pre[:, LRU_BLOCK_CH:2 * LRU_BLOCK_CH]
    _mixer_project_in(h0b_ref, win_ref, z_ref, uc_ref, batch)

    for k in range(S5_BLOCKS):
        ar = jnp.broadcast_to(apr_ref[k], (batch, S5_BLOCK_STATES))
        ai = jnp.broadcast_to(api_ref[k], (batch, S5_BLOCK_STATES))
        hr = st_ref[k, :, sre]
        hi = st_ref[k, :, sim]
        for c in range(chunks):
            r = slice(c * batch, (c + 1) * batch)
            hp_ref[k, r, sre] = hr.astype(BF16)
            hp_ref[k, r, sim] = hi.astype(BF16)
            hr, hi = (ar * hr - ai * hi + inc_ref[k, r, sre],
                      ar * hi + ai * hr + inc_ref[k, r, sim])
        st_ref[k, :, sre] = hr
        st_ref[k, :, sim] = hi
        yo_ref[k] = _dot(hp_ref[k], woc_ref[k])
        if k == 1:
            lam = lam_ref[...]
            softplus_neg_lam = jnp.maximum(-lam, 0.0) + jnp.log1p(jnp.exp(-jnp.abs(lam)))
            for r in wide:
                gate_a = jax.nn.sigmoid(pre_ref[r, 0:D_LRU] + ba_ref[...])
                gate_x = jax.nn.sigmoid(pre_ref[r, D_LRU:2 * D_LRU] + bx_ref[...])
                log_a = -LRU_C * gate_a * softplus_neg_lam
                a = jnp.exp(log_a)
                m2 = -jnp.tanh(log_a) * (a * a + 1.0)
                mult = jnp.where(m2 > 0.0, m2 * lax.rsqrt(m2), 0.0)
                a_ref[r, :] = a
                hx_ref[r, :] = mult * (gate_x * xc_ref[r, :])
    hl = hl_ref[...]
    for t in range(steps):
        r = slice(t * batch, (t + 1) * batch)
        hl = a_ref[r, :] * hl + hx_ref[r, :]
        hx_ref[r, :] = hl
    hl_ref[...] = hl

    within = 2 * S5_BLOCK_STATES
    for step in range(steps):
        c, t = divmod(step, S5_CHUNK)
        r = slice(step * batch, (step + 1) * batch)
        rc = slice(c * batch, (c + 1) * batch)
        cols = slice(t * S5_BLOCK_CH, (t + 1) * S5_BLOCK_CH)
        ys = jnp.concatenate(
            [yo_ref[k, rc, cols] + inc_ref[k, rc, within + cols.start:within + cols.stop]
             for k in range(S5_BLOCKS)], axis=1)
        y = _gelu_tanh(ys + du_ref[r, :])
        y32_ref[r, :] = y
        yb_ref[r, :] = y.astype(BF16)
    gate_ref[...] = _dot(yb_ref[...], wglu_ref[...])
    for r in wide:
        y = y32_ref[r, :] * jax.nn.sigmoid(gate_ref[r, :] + bglu_ref[...])
        y_ref[r, 0:D_S5] = _rms_norm(y, gs5_ref[...]).astype(y_ref.dtype)
        ylru = hx_ref[r, :] * gz_ref[r, :]
        y_ref[r, D_S5:D_MODEL] = _rms_norm(ylru, glru_ref[...]).astype(y_ref.dtype)

    _mixer_conv(z_ref, cw_ref, cb_ref, d_ref, xl_ref, xc_ref, xcb_ref, gz_ref, du_ref, batch)
    _mixer_norm(xnext_ref, lng_ref, lnb_ref, h0b_ref)


def _mixer_body(xfirst_ref, xsecond_ref, xnext_ref, lng_ref, lnb_ref, win_ref, *rest):
    back_consts = rest[:15]
    d_ref, cw_ref, cb_ref = back_consts[4], back_consts[8], back_consts[9]
    y_ref = rest[15]
    scratch = rest[16:]
    z_ref, uc_ref, h0b_ref, _, _, st_ref, xl_ref, xc_ref, xcb_ref = scratch[:9]
    hl_ref, gz_ref, du_ref = scratch[12], scratch[17], scratch[18]
    batch = st_ref.shape[1]
    halo = (CONV_WIDTH - 1) * batch

    @pl.when(pl.program_id(0) == 0)
    def _():
        st_ref[...] = jnp.zeros_like(st_ref)
        hl_ref[...] = jnp.zeros_like(hl_ref)
        xl_ref[0:halo, :] = jnp.zeros((halo, D_LRU), F32)
        _mixer_norm(xfirst_ref, lng_ref, lnb_ref, h0b_ref)
        _mixer_project_in(h0b_ref, win_ref, z_ref, uc_ref, batch)
        _mixer_conv(z_ref, cw_ref, cb_ref, d_ref, xl_ref, xc_ref, xcb_ref, gz_ref, du_ref, batch)
        _mixer_norm(xsecond_ref, lng_ref, lnb_ref, h0b_ref)

    _mixer_step(xnext_ref, lng_ref, lnb_ref, win_ref, *back_consts, y_ref, *scratch)


def _mixer(x_tm, batch, ln_g, ln_b, w_in, w_in_chunk, apow_re, apow_im, w_out_chunk, d, w_glu, b_glu, g_s5,
           conv_w, conv_b, wg, b_a, b_x, lam, g_lru):
    n_rows = x_tm.shape[0]
    rows = MIXER_STEPS * batch
    n_blocks = n_rows // rows
    halo = (CONV_WIDTH - 1) * batch
    chunk_rows = rows // S5_CHUNK
    chunk_cols = S5_CHUNK * S5_BLOCK_CH
    consts = (ln_g, ln_b, w_in, w_in_chunk, apow_re, apow_im, w_out_chunk, d, w_glu, b_glu, g_s5,
              conv_w, conv_b, wg, b_a, b_x, lam, g_lru)
    return pl.pallas_call(
        _mixer_body,
        grid=(n_blocks,),
        in_specs=[_resident_spec((rows, D_MODEL), (0, 0)),
                  _resident_spec((rows, D_MODEL), (min(1, n_blocks - 1), 0)),
                  pl.BlockSpec((rows, D_MODEL), lambda i: (jnp.minimum(i + 2, n_blocks - 1), 0))]
                 + [_resident_spec(c.shape) for c in consts],
        out_specs=pl.BlockSpec((rows, D_MODEL), lambda i: (i, 0)),
        out_shape=jax.ShapeDtypeStruct((n_rows, D_MODEL), BF16),
        scratch_shapes=[
            pltpu.VMEM((rows, D_IN), F32),
            pltpu.VMEM((S5_BLOCKS, chunk_rows, chunk_cols), BF16),
            pltpu.VMEM((rows, D_MODEL), BF16),
            pltpu.VMEM((S5_BLOCKS, chunk_rows, 2 * S5_BLOCK_STATES + chunk_cols), F32),
            pltpu.VMEM((S5_BLOCKS, chunk_rows, 2 * S5_BLOCK_STATES), BF16),
            pltpu.VMEM((S5_BLOCKS, batch, 2 * S5_BLOCK_STATES), F32),
            pltpu.VMEM((rows + halo, D_LRU), F32),
            pltpu.VMEM((rows, D_LRU), F32),
            pltpu.VMEM((rows, D_LRU), BF16),
            pltpu.VMEM((rows, 2 * D_LRU), F32),
            pltpu.VMEM((rows, D_LRU), F32),
            pltpu.VMEM((rows, D_LRU), F32),
            pltpu.VMEM((batch, D_LRU), F32),
            pltpu.VMEM((S5_BLOCKS, chunk_rows, chunk_cols), F32),
            pltpu.VMEM((rows, D_S5), F32),
            pltpu.VMEM((rows, D_S5), BF16),
            pltpu.VMEM((rows, D_S5), F32),
            pltpu.VMEM((rows, D_LRU), F32),
            pltpu.VMEM((rows, D_S5), F32),
        ],
        compiler_params=pltpu.CompilerParams(
            dimension_semantics=("arbitrary",), vmem_limit_bytes=V7X_SCOPED_VMEM_BYTES),
        name="mixer",
    )(x_tm, x_tm, x_tm, *consts)


def _kv_body(mem_ref, g_ref, b_ref, wk_ref, wv_ref, k_ref, v_ref):
    mn = _layer_norm(mem_ref[...], g_ref[...], b_ref[...]).astype(BF16)
    k_ref[...] = (_dot(mn, wk_ref[...]) * (CA_HEAD_DIM ** -0.5)).astype(k_ref.dtype)
    v_ref[...] = _dot(mn, wv_ref[...]).astype(v_ref.dtype)


def _kv(mem, g, b, w_k, w_v):
    bsz, mlen, _ = mem.shape
    n_rows = bsz * mlen
    blk = pl.BlockSpec((KV_ROWS, D_MODEL), lambda i: (i, 0))
    out = jax.ShapeDtypeStruct((n_rows, D_MODEL), BF16)
    k, v = pl.pallas_call(
        _kv_body,
        grid=(n_rows // KV_ROWS,),
        in_specs=[blk, _const_spec(g.shape), _const_spec(b.shape), _const_spec(w_k.shape), _const_spec(w_v.shape)],
        out_specs=(blk, blk),
        out_shape=(out, out),
        compiler_params=pltpu.CompilerParams(
            dimension_semantics=("arbitrary",), vmem_limit_bytes=V7X_SCOPED_VMEM_BYTES),
        name="kv",
    )(mem.reshape(n_rows, D_MODEL), g, b, w_k, w_v)
    return k.reshape(mem.shape), v.reshape(mem.shape)


def _attn_norms(x_ref, mix_ref, lng_ref, lnb_ref, g1_ref, b1_ref, h1_ref, h1b_ref, slot):
    for r in _row_chunks(x_ref.shape[0]):
        h0 = _layer_norm(x_ref[r, :], lng_ref[...], lnb_ref[...])
        h1 = _layer_norm(ALPHA * h0 + mix_ref[r, :], g1_ref[...], b1_ref[...])
        h1_ref[slot, r, :] = h1
        h1b_ref[r, :] = h1.astype(BF16)


def _attn_body(x0_ref, y0_ref, x1_ref, y1_ref, xnext_ref, ynext_ref, k_ref, v_ref, wmix_ref, wq_ref, wo_ref,
               lng_ref, lnb_ref, g1_ref, b1_ref, g2_ref, b2_ref, o_ref,
               h1_ref, q_ref, mix_ref, h1b_ref, s_ref, p_ref, att_ref, ca_ref):
    i = pl.program_id(0)
    slot = i % 2
    rows = o_ref.shape[0]
    norm_refs = (lng_ref, lnb_ref, g1_ref, b1_ref, h1_ref, h1b_ref)

    @pl.when(i == 0)
    def _():
        mix_ref[...] = _dot(y0_ref[...], wmix_ref[...])
        _attn_norms(x0_ref, mix_ref, *norm_refs, 0)
        q_ref[...] = _dot(h1b_ref[...], wq_ref[...]).astype(BF16)
        mix_ref[...] = _dot(y1_ref[...], wmix_ref[...])
        _attn_norms(x1_ref, mix_ref, *norm_refs, 1)

    head_slices = [slice(hd * CA_HEAD_DIM, (hd + 1) * CA_HEAD_DIM) for hd in range(CA_HEADS)]
    for hd, hs in enumerate(head_slices):
        s_ref[hd] = lax.dot_general(q_ref[:, hs], k_ref[:, hs], (((1,), (1,)), ((), ())),
                                    preferred_element_type=F32)
    q_ref[...] = _dot(h1b_ref[...], wq_ref[...]).astype(BF16)
    mix_ref[...] = _dot(ynext_ref[...], wmix_ref[...])
    for r in _row_chunks(rows):
        for hd in range(CA_HEADS):
            s = s_ref[hd, r, :]
            e = jnp.exp(s - jnp.max(s, axis=-1, keepdims=True))
            p_ref[hd, r, :] = (e * (1.0 / jnp.sum(e, axis=-1, keepdims=True))).astype(BF16)
    for hd, hs in enumerate(head_slices):
        att_ref[:, hs] = _dot(p_ref[hd], v_ref[:, hs]).astype(BF16)
    ca_ref[...] = _dot(att_ref[...], wo_ref[...])
    for r in _row_chunks(rows):
        o_ref[r, :] = _layer_norm(ALPHA * h1_ref[slot, r, :] + ca_ref[r, :], g2_ref[...], b2_ref[...])
    _attn_norms(xnext_ref, mix_ref, *norm_refs, slot)


def _attn(x, y, k, v, w_mix, w_q, w_o, ln_g, ln_b, g1, b1, g2, b2):
    bsz, seq, _ = x.shape
    mlen = k.shape[1]
    n_rows = bsz * seq
    n_blocks = n_rows // ATTN_ROWS
    blocks_per_batch = seq // ATTN_ROWS
    blk = (ATTN_ROWS, D_MODEL)
    first_blk = _resident_spec(blk, (0, 0))
    second_blk = _resident_spec(blk, (min(1, n_blocks - 1), 0))
    next_blk = pl.BlockSpec(blk, lambda i: (jnp.minimum(i + 2, n_blocks - 1), 0))
    mem_blk = pl.BlockSpec((None, mlen, D_MODEL), lambda i: (i // blocks_per_batch, 0, 0))
    consts = (w_mix, w_q, w_o, ln_g, ln_b, g1, b1, g2, b2)
    x2 = x.reshape(n_rows, D_MODEL)
    y2 = y.reshape(n_rows, D_MODEL)
    return pl.pallas_call(
        _attn_body,
        grid=(n_blocks,),
        in_specs=[first_blk, first_blk, second_blk, second_blk, next_blk, next_blk, mem_blk, mem_blk]
                 + [_resident_spec(c.shape) for c in consts],
        out_specs=pl.BlockSpec(blk, lambda i: (i, 0)),
        out_shape=jax.ShapeDtypeStruct((n_rows, D_MODEL), F32),
        scratch_shapes=[pltpu.VMEM((2, ATTN_ROWS, D_MODEL), F32),
                        pltpu.VMEM((ATTN_ROWS, D_MODEL), BF16),
                        pltpu.VMEM((ATTN_ROWS, D_MODEL), F32),
                        pltpu.VMEM((ATTN_ROWS, D_MODEL), BF16),
                        pltpu.VMEM((CA_HEADS, ATTN_ROWS, mlen), F32),
                        pltpu.VMEM((CA_HEADS, ATTN_ROWS, mlen), BF16),
                        pltpu.VMEM((ATTN_ROWS, D_MODEL), BF16),
                        pltpu.VMEM((ATTN_ROWS, D_MODEL), F32)],
        compiler_params=pltpu.CompilerParams(
            dimension_semantics=("arbitrary",), vmem_limit_bytes=V7X_SCOPED_VMEM_BYTES),
        name="attn",
    )(x2, y2, x2, y2, x2, y2, k, v, *consts)


def _mlp_body(h_ref, w1_ref, w2_ref, g_ref, b_ref, o_ref):
    for r in _row_chunks(h_ref.shape[0], MLP_GROUP_ROWS):
        h = h_ref[r, :]
        hb = h.astype(BF16)
        ff = jnp.zeros(h.shape, F32)
        for j in range(D_FF // MLP_CHUNK):
            lo, hi = j * MLP_CHUNK, (j + 1) * MLP_CHUNK
            t = jnp.maximum(_dot(hb, w1_ref[:, lo:hi]), 0.0)
            ff = ff + _dot((t * t).astype(BF16), w2_ref[lo:hi, :])
        o_ref[r, :] = _layer_norm(ALPHA * h + ff, g_ref[...], b_ref[...])


def _mlp(h, w1, w2, g, b):
    n_rows = h.shape[0]
    row_blk = pl.BlockSpec((MLP_ROWS, D_MODEL), lambda i: (i, 0))
    return pl.pallas_call(
        _mlp_body,
        grid=(n_rows // MLP_ROWS,),
        in_specs=[row_blk, _resident_spec(w1.shape), _resident_spec(w2.shape),
                  _const_spec(g.shape), _const_spec(b.shape)],
        out_specs=row_blk,
        out_shape=jax.ShapeDtypeStruct(h.shape, F32),
        compiler_params=pltpu.CompilerParams(
            dimension_semantics=("arbitrary",), vmem_limit_bytes=V7X_SCOPED_VMEM_BYTES),
        name="mlp",
    )(h, w1, w2, g, b)


def _s5_block_matrices(bt_re, bt_im, e_re, e_im, lag):
    t_, g8 = S5_CHUNK, S5_GROUPS // S5_BLOCKS
    eye = jnp.eye(g8, dtype=F32)
    blk = lambda a: a.reshape((t_, S5_BLOCKS, g8) + a.shape[2:])

    def to_states(bt):
        return jnp.einsum("tkghp,gj->ktghjp", blk(bt), eye).reshape(
            S5_BLOCKS, t_ * S5_BLOCK_CH, S5_BLOCK_STATES)

    def from_states(e):
        return jnp.einsum("tkgop,gj->kgptjo", blk(e), eye).reshape(
            S5_BLOCKS, S5_BLOCK_STATES, t_ * S5_BLOCK_CH)

    lag = blk(lag)
    zero = jnp.zeros((S5_BLOCKS, S5_BLOCK_CH, S5_BLOCK_CH), F32)
    within = jnp.concatenate([
        jnp.concatenate([
            jnp.einsum("kgoi,gj->kgijo", lag[t_out - t_in], eye).reshape(S5_BLOCKS, S5_BLOCK_CH, S5_BLOCK_CH)
            if t_out >= t_in else zero
            for t_out in range(t_)], axis=2)
        for t_in in range(t_)], axis=1)
    w_in_chunk = jnp.concatenate([to_states(bt_re), to_states(bt_im), within], axis=2).astype(BF16)
    w_out_chunk = jnp.concatenate([from_states(e_re), -from_states(e_im)], axis=1).astype(BF16)
    return w_in_chunk, w_out_chunk


def _lru_gate_matrices(w_a, w_x):
    eye = jnp.eye(4, dtype=F32)

    def expand(w):
        w = w.reshape(LRU_BLOCKS, 4, LRU_HEAD_DIM, LRU_HEAD_DIM)
        return jnp.einsum("khij,hl->khilj", w, eye).reshape(LRU_BLOCKS, LRU_BLOCK_CH, LRU_BLOCK_CH)

    return jnp.concatenate([expand(w_a), expand(w_x)], axis=2).astype(BF16)


def kernel(x, mem, ln_in_g, ln_in_b, w_in, s5_a_re, s5_a_im, s5_log_dt, s5_b_re, s5_b_im, s5_c_re, s5_c_im, s5_d, s5_w_glu, s5_b_glu, conv_w, conv_b, lru_w_a, lru_b_a, lru_w_x, lru_b_x, lru_lambda, g_s5, g_lru, w_mix_out, ln1_g, ln1_b, mem_ln_g, mem_ln_b, w_q, w_k, w_v, w_o, ln2_g, ln2_b, w_ff1, w_ff2, ln3_g, ln3_b):
    bsz, seq, d_model = x.shape
    assert d_model == D_MODEL and w_in.shape == (DEPTH, D_MODEL, D_IN)
    assert seq % MIXER_STEPS == 0 and seq % ATTN_ROWS == 0 and (bsz * seq) % MLP_ROWS == 0
    assert bsz % 16 == 0, "time-major rows of one step must fill whole bf16 tiles"
    row = lambda v: v.reshape(1, -1).astype(F32)
    l = 0

    apow_re, apow_im, bt_re, bt_im, e_re, e_im, lag = _s5_prep(
        s5_a_re[l], s5_a_im[l], s5_log_dt[l], s5_b_re[l], s5_b_im[l], s5_c_re[l], s5_c_im[l])
    w_in_chunk, w_out_chunk = _s5_block_matrices(bt_re, bt_im, e_re, e_im, lag)
    apow_re = apow_re.reshape(S5_BLOCKS, 1, S5_BLOCK_STATES)
    apow_im = apow_im.reshape(S5_BLOCKS, 1, S5_BLOCK_STATES)
    wg = _lru_gate_matrices(lru_w_a[l], lru_w_x[l])

    x_tm = jnp.swapaxes(x, 0, 1).reshape(seq * bsz, D_MODEL)
    y_tm = _mixer(x_tm, bsz, row(ln_in_g), row(ln_in_b), w_in[l].astype(BF16), w_in_chunk, apow_re, apow_im,
                  w_out_chunk,
                  row(s5_d[l]), s5_w_glu[l].astype(BF16), row(s5_b_glu[l]), row(g_s5[l]),
                  conv_w[l].astype(F32), row(conv_b[l]), wg, row(lru_b_a[l]), row(lru_b_x[l]),
                  row(lru_lambda[l]), row(g_lru[l]))
    y = jnp.swapaxes(y_tm.reshape(seq, bsz, D_MODEL), 0, 1)

    k, v = _kv(mem, row(mem_ln_g[l]), row(mem_ln_b[l]), w_k[l].astype(BF16), w_v[l].astype(BF16))
    h2 = _attn(x, y, k, v, w_mix_out[l].astype(BF16), w_q[l].astype(BF16), w_o[l].astype(BF16),
               row(ln_in_g), row(ln_in_b), row(ln1_g[l]), row(ln1_b[l]), row(ln2_g[l]), row(ln2_b[l]))
    out = _mlp(h2, w_ff1[l].astype(BF16), w_ff2[l].astype(BF16), row(ln3_g[l]), row(ln3_b[l]))
    return out.reshape(bsz, seq, D_MODEL)
```

```python
import jax
import jax.numpy as jnp
from jax import lax
from jax.experimental import pallas as pl
from jax.experimental.pallas import tpu as pltpu

F32 = jnp.float32
BF16 = jnp.bfloat16

D_MODEL = 1024
D_S5 = 512
D_LRU = 512
D_IN = D_S5 + 2 * D_LRU
S5_GROUP = 16
S5_GROUPS = 32
S5_STATE = 64
LRU_HEADS = 8
LRU_HEAD_DIM = 64
CONV_WIDTH = 4
LRU_C = 8.0
D_FF = 4096
CA_HEADS = 4
CA_HEAD_DIM = 256
DEPTH = 1
ALPHA = (2 * DEPTH) ** 0.25
LN_EPS = 1e-5
RMS_EPS = 1e-6

S5_BLOCKS = 4
S5_BLOCK_CH = D_S5 // S5_BLOCKS
S5_BLOCK_STATES = 8 * S5_STATE
S5_CHUNK = 4
LRU_BLOCKS = 2
LRU_BLOCK_CH = D_LRU // LRU_BLOCKS

V7X_SCOPED_VMEM_BYTES = 56 * 1024 * 1024

MIXER_STEPS = 32
ATTN_ROWS = 512
MLP_ROWS = 1024
MLP_GROUP_ROWS = 512
KV_ROWS = 1024
MLP_CHUNK = 1024
VEC_ROWS = 16


def _layer_norm(x, g, b):
    mu = jnp.mean(x, axis=-1, keepdims=True)
    xc = x - mu
    var = jnp.mean(xc * xc, axis=-1, keepdims=True)
    return xc * lax.rsqrt(var + LN_EPS) * g + b


def _rms_norm(x, g):
    ms = jnp.mean(x * x, axis=-1, keepdims=True)
    return x * lax.rsqrt(ms + RMS_EPS) * g


def _gelu_tanh(x):
    c = 0.7978845608028654
    return 0.5 * x * (1.0 + jnp.tanh(c * (x + 0.044715 * (x * x * x))))


def _dot(a, b):
    return jnp.dot(a, b, preferred_element_type=F32)


def _const_spec(shape):
    zeros = (0,) * len(shape)
    return pl.BlockSpec(shape, lambda *_: zeros)


def _resident_spec(shape, block=None):
    zeros = (0,) * len(shape)
    index = zeros if block is None else block
    return pl.BlockSpec(shape, lambda *_: index, pipeline_mode=pl.Buffered(1))


def _row_chunks(rows, chunk=None):
    chunk = chunk or VEC_ROWS
    return [slice(r, r + chunk) for r in range(0, rows, chunk)]


def _cmul(a_re, a_im, b_re, b_im):
    return a_re * b_re - a_im * b_im, a_re * b_im + a_im * b_re


def _cmul(a_re, a_im, b_re, b_im):
    return a_re * b_re - a_im * b_im, a_re * b_im + a_im * b_re


def _discretise(a_re, a_im, log_dt):
    dt = jnp.exp(log_dt)
    mag = jnp.exp(dt * a_re)
    abar_re = mag * jnp.cos(dt * a_im)
    abar_im = mag * jnp.sin(dt * a_im)
    den = a_re * a_re + a_im * a_im
    q_re = ((abar_re - 1.0) * a_re + abar_im * a_im) / den
    q_im = (abar_im * a_re - (abar_re - 1.0) * a_im) / den
    return abar_re, abar_im, q_re, q_im


def _powers(abar_re, abar_im):
    pows = [(jnp.ones_like(abar_re), jnp.zeros_like(abar_im))]
    for _ in range(S5_CHUNK):
        pows.append(_cmul(pows[-1][0], pows[-1][1], abar_re, abar_im))
    return pows


def _block_diag_tile(stacked, group_rows, group_cols, n_groups):
    rows = n_groups * group_rows
    cols = n_groups * group_cols
    row_shift = group_rows.bit_length() - 1
    col_shift = group_cols.bit_length() - 1
    assert group_rows == 1 << row_shift and group_cols == 1 << col_shift
    src = lax.broadcasted_iota(jnp.int32, (group_cols, cols), 0)
    dst = lax.broadcasted_iota(jnp.int32, (group_cols, cols), 1)
    replicate = jnp.where((dst & (group_cols - 1)) == src, 1.0, 0.0).astype(BF16)
    tiled = _dot(stacked.astype(BF16), replicate)
    row_group = lax.broadcasted_iota(jnp.int32, (rows, cols), 0) >> row_shift
    col_group = lax.broadcasted_iota(jnp.int32, (rows, cols), 1) >> col_shift
    return jnp.where(row_group == col_group, tiled, 0.0).astype(BF16)


def _s5_prep_body(are_ref, aim_ref, ldt_ref, arec_ref, aimc_ref, ldtc_ref, bre_ref, bim_ref, cre_ref, cim_ref,
                  cret_ref, cimt_ref, wa_ref, wx_ref,
                  apow_re_ref, apow_im_ref, wic_ref, woc_ref, wg_ref):
    g8 = S5_GROUPS // S5_BLOCKS
    abar_re, abar_im, q_re, q_im = _discretise(are_ref[...], aim_ref[...], ldt_ref[...])
    pows = _powers(abar_re, abar_im)
    apow_re_ref[...] = pows[S5_CHUNK][0]
    apow_im_ref[...] = pows[S5_CHUNK][1]
    bbar_re, bbar_im = _cmul(q_re, q_im, bre_ref[...], bim_ref[...])
    c_re = cre_ref[...]
    c_im = cim_ref[...]
    abar_re_c, abar_im_c, _, _ = _discretise(arec_ref[...], aimc_ref[...], ldtc_ref[...])
    pows_c = _powers(abar_re_c, abar_im_c)
    ct_re = cret_ref[...]
    ct_im = cimt_ref[...]

    lane = lax.broadcasted_iota(jnp.int32, (S5_GROUPS, S5_GROUP, S5_GROUP), 2)
    within = 2 * S5_BLOCK_STATES
    wic_ref[:, :, within:] = jnp.zeros((S5_BLOCKS, S5_CHUNK * S5_BLOCK_CH, S5_CHUNK * S5_BLOCK_CH), BF16)
    for j in range(S5_CHUNK):
        bt = _cmul(*pows[S5_CHUNK - 1 - j], bbar_re, bbar_im)
        et = _cmul(*pows_c[j + 1], ct_re, ct_im)
        ce_re, ce_im = _cmul(*pows[j], c_re, c_im)
        lag = jnp.zeros(lane.shape, F32)
        for ho in range(S5_GROUP):
            col = jnp.sum(bbar_re * ce_re[:, ho:ho + 1, :] - bbar_im * ce_im[:, ho:ho + 1, :],
                          axis=-1, keepdims=True)
            lag = jnp.where(lane == ho, col, lag)
        rows_j = slice(j * S5_BLOCK_CH, (j + 1) * S5_BLOCK_CH)
        for k in range(S5_BLOCKS):
            grp = slice(k * g8, (k + 1) * g8)
            for part in range(2):
                cols = slice(part * S5_BLOCK_STATES, (part + 1) * S5_BLOCK_STATES)
                wic_ref[k, rows_j, cols] = _block_diag_tile(
                    bt[part][grp].reshape(S5_BLOCK_CH, S5_STATE), S5_GROUP, S5_STATE, g8)
                tile = _block_diag_tile(
                    et[part][grp].reshape(S5_BLOCK_STATES, S5_GROUP), S5_STATE, S5_GROUP, g8)
                woc_ref[k, cols, rows_j] = tile if part == 0 else -tile
            tile = _block_diag_tile(lag[grp].reshape(S5_BLOCK_CH, S5_GROUP), S5_GROUP, S5_GROUP, g8)
            for t_in in range(S5_CHUNK - j):
                t_out = t_in + j
                wic_ref[k, t_in * S5_BLOCK_CH:(t_in + 1) * S5_BLOCK_CH,
                        within + t_out * S5_BLOCK_CH:within + (t_out + 1) * S5_BLOCK_CH] = tile

    heads = LRU_HEADS // LRU_BLOCKS
    for kb in range(LRU_BLOCKS):
        for part, w_ref in enumerate((wa_ref, wx_ref)):
            stacked = w_ref[kb * heads:(kb + 1) * heads].reshape(LRU_BLOCK_CH, LRU_HEAD_DIM)
            wg_ref[kb, :, part * LRU_BLOCK_CH:(part + 1) * LRU_BLOCK_CH] = _block_diag_tile(
                stacked, LRU_HEAD_DIM, LRU_HEAD_DIM, heads)


def _s5_prep(a_re, a_im, log_dt, b_re, b_im, c_re, c_im, w_a, w_x):
    g, p, h = b_re.shape
    f = jax.ShapeDtypeStruct
    chunk_cols = S5_CHUNK * S5_BLOCK_CH
    return pl.pallas_call(
        _s5_prep_body,
        out_shape=(f((g, 1, p), F32), f((g, 1, p), F32),
                   f((S5_BLOCKS, chunk_cols, 2 * S5_BLOCK_STATES + chunk_cols), BF16),
                   f((S5_BLOCKS, 2 * S5_BLOCK_STATES, chunk_cols), BF16),
                   f((LRU_BLOCKS, LRU_BLOCK_CH, 2 * LRU_BLOCK_CH), BF16)),
        compiler_params=pltpu.CompilerParams(vmem_limit_bytes=V7X_SCOPED_VMEM_BYTES),
        name="s5_prep",
    )(a_re.reshape(g, 1, p), a_im.reshape(g, 1, p), log_dt.reshape(g, 1, 1),
      a_re.reshape(g, p, 1), a_im.reshape(g, p, 1), log_dt.reshape(g, 1, 1),
      jnp.swapaxes(b_re, 1, 2), jnp.swapaxes(b_im, 1, 2), c_re, c_im,
      jnp.swapaxes(c_re, 1, 2), jnp.swapaxes(c_im, 1, 2), w_a, w_x)


def _mixer_norm(x_ref, lng_ref, lnb_ref, h0b_ref):
    for r in _row_chunks(x_ref.shape[0]):
        h0b_ref[r, :] = _layer_norm(x_ref[r, :], lng_ref[...], lnb_ref[...]).astype(BF16)


def _mixer_project_in(h0b_ref, win_ref, z_ref, uc_ref, batch):
    for piece in range(D_IN // D_S5):
        cols = slice(piece * D_S5, (piece + 1) * D_S5)
        z = _dot(h0b_ref[...], win_ref[:, cols])
        z_ref[:, cols] = z
        if piece == 0:
            zb = z.astype(BF16)
            for step in range(z.shape[0] // batch):
                c, t = divmod(step, S5_CHUNK)
                for k in range(S5_BLOCKS):
                    uc_ref[k, c * batch:(c + 1) * batch, t * S5_BLOCK_CH:(t + 1) * S5_BLOCK_CH] = (
                        zb[step * batch:(step + 1) * batch, k * S5_BLOCK_CH:(k + 1) * S5_BLOCK_CH])


def _mixer_conv(z_ref, cw_ref, cb_ref, d_ref, xl_ref, xc_ref, xcb_ref, gz_ref, du_ref, batch):
    rows = z_ref.shape[0]
    halo = (CONV_WIDTH - 1) * batch
    xl_ref[halo:halo + rows, :] = z_ref[:, D_S5:D_S5 + D_LRU]
    for r in _row_chunks(rows, 2 * VEC_ROWS):
        xc = cb_ref[...]
        for j in range(CONV_WIDTH):
            xc = xc + xl_ref[j * batch + r.start:j * batch + r.stop, :] * cw_ref[j:j + 1, :]
        xc_ref[r, :] = xc
        xcb_ref[r, :] = xc.astype(BF16)
        gz_ref[r, :] = _gelu_tanh(z_ref[r, D_S5 + D_LRU:D_IN])
        du_ref[r, :] = d_ref[...] * z_ref[r, 0:D_S5]
    xl_ref[0:halo, :] = xl_ref[rows:rows + halo, :]


def _mixer_step(xnext_ref, lng_ref, lnb_ref, win_ref, wic_ref, apr_ref, api_ref, woc_ref, d_ref,
                wglu_ref, bglu_ref, gs5_ref, cw_ref, cb_ref, wg_ref, ba_ref, bx_ref, lam_ref, glru_ref, y_ref,
                z_ref, uc_ref, h0b_ref, inc_ref, hp_ref, st_ref, xl_ref, xc_ref, xcb_ref, pre_ref, a_ref, hx_ref,
                hl_ref, yo_ref, y32_ref, yb_ref, gate_ref, gz_ref, du_ref):
    rows = y_ref.shape[0]
    batch = st_ref.shape[1]
    steps = rows // batch
    chunks = steps // S5_CHUNK
    wide = _row_chunks(rows, 2 * VEC_ROWS)
    sre = slice(0, S5_BLOCK_STATES)
    sim = slice(S5_BLOCK_STATES, 2 * S5_BLOCK_STATES)

    for k in range(S5_BLOCKS):
        inc_ref[k] = _dot(uc_ref[k], wic_ref[k])
    for kb in range(LRU_BLOCKS):
        lo, hi_ = kb * LRU_BLOCK_CH, (kb + 1) * LRU_BLOCK_CH
        pre = _dot(xcb_ref[:, lo:hi_], wg_ref[kb])
        pre_ref[:, lo:hi_] = pre[:, 0:LRU_BLOCK_CH]
        pre_ref[:, D_LRU + lo:D_LRU + hi_] = pre[:, LRU_BLOCK_CH:2 * LRU_BLOCK_CH]
    _mixer_project_in(h0b_ref, win_ref, z_ref, uc_ref, batch)

    for k in range(S5_BLOCKS):
        ar = jnp.broadcast_to(apr_ref[k], (batch, S5_BLOCK_STATES))
        ai = jnp.broadcast_to(api_ref[k], (batch, S5_BLOCK_STATES))
        hr = st_ref[k, :, sre]
        hi = st_ref[k, :, sim]
        for c in range(chunks):
            r = slice(c * batch, (c + 1) * batch)
            hp_ref[k, r, sre] = hr.astype(BF16)
            hp_ref[k, r, sim] = hi.astype(BF16)
            hr, hi = (ar * hr - ai * hi + inc_ref[k, r, sre],
                      ar * hi + ai * hr + inc_ref[k, r, sim])
        st_ref[k, :, sre] = hr
        st_ref[k, :, sim] = hi
        yo_ref[k] = _dot(hp_ref[k], woc_ref[k])
        if k == 1:
            lam = lam_ref[...]
            softplus_neg_lam = jnp.maximum(-lam, 0.0) + jnp.log1p(jnp.exp(-jnp.abs(lam)))
            for r in wide:
                gate_a = jax.nn.sigmoid(pre_ref[r, 0:D_LRU] + ba_ref[...])
                gate_x = jax.nn.sigmoid(pre_ref[r, D_LRU:2 * D_LRU] + bx_ref[...])
                log_a = -LRU_C * gate_a * softplus_neg_lam
                a = jnp.exp(log_a)
                m2 = -jnp.tanh(log_a) * (a * a + 1.0)
                mult = jnp.where(m2 > 0.0, m2 * lax.rsqrt(m2), 0.0)
                a_ref[r, :] = a
                hx_ref[r, :] = mult * (gate_x * xc_ref[r, :])
    hl = hl_ref[...]
    for t in range(steps):
        r = slice(t * batch, (t + 1) * batch)
        hl = a_ref[r, :] * hl + hx_ref[r, :]
        hx_ref[r, :] = hl
    hl_ref[...] = hl

    within = 2 * S5_BLOCK_STATES
    for step in range(steps):
        c, t = divmod(step, S5_CHUNK)
        r = slice(step * batch, (step + 1) * batch)
        rc = slice(c * batch, (c + 1) * batch)
        cols = slice(t * S5_BLOCK_CH, (t + 1) * S5_BLOCK_CH)
        ys = jnp.concatenate(
            [yo_ref[k, rc, cols] + inc_ref[k, rc, within + cols.start:within + cols.stop]
             for k in range(S5_BLOCKS)], axis=1)
        y = _gelu_tanh(ys + du_ref[r, :])
        y32_ref[r, :] = y
        yb_ref[r, :] = y.astype(BF16)
    gate_ref[...] = _dot(yb_ref[...], wglu_ref[...])
    for r in wide:
        y = y32_ref[r, :] * jax.nn.sigmoid(gate_ref[r, :] + bglu_ref[...])
        y_ref[r, 0:D_S5] = _rms_norm(y, gs5_ref[...]).astype(y_ref.dtype)
        ylru = hx_ref[r, :] * gz_ref[r, :]
        y_ref[r, D_S5:D_MODEL] = _rms_norm(ylru, glru_ref[...]).astype(y_ref.dtype)

    _mixer_conv(z_ref, cw_ref, cb_ref, d_ref, xl_ref, xc_ref, xcb_ref, gz_ref, du_ref, batch)
    _mixer_norm(xnext_ref, lng_ref, lnb_ref, h0b_ref)


def _mixer_body(xfirst_ref, xsecond_ref, xnext_ref, lng_ref, lnb_ref, win_ref, *rest):
    back_consts = rest[:15]
    d_ref, cw_ref, cb_ref = back_consts[4], back_consts[8], back_consts[9]
    y_ref = rest[15]
    scratch = rest[16:]
    z_ref, uc_ref, h0b_ref, _, _, st_ref, xl_ref, xc_ref, xcb_ref = scratch[:9]
    hl_ref, gz_ref, du_ref = scratch[12], scratch[17], scratch[18]
    batch = st_ref.shape[1]
    halo = (CONV_WIDTH - 1) * batch

    @pl.when(pl.program_id(0) == 0)
    def _():
        st_ref[...] = jnp.zeros_like(st_ref)
        hl_ref[...] = jnp.zeros_like(hl_ref)
        xl_ref[0:halo, :] = jnp.zeros((halo, D_LRU), F32)
        _mixer_norm(xfirst_ref, lng_ref, lnb_ref, h0b_ref)
        _mixer_project_in(h0b_ref, win_ref, z_ref, uc_ref, batch)
        _mixer_conv(z_ref, cw_ref, cb_ref, d_ref, xl_ref, xc_ref, xcb_ref, gz_ref, du_ref, batch)
        _mixer_norm(xsecond_ref, lng_ref, lnb_ref, h0b_ref)

    _mixer_step(xnext_ref, lng_ref, lnb_ref, win_ref, *back_consts, y_ref, *scratch)


def _mixer(x_tm, batch, ln_g, ln_b, w_in, w_in_chunk, apow_re, apow_im, w_out_chunk, d, w_glu, b_glu, g_s5,
           conv_w, conv_b, wg, b_a, b_x, lam, g_lru):
    n_rows = x_tm.shape[0]
    rows = MIXER_STEPS * batch
    n_blocks = n_rows // rows
    halo = (CONV_WIDTH - 1) * batch
    chunk_rows = rows // S5_CHUNK
    chunk_cols = S5_CHUNK * S5_BLOCK_CH
    consts = (ln_g, ln_b, w_in, w_in_chunk, apow_re, apow_im, w_out_chunk, d, w_glu, b_glu, g_s5,
              conv_w, conv_b, wg, b_a, b_x, lam, g_lru)
    return pl.pallas_call(
        _mixer_body,
        grid=(n_blocks,),
        in_specs=[_resident_spec((rows, D_MODEL), (0, 0)),
                  _resident_spec((rows, D_MODEL), (min(1, n_blocks - 1), 0)),
                  pl.BlockSpec((rows, D_MODEL), lambda i: (jnp.minimum(i + 2, n_blocks - 1), 0))]
                 + [_resident_spec(c.shape) for c in consts],
        out_specs=pl.BlockSpec((rows, D_MODEL), lambda i: (i, 0)),
        out_shape=jax.ShapeDtypeStruct((n_rows, D_MODEL), BF16),
        scratch_shapes=[
            pltpu.VMEM((rows, D_IN), F32),
            pltpu.VMEM((S5_BLOCKS, chunk_rows, chunk_cols), BF16),
            pltpu.VMEM((rows, D_MODEL), BF16),
            pltpu.VMEM((S5_BLOCKS, chunk_rows, 2 * S5_BLOCK_STATES + chunk_cols), F32),
            pltpu.VMEM((S5_BLOCKS, chunk_rows, 2 * S5_BLOCK_STATES), BF16),
            pltpu.VMEM((S5_BLOCKS, batch, 2 * S5_BLOCK_STATES), F32),
            pltpu.VMEM((rows + halo, D_LRU), F32),
            pltpu.VMEM((rows, D_LRU), F32),
            pltpu.VMEM((rows, D_LRU), BF16),
            pltpu.VMEM((rows, 2 * D_LRU), F32),
            pltpu.VMEM((rows, D_LRU), F32),
            pltpu.VMEM((rows, D_LRU), F32),
            pltpu.VMEM((batch, D_LRU), F32),
            pltpu.VMEM((S5_BLOCKS, chunk_rows, chunk_cols), F32),
            pltpu.VMEM((rows, D_S5), F32),
            pltpu.VMEM((rows, D_S5), BF16),
            pltpu.VMEM((rows, D_S5), F32),
            pltpu.VMEM((rows, D_LRU), F32),
            pltpu.VMEM((rows, D_S5), F32),
        ],
        compiler_params=pltpu.CompilerParams(
            dimension_semantics=("arbitrary",), vmem_limit_bytes=V7X_SCOPED_VMEM_BYTES),
        name="mixer",
    )(x_tm, x_tm, x_tm, *consts)


def _kv_body(mem_ref, g_ref, b_ref, wk_ref, wv_ref, k_ref, v_ref):
    mn = _layer_norm(mem_ref[...], g_ref[...], b_ref[...]).astype(BF16)
    k_ref[...] = (_dot(mn, wk_ref[...]) * (CA_HEAD_DIM ** -0.5)).astype(k_ref.dtype)
    v_ref[...] = _dot(mn, wv_ref[...]).astype(v_ref.dtype)


def _kv(mem, g, b, w_k, w_v):
    bsz, mlen, _ = mem.shape
    n_rows = bsz * mlen
    blk = pl.BlockSpec((KV_ROWS, D_MODEL), lambda i: (i, 0))
    out = jax.ShapeDtypeStruct((n_rows, D_MODEL), BF16)
    k, v = pl.pallas_call(
        _kv_body,
        grid=(n_rows // KV_ROWS,),
        in_specs=[blk, _const_spec(g.shape), _const_spec(b.shape), _const_spec(w_k.shape), _const_spec(w_v.shape)],
        out_specs=(blk, blk),
        out_shape=(out, out),
        compiler_params=pltpu.CompilerParams(
            dimension_semantics=("arbitrary",), vmem_limit_bytes=V7X_SCOPED_VMEM_BYTES),
        name="kv",
    )(mem.reshape(n_rows, D_MODEL), g, b, w_k, w_v)
    return k.reshape(mem.shape), v.reshape(mem.shape)


def _attn_norms(x_ref, mix_ref, lng_ref, lnb_ref, g1_ref, b1_ref, h1_ref, h1b_ref, slot):
    for r in _row_chunks(x_ref.shape[0]):
        h0 = _layer_norm(x_ref[r, :], lng_ref[...], lnb_ref[...])
        h1 = _layer_norm(ALPHA * h0 + mix_ref[r, :], g1_ref[...], b1_ref[...])
        h1_ref[slot, r, :] = h1
        h1b_ref[r, :] = h1.astype(BF16)


def _attn_body(x0_ref, y0_ref, x1_ref, y1_ref, xnext_ref, ynext_ref, k_ref, v_ref, wmix_ref, wq_ref, wo_ref,
               lng_ref, lnb_ref, g1_ref, b1_ref, g2_ref, b2_ref, o_ref,
               h1_ref, q_ref, mix_ref, h1b_ref, s_ref, p_ref, att_ref, ca_ref):
    i = pl.program_id(0)
    slot = i % 2
    rows = o_ref.shape[0]
    norm_refs = (lng_ref, lnb_ref, g1_ref, b1_ref, h1_ref, h1b_ref)

    @pl.when(i == 0)
    def _():
        mix_ref[...] = _dot(y0_ref[...], wmix_ref[...])
        _attn_norms(x0_ref, mix_ref, *norm_refs, 0)
        q_ref[...] = _dot(h1b_ref[...], wq_ref[...]).astype(BF16)
        mix_ref[...] = _dot(y1_ref[...], wmix_ref[...])
        _attn_norms(x1_ref, mix_ref, *norm_refs, 1)

    head_slices = [slice(hd * CA_HEAD_DIM, (hd + 1) * CA_HEAD_DIM) for hd in range(CA_HEADS)]
    for hd, hs in enumerate(head_slices):
        s_ref[hd] = lax.dot_general(q_ref[:, hs], k_ref[:, hs], (((1,), (1,)), ((), ())),
                                    preferred_element_type=F32)
    q_ref[...] = _dot(h1b_ref[...], wq_ref[...]).astype(BF16)
    mix_ref[...] = _dot(ynext_ref[...], wmix_ref[...])
    for r in _row_chunks(rows):
        for hd in range(CA_HEADS):
            s = s_ref[hd, r, :]
            e = jnp.exp(s - jnp.max(s, axis=-1, keepdims=True))
            p_ref[hd, r, :] = (e * (1.0 / jnp.sum(e, axis=-1, keepdims=True))).astype(BF16)
    for hd, hs in enumerate(head_slices):
        att_ref[:, hs] = _dot(p_ref[hd], v_ref[:, hs]).astype(BF16)
    ca_ref[...] = _dot(att_ref[...], wo_ref[...])
    for r in _row_chunks(rows):
        o_ref[r, :] = _layer_norm(ALPHA * h1_ref[slot, r, :] + ca_ref[r, :], g2_ref[...], b2_ref[...])
    _attn_norms(xnext_ref, mix_ref, *norm_refs, slot)


def _attn(x, y, k, v, w_mix, w_q, w_o, ln_g, ln_b, g1, b1, g2, b2):
    bsz, seq, _ = x.shape
    mlen = k.shape[1]
    n_rows = bsz * seq
    n_blocks = n_rows // ATTN_ROWS
    blocks_per_batch = seq // ATTN_ROWS
    blk = (ATTN_ROWS, D_MODEL)
    first_blk = _resident_spec(blk, (0, 0))
    second_blk = _resident_spec(blk, (min(1, n_blocks - 1), 0))
    next_blk = pl.BlockSpec(blk, lambda i: (jnp.minimum(i + 2, n_blocks - 1), 0))
    mem_blk = pl.BlockSpec((None, mlen, D_MODEL), lambda i: (i // blocks_per_batch, 0, 0))
    consts = (w_mix, w_q, w_o, ln_g, ln_b, g1, b1, g2, b2)
    x2 = x.reshape(n_rows, D_MODEL)
    y2 = y.reshape(n_rows, D_MODEL)
    return pl.pallas_call(
        _attn_body,
        grid=(n_blocks,),
        in_specs=[first_blk, first_blk, second_blk, second_blk, next_blk, next_blk, mem_blk, mem_blk]
                 + [_resident_spec(c.shape) for c in consts],
        out_specs=pl.BlockSpec(blk, lambda i: (i, 0)),
        out_shape=jax.ShapeDtypeStruct((n_rows, D_MODEL), F32),
        scratch_shapes=[pltpu.VMEM((2, ATTN_ROWS, D_MODEL), F32),
                        pltpu.VMEM((ATTN_ROWS, D_MODEL), BF16),
                        pltpu.VMEM((ATTN_ROWS, D_MODEL), F32),
                        pltpu.VMEM((ATTN_ROWS, D_MODEL), BF16),
                        pltpu.VMEM((CA_HEADS, ATTN_ROWS, mlen), F32),
                        pltpu.VMEM((CA_HEADS, ATTN_ROWS, mlen), BF16),
                        pltpu.VMEM((ATTN_ROWS, D_MODEL), BF16),
                        pltpu.VMEM((ATTN_ROWS, D_MODEL), F32)],
        compiler_params=pltpu.CompilerParams(
            dimension_semantics=("arbitrary",), vmem_limit_bytes=V7X_SCOPED_VMEM_BYTES),
        name="attn",
    )(x2, y2, x2, y2, x2, y2, k, v, *consts)


def _mlp_body(h_ref, w1_ref, w2_ref, g_ref, b_ref, o_ref):
    for r in _row_chunks(h_ref.shape[0], MLP_GROUP_ROWS):
        h = h_ref[r, :]
        hb = h.astype(BF16)
        ff = jnp.zeros(h.shape, F32)
        for j in range(D_FF // MLP_CHUNK):
            lo, hi = j * MLP_CHUNK, (j + 1) * MLP_CHUNK
            t = jnp.maximum(_dot(hb, w1_ref[:, lo:hi]), 0.0)
            ff = ff + _dot((t * t).astype(BF16), w2_ref[lo:hi, :])
        o_ref[r, :] = _layer_norm(ALPHA * h + ff, g_ref[...], b_ref[...])


def _mlp(h, w1, w2, g, b):
    n_rows = h.shape[0]
    row_blk = pl.BlockSpec((MLP_ROWS, D_MODEL), lambda i: (i, 0))
    return pl.pallas_call(
        _mlp_body,
        grid=(n_rows // MLP_ROWS,),
        in_specs=[row_blk, _resident_spec(w1.shape), _resident_spec(w2.shape),
                  _const_spec(g.shape), _const_spec(b.shape)],
        out_specs=row_blk,
        out_shape=jax.ShapeDtypeStruct(h.shape, F32),
        compiler_params=pltpu.CompilerParams(
            dimension_semantics=("arbitrary",), vmem_limit_bytes=V7X_SCOPED_VMEM_BYTES),
        name="mlp",
    )(h, w1, w2, g, b)


def kernel(x, mem, ln_in_g, ln_in_b, w_in, s5_a_re, s5_a_im, s5_log_dt, s5_b_re, s5_b_im, s5_c_re, s5_c_im, s5_d, s5_w_glu, s5_b_glu, conv_w, conv_b, lru_w_a, lru_b_a, lru_w_x, lru_b_x, lru_lambda, g_s5, g_lru, w_mix_out, ln1_g, ln1_b, mem_ln_g, mem_ln_b, w_q, w_k, w_v, w_o, ln2_g, ln2_b, w_ff1, w_ff2, ln3_g, ln3_b):
    bsz, seq, d_model = x.shape
    assert d_model == D_MODEL and w_in.shape == (DEPTH, D_MODEL, D_IN)
    assert seq % MIXER_STEPS == 0 and seq % ATTN_ROWS == 0 and (bsz * seq) % MLP_ROWS == 0
    assert bsz % 16 == 0, "time-major rows of one step must fill whole bf16 tiles"
    row = lambda v: v.reshape(1, -1).astype(F32)
    l = 0

    apow_re, apow_im, w_in_chunk, w_out_chunk, wg = _s5_prep(
        s5_a_re[l], s5_a_im[l], s5_log_dt[l], s5_b_re[l], s5_b_im[l], s5_c_re[l], s5_c_im[l],
        lru_w_a[l], lru_w_x[l])
    apow_re = apow_re.reshape(S5_BLOCKS, 1, S5_BLOCK_STATES)
    apow_im = apow_im.reshape(S5_BLOCKS, 1, S5_BLOCK_STATES)

    x_tm = jnp.swapaxes(x, 0, 1).reshape(seq * bsz, D_MODEL)
    y_tm = _mixer(x_tm, bsz, row(ln_in_g), row(ln_in_b), w_in[l].astype(BF16), w_in_chunk, apow_re, apow_im,
                  w_out_chunk,
                  row(s5_d[l]), s5_w_glu[l].astype(BF16), row(s5_b_glu[l]), row(g_s5[l]),
                  conv_w[l].astype(F32), row(conv_b[l]), wg, row(lru_b_a[l]), row(lru_b_x[l]),
                  row(lru_lambda[l]), row(g_lru[l]))
    y = jnp.swapaxes(y_tm.reshape(seq, bsz, D_MODEL), 0, 1)

    k, v = _kv(mem, row(mem_ln_g[l]), row(mem_ln_b[l]), w_k[l].astype(BF16), w_v[l].astype(BF16))
    h2 = _attn(x, y, k, v, w_mix_out[l].astype(BF16), w_q[l].astype(BF16), w_o[l].astype(BF16),
               row(ln_in_g), row(ln_in_b), row(ln1_g[l]), row(ln1_b[l]), row(ln2_g[l]), row(ln2_b[l]))
    out = _mlp(h2, w_ff1[l].astype(BF16), w_ff2[l].astype(BF16), row(ln3_g[l]), row(ln3_b[l]))
    return out.reshape(bsz, seq, D_MODEL)
```

```python
import jax
import jax.numpy as jnp
from jax import lax
from jax.experimental import pallas as pl
from jax.experimental.pallas import tpu as pltpu

F32 = jnp.float32
BF16 = jnp.bfloat16

D_MODEL = 1024
D_S5 = 512
D_LRU = 512
D_IN = D_S5 + 2 * D_LRU
S5_GROUP = 16
S5_GROUPS = 32
S5_STATE = 64
LRU_HEADS = 8
LRU_HEAD_DIM = 64
CONV_WIDTH = 4
LRU_C = 8.0
D_FF = 4096
CA_HEADS = 4
CA_HEAD_DIM = 256
DEPTH = 1
ALPHA = (2 * DEPTH) ** 0.25
LN_EPS = 1e-5
RMS_EPS = 1e-6

S5_BLOCKS = 4
S5_BLOCK_CH = D_S5 // S5_BLOCKS
S5_BLOCK_STATES = 8 * S5_STATE
S5_CHUNK = 4
LRU_BLOCKS = 2
LRU_BLOCK_CH = D_LRU // LRU_BLOCKS

V7X_SCOPED_VMEM_BYTES = 56 * 1024 * 1024

MIXER_STEPS = 32
ATTN_ROWS = 512
MLP_ROWS = 1024
MLP_GROUP_ROWS = 512
KV_ROWS = 1024
MLP_CHUNK = 1024
VEC_ROWS = 16


def _layer_norm(x, g, b):
    mu = jnp.mean(x, axis=-1, keepdims=True)
    xc = x - mu
    var = jnp.mean(xc * xc, axis=-1, keepdims=True)
    return xc * lax.rsqrt(var + LN_EPS) * g + b


def _rms_norm(x, g):
    ms = jnp.mean(x * x, axis=-1, keepdims=True)
    return x * lax.rsqrt(ms + RMS_EPS) * g


def _gelu_tanh(x):
    c = 0.7978845608028654
    return 0.5 * x * (1.0 + jnp.tanh(c * (x + 0.044715 * (x * x * x))))


def _dot(a, b):
    return jnp.dot(a, b, preferred_element_type=F32)


def _const_spec(shape):
    zeros = (0,) * len(shape)
    return pl.BlockSpec(shape, lambda *_: zeros)


def _resident_spec(shape, block=None):
    zeros = (0,) * len(shape)
    index = zeros if block is None else block
    return pl.BlockSpec(shape, lambda *_: index, pipeline_mode=pl.Buffered(1))


def _row_chunks(rows, chunk=None):
    chunk = chunk or VEC_ROWS
    return [slice(r, r + chunk) for r in range(0, rows, chunk)]


def _cmul(a_re, a_im, b_re, b_im):
    return a_re * b_re - a_im * b_im, a_re * b_im + a_im * b_re


def _cmul(a_re, a_im, b_re, b_im):
    return a_re * b_re - a_im * b_im, a_re * b_im + a_im * b_re


def _discretise(a_re, a_im, log_dt):
    dt = jnp.exp(log_dt)
    mag = jnp.exp(dt * a_re)
    abar_re = mag * jnp.cos(dt * a_im)
    abar_im = mag * jnp.sin(dt * a_im)
    den = a_re * a_re + a_im * a_im
    q_re = ((abar_re - 1.0) * a_re + abar_im * a_im) / den
    q_im = (abar_im * a_re - (abar_re - 1.0) * a_im) / den
    return abar_re, abar_im, q_re, q_im


def _powers(abar_re, abar_im):
    pows = [(jnp.ones_like(abar_re), jnp.zeros_like(abar_im))]
    for _ in range(S5_CHUNK):
        pows.append(_cmul(pows[-1][0], pows[-1][1], abar_re, abar_im))
    return pows


def _block_diag_tile(stacked, group_rows, group_cols, n_groups):
    rows = n_groups * group_rows
    cols = n_groups * group_cols
    row_shift = group_rows.bit_length() - 1
    col_shift = group_cols.bit_length() - 1
    assert group_rows == 1 << row_shift and group_cols == 1 << col_shift
    src = lax.broadcasted_iota(jnp.int32, (group_cols, cols), 0)
    dst = lax.broadcasted_iota(jnp.int32, (group_cols, cols), 1)
    replicate = jnp.where((dst & (group_cols - 1)) == src, 1.0, 0.0).astype(BF16)
    tiled = _dot(stacked.astype(BF16), replicate)
    row_group = lax.broadcasted_iota(jnp.int32, (rows, cols), 0) >> row_shift
    col_group = lax.broadcasted_iota(jnp.int32, (rows, cols), 1) >> col_shift
    return jnp.where(row_group == col_group, tiled, 0.0).astype(BF16)


def _s5_prep_body(are_ref, aim_ref, ldt_ref, arec_ref, aimc_ref, ldtc_ref, bre_ref, bim_ref, cre_ref, cim_ref,
                  cret_ref, cimt_ref, wa_ref, wx_ref,
                  apow_re_ref, apow_im_ref, wic_ref, woc_ref, wg_ref):
    g8 = S5_GROUPS // S5_BLOCKS
    abar_re, abar_im, q_re, q_im = _discretise(are_ref[...], aim_ref[...], ldt_ref[...])
    pows = _powers(abar_re, abar_im)
    apow_re_ref[...] = pows[S5_CHUNK][0]
    apow_im_ref[...] = pows[S5_CHUNK][1]
    bbar_re, bbar_im = _cmul(q_re, q_im, bre_ref[...], bim_ref[...])
    c_re = cre_ref[...]
    c_im = cim_ref[...]
    abar_re_c, abar_im_c, _, _ = _discretise(arec_ref[...], aimc_ref[...], ldtc_ref[...])
    pows_c = _powers(abar_re_c, abar_im_c)
    ct_re = cret_ref[...]
    ct_im = cimt_ref[...]

    lane = lax.broadcasted_iota(jnp.int32, (S5_GROUPS, S5_GROUP, S5_GROUP), 2)
    within = 2 * S5_BLOCK_STATES
    wic_ref[:, :, within:] = jnp.zeros((S5_BLOCKS, S5_CHUNK * S5_BLOCK_CH, S5_CHUNK * S5_BLOCK_CH), BF16)
    for j in range(S5_CHUNK):
        bt = _cmul(*pows[S5_CHUNK - 1 - j], bbar_re, bbar_im)
        et = _cmul(*pows_c[j + 1], ct_re, ct_im)
        ce_re, ce_im = _cmul(*pows[j], c_re, c_im)
        lag = jnp.zeros(lane.shape, F32)
        for ho in range(S5_GROUP):
            col = jnp.sum(bbar_re * ce_re[:, ho:ho + 1, :] - bbar_im * ce_im[:, ho:ho + 1, :],
                          axis=-1, keepdims=True)
            lag = jnp.where(lane == ho, col, lag)
        rows_j = slice(j * S5_BLOCK_CH, (j + 1) * S5_BLOCK_CH)
        for k in range(S5_BLOCKS):
            grp = slice(k * g8, (k + 1) * g8)
            for part in range(2):
                cols = slice(part * S5_BLOCK_STATES, (part + 1) * S5_BLOCK_STATES)
                wic_ref[k, rows_j, cols] = _block_diag_tile(
                    bt[part][grp].reshape(S5_BLOCK_CH, S5_STATE), S5_GROUP, S5_STATE, g8)
                tile = _block_diag_tile(
                    et[part][grp].reshape(S5_BLOCK_STATES, S5_GROUP), S5_STATE, S5_GROUP, g8)
                woc_ref[k, cols, rows_j] = tile if part == 0 else -tile
            tile = _block_diag_tile(lag[grp].reshape(S5_BLOCK_CH, S5_GROUP), S5_GROUP, S5_GROUP, g8)
            for t_in in range(S5_CHUNK - j):
                t_out = t_in + j
                wic_ref[k, t_in * S5_BLOCK_CH:(t_in + 1) * S5_BLOCK_CH,
                        within + t_out * S5_BLOCK_CH:within + (t_out + 1) * S5_BLOCK_CH] = tile

    heads = LRU_HEADS // LRU_BLOCKS
    for kb in range(LRU_BLOCKS):
        for part, w_ref in enumerate((wa_ref, wx_ref)):
            stacked = w_ref[kb * heads:(kb + 1) * heads].reshape(LRU_BLOCK_CH, LRU_HEAD_DIM)
            wg_ref[kb, :, part * LRU_BLOCK_CH:(part + 1) * LRU_BLOCK_CH] = _block_diag_tile(
                stacked, LRU_HEAD_DIM, LRU_HEAD_DIM, heads)


def _s5_prep(a_re, a_im, log_dt, b_re, b_im, c_re, c_im, w_a, w_x):
    g, p, h = b_re.shape
    f = jax.ShapeDtypeStruct
    chunk_cols = S5_CHUNK * S5_BLOCK_CH
    return pl.pallas_call(
        _s5_prep_body,
        out_shape=(f((g, 1, p), F32), f((g, 1, p), F32),
                   f((S5_BLOCKS, chunk_cols, 2 * S5_BLOCK_STATES + chunk_cols), BF16),
                   f((S5_BLOCKS, 2 * S5_BLOCK_STATES, chunk_cols), BF16),
                   f((LRU_BLOCKS, LRU_BLOCK_CH, 2 * LRU_BLOCK_CH), BF16)),
        compiler_params=pltpu.CompilerParams(vmem_limit_bytes=V7X_SCOPED_VMEM_BYTES),
        name="s5_prep",
    )(a_re.reshape(g, 1, p), a_im.reshape(g, 1, p), log_dt.reshape(g, 1, 1),
      a_re.reshape(g, p, 1), a_im.reshape(g, p, 1), log_dt.reshape(g, 1, 1),
      jnp.swapaxes(b_re, 1, 2), jnp.swapaxes(b_im, 1, 2), c_re, c_im,
      jnp.swapaxes(c_re, 1, 2), jnp.swapaxes(c_im, 1, 2), w_a, w_x)


def _mixer_norm(x_ref, lng_ref, lnb_ref, h0b_ref):
    for r in _row_chunks(x_ref.shape[0]):
        h0b_ref[r, :] = _layer_norm(x_ref[r, :], lng_ref[...], lnb_ref[...]).astype(BF16)


def _mixer_project_in(h0b_ref, win_ref, z_ref, uc_ref, batch):
    for piece in range(D_IN // D_S5):
        cols = slice(piece * D_S5, (piece + 1) * D_S5)
        z = _dot(h0b_ref[...], win_ref[:, cols])
        z_ref[:, cols] = z
        if piece == 0:
            zb = z.astype(BF16)
            for step in range(z.shape[0] // batch):
                c, t = divmod(step, S5_CHUNK)
                for k in range(S5_BLOCKS):
                    uc_ref[k, c * batch:(c + 1) * batch, t * S5_BLOCK_CH:(t + 1) * S5_BLOCK_CH] = (
                        zb[step * batch:(step + 1) * batch, k * S5_BLOCK_CH:(k + 1) * S5_BLOCK_CH])


def _mixer_conv(z_ref, cw_ref, cb_ref, d_ref, xl_ref, xc_ref, xcb_ref, gz_ref, du_ref, batch):
    rows = z_ref.shape[0]
    halo = (CONV_WIDTH - 1) * batch
    xl_ref[halo:halo + rows, :] = z_ref[:, D_S5:D_S5 + D_LRU]
    for r in _row_chunks(rows, 2 * VEC_ROWS):
        xc = cb_ref[...]
        for j in range(CONV_WIDTH):
            xc = xc + xl_ref[j * batch + r.start:j * batch + r.stop, :] * cw_ref[j:j + 1, :]
        xc_ref[r, :] = xc
        xcb_ref[r, :] = xc.astype(BF16)
        gz_ref[r, :] = _gelu_tanh(z_ref[r, D_S5 + D_LRU:D_IN])
        du_ref[r, :] = d_ref[...] * z_ref[r, 0:D_S5]
    xl_ref[0:halo, :] = xl_ref[rows:rows + halo, :]


def _mixer_step(xnext_ref, lng_ref, lnb_ref, win_ref, wic_ref, apr_ref, api_ref, woc_ref, d_ref,
                wglu_ref, bglu_ref, gs5_ref, cw_ref, cb_ref, wg_ref, ba_ref, bx_ref, lam_ref, glru_ref, wmix_ref,
                mix_ref,
                z_ref, uc_ref, h0b_ref, inc_ref, hp_ref, st_ref, xl_ref, xc_ref, xcb_ref, pre_ref, a_ref, hx_ref,
                hl_ref, yo_ref, y32_ref, yb_ref, gate_ref, gz_ref, du_ref, y_ref):
    rows = y_ref.shape[0]
    batch = st_ref.shape[1]
    steps = rows // batch
    chunks = steps // S5_CHUNK
    wide = _row_chunks(rows, 2 * VEC_ROWS)
    sre = slice(0, S5_BLOCK_STATES)
    sim = slice(S5_BLOCK_STATES, 2 * S5_BLOCK_STATES)

    for k in range(S5_BLOCKS):
        inc_ref[k] = _dot(uc_ref[k], wic_ref[k])
    for kb in range(LRU_BLOCKS):
        lo, hi_ = kb * LRU_BLOCK_CH, (kb + 1) * LRU_BLOCK_CH
        pre = _dot(xcb_ref[:, lo:hi_], wg_ref[kb])
        pre_ref[:, lo:hi_] = pre[:, 0:LRU_BLOCK_CH]
        pre_ref[:, D_LRU + lo:D_LRU + hi_] = pre[:, LRU_BLOCK_CH:2 * LRU_BLOCK_CH]
    _mixer_project_in(h0b_ref, win_ref, z_ref, uc_ref, batch)

    for k in range(S5_BLOCKS):
        ar = jnp.broadcast_to(apr_ref[k], (batch, S5_BLOCK_STATES))
        ai = jnp.broadcast_to(api_ref[k], (batch, S5_BLOCK_STATES))
        hr = st_ref[k, :, sre]
        hi = st_ref[k, :, sim]
        for c in range(chunks):
            r = slice(c * batch, (c + 1) * batch)
            hp_ref[k, r, sre] = hr.astype(BF16)
            hp_ref[k, r, sim] = hi.astype(BF16)
            hr, hi = (ar * hr - ai * hi + inc_ref[k, r, sre],
                      ar * hi + ai * hr + inc_ref[k, r, sim])
        st_ref[k, :, sre] = hr
        st_ref[k, :, sim] = hi
        yo_ref[k] = _dot(hp_ref[k], woc_ref[k])
        if k == 1:
            lam = lam_ref[...]
            softplus_neg_lam = jnp.maximum(-lam, 0.0) + jnp.log1p(jnp.exp(-jnp.abs(lam)))
            for r in wide:
                gate_a = jax.nn.sigmoid(pre_ref[r, 0:D_LRU] + ba_ref[...])
                gate_x = jax.nn.sigmoid(pre_ref[r, D_LRU:2 * D_LRU] + bx_ref[...])
                log_a = -LRU_C * gate_a * softplus_neg_lam
                a = jnp.exp(log_a)
                m2 = -jnp.tanh(log_a) * (a * a + 1.0)
                mult = jnp.where(m2 > 0.0, m2 * lax.rsqrt(m2), 0.0)
                a_ref[r, :] = a
                hx_ref[r, :] = mult * (gate_x * xc_ref[r, :])
    hl = hl_ref[...]
    for t in range(steps):
        r = slice(t * batch, (t + 1) * batch)
        hl = a_ref[r, :] * hl + hx_ref[r, :]
        hx_ref[r, :] = hl
    hl_ref[...] = hl

    within = 2 * S5_BLOCK_STATES
    for step in range(steps):
        c, t = divmod(step, S5_CHUNK)
        r = slice(step * batch, (step + 1) * batch)
        rc = slice(c * batch, (c + 1) * batch)
        cols = slice(t * S5_BLOCK_CH, (t + 1) * S5_BLOCK_CH)
        ys = jnp.concatenate(
            [yo_ref[k, rc, cols] + inc_ref[k, rc, within + cols.start:within + cols.stop]
             for k in range(S5_BLOCKS)], axis=1)
        y = _gelu_tanh(ys + du_ref[r, :])
        y32_ref[r, :] = y
        yb_ref[r, :] = y.astype(BF16)
    gate_ref[...] = _dot(yb_ref[...], wglu_ref[...])
    for r in wide:
        y = y32_ref[r, :] * jax.nn.sigmoid(gate_ref[r, :] + bglu_ref[...])
        y_ref[r, 0:D_S5] = _rms_norm(y, gs5_ref[...]).astype(y_ref.dtype)
        ylru = hx_ref[r, :] * gz_ref[r, :]
        y_ref[r, D_S5:D_MODEL] = _rms_norm(ylru, glru_ref[...]).astype(y_ref.dtype)
    mix_ref[...] = _dot(y_ref[...], wmix_ref[...]).astype(mix_ref.dtype)

    _mixer_conv(z_ref, cw_ref, cb_ref, d_ref, xl_ref, xc_ref, xcb_ref, gz_ref, du_ref, batch)
    _mixer_norm(xnext_ref, lng_ref, lnb_ref, h0b_ref)


def _mixer_body(xfirst_ref, xsecond_ref, xnext_ref, lng_ref, lnb_ref, win_ref, *rest):
    back_consts = rest[:16]
    d_ref, cw_ref, cb_ref = back_consts[4], back_consts[8], back_consts[9]
    mix_ref = rest[16]
    scratch = rest[17:]
    z_ref, uc_ref, h0b_ref, _, _, st_ref, xl_ref, xc_ref, xcb_ref = scratch[:9]
    hl_ref, gz_ref, du_ref = scratch[12], scratch[17], scratch[18]
    batch = st_ref.shape[1]
    halo = (CONV_WIDTH - 1) * batch

    @pl.when(pl.program_id(0) == 0)
    def _():
        st_ref[...] = jnp.zeros_like(st_ref)
        hl_ref[...] = jnp.zeros_like(hl_ref)
        xl_ref[0:halo, :] = jnp.zeros((halo, D_LRU), F32)
        _mixer_norm(xfirst_ref, lng_ref, lnb_ref, h0b_ref)
        _mixer_project_in(h0b_ref, win_ref, z_ref, uc_ref, batch)
        _mixer_conv(z_ref, cw_ref, cb_ref, d_ref, xl_ref, xc_ref, xcb_ref, gz_ref, du_ref, batch)
        _mixer_norm(xsecond_ref, lng_ref, lnb_ref, h0b_ref)

    _mixer_step(xnext_ref, lng_ref, lnb_ref, win_ref, *back_consts, mix_ref, *scratch)


def _mixer(x_tm, batch, ln_g, ln_b, w_in, w_in_chunk, apow_re, apow_im, w_out_chunk, d, w_glu, b_glu, g_s5,
           conv_w, conv_b, wg, b_a, b_x, lam, g_lru, w_mix):
    n_rows = x_tm.shape[0]
    rows = MIXER_STEPS * batch
    n_blocks = n_rows // rows
    halo = (CONV_WIDTH - 1) * batch
    chunk_rows = rows // S5_CHUNK
    chunk_cols = S5_CHUNK * S5_BLOCK_CH
    consts = (ln_g, ln_b, w_in, w_in_chunk, apow_re, apow_im, w_out_chunk, d, w_glu, b_glu, g_s5,
              conv_w, conv_b, wg, b_a, b_x, lam, g_lru, w_mix)
    return pl.pallas_call(
        _mixer_body,
        grid=(n_blocks,),
        in_specs=[_resident_spec((rows, D_MODEL), (0, 0)),
                  _resident_spec((rows, D_MODEL), (min(1, n_blocks - 1), 0)),
                  pl.BlockSpec((rows, D_MODEL), lambda i: (jnp.minimum(i + 2, n_blocks - 1), 0))]
                 + [_resident_spec(c.shape) for c in consts],
        out_specs=pl.BlockSpec((rows, D_MODEL), lambda i: (i, 0)),
        out_shape=jax.ShapeDtypeStruct((n_rows, D_MODEL), BF16),
        scratch_shapes=[
            pltpu.VMEM((rows, D_IN), F32),
            pltpu.VMEM((S5_BLOCKS, chunk_rows, chunk_cols), BF16),
            pltpu.VMEM((rows, D_MODEL), BF16),
            pltpu.VMEM((S5_BLOCKS, chunk_rows, 2 * S5_BLOCK_STATES + chunk_cols), F32),
            pltpu.VMEM((S5_BLOCKS, chunk_rows, 2 * S5_BLOCK_STATES), BF16),
            pltpu.VMEM((S5_BLOCKS, batch, 2 * S5_BLOCK_STATES), F32),
            pltpu.VMEM((rows + halo, D_LRU), F32),
            pltpu.VMEM((rows, D_LRU), F32),
            pltpu.VMEM((rows, D_LRU), BF16),
            pltpu.VMEM((rows, 2 * D_LRU), F32),
            pltpu.VMEM((rows, D_LRU), F32),
            pltpu.VMEM((rows, D_LRU), F32),
            pltpu.VMEM((batch, D_LRU), F32),
            pltpu.VMEM((S5_BLOCKS, chunk_rows, chunk_cols), F32),
            pltpu.VMEM((rows, D_S5), F32),
            pltpu.VMEM((rows, D_S5), BF16),
            pltpu.VMEM((rows, D_S5), F32),
            pltpu.VMEM((rows, D_LRU), F32),
            pltpu.VMEM((rows, D_S5), F32),
            pltpu.VMEM((rows, D_MODEL), BF16),
        ],
        compiler_params=pltpu.CompilerParams(
            dimension_semantics=("arbitrary",), vmem_limit_bytes=V7X_SCOPED_VMEM_BYTES),
        name="mixer",
    )(x_tm, x_tm, x_tm, *consts)


def _kv_body(mem_ref, g_ref, b_ref, wk_ref, wv_ref, k_ref, v_ref):
    mn = _layer_norm(mem_ref[...], g_ref[...], b_ref[...]).astype(BF16)
    k_ref[...] = (_dot(mn, wk_ref[...]) * (CA_HEAD_DIM ** -0.5)).astype(k_ref.dtype)
    v_ref[...] = _dot(mn, wv_ref[...]).astype(v_ref.dtype)


def _kv(mem, g, b, w_k, w_v):
    bsz, mlen, _ = mem.shape
    n_rows = bsz * mlen
    blk = pl.BlockSpec((KV_ROWS, D_MODEL), lambda i: (i, 0))
    out = jax.ShapeDtypeStruct((n_rows, D_MODEL), BF16)
    k, v = pl.pallas_call(
        _kv_body,
        grid=(n_rows // KV_ROWS,),
        in_specs=[blk, _const_spec(g.shape), _const_spec(b.shape), _const_spec(w_k.shape), _const_spec(w_v.shape)],
        out_specs=(blk, blk),
        out_shape=(out, out),
        compiler_params=pltpu.CompilerParams(
            dimension_semantics=("arbitrary",), vmem_limit_bytes=V7X_SCOPED_VMEM_BYTES),
        name="kv",
    )(mem.reshape(n_rows, D_MODEL), g, b, w_k, w_v)
    return k.reshape(mem.shape), v.reshape(mem.shape)


def _attn_norms(x_ref, mix_ref, lng_ref, lnb_ref, g1_ref, b1_ref, h1_ref, h1b_ref, slot):
    for r in _row_chunks(x_ref.shape[0]):
        h0 = _layer_norm(x_ref[r, :], lng_ref[...], lnb_ref[...])
        h1 = _layer_norm(ALPHA * h0 + mix_ref[r, :], g1_ref[...], b1_ref[...])
        h1_ref[slot, r, :] = h1
        h1b_ref[r, :] = h1.astype(BF16)


def _attn_body(x0_ref, m0_ref, x1_ref, m1_ref, xnext_ref, mnext_ref, k_ref, v_ref, wq_ref, wo_ref,
               lng_ref, lnb_ref, g1_ref, b1_ref, g2_ref, b2_ref, o_ref,
               h1_ref, q_ref, h1b_ref, s_ref, p_ref, att_ref, ca_ref):
    i = pl.program_id(0)
    slot = i % 2
    rows = o_ref.shape[0]
    norm_refs = (lng_ref, lnb_ref, g1_ref, b1_ref, h1_ref, h1b_ref)

    @pl.when(i == 0)
    def _():
        _attn_norms(x0_ref, m0_ref, *norm_refs, 0)
        q_ref[...] = _dot(h1b_ref[...], wq_ref[...]).astype(BF16)
        _attn_norms(x1_ref, m1_ref, *norm_refs, 1)

    head_slices = [slice(hd * CA_HEAD_DIM, (hd + 1) * CA_HEAD_DIM) for hd in range(CA_HEADS)]
    for hd, hs in enumerate(head_slices):
        s_ref[hd] = lax.dot_general(q_ref[:, hs], k_ref[:, hs], (((1,), (1,)), ((), ())),
                                    preferred_element_type=F32)
    q_ref[...] = _dot(h1b_ref[...], wq_ref[...]).astype(BF16)
    for r in _row_chunks(rows):
        for hd in range(CA_HEADS):
            s = s_ref[hd, r, :]
            e = jnp.exp(s - jnp.max(s, axis=-1, keepdims=True))
            p_ref[hd, r, :] = (e * (1.0 / jnp.sum(e, axis=-1, keepdims=True))).astype(BF16)
    for hd, hs in enumerate(head_slices):
        att_ref[:, hs] = _dot(p_ref[hd], v_ref[:, hs]).astype(BF16)
    ca_ref[...] = _dot(att_ref[...], wo_ref[...])
    for r in _row_chunks(rows):
        o_ref[r, :] = _layer_norm(ALPHA * h1_ref[slot, r, :] + ca_ref[r, :], g2_ref[...], b2_ref[...])
    _attn_norms(xnext_ref, mnext_ref, *norm_refs, slot)


def _attn(x, mix, k, v, w_q, w_o, ln_g, ln_b, g1, b1, g2, b2):
    bsz, seq, _ = x.shape
    mlen = k.shape[1]
    n_rows = bsz * seq
    n_blocks = n_rows // ATTN_ROWS
    blocks_per_batch = seq // ATTN_ROWS
    blk = (ATTN_ROWS, D_MODEL)
    first_blk = _resident_spec(blk, (0, 0))
    second_blk = _resident_spec(blk, (min(1, n_blocks - 1), 0))
    next_blk = pl.BlockSpec(blk, lambda i: (jnp.minimum(i + 2, n_blocks - 1), 0))
    mem_blk = pl.BlockSpec((None, mlen, D_MODEL), lambda i: (i // blocks_per_batch, 0, 0))
    consts = (w_q, w_o, ln_g, ln_b, g1, b1, g2, b2)
    x2 = x.reshape(n_rows, D_MODEL)
    y2 = mix.reshape(n_rows, D_MODEL)
    return pl.pallas_call(
        _attn_body,
        grid=(n_blocks,),
        in_specs=[first_blk, first_blk, second_blk, second_blk, next_blk, next_blk, mem_blk, mem_blk]
                 + [_resident_spec(c.shape) for c in consts],
        out_specs=pl.BlockSpec(blk, lambda i: (i, 0)),
        out_shape=jax.ShapeDtypeStruct((n_rows, D_MODEL), F32),
        scratch_shapes=[pltpu.VMEM((2, ATTN_ROWS, D_MODEL), F32),
                        pltpu.VMEM((ATTN_ROWS, D_MODEL), BF16),
                        pltpu.VMEM((ATTN_ROWS, D_MODEL), BF16),
                        pltpu.VMEM((CA_HEADS, ATTN_ROWS, mlen), F32),
                        pltpu.VMEM((CA_HEADS, ATTN_ROWS, mlen), BF16),
                        pltpu.VMEM((ATTN_ROWS, D_MODEL), BF16),
                        pltpu.VMEM((ATTN_ROWS, D_MODEL), F32)],
        compiler_params=pltpu.CompilerParams(
            dimension_semantics=("arbitrary",), vmem_limit_bytes=V7X_SCOPED_VMEM_BYTES),
        name="attn",
    )(x2, y2, x2, y2, x2, y2, k, v, *consts)


def _mlp_body(h_ref, w1_ref, w2_ref, g_ref, b_ref, o_ref):
    for r in _row_chunks(h_ref.shape[0], MLP_GROUP_ROWS):
        h = h_ref[r, :]
        hb = h.astype(BF16)
        ff = jnp.zeros(h.shape, F32)
        for j in range(D_FF // MLP_CHUNK):
            lo, hi = j * MLP_CHUNK, (j + 1) * MLP_CHUNK
            t = jnp.maximum(_dot(hb, w1_ref[:, lo:hi]), 0.0)
            ff = ff + _dot((t * t).astype(BF16), w2_ref[lo:hi, :])
        o_ref[r, :] = _layer_norm(ALPHA * h + ff, g_ref[...], b_ref[...])


def _mlp(h, w1, w2, g, b):
    n_rows = h.shape[0]
    row_blk = pl.BlockSpec((MLP_ROWS, D_MODEL), lambda i: (i, 0))
    return pl.pallas_call(
        _mlp_body,
        grid=(n_rows // MLP_ROWS,),
        in_specs=[row_blk, _resident_spec(w1.shape), _resident_spec(w2.shape),
                  _const_spec(g.shape), _const_spec(b.shape)],
        out_specs=row_blk,
        out_shape=jax.ShapeDtypeStruct(h.shape, F32),
        compiler_params=pltpu.CompilerParams(
            dimension_semantics=("arbitrary",), vmem_limit_bytes=V7X_SCOPED_VMEM_BYTES),
        name="mlp",
    )(h, w1, w2, g, b)


def kernel(x, mem, ln_in_g, ln_in_b, w_in, s5_a_re, s5_a_im, s5_log_dt, s5_b_re, s5_b_im, s5_c_re, s5_c_im, s5_d, s5_w_glu, s5_b_glu, conv_w, conv_b, lru_w_a, lru_b_a, lru_w_x, lru_b_x, lru_lambda, g_s5, g_lru, w_mix_out, ln1_g, ln1_b, mem_ln_g, mem_ln_b, w_q, w_k, w_v, w_o, ln2_g, ln2_b, w_ff1, w_ff2, ln3_g, ln3_b):
    bsz, seq, d_model = x.shape
    assert d_model == D_MODEL and w_in.shape == (DEPTH, D_MODEL, D_IN)
    assert seq % MIXER_STEPS == 0 and seq % ATTN_ROWS == 0 and (bsz * seq) % MLP_ROWS == 0
    assert bsz % 16 == 0, "time-major rows of one step must fill whole bf16 tiles"
    row = lambda v: v.reshape(1, -1).astype(F32)
    l = 0

    apow_re, apow_im, w_in_chunk, w_out_chunk, wg = _s5_prep(
        s5_a_re[l], s5_a_im[l], s5_log_dt[l], s5_b_re[l], s5_b_im[l], s5_c_re[l], s5_c_im[l],
        lru_w_a[l], lru_w_x[l])
    apow_re = apow_re.reshape(S5_BLOCKS, 1, S5_BLOCK_STATES)
    apow_im = apow_im.reshape(S5_BLOCKS, 1, S5_BLOCK_STATES)

    x_tm = jnp.swapaxes(x, 0, 1).reshape(seq * bsz, D_MODEL)
    mix_tm = _mixer(x_tm, bsz, row(ln_in_g), row(ln_in_b), w_in[l].astype(BF16), w_in_chunk, apow_re, apow_im,
                    w_out_chunk,
                    row(s5_d[l]), s5_w_glu[l].astype(BF16), row(s5_b_glu[l]), row(g_s5[l]),
                    conv_w[l].astype(F32), row(conv_b[l]), wg, row(lru_b_a[l]), row(lru_b_x[l]),
                    row(lru_lambda[l]), row(g_lru[l]), w_mix_out[l].astype(BF16))
    mix_tm, mem_l, w_k_l, w_v_l = lax.optimization_barrier((mix_tm, mem, w_k[l], w_v[l]))
    mix = jnp.swapaxes(mix_tm.reshape(seq, bsz, D_MODEL), 0, 1)

    k, v = _kv(mem_l, row(mem_ln_g[l]), row(mem_ln_b[l]), w_k_l.astype(BF16), w_v_l.astype(BF16))
    h2 = _attn(x, mix, k, v, w_q[l].astype(BF16), w_o[l].astype(BF16),
               row(ln_in_g), row(ln_in_b), row(ln1_g[l]), row(ln1_b[l]), row(ln2_g[l]), row(ln2_b[l]))
    out = _mlp(h2, w_ff1[l].astype(BF16), w_ff2[l].astype(BF16), row(ln3_g[l]), row(ln3_b[l]))
    return out.reshape(bsz, seq, D_MODEL)
```

```python
import jax
import jax.numpy as jnp
from jax import lax
from jax.experimental import pallas as pl
from jax.experimental.pallas import tpu as pltpu

F32 = jnp.float32
BF16 = jnp.bfloat16

D_MODEL = 1024
D_S5 = 512
D_LRU = 512
D_IN = D_S5 + 2 * D_LRU
S5_GROUP = 16
S5_GROUPS = 32
S5_STATE = 64
LRU_HEADS = 8
LRU_HEAD_DIM = 64
CONV_WIDTH = 4
LRU_C = 8.0
D_FF = 4096
CA_HEADS = 4
CA_HEAD_DIM = 256
DEPTH = 1
ALPHA = (2 * DEPTH) ** 0.25
LN_EPS = 1e-5
RMS_EPS = 1e-6

S5_BLOCKS = 4
S5_BLOCK_CH = D_S5 // S5_BLOCKS
S5_BLOCK_STATES = 8 * S5_STATE
S5_CHUNK = 4
LRU_BLOCKS = 2
LRU_BLOCK_CH = D_LRU // LRU_BLOCKS

V7X_VMEM_BYTES = 64 * 1024 * 1024
V7X_SCOPED_VMEM_BYTES = V7X_VMEM_BYTES - 8 * 1024 * 1024
V7X_TAIL_VMEM_BYTES = V7X_VMEM_BYTES - 2 * 1024 * 1024

MIXER_STEPS = 32
ATTN_ROWS = 512
KV_ROWS = 1024
MLP_CHUNK = 1024
VEC_ROWS = 16


def _layer_norm(x, g, b):
    mu = jnp.mean(x, axis=-1, keepdims=True)
    xc = x - mu
    var = jnp.mean(xc * xc, axis=-1, keepdims=True)
    return xc * lax.rsqrt(var + LN_EPS) * g + b


def _rms_norm(x, g):
    ms = jnp.mean(x * x, axis=-1, keepdims=True)
    return x * lax.rsqrt(ms + RMS_EPS) * g


def _gelu_tanh(x):
    c = 0.7978845608028654
    return 0.5 * x * (1.0 + jnp.tanh(c * (x + 0.044715 * (x * x * x))))


def _dot(a, b):
    return jnp.dot(a, b, preferred_element_type=F32)


def _const_spec(shape):
    zeros = (0,) * len(shape)
    return pl.BlockSpec(shape, lambda *_: zeros)


def _resident_spec(shape, block=None):
    zeros = (0,) * len(shape)
    index = zeros if block is None else block
    return pl.BlockSpec(shape, lambda *_: index, pipeline_mode=pl.Buffered(1))


def _row_chunks(rows, chunk=None):
    chunk = chunk or VEC_ROWS
    return [slice(r, r + chunk) for r in range(0, rows, chunk)]


def _cmul(a_re, a_im, b_re, b_im):
    return a_re * b_re - a_im * b_im, a_re * b_im + a_im * b_re


def _cmul(a_re, a_im, b_re, b_im):
    return a_re * b_re - a_im * b_im, a_re * b_im + a_im * b_re


def _discretise(a_re, a_im, log_dt):
    dt = jnp.exp(log_dt)
    mag = jnp.exp(dt * a_re)
    abar_re = mag * jnp.cos(dt * a_im)
    abar_im = mag * jnp.sin(dt * a_im)
    den = a_re * a_re + a_im * a_im
    q_re = ((abar_re - 1.0) * a_re + abar_im * a_im) / den
    q_im = (abar_im * a_re - (abar_re - 1.0) * a_im) / den
    return abar_re, abar_im, q_re, q_im


def _powers(abar_re, abar_im):
    pows = [(jnp.ones_like(abar_re), jnp.zeros_like(abar_im))]
    for _ in range(S5_CHUNK):
        pows.append(_cmul(pows[-1][0], pows[-1][1], abar_re, abar_im))
    return pows


def _block_diag_tile(stacked, group_rows, group_cols, n_groups):
    rows = n_groups * group_rows
    cols = n_groups * group_cols
    row_shift = group_rows.bit_length() - 1
    col_shift = group_cols.bit_length() - 1
    assert group_rows == 1 << row_shift and group_cols == 1 << col_shift
    src = lax.broadcasted_iota(jnp.int32, (group_cols, cols), 0)
    dst = lax.broadcasted_iota(jnp.int32, (group_cols, cols), 1)
    replicate = jnp.where((dst & (group_cols - 1)) == src, 1.0, 0.0).astype(BF16)
    tiled = _dot(stacked.astype(BF16), replicate)
    row_group = lax.broadcasted_iota(jnp.int32, (rows, cols), 0) >> row_shift
    col_group = lax.broadcasted_iota(jnp.int32, (rows, cols), 1) >> col_shift
    return jnp.where(row_group == col_group, tiled, 0.0).astype(BF16)


def _s5_prep_body(are_ref, aim_ref, ldt_ref, arec_ref, aimc_ref, ldtc_ref, bre_ref, bim_ref, cre_ref, cim_ref,
                  cret_ref, cimt_ref, wa_ref, wx_ref,
                  apow_re_ref, apow_im_ref, wic_ref, woc_ref, wg_ref):
    g8 = S5_GROUPS // S5_BLOCKS
    abar_re, abar_im, q_re, q_im = _discretise(are_ref[...], aim_ref[...], ldt_ref[...])
    pows = _powers(abar_re, abar_im)
    apow_re_ref[...] = pows[S5_CHUNK][0]
    apow_im_ref[...] = pows[S5_CHUNK][1]
    bbar_re, bbar_im = _cmul(q_re, q_im, bre_ref[...], bim_ref[...])
    c_re = cre_ref[...]
    c_im = cim_ref[...]
    abar_re_c, abar_im_c, _, _ = _discretise(arec_ref[...], aimc_ref[...], ldtc_ref[...])
    pows_c = _powers(abar_re_c, abar_im_c)
    ct_re = cret_ref[...]
    ct_im = cimt_ref[...]

    lane = lax.broadcasted_iota(jnp.int32, (S5_GROUPS, S5_GROUP, S5_GROUP), 2)
    within = 2 * S5_BLOCK_STATES
    wic_ref[:, :, within:] = jnp.zeros((S5_BLOCKS, S5_CHUNK * S5_BLOCK_CH, S5_CHUNK * S5_BLOCK_CH), BF16)
    for j in range(S5_CHUNK):
        bt = _cmul(*pows[S5_CHUNK - 1 - j], bbar_re, bbar_im)
        et = _cmul(*pows_c[j + 1], ct_re, ct_im)
        ce_re, ce_im = _cmul(*pows[j], c_re, c_im)
        lag = jnp.zeros(lane.shape, F32)
        for ho in range(S5_GROUP):
            col = jnp.sum(bbar_re * ce_re[:, ho:ho + 1, :] - bbar_im * ce_im[:, ho:ho + 1, :],
                          axis=-1, keepdims=True)
            lag = jnp.where(lane == ho, col, lag)
        rows_j = slice(j * S5_BLOCK_CH, (j + 1) * S5_BLOCK_CH)
        for k in range(S5_BLOCKS):
            grp = slice(k * g8, (k + 1) * g8)
            for part in range(2):
                cols = slice(part * S5_BLOCK_STATES, (part + 1) * S5_BLOCK_STATES)
                wic_ref[k, rows_j, cols] = _block_diag_tile(
                    bt[part][grp].reshape(S5_BLOCK_CH, S5_STATE), S5_GROUP, S5_STATE, g8)
                tile = _block_diag_tile(
                    et[part][grp].reshape(S5_BLOCK_STATES, S5_GROUP), S5_STATE, S5_GROUP, g8)
                woc_ref[k, cols, rows_j] = tile if part == 0 else -tile
            tile = _block_diag_tile(lag[grp].reshape(S5_BLOCK_CH, S5_GROUP), S5_GROUP, S5_GROUP, g8)
            for t_in in range(S5_CHUNK - j):
                t_out = t_in + j
                wic_ref[k, t_in * S5_BLOCK_CH:(t_in + 1) * S5_BLOCK_CH,
                        within + t_out * S5_BLOCK_CH:within + (t_out + 1) * S5_BLOCK_CH] = tile

    heads = LRU_HEADS // LRU_BLOCKS
    for kb in range(LRU_BLOCKS):
        for part, w_ref in enumerate((wa_ref, wx_ref)):
            stacked = w_ref[kb * heads:(kb + 1) * heads].reshape(LRU_BLOCK_CH, LRU_HEAD_DIM)
            wg_ref[kb, :, part * LRU_BLOCK_CH:(part + 1) * LRU_BLOCK_CH] = _block_diag_tile(
                stacked, LRU_HEAD_DIM, LRU_HEAD_DIM, heads)


def _s5_prep(a_re, a_im, log_dt, b_re, b_im, c_re, c_im, w_a, w_x):
    g, p, h = b_re.shape
    f = jax.ShapeDtypeStruct
    chunk_cols = S5_CHUNK * S5_BLOCK_CH
    return pl.pallas_call(
        _s5_prep_body,
        out_shape=(f((g, 1, p), F32), f((g, 1, p), F32),
                   f((S5_BLOCKS, chunk_cols, 2 * S5_BLOCK_STATES + chunk_cols), BF16),
                   f((S5_BLOCKS, 2 * S5_BLOCK_STATES, chunk_cols), BF16),
                   f((LRU_BLOCKS, LRU_BLOCK_CH, 2 * LRU_BLOCK_CH), BF16)),
        compiler_params=pltpu.CompilerParams(vmem_limit_bytes=V7X_SCOPED_VMEM_BYTES),
        name="s5_prep",
    )(a_re.reshape(g, 1, p), a_im.reshape(g, 1, p), log_dt.reshape(g, 1, 1),
      a_re.reshape(g, p, 1), a_im.reshape(g, p, 1), log_dt.reshape(g, 1, 1),
      jnp.swapaxes(b_re, 1, 2), jnp.swapaxes(b_im, 1, 2), c_re, c_im,
      jnp.swapaxes(c_re, 1, 2), jnp.swapaxes(c_im, 1, 2), w_a, w_x)


def _mixer_norm(x_ref, lng_ref, lnb_ref, h0b_ref):
    for r in _row_chunks(x_ref.shape[0]):
        h0b_ref[r, :] = _layer_norm(x_ref[r, :], lng_ref[...], lnb_ref[...]).astype(BF16)


def _mixer_project_in(h0b_ref, win_ref, z_ref, uc_ref, batch):
    for piece in range(D_IN // D_S5):
        cols = slice(piece * D_S5, (piece + 1) * D_S5)
        z = _dot(h0b_ref[...], win_ref[:, cols])
        z_ref[:, cols] = z
        if piece == 0:
            zb = z.astype(BF16)
            for step in range(z.shape[0] // batch):
                c, t = divmod(step, S5_CHUNK)
                for k in range(S5_BLOCKS):
                    uc_ref[k, c * batch:(c + 1) * batch, t * S5_BLOCK_CH:(t + 1) * S5_BLOCK_CH] = (
                        zb[step * batch:(step + 1) * batch, k * S5_BLOCK_CH:(k + 1) * S5_BLOCK_CH])


def _mixer_conv(z_ref, cw_ref, cb_ref, d_ref, xl_ref, xc_ref, xcb_ref, gz_ref, du_ref, batch):
    rows = z_ref.shape[0]
    halo = (CONV_WIDTH - 1) * batch
    xl_ref[halo:halo + rows, :] = z_ref[:, D_S5:D_S5 + D_LRU]
    for r in _row_chunks(rows, 2 * VEC_ROWS):
        xc = cb_ref[...]
        for j in range(CONV_WIDTH):
            xc = xc + xl_ref[j * batch + r.start:j * batch + r.stop, :] * cw_ref[j:j + 1, :]
        xc_ref[r, :] = xc
        xcb_ref[r, :] = xc.astype(BF16)
        gz_ref[r, :] = _gelu_tanh(z_ref[r, D_S5 + D_LRU:D_IN])
        du_ref[r, :] = d_ref[...] * z_ref[r, 0:D_S5]
    xl_ref[0:halo, :] = xl_ref[rows:rows + halo, :]


def _mixer_step(xnext_ref, lng_ref, lnb_ref, win_ref, wic_ref, apr_ref, api_ref, woc_ref, d_ref,
                wglu_ref, bglu_ref, gs5_ref, cw_ref, cb_ref, wg_ref, ba_ref, bx_ref, lam_ref, glru_ref, wmix_ref,
                mix_ref,
                z_ref, uc_ref, h0b_ref, inc_ref, hp_ref, st_ref, xl_ref, xc_ref, xcb_ref, pre_ref, a_ref, hx_ref,
                hl_ref, yo_ref, y32_ref, yb_ref, gate_ref, gz_ref, du_ref, y_ref):
    rows = y_ref.shape[0]
    batch = st_ref.shape[1]
    steps = rows // batch
    chunks = steps // S5_CHUNK
    wide = _row_chunks(rows, 2 * VEC_ROWS)
    sre = slice(0, S5_BLOCK_STATES)
    sim = slice(S5_BLOCK_STATES, 2 * S5_BLOCK_STATES)

    for k in range(S5_BLOCKS):
        inc_ref[k] = _dot(uc_ref[k], wic_ref[k])
    for kb in range(LRU_BLOCKS):
        lo, hi_ = kb * LRU_BLOCK_CH, (kb + 1) * LRU_BLOCK_CH
        pre = _dot(xcb_ref[:, lo:hi_], wg_ref[kb])
        pre_ref[:, lo:hi_] = pre[:, 0:LRU_BLOCK_CH]
        pre_ref[:, D_LRU + lo:D_LRU + hi_] = pre[:, LRU_BLOCK_CH:2 * LRU_BLOCK_CH]
    _mixer_project_in(h0b_ref, win_ref, z_ref, uc_ref, batch)

    for k in range(S5_BLOCKS):
        ar = jnp.broadcast_to(apr_ref[k], (batch, S5_BLOCK_STATES))
        ai = jnp.broadcast_to(api_ref[k], (batch, S5_BLOCK_STATES))
        hr = st_ref[k, :, sre]
        hi = st_ref[k, :, sim]
        for c in range(chunks):
            r = slice(c * batch, (c + 1) * batch)
            hp_ref[k, r, sre] = hr.astype(BF16)
            hp_ref[k, r, sim] = hi.astype(BF16)
            hr, hi = (ar * hr - ai * hi + inc_ref[k, r, sre],
                      ar * hi + ai * hr + inc_ref[k, r, sim])
        st_ref[k, :, sre] = hr
        st_ref[k, :, sim] = hi
        yo_ref[k] = _dot(hp_ref[k], woc_ref[k])
        if k == 1:
            lam = lam_ref[...]
            softplus_neg_lam = jnp.maximum(-lam, 0.0) + jnp.log1p(jnp.exp(-jnp.abs(lam)))
            for r in wide:
                gate_a = jax.nn.sigmoid(pre_ref[r, 0:D_LRU] + ba_ref[...])
                gate_x = jax.nn.sigmoid(pre_ref[r, D_LRU:2 * D_LRU] + bx_ref[...])
                log_a = -LRU_C * gate_a * softplus_neg_lam
                a = jnp.exp(log_a)
                m2 = -jnp.tanh(log_a) * (a * a + 1.0)
                mult = jnp.where(m2 > 0.0, m2 * lax.rsqrt(m2), 0.0)
                a_ref[r, :] = a
                hx_ref[r, :] = mult * (gate_x * xc_ref[r, :])
    hl = hl_ref[...]
    for t in range(steps):
        r = slice(t * batch, (t + 1) * batch)
        hl = a_ref[r, :] * hl + hx_ref[r, :]
        hx_ref[r, :] = hl
    hl_ref[...] = hl

    within = 2 * S5_BLOCK_STATES
    for step in range(steps):
        c, t = divmod(step, S5_CHUNK)
        r = slice(step * batch, (step + 1) * batch)
        rc = slice(c * batch, (c + 1) * batch)
        cols = slice(t * S5_BLOCK_CH, (t + 1) * S5_BLOCK_CH)
        ys = jnp.concatenate(
            [yo_ref[k, rc, cols] + inc_ref[k, rc, within + cols.start:within + cols.stop]
             for k in range(S5_BLOCKS)], axis=1)
        y = _gelu_tanh(ys + du_ref[r, :])
        y32_ref[r, :] = y
        yb_ref[r, :] = y.astype(BF16)
    gate_ref[...] = _dot(yb_ref[...], wglu_ref[...])
    for r in wide:
        y = y32_ref[r, :] * jax.nn.sigmoid(gate_ref[r, :] + bglu_ref[...])
        y_ref[r, 0:D_S5] = _rms_norm(y, gs5_ref[...]).astype(y_ref.dtype)
        ylru = hx_ref[r, :] * gz_ref[r, :]
        y_ref[r, D_S5:D_MODEL] = _rms_norm(ylru, glru_ref[...]).astype(y_ref.dtype)
    mix_ref[...] = _dot(y_ref[...], wmix_ref[...]).astype(mix_ref.dtype)

    _mixer_conv(z_ref, cw_ref, cb_ref, d_ref, xl_ref, xc_ref, xcb_ref, gz_ref, du_ref, batch)
    _mixer_norm(xnext_ref, lng_ref, lnb_ref, h0b_ref)


def _mixer_body(xfirst_ref, xsecond_ref, xnext_ref, lng_ref, lnb_ref, win_ref, *rest):
    back_consts = rest[:16]
    d_ref, cw_ref, cb_ref = back_consts[4], back_consts[8], back_consts[9]
    mix_ref = rest[16]
    scratch = rest[17:]
    z_ref, uc_ref, h0b_ref, _, _, st_ref, xl_ref, xc_ref, xcb_ref = scratch[:9]
    hl_ref, gz_ref, du_ref = scratch[12], scratch[17], scratch[18]
    batch = st_ref.shape[1]
    halo = (CONV_WIDTH - 1) * batch

    @pl.when(pl.program_id(0) == 0)
    def _():
        st_ref[...] = jnp.zeros_like(st_ref)
        hl_ref[...] = jnp.zeros_like(hl_ref)
        xl_ref[0:halo, :] = jnp.zeros((halo, D_LRU), F32)
        _mixer_norm(xfirst_ref, lng_ref, lnb_ref, h0b_ref)
        _mixer_project_in(h0b_ref, win_ref, z_ref, uc_ref, batch)
        _mixer_conv(z_ref, cw_ref, cb_ref, d_ref, xl_ref, xc_ref, xcb_ref, gz_ref, du_ref, batch)
        _mixer_norm(xsecond_ref, lng_ref, lnb_ref, h0b_ref)

    _mixer_step(xnext_ref, lng_ref, lnb_ref, win_ref, *back_consts, mix_ref, *scratch)


def _mixer(x_tm, batch, ln_g, ln_b, w_in, w_in_chunk, apow_re, apow_im, w_out_chunk, d, w_glu, b_glu, g_s5,
           conv_w, conv_b, wg, b_a, b_x, lam, g_lru, w_mix):
    n_rows = x_tm.shape[0]
    rows = MIXER_STEPS * batch
    n_blocks = n_rows // rows
    halo = (CONV_WIDTH - 1) * batch
    chunk_rows = rows // S5_CHUNK
    chunk_cols = S5_CHUNK * S5_BLOCK_CH
    consts = (ln_g, ln_b, w_in, w_in_chunk, apow_re, apow_im, w_out_chunk, d, w_glu, b_glu, g_s5,
              conv_w, conv_b, wg, b_a, b_x, lam, g_lru, w_mix)
    return pl.pallas_call(
        _mixer_body,
        grid=(n_blocks,),
        in_specs=[_resident_spec((rows, D_MODEL), (0, 0)),
                  _resident_spec((rows, D_MODEL), (min(1, n_blocks - 1), 0)),
                  pl.BlockSpec((rows, D_MODEL), lambda i: (jnp.minimum(i + 2, n_blocks - 1), 0))]
                 + [_resident_spec(c.shape) for c in consts],
        out_specs=pl.BlockSpec((rows, D_MODEL), lambda i: (i, 0)),
        out_shape=jax.ShapeDtypeStruct((n_rows, D_MODEL), BF16),
        scratch_shapes=[
            pltpu.VMEM((rows, D_IN), F32),
            pltpu.VMEM((S5_BLOCKS, chunk_rows, chunk_cols), BF16),
            pltpu.VMEM((rows, D_MODEL), BF16),
            pltpu.VMEM((S5_BLOCKS, chunk_rows, 2 * S5_BLOCK_STATES + chunk_cols), F32),
            pltpu.VMEM((S5_BLOCKS, chunk_rows, 2 * S5_BLOCK_STATES), BF16),
            pltpu.VMEM((S5_BLOCKS, batch, 2 * S5_BLOCK_STATES), F32),
            pltpu.VMEM((rows + halo, D_LRU), F32),
            pltpu.VMEM((rows, D_LRU), F32),
            pltpu.VMEM((rows, D_LRU), BF16),
            pltpu.VMEM((rows, 2 * D_LRU), F32),
            pltpu.VMEM((rows, D_LRU), F32),
            pltpu.VMEM((rows, D_LRU), F32),
            pltpu.VMEM((batch, D_LRU), F32),
            pltpu.VMEM((S5_BLOCKS, chunk_rows, chunk_cols), F32),
            pltpu.VMEM((rows, D_S5), F32),
            pltpu.VMEM((rows, D_S5), BF16),
            pltpu.VMEM((rows, D_S5), F32),
            pltpu.VMEM((rows, D_LRU), F32),
            pltpu.VMEM((rows, D_S5), F32),
            pltpu.VMEM((rows, D_MODEL), BF16),
        ],
        compiler_params=pltpu.CompilerParams(
            dimension_semantics=("arbitrary",), vmem_limit_bytes=V7X_SCOPED_VMEM_BYTES),
        name="mixer",
    )(x_tm, x_tm, x_tm, *consts)


def _kv_body(mem_ref, g_ref, b_ref, wk_ref, wv_ref, k_ref, v_ref):
    mn = _layer_norm(mem_ref[...], g_ref[...], b_ref[...]).astype(BF16)
    k_ref[...] = (_dot(mn, wk_ref[...]) * (CA_HEAD_DIM ** -0.5)).astype(k_ref.dtype)
    v_ref[...] = _dot(mn, wv_ref[...]).astype(v_ref.dtype)


def _kv(mem, g, b, w_k, w_v):
    bsz, mlen, _ = mem.shape
    n_rows = bsz * mlen
    blk = pl.BlockSpec((KV_ROWS, D_MODEL), lambda i: (i, 0))
    out = jax.ShapeDtypeStruct((n_rows, D_MODEL), BF16)
    k, v = pl.pallas_call(
        _kv_body,
        grid=(n_rows // KV_ROWS,),
        in_specs=[blk, _const_spec(g.shape), _const_spec(b.shape), _const_spec(w_k.shape), _const_spec(w_v.shape)],
        out_specs=(blk, blk),
        out_shape=(out, out),
        compiler_params=pltpu.CompilerParams(
            dimension_semantics=("arbitrary",), vmem_limit_bytes=V7X_SCOPED_VMEM_BYTES),
        name="kv",
    )(mem.reshape(n_rows, D_MODEL), g, b, w_k, w_v)
    return k.reshape(mem.shape), v.reshape(mem.shape)


def _attn_norms(x_ref, mix_ref, lng_ref, lnb_ref, g1_ref, b1_ref, h1_ref, h1b_ref, slot):
    for r in _row_chunks(x_ref.shape[0]):
        h0 = _layer_norm(x_ref[r, :], lng_ref[...], lnb_ref[...])
        h1 = _layer_norm(ALPHA * h0 + mix_ref[r, :], g1_ref[...], b1_ref[...])
        h1_ref[slot, r, :] = h1
        h1b_ref[r, :] = h1.astype(BF16)


def _tail_body(x0_ref, m0_ref, xnext_ref, mnext_ref, k_ref, v_ref, wq_ref, wo_ref, w1_ref, w2_ref,
               lng_ref, lnb_ref, g1_ref, b1_ref, g2_ref, b2_ref, g3_ref, b3_ref, o_ref,
               h1_ref, q_ref, h1b_ref, s_ref, p_ref, att_ref, ca_ref, h2_ref, h2b_ref, ff_ref):
    i = pl.program_id(0)
    slot = i % 2
    rows = o_ref.shape[0]
    norm_refs = (lng_ref, lnb_ref, g1_ref, b1_ref, h1_ref, h1b_ref)

    @pl.when(i == 0)
    def _():
        _attn_norms(x0_ref, m0_ref, *norm_refs, 0)
        q_ref[...] = _dot(h1b_ref[...], wq_ref[...]).astype(BF16)
        h2_ref[1] = jnp.zeros(h2_ref.shape[1:], F32)
        h2b_ref[1] = jnp.zeros(h2b_ref.shape[1:], BF16)

    head_slices = [slice(hd * CA_HEAD_DIM, (hd + 1) * CA_HEAD_DIM) for hd in range(CA_HEADS)]
    for hd, hs in enumerate(head_slices):
        s_ref[hd] = lax.dot_general(q_ref[:, hs], k_ref[:, hs], (((1,), (1,)), ((), ())),
                                    preferred_element_type=F32)
    for r in _row_chunks(rows):
        for hd in range(CA_HEADS):
            s = s_ref[hd, r, :]
            e = jnp.exp(s - jnp.max(s, axis=-1, keepdims=True))
            p_ref[hd, r, :] = (e * (1.0 / jnp.sum(e, axis=-1, keepdims=True))).astype(BF16)
    hb_prev = h2b_ref[1 - slot]
    ff = jnp.zeros((rows, D_MODEL), F32)
    for j in range(D_FF // MLP_CHUNK):
        lo, hi = j * MLP_CHUNK, (j + 1) * MLP_CHUNK
        t = jnp.maximum(_dot(hb_prev, w1_ref[:, lo:hi]), 0.0)
        ff = ff + _dot((t * t).astype(BF16), w2_ref[lo:hi, :])
        if j == 0:
            for hd, hs in enumerate(head_slices):
                att_ref[:, hs] = _dot(p_ref[hd], v_ref[:, hs]).astype(BF16)
            ca_ref[...] = _dot(att_ref[...], wo_ref[...])
    _attn_norms(xnext_ref, mnext_ref, *norm_refs, 1 - slot)
    q_ref[...] = _dot(h1b_ref[...], wq_ref[...]).astype(BF16)
    o_ref[...] = _layer_norm(ALPHA * h2_ref[1 - slot] + ff, g3_ref[...], b3_ref[...])
    for r in _row_chunks(rows):
        h2 = _layer_norm(ALPHA * h1_ref[slot, r, :] + ca_ref[r, :], g2_ref[...], b2_ref[...])
        h2_ref[slot, r, :] = h2
        h2b_ref[slot, r, :] = h2.astype(BF16)


def _tail(x, mix, k, v, w_q, w_o, w1, w2, ln_g, ln_b, g1, b1, g2, b2, g3, b3):
    bsz, seq, _ = x.shape
    mlen = k.shape[1]
    n_rows = bsz * seq
    n_blocks = n_rows // ATTN_ROWS
    blocks_per_batch = seq // ATTN_ROWS
    last = n_blocks - 1
    blk = (ATTN_ROWS, D_MODEL)
    first_blk = _resident_spec(blk, (0, 0))
    next_blk = pl.BlockSpec(blk, lambda i: (jnp.minimum(i + 1, last), 0))
    mem_blk = pl.BlockSpec((None, mlen, D_MODEL), lambda i: (jnp.minimum(i, last) // blocks_per_batch, 0, 0))
    consts = (w_q, w_o, w1, w2, ln_g, ln_b, g1, b1, g2, b2, g3, b3)
    x2 = x.reshape(n_rows, D_MODEL)
    m2 = mix.reshape(n_rows, D_MODEL)
    return pl.pallas_call(
        _tail_body,
        grid=(n_blocks + 1,),
        in_specs=[first_blk, first_blk, next_blk, next_blk, mem_blk, mem_blk]
                 + [_resident_spec(c.shape) for c in consts],
        out_specs=pl.BlockSpec(blk, lambda i: (jnp.maximum(i - 1, 0), 0)),
        out_shape=jax.ShapeDtypeStruct((n_rows, D_MODEL), F32),
        scratch_shapes=[pltpu.VMEM((2, ATTN_ROWS, D_MODEL), F32),
                        pltpu.VMEM((ATTN_ROWS, D_MODEL), BF16),
                        pltpu.VMEM((ATTN_ROWS, D_MODEL), BF16),
                        pltpu.VMEM((CA_HEADS, ATTN_ROWS, mlen), F32),
                        pltpu.VMEM((CA_HEADS, ATTN_ROWS, mlen), BF16),
                        pltpu.VMEM((ATTN_ROWS, D_MODEL), BF16),
                        pltpu.VMEM((ATTN_ROWS, D_MODEL), F32),
                        pltpu.VMEM((2, ATTN_ROWS, D_MODEL), F32),
                        pltpu.VMEM((2, ATTN_ROWS, D_MODEL), BF16),
                        pltpu.VMEM((ATTN_ROWS, D_MODEL), F32)],
        compiler_params=pltpu.CompilerParams(
            dimension_semantics=("arbitrary",), vmem_limit_bytes=V7X_TAIL_VMEM_BYTES),
        name="tail",
    )(x2, m2, x2, m2, k, v, *consts)


def kernel(x, mem, ln_in_g, ln_in_b, w_in, s5_a_re, s5_a_im, s5_log_dt, s5_b_re, s5_b_im, s5_c_re, s5_c_im, s5_d, s5_w_glu, s5_b_glu, conv_w, conv_b, lru_w_a, lru_b_a, lru_w_x, lru_b_x, lru_lambda, g_s5, g_lru, w_mix_out, ln1_g, ln1_b, mem_ln_g, mem_ln_b, w_q, w_k, w_v, w_o, ln2_g, ln2_b, w_ff1, w_ff2, ln3_g, ln3_b):
    bsz, seq, d_model = x.shape
    assert d_model == D_MODEL and w_in.shape == (DEPTH, D_MODEL, D_IN)
    assert seq % MIXER_STEPS == 0 and seq % ATTN_ROWS == 0
    assert bsz % 16 == 0, "time-major rows of one step must fill whole bf16 tiles"
    row = lambda v: v.reshape(1, -1).astype(F32)
    l = 0

    apow_re, apow_im, w_in_chunk, w_out_chunk, wg = _s5_prep(
        s5_a_re[l], s5_a_im[l], s5_log_dt[l], s5_b_re[l], s5_b_im[l], s5_c_re[l], s5_c_im[l],
        lru_w_a[l], lru_w_x[l])
    apow_re = apow_re.reshape(S5_BLOCKS, 1, S5_BLOCK_STATES)
    apow_im = apow_im.reshape(S5_BLOCKS, 1, S5_BLOCK_STATES)

    x_tm = jnp.swapaxes(x, 0, 1).reshape(seq * bsz, D_MODEL)
    mix_tm = _mixer(x_tm, bsz, row(ln_in_g), row(ln_in_b), w_in[l].astype(BF16), w_in_chunk, apow_re, apow_im,
                    w_out_chunk,
                    row(s5_d[l]), s5_w_glu[l].astype(BF16), row(s5_b_glu[l]), row(g_s5[l]),
                    conv_w[l].astype(F32), row(conv_b[l]), wg, row(lru_b_a[l]), row(lru_b_x[l]),
                    row(lru_lambda[l]), row(g_lru[l]), w_mix_out[l].astype(BF16))
    mix_tm, mem_l, w_k_l, w_v_l = lax.optimization_barrier((mix_tm, mem, w_k[l], w_v[l]))
    mix = jnp.swapaxes(mix_tm.reshape(seq, bsz, D_MODEL), 0, 1)

    k, v = _kv(mem_l, row(mem_ln_g[l]), row(mem_ln_b[l]), w_k_l.astype(BF16), w_v_l.astype(BF16))
    out = _tail(x, mix, k, v, w_q[l].astype(BF16), w_o[l].astype(BF16), w_ff1[l].astype(BF16),
                w_ff2[l].astype(BF16), row(ln_in_g), row(ln_in_b), row(ln1_g[l]), row(ln1_b[l]),
                row(ln2_g[l]), row(ln2_b[l]), row(ln3_g[l]), row(ln3_b[l]))
    return out.reshape(bsz, seq, D_MODEL)
```

```python
import jax
import jax.numpy as jnp
from jax import lax
from jax.experimental import pallas as pl
from jax.experimental.pallas import tpu as pltpu

F32 = jnp.float32
BF16 = jnp.bfloat16

D_MODEL = 1024
D_S5 = 512
D_LRU = 512
D_IN = D_S5 + 2 * D_LRU
S5_GROUP = 16
S5_GROUPS = 32
S5_STATE = 64
LRU_HEADS = 8
LRU_HEAD_DIM = 64
CONV_WIDTH = 4
LRU_C = 8.0
D_FF = 4096
CA_HEADS = 4
CA_HEAD_DIM = 256
DEPTH = 1
ALPHA = (2 * DEPTH) ** 0.25
LN_EPS = 1e-5
RMS_EPS = 1e-6

S5_BLOCKS = 4
S5_BLOCK_CH = D_S5 // S5_BLOCKS
S5_BLOCK_STATES = 8 * S5_STATE
S5_CHUNK = 4
LRU_BLOCKS = 2
LRU_BLOCK_CH = D_LRU // LRU_BLOCKS

V7X_SCOPED_VMEM_BYTES = 56 * 1024 * 1024

MIXER_STEPS = 32
ATTN_ROWS = 512
MLP_ROWS = 1024
MLP_GROUP_ROWS = 512
KV_ROWS = 1024
MLP_CHUNK = 1024
VEC_ROWS = 16


def _layer_norm(x, g, b):
    mu = jnp.mean(x, axis=-1, keepdims=True)
    xc = x - mu
    var = jnp.mean(xc * xc, axis=-1, keepdims=True)
    return xc * lax.rsqrt(var + LN_EPS) * g + b


def _rms_norm(x, g):
    ms = jnp.mean(x * x, axis=-1, keepdims=True)
    return x * lax.rsqrt(ms + RMS_EPS) * g


def _gelu_tanh(x):
    c = 0.7978845608028654
    return 0.5 * x * (1.0 + jnp.tanh(c * (x + 0.044715 * (x * x * x))))


def _dot(a, b):
    return jnp.dot(a, b, preferred_element_type=F32)


def _const_spec(shape):
    zeros = (0,) * len(shape)
    return pl.BlockSpec(shape, lambda *_: zeros)


def _resident_spec(shape, block=None):
    zeros = (0,) * len(shape)
    index = zeros if block is None else block
    return pl.BlockSpec(shape, lambda *_: index, pipeline_mode=pl.Buffered(1))


def _row_chunks(rows, chunk=None):
    chunk = chunk or VEC_ROWS
    return [slice(r, r + chunk) for r in range(0, rows, chunk)]


def _cmul(a_re, a_im, b_re, b_im):
    return a_re * b_re - a_im * b_im, a_re * b_im + a_im * b_re


def _cmul(a_re, a_im, b_re, b_im):
    return a_re * b_re - a_im * b_im, a_re * b_im + a_im * b_re


def _discretise(a_re, a_im, log_dt):
    dt = jnp.exp(log_dt)
    mag = jnp.exp(dt * a_re)
    abar_re = mag * jnp.cos(dt * a_im)
    abar_im = mag * jnp.sin(dt * a_im)
    den = a_re * a_re + a_im * a_im
    q_re = ((abar_re - 1.0) * a_re + abar_im * a_im) / den
    q_im = (abar_im * a_re - (abar_re - 1.0) * a_im) / den
    return abar_re, abar_im, q_re, q_im


def _powers(abar_re, abar_im):
    pows = [(jnp.ones_like(abar_re), jnp.zeros_like(abar_im))]
    for _ in range(S5_CHUNK):
        pows.append(_cmul(pows[-1][0], pows[-1][1], abar_re, abar_im))
    return pows


def _block_diag_tiles(stacked, group_rows, group_cols, n_groups):
    rows = n_groups * group_rows
    cols = n_groups * group_cols
    row_shift = group_rows.bit_length() - 1
    col_shift = group_cols.bit_length() - 1
    assert group_rows == 1 << row_shift and group_cols == 1 << col_shift
    src = lax.broadcasted_iota(jnp.int32, (group_cols, cols), 0)
    dst = lax.broadcasted_iota(jnp.int32, (group_cols, cols), 1)
    replicate = jnp.where((dst & (group_cols - 1)) == src, 1.0, 0.0).astype(BF16)
    tiled = _dot(stacked.astype(BF16), replicate)
    row_group = lax.broadcasted_iota(jnp.int32, (rows, cols), 0) >> row_shift
    col_group = lax.broadcasted_iota(jnp.int32, (rows, cols), 1) >> col_shift
    mask = row_group == col_group
    return [jnp.where(mask, tiled[i * rows:(i + 1) * rows], 0.0).astype(BF16)
            for i in range(stacked.shape[0] // rows)]


def _s5_prep_body(are_ref, aim_ref, ldt_ref, arec_ref, aimc_ref, ldtc_ref, bre_ref, bim_ref, cre_ref, cim_ref,
                  cret_ref, cimt_ref, wa_ref, wx_ref,
                  apow_re_ref, apow_im_ref, wic_ref, woc_ref, wg_ref):
    g8 = S5_GROUPS // S5_BLOCKS
    abar_re, abar_im, q_re, q_im = _discretise(are_ref[...], aim_ref[...], ldt_ref[...])
    pows = _powers(abar_re, abar_im)
    apow_re_ref[...] = pows[S5_CHUNK][0]
    apow_im_ref[...] = pows[S5_CHUNK][1]
    bbar_re, bbar_im = _cmul(q_re, q_im, bre_ref[...], bim_ref[...])
    c_re = cre_ref[...]
    c_im = cim_ref[...]
    abar_re_c, abar_im_c, _, _ = _discretise(arec_ref[...], aimc_ref[...], ldtc_ref[...])
    pows_c = _powers(abar_re_c, abar_im_c)
    ct_re = cret_ref[...]
    ct_im = cimt_ref[...]

    lane = lax.broadcasted_iota(jnp.int32, (S5_GROUPS, S5_GROUP, S5_GROUP), 2)
    within = 2 * S5_BLOCK_STATES
    wic_ref[:, :, within:] = jnp.zeros((S5_BLOCKS, S5_CHUNK * S5_BLOCK_CH, S5_CHUNK * S5_BLOCK_CH), BF16)
    for j in range(S5_CHUNK):
        bt = _cmul(*pows[S5_CHUNK - 1 - j], bbar_re, bbar_im)
        et = _cmul(*pows_c[j + 1], ct_re, ct_im)
        ce_re, ce_im = _cmul(*pows[j], c_re, c_im)
        lag = jnp.zeros(lane.shape, F32)
        for ho in range(S5_GROUP):
            col = jnp.sum(bbar_re * ce_re[:, ho:ho + 1, :] - bbar_im * ce_im[:, ho:ho + 1, :],
                          axis=-1, keepdims=True)
            lag = jnp.where(lane == ho, col, lag)
        rows_j = slice(j * S5_BLOCK_CH, (j + 1) * S5_BLOCK_CH)
        for part in range(2):
            cols = slice(part * S5_BLOCK_STATES, (part + 1) * S5_BLOCK_STATES)
            in_tiles = _block_diag_tiles(bt[part].reshape(S5_GROUPS * S5_GROUP, S5_STATE), S5_GROUP, S5_STATE, g8)
            out_tiles = _block_diag_tiles(et[part].reshape(S5_GROUPS * S5_STATE, S5_GROUP), S5_STATE, S5_GROUP, g8)
            for k in range(S5_BLOCKS):
                wic_ref[k, rows_j, cols] = in_tiles[k]
                woc_ref[k, cols, rows_j] = out_tiles[k] if part == 0 else -out_tiles[k]
        lag_tiles = _block_diag_tiles(lag.reshape(S5_GROUPS * S5_GROUP, S5_GROUP), S5_GROUP, S5_GROUP, g8)
        for k in range(S5_BLOCKS):
            for t_in in range(S5_CHUNK - j):
                t_out = t_in + j
                wic_ref[k, t_in * S5_BLOCK_CH:(t_in + 1) * S5_BLOCK_CH,
                        within + t_out * S5_BLOCK_CH:within + (t_out + 1) * S5_BLOCK_CH] = lag_tiles[k]

    heads = LRU_HEADS // LRU_BLOCKS
    for part, w_ref in enumerate((wa_ref, wx_ref)):
        gate_tiles = _block_diag_tiles(w_ref[...].reshape(D_LRU, LRU_HEAD_DIM), LRU_HEAD_DIM, LRU_HEAD_DIM, heads)
        for kb in range(LRU_BLOCKS):
            wg_ref[kb, :, part * LRU_BLOCK_CH:(part + 1) * LRU_BLOCK_CH] = gate_tiles[kb]


def _s5_prep(a_re, a_im, log_dt, b_re, b_im, c_re, c_im, w_a, w_x):
    g, p, h = b_re.shape
    f = jax.ShapeDtypeStruct
    chunk_cols = S5_CHUNK * S5_BLOCK_CH
    return pl.pallas_call(
        _s5_prep_body,
        out_shape=(f((g, 1, p), F32), f((g, 1, p), F32),
                   f((S5_BLOCKS, chunk_cols, 2 * S5_BLOCK_STATES + chunk_cols), BF16),
                   f((S5_BLOCKS, 2 * S5_BLOCK_STATES, chunk_cols), BF16),
                   f((LRU_BLOCKS, LRU_BLOCK_CH, 2 * LRU_BLOCK_CH), BF16)),
        compiler_params=pltpu.CompilerParams(vmem_limit_bytes=V7X_SCOPED_VMEM_BYTES),
        name="s5_prep",
    )(a_re.reshape(g, 1, p), a_im.reshape(g, 1, p), log_dt.reshape(g, 1, 1),
      a_re.reshape(g, p, 1), a_im.reshape(g, p, 1), log_dt.reshape(g, 1, 1),
      jnp.swapaxes(b_re, 1, 2), jnp.swapaxes(b_im, 1, 2), c_re, c_im,
      jnp.swapaxes(c_re, 1, 2), jnp.swapaxes(c_im, 1, 2), w_a, w_x)


def _mixer_norm(x_ref, lng_ref, lnb_ref, h0b_ref):
    for r in _row_chunks(x_ref.shape[0]):
        h0b_ref[r, :] = _layer_norm(x_ref[r, :], lng_ref[...], lnb_ref[...]).astype(BF16)


def _mixer_project_in(h0b_ref, win_ref, z_ref, uc_ref, batch):
    for piece in range(D_IN // D_S5):
        cols = slice(piece * D_S5, (piece + 1) * D_S5)
        z = _dot(h0b_ref[...], win_ref[:, cols])
        z_ref[:, cols] = z
        if piece == 0:
            zb = z.astype(BF16)
            for step in range(z.shape[0] // batch):
                c, t = divmod(step, S5_CHUNK)
                for k in range(S5_BLOCKS):
                    uc_ref[k, c * batch:(c + 1) * batch, t * S5_BLOCK_CH:(t + 1) * S5_BLOCK_CH] = (
                        zb[step * batch:(step + 1) * batch, k * S5_BLOCK_CH:(k + 1) * S5_BLOCK_CH])


def _mixer_conv(z_ref, cw_ref, cb_ref, d_ref, xl_ref, xc_ref, xcb_ref, gz_ref, du_ref, batch):
    rows = z_ref.shape[0]
    halo = (CONV_WIDTH - 1) * batch
    xl_ref[halo:halo + rows, :] = z_ref[:, D_S5:D_S5 + D_LRU]
    for r in _row_chunks(rows, 2 * VEC_ROWS):
        xc = cb_ref[...]
        for j in range(CONV_WIDTH):
            xc = xc + xl_ref[j * batch + r.start:j * batch + r.stop, :] * cw_ref[j:j + 1, :]
        xc_ref[r, :] = xc
        xcb_ref[r, :] = xc.astype(BF16)
        gz_ref[r, :] = _gelu_tanh(z_ref[r, D_S5 + D_LRU:D_IN])
        du_ref[r, :] = d_ref[...] * z_ref[r, 0:D_S5]
    xl_ref[0:halo, :] = xl_ref[rows:rows + halo, :]


def _mixer_step(xnext_ref, lng_ref, lnb_ref, win_ref, wic_ref, apr_ref, api_ref, woc_ref, d_ref,
                wglu_ref, bglu_ref, gs5_ref, cw_ref, cb_ref, wg_ref, ba_ref, bx_ref, lam_ref, glru_ref, wmix_ref,
                mix_ref,
                z_ref, uc_ref, h0b_ref, inc_ref, hp_ref, st_ref, xl_ref, xc_ref, xcb_ref, pre_ref, a_ref, hx_ref,
                hl_ref, yo_ref, y32_ref, yb_ref, gate_ref, gz_ref, du_ref, y_ref):
    rows = y_ref.shape[0]
    batch = st_ref.shape[1]
    steps = rows // batch
    chunks = steps // S5_CHUNK
    wide = _row_chunks(rows, 2 * VEC_ROWS)
    sre = slice(0, S5_BLOCK_STATES)
    sim = slice(S5_BLOCK_STATES, 2 * S5_BLOCK_STATES)

    for k in range(S5_BLOCKS):
        inc_ref[k] = _dot(uc_ref[k], wic_ref[k])
    for kb in range(LRU_BLOCKS):
        lo, hi_ = kb * LRU_BLOCK_CH, (kb + 1) * LRU_BLOCK_CH
        pre = _dot(xcb_ref[:, lo:hi_], wg_ref[kb])
        pre_ref[:, lo:hi_] = pre[:, 0:LRU_BLOCK_CH]
        pre_ref[:, D_LRU + lo:D_LRU + hi_] = pre[:, LRU_BLOCK_CH:2 * LRU_BLOCK_CH]
    _mixer_project_in(h0b_ref, win_ref, z_ref, uc_ref, batch)

    for k in range(S5_BLOCKS):
        ar = jnp.broadcast_to(apr_ref[k], (batch, S5_BLOCK_STATES))
        ai = jnp.broadcast_to(api_ref[k], (batch, S5_BLOCK_STATES))
        hr = st_ref[k, :, sre]
        hi = st_ref[k, :, sim]
        for c in range(chunks):
            r = slice(c * batch, (c + 1) * batch)
            hp_ref[k, r, sre] = hr.astype(BF16)
            hp_ref[k, r, sim] = hi.astype(BF16)
            hr, hi = (ar * hr - ai * hi + inc_ref[k, r, sre],
                      ar * hi + ai * hr + inc_ref[k, r, sim])
        st_ref[k, :, sre] = hr
        st_ref[k, :, sim] = hi
        yo_ref[k] = _dot(hp_ref[k], woc_ref[k])
        if k == 1:
            lam = lam_ref[...]
            softplus_neg_lam = jnp.maximum(-lam, 0.0) + jnp.log1p(jnp.exp(-jnp.abs(lam)))
            for r in wide:
                gate_a = jax.nn.sigmoid(pre_ref[r, 0:D_LRU] + ba_ref[...])
                gate_x = jax.nn.sigmoid(pre_ref[r, D_LRU:2 * D_LRU] + bx_ref[...])
                log_a = -LRU_C * gate_a * softplus_neg_lam
                a = jnp.exp(log_a)
                m2 = -jnp.tanh(log_a) * (a * a + 1.0)
                mult = jnp.where(m2 > 0.0, m2 * lax.rsqrt(m2), 0.0)
                a_ref[r, :] = a
                hx_ref[r, :] = mult * (gate_x * xc_ref[r, :])
    hl = hl_ref[...]
    for t in range(steps):
        r = slice(t * batch, (t + 1) * batch)
        hl = a_ref[r, :] * hl + hx_ref[r, :]
        hx_ref[r, :] = hl
    hl_ref[...] = hl

    within = 2 * S5_BLOCK_STATES
    for step in range(steps):
        c, t = divmod(step, S5_CHUNK)
        r = slice(step * batch, (step + 1) * batch)
        rc = slice(c * batch, (c + 1) * batch)
        cols = slice(t * S5_BLOCK_CH, (t + 1) * S5_BLOCK_CH)
        ys = jnp.concatenate(
            [yo_ref[k, rc, cols] + inc_ref[k, rc, within + cols.start:within + cols.stop]
             for k in range(S5_BLOCKS)], axis=1)
        y = _gelu_tanh(ys + du_ref[r, :])
        y32_ref[r, :] = y
        yb_ref[r, :] = y.astype(BF16)
    gate_ref[...] = _dot(yb_ref[...], wglu_ref[...])
    for r in wide:
        y = y32_ref[r, :] * jax.nn.sigmoid(gate_ref[r, :] + bglu_ref[...])
        y_ref[r, 0:D_S5] = _rms_norm(y, gs5_ref[...]).astype(y_ref.dtype)
        ylru = hx_ref[r, :] * gz_ref[r, :]
        y_ref[r, D_S5:D_MODEL] = _rms_norm(ylru, glru_ref[...]).astype(y_ref.dtype)
    mix_ref[...] = _dot(y_ref[...], wmix_ref[...]).astype(mix_ref.dtype)

    _mixer_conv(z_ref, cw_ref, cb_ref, d_ref, xl_ref, xc_ref, xcb_ref, gz_ref, du_ref, batch)
    _mixer_norm(xnext_ref, lng_ref, lnb_ref, h0b_ref)


def _cast_rows(src_ref, dst_ref, chunk=64):
    for r in _row_chunks(src_ref.shape[0], chunk):
        dst_ref[r, :] = src_ref[r, :].astype(dst_ref.dtype)


def _mixer_body(xfirst_ref, xsecond_ref, xnext_ref, lng_ref, lnb_ref, win32_ref, *rest):
    back_consts = list(rest[:16])
    d_ref, cw_ref, cb_ref = back_consts[4], back_consts[8], back_consts[9]
    mix_ref = rest[16]
    scratch = rest[17:-3]
    win_ref, wglu_ref, wmix_ref = rest[-3:]
    wglu32_ref, wmix32_ref = back_consts[5], back_consts[15]
    back_consts[5], back_consts[15] = wglu_ref, wmix_ref
    z_ref, uc_ref, h0b_ref, _, _, st_ref, xl_ref, xc_ref, xcb_ref = scratch[:9]
    hl_ref, gz_ref, du_ref = scratch[12], scratch[17], scratch[18]
    batch = st_ref.shape[1]
    halo = (CONV_WIDTH - 1) * batch

    @pl.when(pl.program_id(0) == 0)
    def _():
        _cast_rows(win32_ref, win_ref)
        _cast_rows(wglu32_ref, wglu_ref)
        _cast_rows(wmix32_ref, wmix_ref)
        st_ref[...] = jnp.zeros_like(st_ref)
        hl_ref[...] = jnp.zeros_like(hl_ref)
        xl_ref[0:halo, :] = jnp.zeros((halo, D_LRU), F32)
        _mixer_norm(xfirst_ref, lng_ref, lnb_ref, h0b_ref)
        _mixer_project_in(h0b_ref, win_ref, z_ref, uc_ref, batch)
        _mixer_conv(z_ref, cw_ref, cb_ref, d_ref, xl_ref, xc_ref, xcb_ref, gz_ref, du_ref, batch)
        _mixer_norm(xsecond_ref, lng_ref, lnb_ref, h0b_ref)

    _mixer_step(xnext_ref, lng_ref, lnb_ref, win_ref, *back_consts, mix_ref, *scratch)


def _mixer(x_tm, batch, ln_g, ln_b, w_in, w_in_chunk, apow_re, apow_im, w_out_chunk, d, w_glu, b_glu, g_s5,
           conv_w, conv_b, wg, b_a, b_x, lam, g_lru, w_mix):
    n_rows = x_tm.shape[0]
    rows = MIXER_STEPS * batch
    n_blocks = n_rows // rows
    halo = (CONV_WIDTH - 1) * batch
    chunk_rows = rows // S5_CHUNK
    chunk_cols = S5_CHUNK * S5_BLOCK_CH
    consts = (ln_g, ln_b, w_in, w_in_chunk, apow_re, apow_im, w_out_chunk, d, w_glu, b_glu, g_s5,
              conv_w, conv_b, wg, b_a, b_x, lam, g_lru, w_mix)
    return pl.pallas_call(
        _mixer_body,
        grid=(n_blocks,),
        in_specs=[_resident_spec((rows, D_MODEL), (0, 0)),
                  _resident_spec((rows, D_MODEL), (min(1, n_blocks - 1), 0)),
                  pl.BlockSpec((rows, D_MODEL), lambda i: (jnp.minimum(i + 2, n_blocks - 1), 0))]
                 + [_resident_spec(c.shape) for c in consts],
        out_specs=pl.BlockSpec((rows, D_MODEL), lambda i: (i, 0)),
        out_shape=jax.ShapeDtypeStruct((n_rows, D_MODEL), BF16),
        scratch_shapes=[
            pltpu.VMEM((rows, D_IN), F32),
            pltpu.VMEM((S5_BLOCKS, chunk_rows, chunk_cols), BF16),
            pltpu.VMEM((rows, D_MODEL), BF16),
            pltpu.VMEM((S5_BLOCKS, chunk_rows, 2 * S5_BLOCK_STATES + chunk_cols), F32),
            pltpu.VMEM((S5_BLOCKS, chunk_rows, 2 * S5_BLOCK_STATES), BF16),
            pltpu.VMEM((S5_BLOCKS, batch, 2 * S5_BLOCK_STATES), F32),
            pltpu.VMEM((rows + halo, D_LRU), F32),
            pltpu.VMEM((rows, D_LRU), F32),
            pltpu.VMEM((rows, D_LRU), BF16),
            pltpu.VMEM((rows, 2 * D_LRU), F32),
            pltpu.VMEM((rows, D_LRU), F32),
            pltpu.VMEM((rows, D_LRU), F32),
            pltpu.VMEM((batch, D_LRU), F32),
            pltpu.VMEM((S5_BLOCKS, chunk_rows, chunk_cols), F32),
            pltpu.VMEM((rows, D_S5), F32),
            pltpu.VMEM((rows, D_S5), BF16),
            pltpu.VMEM((rows, D_S5), F32),
            pltpu.VMEM((rows, D_LRU), F32),
            pltpu.VMEM((rows, D_S5), F32),
            pltpu.VMEM((rows, D_MODEL), BF16),
            pltpu.VMEM(w_in.shape, BF16),
            pltpu.VMEM(w_glu.shape, BF16),
            pltpu.VMEM(w_mix.shape, BF16),
        ],
        compiler_params=pltpu.CompilerParams(
            dimension_semantics=("arbitrary",), vmem_limit_bytes=V7X_SCOPED_VMEM_BYTES),
        name="mixer",
    )(x_tm, x_tm, x_tm, *consts)


def _kv_body(mem_ref, g_ref, b_ref, wk32_ref, wv32_ref, k_ref, v_ref, wk_ref, wv_ref):
    @pl.when(pl.program_id(0) == 0)
    def _():
        _cast_rows(wk32_ref, wk_ref)
        _cast_rows(wv32_ref, wv_ref)

    mn = _layer_norm(mem_ref[...], g_ref[...], b_ref[...]).astype(BF16)
    k_ref[...] = (_dot(mn, wk_ref[...]) * (CA_HEAD_DIM ** -0.5)).astype(k_ref.dtype)
    v_ref[...] = _dot(mn, wv_ref[...]).astype(v_ref.dtype)


def _kv(mem, g, b, w_k, w_v):
    bsz, mlen, _ = mem.shape
    n_rows = bsz * mlen
    blk = pl.BlockSpec((KV_ROWS, D_MODEL), lambda i: (i, 0))
    out = jax.ShapeDtypeStruct((n_rows, D_MODEL), BF16)
    k, v = pl.pallas_call(
        _kv_body,
        grid=(n_rows // KV_ROWS,),
        in_specs=[blk, _const_spec(g.shape), _const_spec(b.shape), _resident_spec(w_k.shape),
                  _resident_spec(w_v.shape)],
        out_specs=(blk, blk),
        out_shape=(out, out),
        scratch_shapes=[pltpu.VMEM(w_k.shape, BF16), pltpu.VMEM(w_v.shape, BF16)],
        compiler_params=pltpu.CompilerParams(
            dimension_semantics=("arbitrary",), vmem_limit_bytes=V7X_SCOPED_VMEM_BYTES),
        name="kv",
    )(mem.reshape(n_rows, D_MODEL), g, b, w_k, w_v)
    return k.reshape(mem.shape), v.reshape(mem.shape)


def _attn_norms(x_ref, mix_ref, lng_ref, lnb_ref, g1_ref, b1_ref, h1_ref, h1b_ref, slot):
    for r in _row_chunks(x_ref.shape[0]):
        h0 = _layer_norm(x_ref[r, :], lng_ref[...], lnb_ref[...])
        h1 = _layer_norm(ALPHA * h0 + mix_ref[r, :], g1_ref[...], b1_ref[...])
        h1_ref[slot, r, :] = h1
        h1b_ref[r, :] = h1.astype(BF16)


def _attn_body(x0_ref, m0_ref, x1_ref, m1_ref, xnext_ref, mnext_ref, k_ref, v_ref, wq32_ref, wo32_ref,
               lng_ref, lnb_ref, g1_ref, b1_ref, g2_ref, b2_ref, o_ref,
               h1_ref, q_ref, h1b_ref, s_ref, p_ref, att_ref, ca_ref, wq_ref, wo_ref):
    i = pl.program_id(0)
    slot = i % 2
    rows = o_ref.shape[0]
    norm_refs = (lng_ref, lnb_ref, g1_ref, b1_ref, h1_ref, h1b_ref)

    @pl.when(i == 0)
    def _():
        _cast_rows(wq32_ref, wq_ref)
        _cast_rows(wo32_ref, wo_ref)
        _attn_norms(x0_ref, m0_ref, *norm_refs, 0)
        q_ref[...] = _dot(h1b_ref[...], wq_ref[...]).astype(BF16)
        _attn_norms(x1_ref, m1_ref, *norm_refs, 1)

    head_slices = [slice(hd * CA_HEAD_DIM, (hd + 1) * CA_HEAD_DIM) for hd in range(CA_HEADS)]
    for hd, hs in enumerate(head_slices):
        s_ref[hd] = lax.dot_general(q_ref[:, hs], k_ref[:, hs], (((1,), (1,)), ((), ())),
                                    preferred_element_type=F32)
    q_ref[...] = _dot(h1b_ref[...], wq_ref[...]).astype(BF16)
    for r in _row_chunks(rows):
        for hd in range(CA_HEADS):
            s = s_ref[hd, r, :]
            e = jnp.exp(s - jnp.max(s, axis=-1, keepdims=True))
            p_ref[hd, r, :] = (e * (1.0 / jnp.sum(e, axis=-1, keepdims=True))).astype(BF16)
    for hd, hs in enumerate(head_slices):
        att_ref[:, hs] = _dot(p_ref[hd], v_ref[:, hs]).astype(BF16)
    ca_ref[...] = _dot(att_ref[...], wo_ref[...])
    for r in _row_chunks(rows):
        o_ref[r, :] = _layer_norm(ALPHA * h1_ref[slot, r, :] + ca_ref[r, :], g2_ref[...], b2_ref[...])
    _attn_norms(xnext_ref, mnext_ref, *norm_refs, slot)


def _attn(x, mix, k, v, w_q, w_o, ln_g, ln_b, g1, b1, g2, b2):
    bsz, seq, _ = x.shape
    mlen = k.shape[1]
    n_rows = bsz * seq
    n_blocks = n_rows // ATTN_ROWS
    blocks_per_batch = seq // ATTN_ROWS
    blk = (ATTN_ROWS, D_MODEL)
    first_blk = _resident_spec(blk, (0, 0))
    second_blk = _resident_spec(blk, (min(1, n_blocks - 1), 0))
    next_blk = pl.BlockSpec(blk, lambda i: (jnp.minimum(i + 2, n_blocks - 1), 0))
    mem_blk = pl.BlockSpec((None, mlen, D_MODEL), lambda i: (i // blocks_per_batch, 0, 0))
    consts = (w_q, w_o, ln_g, ln_b, g1, b1, g2, b2)
    x2 = x.reshape(n_rows, D_MODEL)
    y2 = mix.reshape(n_rows, D_MODEL)
    return pl.pallas_call(
        _attn_body,
        grid=(n_blocks,),
        in_specs=[first_blk, first_blk, second_blk, second_blk, next_blk, next_blk, mem_blk, mem_blk]
                 + [_resident_spec(c.shape) for c in consts],
        out_specs=pl.BlockSpec(blk, lambda i: (i, 0)),
        out_shape=jax.ShapeDtypeStruct((n_rows, D_MODEL), F32),
        scratch_shapes=[pltpu.VMEM((2, ATTN_ROWS, D_MODEL), F32),
                        pltpu.VMEM((ATTN_ROWS, D_MODEL), BF16),
                        pltpu.VMEM((ATTN_ROWS, D_MODEL), BF16),
                        pltpu.VMEM((CA_HEADS, ATTN_ROWS, mlen), F32),
                        pltpu.VMEM((CA_HEADS, ATTN_ROWS, mlen), BF16),
                        pltpu.VMEM((ATTN_ROWS, D_MODEL), BF16),
                        pltpu.VMEM((ATTN_ROWS, D_MODEL), F32),
                        pltpu.VMEM(w_q.shape, BF16),
                        pltpu.VMEM(w_o.shape, BF16)],
        compiler_params=pltpu.CompilerParams(
            dimension_semantics=("arbitrary",), vmem_limit_bytes=V7X_SCOPED_VMEM_BYTES),
        name="attn",
    )(x2, y2, x2, y2, x2, y2, k, v, *consts)


def _mlp_body(h_ref, w1_ref, w2_ref, g_ref, b_ref, o_ref):
    for r in _row_chunks(h_ref.shape[0], MLP_GROUP_ROWS):
        h = h_ref[r, :]
        hb = h.astype(BF16)
        ff = jnp.zeros(h.shape, F32)
        for j in range(D_FF // MLP_CHUNK):
            lo, hi = j * MLP_CHUNK, (j + 1) * MLP_CHUNK
            t = jnp.maximum(_dot(hb, w1_ref[:, lo:hi]), 0.0)
            ff = ff + _dot((t * t).astype(BF16), w2_ref[lo:hi, :])
        o_ref[r, :] = _layer_norm(ALPHA * h + ff, g_ref[...], b_ref[...])


def _mlp(h, w1, w2, g, b):
    n_rows = h.shape[0]
    row_blk = pl.BlockSpec((MLP_ROWS, D_MODEL), lambda i: (i, 0))
    return pl.pallas_call(
        _mlp_body,
        grid=(n_rows // MLP_ROWS,),
        in_specs=[row_blk, _resident_spec(w1.shape), _resident_spec(w2.shape),
                  _const_spec(g.shape), _const_spec(b.shape)],
        out_specs=row_blk,
        out_shape=jax.ShapeDtypeStruct(h.shape, F32),
        compiler_params=pltpu.CompilerParams(
            dimension_semantics=("arbitrary",), vmem_limit_bytes=V7X_SCOPED_VMEM_BYTES),
        name="mlp",
    )(h, w1, w2, g, b)


def kernel(x, mem, ln_in_g, ln_in_b, w_in, s5_a_re, s5_a_im, s5_log_dt, s5_b_re, s5_b_im, s5_c_re, s5_c_im, s5_d, s5_w_glu, s5_b_glu, conv_w, conv_b, lru_w_a, lru_b_a, lru_w_x, lru_b_x, lru_lambda, g_s5, g_lru, w_mix_out, ln1_g, ln1_b, mem_ln_g, mem_ln_b, w_q, w_k, w_v, w_o, ln2_g, ln2_b, w_ff1, w_ff2, ln3_g, ln3_b):
    bsz, seq, d_model = x.shape
    assert d_model == D_MODEL and w_in.shape == (DEPTH, D_MODEL, D_IN)
    assert seq % MIXER_STEPS == 0 and seq % ATTN_ROWS == 0 and (bsz * seq) % MLP_ROWS == 0
    assert bsz % 16 == 0, "time-major rows of one step must fill whole bf16 tiles"
    row = lambda v: v.reshape(1, -1).astype(F32)
    l = 0

    apow_re, apow_im, w_in_chunk, w_out_chunk, wg = _s5_prep(
        s5_a_re[l], s5_a_im[l], s5_log_dt[l], s5_b_re[l], s5_b_im[l], s5_c_re[l], s5_c_im[l],
        lru_w_a[l], lru_w_x[l])
    apow_re = apow_re.reshape(S5_BLOCKS, 1, S5_BLOCK_STATES)
    apow_im = apow_im.reshape(S5_BLOCKS, 1, S5_BLOCK_STATES)

    x_tm = jnp.swapaxes(x, 0, 1).reshape(seq * bsz, D_MODEL)
    mix_tm = _mixer(x_tm, bsz, row(ln_in_g), row(ln_in_b), w_in[l], w_in_chunk, apow_re, apow_im,
                    w_out_chunk,
                    row(s5_d[l]), s5_w_glu[l], row(s5_b_glu[l]), row(g_s5[l]),
                    conv_w[l].astype(F32), row(conv_b[l]), wg, row(lru_b_a[l]), row(lru_b_x[l]),
                    row(lru_lambda[l]), row(g_lru[l]), w_mix_out[l])
    mix_tm, mem_l, w_k_l, w_v_l = lax.optimization_barrier((mix_tm, mem, w_k[l], w_v[l]))
    mix = jnp.swapaxes(mix_tm.reshape(seq, bsz, D_MODEL), 0, 1)

    k, v = _kv(mem_l, row(mem_ln_g[l]), row(mem_ln_b[l]), w_k_l, w_v_l)
    h2 = _attn(x, mix, k, v, w_q[l], w_o[l],
               row(ln_in_g), row(ln_in_b), row(ln1_g[l]), row(ln1_b[l]), row(ln2_g[l]), row(ln2_b[l]))
    out = _mlp(h2, w_ff1[l].astype(BF16), w_ff2[l].astype(BF16), row(ln3_g[l]), row(ln3_b[l]))
    return out.reshape(bsz, seq, D_MODEL)
```

```python
import jax
import jax.numpy as jnp
from jax import lax
from jax.experimental import pallas as pl
from jax.experimental.pallas import tpu as pltpu

F32 = jnp.float32
BF16 = jnp.bfloat16

D_MODEL = 1024
D_S5 = 512
D_LRU = 512
D_IN = D_S5 + 2 * D_LRU
S5_GROUP = 16
S5_GROUPS = 32
S5_STATE = 64
LRU_HEADS = 8
LRU_HEAD_DIM = 64
CONV_WIDTH = 4
LRU_C = 8.0
D_FF = 4096
CA_HEADS = 4
CA_HEAD_DIM = 256
DEPTH = 1
ALPHA = (2 * DEPTH) ** 0.25
LN_EPS = 1e-5
RMS_EPS = 1e-6

S5_BLOCKS = 4
S5_BLOCK_CH = D_S5 // S5_BLOCKS
S5_BLOCK_STATES = 8 * S5_STATE
S5_CHUNK = 4
LRU_BLOCKS = 2
LRU_BLOCK_CH = D_LRU // LRU_BLOCKS

V7X_SCOPED_VMEM_BYTES = 56 * 1024 * 1024

MIXER_STEPS = 32
ATTN_ROWS = 512
MLP_ROWS = 1024
MLP_GROUP_ROWS = 512
KV_ROWS = 1024
MLP_CHUNK = 1024
VEC_ROWS = 16


def _layer_norm(x, g, b):
    mu = jnp.mean(x, axis=-1, keepdims=True)
    xc = x - mu
    var = jnp.mean(xc * xc, axis=-1, keepdims=True)
    return xc * lax.rsqrt(var + LN_EPS) * g + b


def _rms_norm(x, g):
    ms = jnp.mean(x * x, axis=-1, keepdims=True)
    return x * lax.rsqrt(ms + RMS_EPS) * g


def _gelu_tanh(x):
    c = 0.7978845608028654
    return 0.5 * x * (1.0 + jnp.tanh(c * (x + 0.044715 * (x * x * x))))


def _dot(a, b):
    return jnp.dot(a, b, preferred_element_type=F32)


def _const_spec(shape):
    zeros = (0,) * len(shape)
    return pl.BlockSpec(shape, lambda *_: zeros)


def _resident_spec(shape, block=None):
    zeros = (0,) * len(shape)
    index = zeros if block is None else block
    return pl.BlockSpec(shape, lambda *_: index, pipeline_mode=pl.Buffered(1))


def _row_chunks(rows, chunk=None):
    chunk = chunk or VEC_ROWS
    return [slice(r, r + chunk) for r in range(0, rows, chunk)]


def _cmul(a_re, a_im, b_re, b_im):
    return a_re * b_re - a_im * b_im, a_re * b_im + a_im * b_re


def _cmul(a_re, a_im, b_re, b_im):
    return a_re * b_re - a_im * b_im, a_re * b_im + a_im * b_re


def _discretise(a_re, a_im, log_dt):
    dt = jnp.exp(log_dt)
    mag = jnp.exp(dt * a_re)
    abar_re = mag * jnp.cos(dt * a_im)
    abar_im = mag * jnp.sin(dt * a_im)
    den = a_re * a_re + a_im * a_im
    q_re = ((abar_re - 1.0) * a_re + abar_im * a_im) / den
    q_im = (abar_im * a_re - (abar_re - 1.0) * a_im) / den
    return abar_re, abar_im, q_re, q_im


def _powers(abar_re, abar_im):
    pows = [(jnp.ones_like(abar_re), jnp.zeros_like(abar_im))]
    for _ in range(S5_CHUNK):
        pows.append(_cmul(pows[-1][0], pows[-1][1], abar_re, abar_im))
    return pows


def _block_diag_tiles(stacked, group_rows, group_cols, n_groups):
    rows = n_groups * group_rows
    cols = n_groups * group_cols
    row_shift = group_rows.bit_length() - 1
    col_shift = group_cols.bit_length() - 1
    assert group_rows == 1 << row_shift and group_cols == 1 << col_shift
    src = lax.broadcasted_iota(jnp.int32, (group_cols, cols), 0)
    dst = lax.broadcasted_iota(jnp.int32, (group_cols, cols), 1)
    replicate = jnp.where((dst & (group_cols - 1)) == src, 1.0, 0.0).astype(BF16)
    tiled = _dot(stacked.astype(BF16), replicate)
    row_group = lax.broadcasted_iota(jnp.int32, (rows, cols), 0) >> row_shift
    col_group = lax.broadcasted_iota(jnp.int32, (rows, cols), 1) >> col_shift
    mask = row_group == col_group
    return [jnp.where(mask, tiled[i * rows:(i + 1) * rows], 0.0).astype(BF16)
            for i in range(stacked.shape[0] // rows)]


def _s5_prep_body(are_ref, aim_ref, ldt_ref, arec_ref, aimc_ref, ldtc_ref, bre_ref, bim_ref, cre_ref, cim_ref,
                  cret_ref, cimt_ref, wa_ref, wx_ref,
                  apow_re_ref, apow_im_ref, wic_ref, woc_ref, wg_ref):
    g8 = S5_GROUPS // S5_BLOCKS
    abar_re, abar_im, q_re, q_im = _discretise(are_ref[...], aim_ref[...], ldt_ref[...])
    pows = _powers(abar_re, abar_im)
    apow_re_ref[...] = pows[S5_CHUNK][0]
    apow_im_ref[...] = pows[S5_CHUNK][1]
    bbar_re, bbar_im = _cmul(q_re, q_im, bre_ref[...], bim_ref[...])
    c_re = cre_ref[...]
    c_im = cim_ref[...]
    abar_re_c, abar_im_c, _, _ = _discretise(arec_ref[...], aimc_ref[...], ldtc_ref[...])
    pows_c = _powers(abar_re_c, abar_im_c)
    ct_re = cret_ref[...]
    ct_im = cimt_ref[...]

    lane = lax.broadcasted_iota(jnp.int32, (S5_GROUPS, S5_GROUP, S5_GROUP), 2)
    within = 2 * S5_BLOCK_STATES
    wic_ref[:, :, within:] = jnp.zeros((S5_BLOCKS, S5_CHUNK * S5_BLOCK_CH, S5_CHUNK * S5_BLOCK_CH), BF16)
    for j in range(S5_CHUNK):
        bt = _cmul(*pows[S5_CHUNK - 1 - j], bbar_re, bbar_im)
        et = _cmul(*pows_c[j + 1], ct_re, ct_im)
        ce_re, ce_im = _cmul(*pows[j], c_re, c_im)
        lag = jnp.zeros(lane.shape, F32)
        for ho in range(S5_GROUP):
            col = jnp.sum(bbar_re * ce_re[:, ho:ho + 1, :] - bbar_im * ce_im[:, ho:ho + 1, :],
                          axis=-1, keepdims=True)
            lag = jnp.where(lane == ho, col, lag)
        rows_j = slice(j * S5_BLOCK_CH, (j + 1) * S5_BLOCK_CH)
        for part in range(2):
            cols = slice(part * S5_BLOCK_STATES, (part + 1) * S5_BLOCK_STATES)
            in_tiles = _block_diag_tiles(bt[part].reshape(S5_GROUPS * S5_GROUP, S5_STATE), S5_GROUP, S5_STATE, g8)
            out_tiles = _block_diag_tiles(et[part].reshape(S5_GROUPS * S5_STATE, S5_GROUP), S5_STATE, S5_GROUP, g8)
            for k in range(S5_BLOCKS):
                wic_ref[k, rows_j, cols] = in_tiles[k]
                woc_ref[k, cols, rows_j] = out_tiles[k] if part == 0 else -out_tiles[k]
        lag_tiles = _block_diag_tiles(lag.reshape(S5_GROUPS * S5_GROUP, S5_GROUP), S5_GROUP, S5_GROUP, g8)
        for k in range(S5_BLOCKS):
            for t_in in range(S5_CHUNK - j):
                t_out = t_in + j
                wic_ref[k, t_in * S5_BLOCK_CH:(t_in + 1) * S5_BLOCK_CH,
                        within + t_out * S5_BLOCK_CH:within + (t_out + 1) * S5_BLOCK_CH] = lag_tiles[k]

    heads = LRU_HEADS // LRU_BLOCKS
    for part, w_ref in enumerate((wa_ref, wx_ref)):
        gate_tiles = _block_diag_tiles(w_ref[...].reshape(D_LRU, LRU_HEAD_DIM), LRU_HEAD_DIM, LRU_HEAD_DIM, heads)
        for kb in range(LRU_BLOCKS):
            wg_ref[kb, :, part * LRU_BLOCK_CH:(part + 1) * LRU_BLOCK_CH] = gate_tiles[kb]


def _s5_prep(a_re, a_im, log_dt, b_re, b_im, c_re, c_im, w_a, w_x):
    g, p, h = b_re.shape
    f = jax.ShapeDtypeStruct
    chunk_cols = S5_CHUNK * S5_BLOCK_CH
    return pl.pallas_call(
        _s5_prep_body,
        out_shape=(f((g, 1, p), F32), f((g, 1, p), F32),
                   f((S5_BLOCKS, chunk_cols, 2 * S5_BLOCK_STATES + chunk_cols), BF16),
                   f((S5_BLOCKS, 2 * S5_BLOCK_STATES, chunk_cols), BF16),
                   f((LRU_BLOCKS, LRU_BLOCK_CH, 2 * LRU_BLOCK_CH), BF16)),
        compiler_params=pltpu.CompilerParams(vmem_limit_bytes=V7X_SCOPED_VMEM_BYTES),
        name="s5_prep",
    )(a_re.reshape(g, 1, p), a_im.reshape(g, 1, p), log_dt.reshape(g, 1, 1),
      a_re.reshape(g, p, 1), a_im.reshape(g, p, 1), log_dt.reshape(g, 1, 1),
      jnp.swapaxes(b_re, 1, 2), jnp.swapaxes(b_im, 1, 2), c_re, c_im,
      jnp.swapaxes(c_re, 1, 2), jnp.swapaxes(c_im, 1, 2), w_a, w_x)


def _mixer_norm(x_ref, lng_ref, lnb_ref, h0b_ref):
    for r in _row_chunks(x_ref.shape[0]):
        h0b_ref[r, :] = _layer_norm(x_ref[r, :], lng_ref[...], lnb_ref[...]).astype(BF16)


def _mixer_project_in(h0b_ref, win_ref, z_ref, uc_ref, batch):
    for piece in range(D_IN // D_S5):
        cols = slice(piece * D_S5, (piece + 1) * D_S5)
        z = _dot(h0b_ref[...], win_ref[:, cols])
        z_ref[:, cols] = z
        if piece == 0:
            zb = z.astype(BF16)
            for step in range(z.shape[0] // batch):
                c, t = divmod(step, S5_CHUNK)
                for k in range(S5_BLOCKS):
                    uc_ref[k, c * batch:(c + 1) * batch, t * S5_BLOCK_CH:(t + 1) * S5_BLOCK_CH] = (
                        zb[step * batch:(step + 1) * batch, k * S5_BLOCK_CH:(k + 1) * S5_BLOCK_CH])


def _mixer_conv(z_ref, cw_ref, cb_ref, d_ref, xl_ref, xc_ref, xcb_ref, gz_ref, du_ref, batch):
    rows = z_ref.shape[0]
    halo = (CONV_WIDTH - 1) * batch
    xl_ref[halo:halo + rows, :] = z_ref[:, D_S5:D_S5 + D_LRU]
    for r in _row_chunks(rows, 2 * VEC_ROWS):
        xc = cb_ref[...]
        for j in range(CONV_WIDTH):
            xc = xc + xl_ref[j * batch + r.start:j * batch + r.stop, :] * cw_ref[j:j + 1, :]
        xc_ref[r, :] = xc
        xcb_ref[r, :] = xc.astype(BF16)
        gz_ref[r, :] = _gelu_tanh(z_ref[r, D_S5 + D_LRU:D_IN])
        du_ref[r, :] = d_ref[...] * z_ref[r, 0:D_S5]
    xl_ref[0:halo, :] = xl_ref[rows:rows + halo, :]


def _mixer_step(xnext_ref, lng_ref, lnb_ref, win_ref, wic_ref, apr_ref, api_ref, woc_ref, d_ref,
                wglu_ref, bglu_ref, gs5_ref, cw_ref, cb_ref, wg_ref, ba_ref, bx_ref, lam_ref, glru_ref, wmix_ref,
                mix_ref,
                z_ref, uc_ref, h0b_ref, inc_ref, hp_ref, st_ref, xl_ref, xc_ref, xcb_ref, pre_ref, a_ref, hx_ref,
                hl_ref, yo_ref, y32_ref, yb_ref, gate_ref, gz_ref, du_ref, y_ref):
    rows = y_ref.shape[0]
    batch = st_ref.shape[1]
    steps = rows // batch
    chunks = steps // S5_CHUNK
    wide = _row_chunks(rows, 2 * VEC_ROWS)
    sre = slice(0, S5_BLOCK_STATES)
    sim = slice(S5_BLOCK_STATES, 2 * S5_BLOCK_STATES)

    for k in range(S5_BLOCKS):
        inc_ref[k] = _dot(uc_ref[k], wic_ref[k])
    for kb in range(LRU_BLOCKS):
        lo, hi_ = kb * LRU_BLOCK_CH, (kb + 1) * LRU_BLOCK_CH
        pre = _dot(xcb_ref[:, lo:hi_], wg_ref[kb])
        pre_ref[:, lo:hi_] = pre[:, 0:LRU_BLOCK_CH]
        pre_ref[:, D_LRU + lo:D_LRU + hi_] = pre[:, LRU_BLOCK_CH:2 * LRU_BLOCK_CH]
    _mixer_project_in(h0b_ref, win_ref, z_ref, uc_ref, batch)

    for k in range(S5_BLOCKS):
        ar = jnp.broadcast_to(apr_ref[k], (batch, S5_BLOCK_STATES))
        ai = jnp.broadcast_to(api_ref[k], (batch, S5_BLOCK_STATES))
        hr = st_ref[k, :, sre]
        hi = st_ref[k, :, sim]
        for c in range(chunks):
            r = slice(c * batch, (c + 1) * batch)
            hp_ref[k, r, sre] = hr.astype(BF16)
            hp_ref[k, r, sim] = hi.astype(BF16)
            hr, hi = (ar * hr - ai * hi + inc_ref[k, r, sre],
                      ar * hi + ai * hr + inc_ref[k, r, sim])
        st_ref[k, :, sre] = hr
        st_ref[k, :, sim] = hi
        yo_ref[k] = _dot(hp_ref[k], woc_ref[k])
        if k == 1:
            lam = lam_ref[...]
            softplus_neg_lam = jnp.maximum(-lam, 0.0) + jnp.log1p(jnp.exp(-jnp.abs(lam)))
            for r in wide:
                gate_a = jax.nn.sigmoid(pre_ref[r, 0:D_LRU] + ba_ref[...])
                gate_x = jax.nn.sigmoid(pre_ref[r, D_LRU:2 * D_LRU] + bx_ref[...])
                log_a = -LRU_C * gate_a * softplus_neg_lam
                a = jnp.exp(log_a)
                m2 = -jnp.tanh(log_a) * (a * a + 1.0)
                mult = jnp.where(m2 > 0.0, m2 * lax.rsqrt(m2), 0.0)
                a_ref[r, :] = a
                hx_ref[r, :] = mult * (gate_x * xc_ref[r, :])
    hl = hl_ref[...]
    for t in range(steps):
        r = slice(t * batch, (t + 1) * batch)
        hl = a_ref[r, :] * hl + hx_ref[r, :]
        hx_ref[r, :] = hl
    hl_ref[...] = hl

    within = 2 * S5_BLOCK_STATES
    for step in range(steps):
        c, t = divmod(step, S5_CHUNK)
        r = slice(step * batch, (step + 1) * batch)
        rc = slice(c * batch, (c + 1) * batch)
        cols = slice(t * S5_BLOCK_CH, (t + 1) * S5_BLOCK_CH)
        ys = jnp.concatenate(
            [yo_ref[k, rc, cols] + inc_ref[k, rc, within + cols.start:within + cols.stop]
             for k in range(S5_BLOCKS)], axis=1)
        y = _gelu_tanh(ys + du_ref[r, :])
        y32_ref[r, :] = y
        yb_ref[r, :] = y.astype(BF16)
    gate_ref[...] = _dot(yb_ref[...], wglu_ref[...])
    for r in wide:
        y = y32_ref[r, :] * jax.nn.sigmoid(gate_ref[r, :] + bglu_ref[...])
        y_ref[r, 0:D_S5] = _rms_norm(y, gs5_ref[...]).astype(y_ref.dtype)
        ylru = hx_ref[r, :] * gz_ref[r, :]
        y_ref[r, D_S5:D_MODEL] = _rms_norm(ylru, glru_ref[...]).astype(y_ref.dtype)
    mix_ref[...] = _dot(y_ref[...], wmix_ref[...]).astype(mix_ref.dtype)

    _mixer_conv(z_ref, cw_ref, cb_ref, d_ref, xl_ref, xc_ref, xcb_ref, gz_ref, du_ref, batch)
    _mixer_norm(xnext_ref, lng_ref, lnb_ref, h0b_ref)


def _cast_rows(src_ref, dst_ref, chunk=64):
    for r in _row_chunks(src_ref.shape[0], chunk):
        dst_ref[r, :] = src_ref[r, :].astype(dst_ref.dtype)


def _mixer_body(xfirst_ref, xsecond_ref, xnext_ref, lng_ref, lnb_ref, win32_ref, *rest):
    back_consts = list(rest[:16])
    d_ref, cw_ref, cb_ref = back_consts[4], back_consts[8], back_consts[9]
    mix_ref = rest[16]
    scratch = rest[17:-3]
    win_ref, wglu_ref, wmix_ref = rest[-3:]
    wglu32_ref, wmix32_ref = back_consts[5], back_consts[15]
    back_consts[5], back_consts[15] = wglu_ref, wmix_ref
    z_ref, uc_ref, h0b_ref, _, _, st_ref, xl_ref, xc_ref, xcb_ref = scratch[:9]
    hl_ref, gz_ref, du_ref = scratch[12], scratch[17], scratch[18]
    batch = st_ref.shape[1]
    halo = (CONV_WIDTH - 1) * batch

    @pl.when(pl.program_id(0) == 0)
    def _():
        _cast_rows(win32_ref, win_ref)
        _cast_rows(wglu32_ref, wglu_ref)
        _cast_rows(wmix32_ref, wmix_ref)
        st_ref[...] = jnp.zeros_like(st_ref)
        hl_ref[...] = jnp.zeros_like(hl_ref)
        xl_ref[0:halo, :] = jnp.zeros((halo, D_LRU), F32)
        _mixer_norm(xfirst_ref, lng_ref, lnb_ref, h0b_ref)
        _mixer_project_in(h0b_ref, win_ref, z_ref, uc_ref, batch)
        _mixer_conv(z_ref, cw_ref, cb_ref, d_ref, xl_ref, xc_ref, xcb_ref, gz_ref, du_ref, batch)
        _mixer_norm(xsecond_ref, lng_ref, lnb_ref, h0b_ref)

    _mixer_step(xnext_ref, lng_ref, lnb_ref, win_ref, *back_consts, mix_ref, *scratch)


def _mixer(x_tm, batch, ln_g, ln_b, w_in, w_in_chunk, apow_re, apow_im, w_out_chunk, d, w_glu, b_glu, g_s5,
           conv_w, conv_b, wg, b_a, b_x, lam, g_lru, w_mix):
    n_rows = x_tm.shape[0]
    rows = MIXER_STEPS * batch
    n_blocks = n_rows // rows
    halo = (CONV_WIDTH - 1) * batch
    chunk_rows = rows // S5_CHUNK
    chunk_cols = S5_CHUNK * S5_BLOCK_CH
    consts = (ln_g, ln_b, w_in, w_in_chunk, apow_re, apow_im, w_out_chunk, d, w_glu, b_glu, g_s5,
              conv_w, conv_b, wg, b_a, b_x, lam, g_lru, w_mix)
    return pl.pallas_call(
        _mixer_body,
        grid=(n_blocks,),
        in_specs=[_resident_spec((rows, D_MODEL), (0, 0)),
                  _resident_spec((rows, D_MODEL), (min(1, n_blocks - 1), 0)),
                  pl.BlockSpec((rows, D_MODEL), lambda i: (jnp.minimum(i + 2, n_blocks - 1), 0))]
                 + [_resident_spec(c.shape) for c in consts],
        out_specs=pl.BlockSpec((rows, D_MODEL), lambda i: (i, 0)),
        out_shape=jax.ShapeDtypeStruct((n_rows, D_MODEL), BF16),
        scratch_shapes=[
            pltpu.VMEM((rows, D_IN), F32),
            pltpu.VMEM((S5_BLOCKS, chunk_rows, chunk_cols), BF16),
            pltpu.VMEM((rows, D_MODEL), BF16),
            pltpu.VMEM((S5_BLOCKS, chunk_rows, 2 * S5_BLOCK_STATES + chunk_cols), F32),
            pltpu.VMEM((S5_BLOCKS, chunk_rows, 2 * S5_BLOCK_STATES), BF16),
            pltpu.VMEM((S5_BLOCKS, batch, 2 * S5_BLOCK_STATES), F32),
            pltpu.VMEM((rows + halo, D_LRU), F32),
            pltpu.VMEM((rows, D_LRU), F32),
            pltpu.VMEM((rows, D_LRU), BF16),
            pltpu.VMEM((rows, 2 * D_LRU), F32),
            pltpu.VMEM((rows, D_LRU), F32),
            pltpu.VMEM((rows, D_LRU), F32),
            pltpu.VMEM((batch, D_LRU), F32),
            pltpu.VMEM((S5_BLOCKS, chunk_rows, chunk_cols), F32),
            pltpu.VMEM((rows, D_S5), F32),
            pltpu.VMEM((rows, D_S5), BF16),
            pltpu.VMEM((rows, D_S5), F32),
            pltpu.VMEM((rows, D_LRU), F32),
            pltpu.VMEM((rows, D_S5), F32),
            pltpu.VMEM((rows, D_MODEL), BF16),
            pltpu.VMEM(w_in.shape, BF16),
            pltpu.VMEM(w_glu.shape, BF16),
            pltpu.VMEM(w_mix.shape, BF16),
        ],
        compiler_params=pltpu.CompilerParams(
            dimension_semantics=("arbitrary",), vmem_limit_bytes=V7X_SCOPED_VMEM_BYTES),
        name="mixer",
    )(x_tm, x_tm, x_tm, *consts)


def _kv_body(mem_ref, g_ref, b_ref, wk32_ref, wv32_ref, k_ref, v_ref, wk_ref, wv_ref):
    @pl.when(pl.program_id(0) == 0)
    def _():
        _cast_rows(wk32_ref, wk_ref)
        _cast_rows(wv32_ref, wv_ref)

    mn = _layer_norm(mem_ref[...], g_ref[...], b_ref[...]).astype(BF16)
    k_ref[...] = (_dot(mn, wk_ref[...]) * (CA_HEAD_DIM ** -0.5)).astype(k_ref.dtype)
    v_ref[...] = _dot(mn, wv_ref[...]).astype(v_ref.dtype)


def _kv(mem, g, b, w_k, w_v):
    bsz, mlen, _ = mem.shape
    n_rows = bsz * mlen
    blk = pl.BlockSpec((KV_ROWS, D_MODEL), lambda i: (i, 0))
    out = jax.ShapeDtypeStruct((n_rows, D_MODEL), BF16)
    k, v = pl.pallas_call(
        _kv_body,
        grid=(n_rows // KV_ROWS,),
        in_specs=[blk, _const_spec(g.shape), _const_spec(b.shape), _resident_spec(w_k.shape),
                  _resident_spec(w_v.shape)],
        out_specs=(blk, blk),
        out_shape=(out, out),
        scratch_shapes=[pltpu.VMEM(w_k.shape, BF16), pltpu.VMEM(w_v.shape, BF16)],
        compiler_params=pltpu.CompilerParams(
            dimension_semantics=("arbitrary",), vmem_limit_bytes=V7X_SCOPED_VMEM_BYTES),
        name="kv",
    )(mem.reshape(n_rows, D_MODEL), g, b, w_k, w_v)
    return k.reshape(mem.shape), v.reshape(mem.shape)


def _attn_norms(x_ref, mix_ref, lng_ref, lnb_ref, g1_ref, b1_ref, h1_ref, h1b_ref, slot):
    for r in _row_chunks(x_ref.shape[0]):
        h0 = _layer_norm(x_ref[r, :], lng_ref[...], lnb_ref[...])
        h1 = _layer_norm(ALPHA * h0 + mix_ref[r, :], g1_ref[...], b1_ref[...])
        h1_ref[slot, r, :] = h1
        h1b_ref[r, :] = h1.astype(BF16)


def _attn_body(x0_ref, m0_ref, x1_ref, m1_ref, xnext_ref, mnext_ref, k_ref, v_ref, wq32_ref, wo32_ref,
               lng_ref, lnb_ref, g1_ref, b1_ref, g2_ref, b2_ref, o_ref,
               h1_ref, q_ref, h1b_ref, s_ref, p_ref, att_ref, ca_ref, wq_ref, wo_ref):
    i = pl.program_id(0)
    slot = i % 2
    rows = o_ref.shape[0]
    norm_refs = (lng_ref, lnb_ref, g1_ref, b1_ref, h1_ref, h1b_ref)

    @pl.when(i == 0)
    def _():
        _cast_rows(wq32_ref, wq_ref)
        _cast_rows(wo32_ref, wo_ref)
        _attn_norms(x0_ref, m0_ref, *norm_refs, 0)
        q_ref[...] = _dot(h1b_ref[...], wq_ref[...]).astype(BF16)
        _attn_norms(x1_ref, m1_ref, *norm_refs, 1)

    head_slices = [slice(hd * CA_HEAD_DIM, (hd + 1) * CA_HEAD_DIM) for hd in range(CA_HEADS)]
    for hd, hs in enumerate(head_slices):
        s_ref[hd] = lax.dot_general(q_ref[:, hs], k_ref[:, hs], (((1,), (1,)), ((), ())),
                                    preferred_element_type=F32)
    q_ref[...] = _dot(h1b_ref[...], wq_ref[...]).astype(BF16)
    for r in _row_chunks(rows):
        for hd in range(CA_HEADS):
            s = s_ref[hd, r, :]
            e = jnp.exp(s - jnp.max(s, axis=-1, keepdims=True))
            p_ref[hd, r, :] = (e * (1.0 / jnp.sum(e, axis=-1, keepdims=True))).astype(BF16)
    for hd, hs in enumerate(head_slices):
        att_ref[:, hs] = _dot(p_ref[hd], v_ref[:, hs]).astype(BF16)
    ca_ref[...] = _dot(att_ref[...], wo_ref[...])
    for r in _row_chunks(rows):
        o_ref[r, :] = _layer_norm(ALPHA * h1_ref[slot, r, :] + ca_ref[r, :], g2_ref[...], b2_ref[...])
    _attn_norms(xnext_ref, mnext_ref, *norm_refs, slot)


def _attn(x, mix, k, v, w_q, w_o, ln_g, ln_b, g1, b1, g2, b2):
    bsz, seq, _ = x.shape
    mlen = k.shape[1]
    n_rows = bsz * seq
    n_blocks = n_rows // ATTN_ROWS
    blocks_per_batch = seq // ATTN_ROWS
    blk = (ATTN_ROWS, D_MODEL)
    first_blk = _resident_spec(blk, (0, 0))
    second_blk = _resident_spec(blk, (min(1, n_blocks - 1), 0))
    next_blk = pl.BlockSpec(blk, lambda i: (jnp.minimum(i + 2, n_blocks - 1), 0))
    mem_blk = pl.BlockSpec((None, mlen, D_MODEL), lambda i: (i // blocks_per_batch, 0, 0))
    consts = (w_q, w_o, ln_g, ln_b, g1, b1, g2, b2)
    x2 = x.reshape(n_rows, D_MODEL)
    y2 = mix.reshape(n_rows, D_MODEL)
    return pl.pallas_call(
        _attn_body,
        grid=(n_blocks,),
        in_specs=[first_blk, first_blk, second_blk, second_blk, next_blk, next_blk, mem_blk, mem_blk]
                 + [_resident_spec(c.shape) for c in consts],
        out_specs=pl.BlockSpec(blk, lambda i: (i, 0)),
        out_shape=jax.ShapeDtypeStruct((n_rows, D_MODEL), F32),
        scratch_shapes=[pltpu.VMEM((2, ATTN_ROWS, D_MODEL), F32),
                        pltpu.VMEM((ATTN_ROWS, D_MODEL), BF16),
                        pltpu.VMEM((ATTN_ROWS, D_MODEL), BF16),
                        pltpu.VMEM((CA_HEADS, ATTN_ROWS, mlen), F32),
                        pltpu.VMEM((CA_HEADS, ATTN_ROWS, mlen), BF16),
                        pltpu.VMEM((ATTN_ROWS, D_MODEL), BF16),
                        pltpu.VMEM((ATTN_ROWS, D_MODEL), F32),
                        pltpu.VMEM(w_q.shape, BF16),
                        pltpu.VMEM(w_o.shape, BF16)],
        compiler_params=pltpu.CompilerParams(
            dimension_semantics=("arbitrary",), vmem_limit_bytes=V7X_SCOPED_VMEM_BYTES),
        name="attn",
    )(x2, y2, x2, y2, x2, y2, k, v, *consts)


def _mlp_body(h_ref, w1_ref, w2_ref, g_ref, b_ref, o_ref):
    for r in _row_chunks(h_ref.shape[0], MLP_GROUP_ROWS):
        h = h_ref[r, :]
        hb = h.astype(BF16)
        ff = jnp.zeros(h.shape, F32)
        for j in range(D_FF // MLP_CHUNK):
            lo, hi = j * MLP_CHUNK, (j + 1) * MLP_CHUNK
            t = jnp.maximum(_dot(hb, w1_ref[:, lo:hi]), 0.0)
            ff = ff + _dot((t * t).astype(BF16), w2_ref[lo:hi, :])
        o_ref[r, :] = _layer_norm(ALPHA * h + ff, g_ref[...], b_ref[...])


def _mlp(h, w1, w2, g, b):
    n_rows = h.shape[0]
    row_blk = pl.BlockSpec((MLP_ROWS, D_MODEL), lambda i: (i, 0))
    return pl.pallas_call(
        _mlp_body,
        grid=(n_rows // MLP_ROWS,),
        in_specs=[row_blk, _resident_spec(w1.shape), _resident_spec(w2.shape),
                  _const_spec(g.shape), _const_spec(b.shape)],
        out_specs=row_blk,
        out_shape=jax.ShapeDtypeStruct(h.shape, F32),
        compiler_params=pltpu.CompilerParams(
            dimension_semantics=("arbitrary",), vmem_limit_bytes=V7X_SCOPED_VMEM_BYTES),
        name="mlp",
    )(h, w1, w2, g, b)


def kernel(x, mem, ln_in_g, ln_in_b, w_in, s5_a_re, s5_a_im, s5_log_dt, s5_b_re, s5_b_im, s5_c_re, s5_c_im, s5_d, s5_w_glu, s5_b_glu, conv_w, conv_b, lru_w_a, lru_b_a, lru_w_x, lru_b_x, lru_lambda, g_s5, g_lru, w_mix_out, ln1_g, ln1_b, mem_ln_g, mem_ln_b, w_q, w_k, w_v, w_o, ln2_g, ln2_b, w_ff1, w_ff2, ln3_g, ln3_b):
    bsz, seq, d_model = x.shape
    assert d_model == D_MODEL and w_in.shape == (DEPTH, D_MODEL, D_IN)
    assert seq % MIXER_STEPS == 0 and seq % ATTN_ROWS == 0 and (bsz * seq) % MLP_ROWS == 0
    assert bsz % 16 == 0, "time-major rows of one step must fill whole bf16 tiles"
    row = lambda v: v.reshape(1, -1).astype(F32)
    l = 0

    apow_re, apow_im, w_in_chunk, w_out_chunk, wg = _s5_prep(
        s5_a_re[l], s5_a_im[l], s5_log_dt[l], s5_b_re[l], s5_b_im[l], s5_c_re[l], s5_c_im[l],
        lru_w_a[l], lru_w_x[l])
    apow_re = apow_re.reshape(S5_BLOCKS, 1, S5_BLOCK_STATES)
    apow_im = apow_im.reshape(S5_BLOCKS, 1, S5_BLOCK_STATES)

    x_tm = jnp.swapaxes(x, 0, 1).reshape(seq * bsz, D_MODEL)
    x_tm, w_ff1_b, w_ff2_b = lax.optimization_barrier((x_tm, w_ff1[l].astype(BF16), w_ff2[l].astype(BF16)))
    mix_tm = _mixer(x_tm, bsz, row(ln_in_g), row(ln_in_b), w_in[l], w_in_chunk, apow_re, apow_im,
                    w_out_chunk,
                    row(s5_d[l]), s5_w_glu[l], row(s5_b_glu[l]), row(g_s5[l]),
                    conv_w[l].astype(F32), row(conv_b[l]), wg, row(lru_b_a[l]), row(lru_b_x[l]),
                    row(lru_lambda[l]), row(g_lru[l]), w_mix_out[l])
    mix_tm, mem_l, w_k_l, w_v_l = lax.optimization_barrier((mix_tm, mem, w_k[l], w_v[l]))
    mix = jnp.swapaxes(mix_tm.reshape(seq, bsz, D_MODEL), 0, 1)

    k, v = _kv(mem_l, row(mem_ln_g[l]), row(mem_ln_b[l]), w_k_l, w_v_l)
    h2 = _attn(x, mix, k, v, w_q[l], w_o[l],
               row(ln_in_g), row(ln_in_b), row(ln1_g[l]), row(ln1_b[l]), row(ln2_g[l]), row(ln2_b[l]))
    out = _mlp(h2, w_ff1_b, w_ff2_b, row(ln3_g[l]), row(ln3_b[l]))
    return out.reshape(bsz, seq, D_MODEL)
```

```python
import jax
import jax.numpy as jnp
from jax import lax
from jax.experimental import pallas as pl
from jax.experimental.pallas import tpu as pltpu

F32 = jnp.float32
BF16 = jnp.bfloat16

D_MODEL = 1024
D_S5 = 512
D_LRU = 512
D_IN = D_S5 + 2 * D_LRU
S5_GROUP = 16
S5_GROUPS = 32
S5_STATE = 64
LRU_HEADS = 8
LRU_HEAD_DIM = 64
CONV_WIDTH = 4
LRU_C = 8.0
D_FF = 4096
CA_HEADS = 4
CA_HEAD_DIM = 256
DEPTH = 1
ALPHA = (2 * DEPTH) ** 0.25
LN_EPS = 1e-5
RMS_EPS = 1e-6

S5_BLOCKS = 4
S5_BLOCK_CH = D_S5 // S5_BLOCKS
S5_BLOCK_STATES = 8 * S5_STATE
S5_CHUNK = 2
LRU_BLOCKS = 2
LRU_BLOCK_CH = D_LRU // LRU_BLOCKS

V7X_SCOPED_VMEM_BYTES = 56 * 1024 * 1024

MIXER_STEPS = 32
ATTN_ROWS = 512
MLP_ROWS = 1024
MLP_GROUP_ROWS = 512
KV_ROWS = 1024
MLP_CHUNK = 1024
VEC_ROWS = 16


def _layer_norm(x, g, b):
    mu = jnp.mean(x, axis=-1, keepdims=True)
    xc = x - mu
    var = jnp.mean(xc * xc, axis=-1, keepdims=True)
    return xc * lax.rsqrt(var + LN_EPS) * g + b


def _rms_norm(x, g):
    ms = jnp.mean(x * x, axis=-1, keepdims=True)
    return x * lax.rsqrt(ms + RMS_EPS) * g


def _gelu_tanh(x):
    c = 0.7978845608028654
    return 0.5 * x * (1.0 + jnp.tanh(c * (x + 0.044715 * (x * x * x))))


def _dot(a, b):
    return jnp.dot(a, b, preferred_element_type=F32)


def _const_spec(shape):
    zeros = (0,) * len(shape)
    return pl.BlockSpec(shape, lambda *_: zeros)


def _resident_spec(shape, block=None):
    zeros = (0,) * len(shape)
    index = zeros if block is None else block
    return pl.BlockSpec(shape, lambda *_: index, pipeline_mode=pl.Buffered(1))


def _row_chunks(rows, chunk=None):
    chunk = chunk or VEC_ROWS
    return [slice(r, r + chunk) for r in range(0, rows, chunk)]


def _cmul(a_re, a_im, b_re, b_im):
    return a_re * b_re - a_im * b_im, a_re * b_im + a_im * b_re


def _cmul(a_re, a_im, b_re, b_im):
    return a_re * b_re - a_im * b_im, a_re * b_im + a_im * b_re


def _discretise(a_re, a_im, log_dt):
    dt = jnp.exp(log_dt)
    mag = jnp.exp(dt * a_re)
    abar_re = mag * jnp.cos(dt * a_im)
    abar_im = mag * jnp.sin(dt * a_im)
    den = a_re * a_re + a_im * a_im
    q_re = ((abar_re - 1.0) * a_re + abar_im * a_im) / den
    q_im = (abar_im * a_re - (abar_re - 1.0) * a_im) / den
    return abar_re, abar_im, q_re, q_im


def _powers(abar_re, abar_im):
    pows = [(jnp.ones_like(abar_re), jnp.zeros_like(abar_im))]
    for _ in range(S5_CHUNK):
        pows.append(_cmul(pows[-1][0], pows[-1][1], abar_re, abar_im))
    return pows


def _block_diag_tiles(stacked, group_rows, group_cols, n_groups):
    rows = n_groups * group_rows
    cols = n_groups * group_cols
    row_shift = group_rows.bit_length() - 1
    col_shift = group_cols.bit_length() - 1
    assert group_rows == 1 << row_shift and group_cols == 1 << col_shift
    src = lax.broadcasted_iota(jnp.int32, (group_cols, cols), 0)
    dst = lax.broadcasted_iota(jnp.int32, (group_cols, cols), 1)
    replicate = jnp.where((dst & (group_cols - 1)) == src, 1.0, 0.0).astype(BF16)
    tiled = _dot(stacked.astype(BF16), replicate)
    row_group = lax.broadcasted_iota(jnp.int32, (rows, cols), 0) >> row_shift
    col_group = lax.broadcasted_iota(jnp.int32, (rows, cols), 1) >> col_shift
    mask = row_group == col_group
    return [jnp.where(mask, tiled[i * rows:(i + 1) * rows], 0.0).astype(BF16)
            for i in range(stacked.shape[0] // rows)]


def _s5_prep_body(are_ref, aim_ref, ldt_ref, arec_ref, aimc_ref, ldtc_ref, bre_ref, bim_ref, cre_ref, cim_ref,
                  cret_ref, cimt_ref, wa_ref, wx_ref,
                  apow_re_ref, apow_im_ref, wic_ref, woc_ref, wg_ref):
    g8 = S5_GROUPS // S5_BLOCKS
    abar_re, abar_im, q_re, q_im = _discretise(are_ref[...], aim_ref[...], ldt_ref[...])
    pows = _powers(abar_re, abar_im)
    apow_re_ref[...] = pows[S5_CHUNK][0]
    apow_im_ref[...] = pows[S5_CHUNK][1]
    bbar_re, bbar_im = _cmul(q_re, q_im, bre_ref[...], bim_ref[...])
    c_re = cre_ref[...]
    c_im = cim_ref[...]
    abar_re_c, abar_im_c, _, _ = _discretise(arec_ref[...], aimc_ref[...], ldtc_ref[...])
    pows_c = _powers(abar_re_c, abar_im_c)
    ct_re = cret_ref[...]
    ct_im = cimt_ref[...]

    lane = lax.broadcasted_iota(jnp.int32, (S5_GROUPS, S5_GROUP, S5_GROUP), 2)
    within = 2 * S5_BLOCK_STATES
    wic_ref[:, :, within:] = jnp.zeros((S5_BLOCKS, S5_CHUNK * S5_BLOCK_CH, S5_CHUNK * S5_BLOCK_CH), BF16)
    for j in range(S5_CHUNK):
        bt = _cmul(*pows[S5_CHUNK - 1 - j], bbar_re, bbar_im)
        et = _cmul(*pows_c[j + 1], ct_re, ct_im)
        ce_re, ce_im = _cmul(*pows[j], c_re, c_im)
        lag = jnp.zeros(lane.shape, F32)
        for ho in range(S5_GROUP):
            col = jnp.sum(bbar_re * ce_re[:, ho:ho + 1, :] - bbar_im * ce_im[:, ho:ho + 1, :],
                          axis=-1, keepdims=True)
            lag = jnp.where(lane == ho, col, lag)
        rows_j = slice(j * S5_BLOCK_CH, (j + 1) * S5_BLOCK_CH)
        for part in range(2):
            cols = slice(part * S5_BLOCK_STATES, (part + 1) * S5_BLOCK_STATES)
            in_tiles = _block_diag_tiles(bt[part].reshape(S5_GROUPS * S5_GROUP, S5_STATE), S5_GROUP, S5_STATE, g8)
            out_tiles = _block_diag_tiles(et[part].reshape(S5_GROUPS * S5_STATE, S5_GROUP), S5_STATE, S5_GROUP, g8)
            for k in range(S5_BLOCKS):
                wic_ref[k, rows_j, cols] = in_tiles[k]
                woc_ref[k, cols, rows_j] = out_tiles[k] if part == 0 else -out_tiles[k]
        lag_tiles = _block_diag_tiles(lag.reshape(S5_GROUPS * S5_GROUP, S5_GROUP), S5_GROUP, S5_GROUP, g8)
        for k in range(S5_BLOCKS):
            for t_in in range(S5_CHUNK - j):
                t_out = t_in + j
                wic_ref[k, t_in * S5_BLOCK_CH:(t_in + 1) * S5_BLOCK_CH,
                        within + t_out * S5_BLOCK_CH:within + (t_out + 1) * S5_BLOCK_CH] = lag_tiles[k]

    heads = LRU_HEADS // LRU_BLOCKS
    for part, w_ref in enumerate((wa_ref, wx_ref)):
        gate_tiles = _block_diag_tiles(w_ref[...].reshape(D_LRU, LRU_HEAD_DIM), LRU_HEAD_DIM, LRU_HEAD_DIM, heads)
        for kb in range(LRU_BLOCKS):
            wg_ref[kb, :, part * LRU_BLOCK_CH:(part + 1) * LRU_BLOCK_CH] = gate_tiles[kb]


def _s5_prep(a_re, a_im, log_dt, b_re, b_im, c_re, c_im, w_a, w_x):
    g, p, h = b_re.shape
    f = jax.ShapeDtypeStruct
    chunk_cols = S5_CHUNK * S5_BLOCK_CH
    return pl.pallas_call(
        _s5_prep_body,
        out_shape=(f((g, 1, p), F32), f((g, 1, p), F32),
                   f((S5_BLOCKS, chunk_cols, 2 * S5_BLOCK_STATES + chunk_cols), BF16),
                   f((S5_BLOCKS, 2 * S5_BLOCK_STATES, chunk_cols), BF16),
                   f((LRU_BLOCKS, LRU_BLOCK_CH, 2 * LRU_BLOCK_CH), BF16)),
        compiler_params=pltpu.CompilerParams(vmem_limit_bytes=V7X_SCOPED_VMEM_BYTES),
        name="s5_prep",
    )(a_re.reshape(g, 1, p), a_im.reshape(g, 1, p), log_dt.reshape(g, 1, 1),
      a_re.reshape(g, p, 1), a_im.reshape(g, p, 1), log_dt.reshape(g, 1, 1),
      jnp.swapaxes(b_re, 1, 2), jnp.swapaxes(b_im, 1, 2), c_re, c_im,
      jnp.swapaxes(c_re, 1, 2), jnp.swapaxes(c_im, 1, 2), w_a, w_x)


def _mixer_norm(x_ref, lng_ref, lnb_ref, h0b_ref):
    for r in _row_chunks(x_ref.shape[0]):
        h0b_ref[r, :] = _layer_norm(x_ref[r, :].astype(F32), lng_ref[...], lnb_ref[...]).astype(BF16)


def _mixer_project_in(h0b_ref, win_ref, z_ref, uc_ref, batch):
    for piece in range(D_IN // D_S5):
        cols = slice(piece * D_S5, (piece + 1) * D_S5)
        z = _dot(h0b_ref[...], win_ref[:, cols])
        z_ref[:, cols] = z
        if piece == 0:
            zb = z.astype(BF16)
            for step in range(z.shape[0] // batch):
                c, t = divmod(step, S5_CHUNK)
                for k in range(S5_BLOCKS):
                    uc_ref[k, c * batch:(c + 1) * batch, t * S5_BLOCK_CH:(t + 1) * S5_BLOCK_CH] = (
                        zb[step * batch:(step + 1) * batch, k * S5_BLOCK_CH:(k + 1) * S5_BLOCK_CH])


def _mixer_conv(z_ref, cw_ref, cb_ref, d_ref, xl_ref, xc_ref, xcb_ref, gz_ref, du_ref, batch):
    rows = z_ref.shape[0]
    halo = (CONV_WIDTH - 1) * batch
    xl_ref[halo:halo + rows, :] = z_ref[:, D_S5:D_S5 + D_LRU]
    for r in _row_chunks(rows, 2 * VEC_ROWS):
        xc = cb_ref[...]
        for j in range(CONV_WIDTH):
            xc = xc + xl_ref[j * batch + r.start:j * batch + r.stop, :] * cw_ref[j:j + 1, :]
        xc_ref[r, :] = xc
        xcb_ref[r, :] = xc.astype(BF16)
        gz_ref[r, :] = _gelu_tanh(z_ref[r, D_S5 + D_LRU:D_IN])
        du_ref[r, :] = d_ref[...] * z_ref[r, 0:D_S5]
    xl_ref[0:halo, :] = xl_ref[rows:rows + halo, :]


def _mixer_step(xnext_ref, lng_ref, lnb_ref, win_ref, wic_ref, apr_ref, api_ref, woc_ref, d_ref,
                wglu_ref, bglu_ref, gs5_ref, cw_ref, cb_ref, wg_ref, ba_ref, bx_ref, lam_ref, glru_ref, wmix_ref,
                mix_ref,
                z_ref, uc_ref, h0b_ref, inc_ref, hp_ref, st_ref, xl_ref, xc_ref, xcb_ref, pre_ref, a_ref, hx_ref,
                hl_ref, yo_ref, y32_ref, yb_ref, gate_ref, gz_ref, du_ref, y_ref):
    rows = y_ref.shape[0]
    batch = st_ref.shape[1]
    steps = rows // batch
    chunks = steps // S5_CHUNK
    wide = _row_chunks(rows, 2 * VEC_ROWS)
    sre = slice(0, S5_BLOCK_STATES)
    sim = slice(S5_BLOCK_STATES, 2 * S5_BLOCK_STATES)

    for k in range(S5_BLOCKS):
        inc_ref[k] = _dot(uc_ref[k], wic_ref[k])
    for kb in range(LRU_BLOCKS):
        lo, hi_ = kb * LRU_BLOCK_CH, (kb + 1) * LRU_BLOCK_CH
        pre = _dot(xcb_ref[:, lo:hi_], wg_ref[kb])
        pre_ref[:, lo:hi_] = pre[:, 0:LRU_BLOCK_CH]
        pre_ref[:, D_LRU + lo:D_LRU + hi_] = pre[:, LRU_BLOCK_CH:2 * LRU_BLOCK_CH]
    _mixer_project_in(h0b_ref, win_ref, z_ref, uc_ref, batch)

    for k in range(S5_BLOCKS):
        ar = jnp.broadcast_to(apr_ref[k], (batch, S5_BLOCK_STATES))
        ai = jnp.broadcast_to(api_ref[k], (batch, S5_BLOCK_STATES))
        hr = st_ref[k, :, sre]
        hi = st_ref[k, :, sim]
        for c in range(chunks):
            r = slice(c * batch, (c + 1) * batch)
            hp_ref[k, r, sre] = hr.astype(BF16)
            hp_ref[k, r, sim] = hi.astype(BF16)
            hr, hi = (ar * hr - ai * hi + inc_ref[k, r, sre],
                      ar * hi + ai * hr + inc_ref[k, r, sim])
        st_ref[k, :, sre] = hr
        st_ref[k, :, sim] = hi
        yo_ref[k] = _dot(hp_ref[k], woc_ref[k])
        if k == 1:
            lam = lam_ref[...]
            softplus_neg_lam = jnp.maximum(-lam, 0.0) + jnp.log1p(jnp.exp(-jnp.abs(lam)))
            for r in wide:
                gate_a = jax.nn.sigmoid(pre_ref[r, 0:D_LRU] + ba_ref[...])
                gate_x = jax.nn.sigmoid(pre_ref[r, D_LRU:2 * D_LRU] + bx_ref[...])
                log_a = -LRU_C * gate_a * softplus_neg_lam
                a = jnp.exp(log_a)
                m2 = -jnp.tanh(log_a) * (a * a + 1.0)
                mult = jnp.where(m2 > 0.0, m2 * lax.rsqrt(m2), 0.0)
                a_ref[r, :] = a
                hx_ref[r, :] = mult * (gate_x * xc_ref[r, :])
    hl = hl_ref[...]
    for t in range(steps):
        r = slice(t * batch, (t + 1) * batch)
        hl = a_ref[r, :] * hl + hx_ref[r, :]
        hx_ref[r, :] = hl
    hl_ref[...] = hl

    within = 2 * S5_BLOCK_STATES
    for step in range(steps):
        c, t = divmod(step, S5_CHUNK)
        r = slice(step * batch, (step + 1) * batch)
        rc = slice(c * batch, (c + 1) * batch)
        cols = slice(t * S5_BLOCK_CH, (t + 1) * S5_BLOCK_CH)
        ys = jnp.concatenate(
            [yo_ref[k, rc, cols] + inc_ref[k, rc, within + cols.start:within + cols.stop]
             for k in range(S5_BLOCKS)], axis=1)
        y = _gelu_tanh(ys + du_ref[r, :])
        y32_ref[r, :] = y
        yb_ref[r, :] = y.astype(BF16)
    gate_ref[...] = _dot(yb_ref[...], wglu_ref[...])
    for r in wide:
        y = y32_ref[r, :] * jax.nn.sigmoid(gate_ref[r, :] + bglu_ref[...])
        y_ref[r, 0:D_S5] = _rms_norm(y, gs5_ref[...]).astype(y_ref.dtype)
        ylru = hx_ref[r, :] * gz_ref[r, :]
        y_ref[r, D_S5:D_MODEL] = _rms_norm(ylru, glru_ref[...]).astype(y_ref.dtype)
    mix_ref[...] = _dot(y_ref[...], wmix_ref[...]).astype(mix_ref.dtype)

    _mixer_conv(z_ref, cw_ref, cb_ref, d_ref, xl_ref, xc_ref, xcb_ref, gz_ref, du_ref, batch)
    _mixer_norm(xnext_ref, lng_ref, lnb_ref, h0b_ref)


def _cast_rows(src_ref, dst_ref, chunk=64):
    for r in _row_chunks(src_ref.shape[0], chunk):
        dst_ref[r, :] = src_ref[r, :].astype(dst_ref.dtype)


def _mixer_body(xfirst_ref, xsecond_ref, xnext_ref, lng_ref, lnb_ref, win32_ref, *rest):
    back_consts = list(rest[:16])
    d_ref, cw_ref, cb_ref = back_consts[4], back_consts[8], back_consts[9]
    mix_ref = rest[16]
    scratch = rest[17:-3]
    win_ref, wglu_ref, wmix_ref = rest[-3:]
    wglu32_ref, wmix32_ref = back_consts[5], back_consts[15]
    back_consts[5], back_consts[15] = wglu_ref, wmix_ref
    z_ref, uc_ref, h0b_ref, _, _, st_ref, xl_ref, xc_ref, xcb_ref = scratch[:9]
    hl_ref, gz_ref, du_ref = scratch[12], scratch[17], scratch[18]
    batch = st_ref.shape[1]
    halo = (CONV_WIDTH - 1) * batch

    @pl.when(pl.program_id(0) == 0)
    def _():
        _cast_rows(win32_ref, win_ref)
        _cast_rows(wglu32_ref, wglu_ref)
        _cast_rows(wmix32_ref, wmix_ref)
        st_ref[...] = jnp.zeros_like(st_ref)
        hl_ref[...] = jnp.zeros_like(hl_ref)
        xl_ref[0:halo, :] = jnp.zeros((halo, D_LRU), F32)
        _mixer_norm(xfirst_ref, lng_ref, lnb_ref, h0b_ref)
        _mixer_project_in(h0b_ref, win_ref, z_ref, uc_ref, batch)
        _mixer_conv(z_ref, cw_ref, cb_ref, d_ref, xl_ref, xc_ref, xcb_ref, gz_ref, du_ref, batch)
        _mixer_norm(xsecond_ref, lng_ref, lnb_ref, h0b_ref)

    _mixer_step(xnext_ref, lng_ref, lnb_ref, win_ref, *back_consts, mix_ref, *scratch)


def _mixer(x_tm, batch, ln_g, ln_b, w_in, w_in_chunk, apow_re, apow_im, w_out_chunk, d, w_glu, b_glu, g_s5,
           conv_w, conv_b, wg, b_a, b_x, lam, g_lru, w_mix):
    n_rows = x_tm.shape[0]
    rows = MIXER_STEPS * batch
    n_blocks = n_rows // rows
    halo = (CONV_WIDTH - 1) * batch
    chunk_rows = rows // S5_CHUNK
    chunk_cols = S5_CHUNK * S5_BLOCK_CH
    consts = (ln_g, ln_b, w_in, w_in_chunk, apow_re, apow_im, w_out_chunk, d, w_glu, b_glu, g_s5,
              conv_w, conv_b, wg, b_a, b_x, lam, g_lru, w_mix)
    return pl.pallas_call(
        _mixer_body,
        grid=(n_blocks,),
        in_specs=[_resident_spec((rows, D_MODEL), (0, 0)),
                  _resident_spec((rows, D_MODEL), (min(1, n_blocks - 1), 0)),
                  pl.BlockSpec((rows, D_MODEL), lambda i: (jnp.minimum(i + 2, n_blocks - 1), 0))]
                 + [_resident_spec(c.shape) for c in consts],
        out_specs=pl.BlockSpec((rows, D_MODEL), lambda i: (i, 0)),
        out_shape=jax.ShapeDtypeStruct((n_rows, D_MODEL), BF16),
        scratch_shapes=[
            pltpu.VMEM((rows, D_IN), F32),
            pltpu.VMEM((S5_BLOCKS, chunk_rows, chunk_cols), BF16),
            pltpu.VMEM((rows, D_MODEL), BF16),
            pltpu.VMEM((S5_BLOCKS, chunk_rows, 2 * S5_BLOCK_STATES + chunk_cols), F32),
            pltpu.VMEM((S5_BLOCKS, chunk_rows, 2 * S5_BLOCK_STATES), BF16),
            pltpu.VMEM((S5_BLOCKS, batch, 2 * S5_BLOCK_STATES), F32),
            pltpu.VMEM((rows + halo, D_LRU), F32),
            pltpu.VMEM((rows, D_LRU), F32),
            pltpu.VMEM((rows, D_LRU), BF16),
            pltpu.VMEM((rows, 2 * D_LRU), F32),
            pltpu.VMEM((rows, D_LRU), F32),
            pltpu.VMEM((rows, D_LRU), F32),
            pltpu.VMEM((batch, D_LRU), F32),
            pltpu.VMEM((S5_BLOCKS, chunk_rows, chunk_cols), F32),
            pltpu.VMEM((rows, D_S5), F32),
            pltpu.VMEM((rows, D_S5), BF16),
            pltpu.VMEM((rows, D_S5), F32),
            pltpu.VMEM((rows, D_LRU), F32),
            pltpu.VMEM((rows, D_S5), F32),
            pltpu.VMEM((rows, D_MODEL), BF16),
            pltpu.VMEM(w_in.shape, BF16),
            pltpu.VMEM(w_glu.shape, BF16),
            pltpu.VMEM(w_mix.shape, BF16),
        ],
        compiler_params=pltpu.CompilerParams(
            dimension_semantics=("arbitrary",), vmem_limit_bytes=V7X_SCOPED_VMEM_BYTES),
        name="mixer",
    )(x_tm, x_tm, x_tm, *consts)


def _kv_body(mem_ref, g_ref, b_ref, wk32_ref, wv32_ref, k_ref, v_ref, wk_ref, wv_ref):
    @pl.when(pl.program_id(0) == 0)
    def _():
        _cast_rows(wk32_ref, wk_ref)
        _cast_rows(wv32_ref, wv_ref)

    mn = _layer_norm(mem_ref[...], g_ref[...], b_ref[...]).astype(BF16)
    k_ref[...] = (_dot(mn, wk_ref[...]) * (CA_HEAD_DIM ** -0.5)).astype(k_ref.dtype)
    v_ref[...] = _dot(mn, wv_ref[...]).astype(v_ref.dtype)


def _kv(mem, g, b, w_k, w_v):
    bsz, mlen, _ = mem.shape
    n_rows = bsz * mlen
    blk = pl.BlockSpec((KV_ROWS, D_MODEL), lambda i: (i, 0))
    out = jax.ShapeDtypeStruct((n_rows, D_MODEL), BF16)
    k, v = pl.pallas_call(
        _kv_body,
        grid=(n_rows // KV_ROWS,),
        in_specs=[blk, _const_spec(g.shape), _const_spec(b.shape), _resident_spec(w_k.shape),
                  _resident_spec(w_v.shape)],
        out_specs=(blk, blk),
        out_shape=(out, out),
        scratch_shapes=[pltpu.VMEM(w_k.shape, BF16), pltpu.VMEM(w_v.shape, BF16)],
        compiler_params=pltpu.CompilerParams(
            dimension_semantics=("arbitrary",), vmem_limit_bytes=V7X_SCOPED_VMEM_BYTES),
        name="kv",
    )(mem.reshape(n_rows, D_MODEL), g, b, w_k, w_v)
    return k.reshape(mem.shape), v.reshape(mem.shape)


def _attn_norms(x_ref, mix_ref, lng_ref, lnb_ref, g1_ref, b1_ref, h1_ref, h1b_ref, slot):
    for r in _row_chunks(x_ref.shape[0]):
        h0 = _layer_norm(x_ref[r, :], lng_ref[...], lnb_ref[...])
        h1 = _layer_norm(ALPHA * h0 + mix_ref[r, :], g1_ref[...], b1_ref[...])
        h1_ref[slot, r, :] = h1
        h1b_ref[r, :] = h1.astype(BF16)


def _attn_body(x0_ref, m0_ref, x1_ref, m1_ref, xnext_ref, mnext_ref, k_ref, v_ref, wq32_ref, wo32_ref,
               lng_ref, lnb_ref, g1_ref, b1_ref, o_ref,
               h1_ref, q_ref, h1b_ref, s_ref, p_ref, att_ref, ca_ref, wq_ref, wo_ref):
    i = pl.program_id(0)
    slot = i % 2
    rows = o_ref.shape[0]
    norm_refs = (lng_ref, lnb_ref, g1_ref, b1_ref, h1_ref, h1b_ref)

    @pl.when(i == 0)
    def _():
        _cast_rows(wq32_ref, wq_ref)
        _cast_rows(wo32_ref, wo_ref)
        _attn_norms(x0_ref, m0_ref, *norm_refs, 0)
        q_ref[...] = _dot(h1b_ref[...], wq_ref[...]).astype(BF16)
        _attn_norms(x1_ref, m1_ref, *norm_refs, 1)

    head_slices = [slice(hd * CA_HEAD_DIM, (hd + 1) * CA_HEAD_DIM) for hd in range(CA_HEADS)]
    for hd, hs in enumerate(head_slices):
        s_ref[hd] = lax.dot_general(q_ref[:, hs], k_ref[:, hs], (((1,), (1,)), ((), ())),
                                    preferred_element_type=F32)
    q_ref[...] = _dot(h1b_ref[...], wq_ref[...]).astype(BF16)
    for r in _row_chunks(rows):
        for hd in range(CA_HEADS):
            s = s_ref[hd, r, :]
            e = jnp.exp(s - jnp.max(s, axis=-1, keepdims=True))
            p_ref[hd, r, :] = (e * (1.0 / jnp.sum(e, axis=-1, keepdims=True))).astype(BF16)
    for hd, hs in enumerate(head_slices):
        att_ref[:, hs] = _dot(p_ref[hd], v_ref[:, hs]).astype(BF16)
    ca_ref[...] = _dot(att_ref[...], wo_ref[...])
    for r in _row_chunks(rows):
        o_ref[r, :] = ALPHA * h1_ref[slot, r, :] + ca_ref[r, :]
    _attn_norms(xnext_ref, mnext_ref, *norm_refs, slot)


def _attn(x, mix, k, v, w_q, w_o, ln_g, ln_b, g1, b1):
    bsz, seq, _ = x.shape
    mlen = k.shape[1]
    n_rows = bsz * seq
    n_blocks = n_rows // ATTN_ROWS
    blocks_per_batch = seq // ATTN_ROWS
    blk = (ATTN_ROWS, D_MODEL)
    first_blk = _resident_spec(blk, (0, 0))
    second_blk = _resident_spec(blk, (min(1, n_blocks - 1), 0))
    next_blk = pl.BlockSpec(blk, lambda i: (jnp.minimum(i + 2, n_blocks - 1), 0))
    mem_blk = pl.BlockSpec((None, mlen, D_MODEL), lambda i: (i // blocks_per_batch, 0, 0))
    consts = (w_q, w_o, ln_g, ln_b, g1, b1)
    x2 = x.reshape(n_rows, D_MODEL)
    y2 = mix.reshape(n_rows, D_MODEL)
    return pl.pallas_call(
        _attn_body,
        grid=(n_blocks,),
        in_specs=[first_blk, first_blk, second_blk, second_blk, next_blk, next_blk, mem_blk, mem_blk]
                 + [_resident_spec(c.shape) for c in consts],
        out_specs=pl.BlockSpec(blk, lambda i: (i, 0)),
        out_shape=jax.ShapeDtypeStruct((n_rows, D_MODEL), F32),
        scratch_shapes=[pltpu.VMEM((2, ATTN_ROWS, D_MODEL), F32),
                        pltpu.VMEM((ATTN_ROWS, D_MODEL), BF16),
                        pltpu.VMEM((ATTN_ROWS, D_MODEL), BF16),
                        pltpu.VMEM((CA_HEADS, ATTN_ROWS, mlen), F32),
                        pltpu.VMEM((CA_HEADS, ATTN_ROWS, mlen), BF16),
                        pltpu.VMEM((ATTN_ROWS, D_MODEL), BF16),
                        pltpu.VMEM((ATTN_ROWS, D_MODEL), F32),
                        pltpu.VMEM(w_q.shape, BF16),
                        pltpu.VMEM(w_o.shape, BF16)],
        compiler_params=pltpu.CompilerParams(
            dimension_semantics=("arbitrary",), vmem_limit_bytes=V7X_SCOPED_VMEM_BYTES),
        name="attn",
    )(x2, y2, x2, y2, x2, y2, k, v, *consts)


def _mlp_body(s_ref, w1_ref, w2_ref, g2_ref, b2_ref, g_ref, b_ref, o_ref):
    for r in _row_chunks(s_ref.shape[0], MLP_GROUP_ROWS):
        h = _layer_norm(s_ref[r, :], g2_ref[...], b2_ref[...])
        hb = h.astype(BF16)
        ff = jnp.zeros(h.shape, F32)
        for j in range(D_FF // MLP_CHUNK):
            lo, hi = j * MLP_CHUNK, (j + 1) * MLP_CHUNK
            t = jnp.maximum(_dot(hb, w1_ref[:, lo:hi]), 0.0)
            ff = ff + _dot((t * t).astype(BF16), w2_ref[lo:hi, :])
        o_ref[r, :] = _layer_norm(ALPHA * h + ff, g_ref[...], b_ref[...])


def _mlp(s, w1, w2, g2, b2, g, b):
    n_rows = s.shape[0]
    row_blk = pl.BlockSpec((MLP_ROWS, D_MODEL), lambda i: (i, 0))
    vecs = (g2, b2, g, b)
    return pl.pallas_call(
        _mlp_body,
        grid=(n_rows // MLP_ROWS,),
        in_specs=[row_blk, _resident_spec(w1.shape), _resident_spec(w2.shape)]
                 + [_const_spec(v.shape) for v in vecs],
        out_specs=row_blk,
        out_shape=jax.ShapeDtypeStruct(s.shape, F32),
        compiler_params=pltpu.CompilerParams(
            dimension_semantics=("arbitrary",), vmem_limit_bytes=V7X_SCOPED_VMEM_BYTES),
        name="mlp",
    )(s, w1, w2, *vecs)


def kernel(x, mem, ln_in_g, ln_in_b, w_in, s5_a_re, s5_a_im, s5_log_dt, s5_b_re, s5_b_im, s5_c_re, s5_c_im, s5_d, s5_w_glu, s5_b_glu, conv_w, conv_b, lru_w_a, lru_b_a, lru_w_x, lru_b_x, lru_lambda, g_s5, g_lru, w_mix_out, ln1_g, ln1_b, mem_ln_g, mem_ln_b, w_q, w_k, w_v, w_o, ln2_g, ln2_b, w_ff1, w_ff2, ln3_g, ln3_b):
    bsz, seq, d_model = x.shape
    assert d_model == D_MODEL and w_in.shape == (DEPTH, D_MODEL, D_IN)
    assert seq % MIXER_STEPS == 0 and seq % ATTN_ROWS == 0 and (bsz * seq) % MLP_ROWS == 0
    assert bsz % 16 == 0, "time-major rows of one step must fill whole bf16 tiles"
    row = lambda v: v.reshape(1, -1).astype(F32)
    l = 0

    apow_re, apow_im, w_in_chunk, w_out_chunk, wg = _s5_prep(
        s5_a_re[l], s5_a_im[l], s5_log_dt[l], s5_b_re[l], s5_b_im[l], s5_c_re[l], s5_c_im[l],
        lru_w_a[l], lru_w_x[l])
    apow_re = apow_re.reshape(S5_BLOCKS, 1, S5_BLOCK_STATES)
    apow_im = apow_im.reshape(S5_BLOCKS, 1, S5_BLOCK_STATES)

    x_tm = jnp.swapaxes(x.astype(BF16), 0, 1).reshape(seq * bsz, D_MODEL)
    x_tm, w_ff1_b, w_ff2_b = lax.optimization_barrier((x_tm, w_ff1[l].astype(BF16), w_ff2[l].astype(BF16)))
    mix_tm = _mixer(x_tm, bsz, row(ln_in_g), row(ln_in_b), w_in[l], w_in_chunk, apow_re, apow_im,
                    w_out_chunk,
                    row(s5_d[l]), s5_w_glu[l], row(s5_b_glu[l]), row(g_s5[l]),
                    conv_w[l].astype(F32), row(conv_b[l]), wg, row(lru_b_a[l]), row(lru_b_x[l]),
                    row(lru_lambda[l]), row(g_lru[l]), w_mix_out[l])
    mix_tm, mem_l, w_k_l, w_v_l = lax.optimization_barrier((mix_tm, mem, w_k[l], w_v[l]))
    mix = jnp.swapaxes(mix_tm.reshape(seq, bsz, D_MODEL), 0, 1)

    k, v = _kv(mem_l, row(mem_ln_g[l]), row(mem_ln_b[l]), w_k_l, w_v_l)
    pre2 = _attn(x, mix, k, v, w_q[l], w_o[l], row(ln_in_g), row(ln_in_b), row(ln1_g[l]), row(ln1_b[l]))
    out = _mlp(pre2, w_ff1_b, w_ff2_b, row(ln2_g[l]), row(ln2_b[l]), row(ln3_g[l]), row(ln3_b[l]))
    return out.reshape(bsz, seq, D_MODEL)
```

```python
import functools

import jax
import jax.numpy as jnp
from jax import lax
from jax.experimental import pallas as pl
from jax.experimental.pallas import tpu as pltpu

F32 = jnp.float32
BF16 = jnp.bfloat16

D_MODEL = 1024
D_S5 = 512
D_LRU = 512
D_IN = D_S5 + 2 * D_LRU
S5_GROUP = 16
S5_GROUPS = 32
S5_STATE = 64
LRU_HEADS = 8
LRU_HEAD_DIM = 64
CONV_WIDTH = 4
LRU_C = 8.0
D_FF = 4096
CA_HEADS = 4
CA_HEAD_DIM = 256
DEPTH = 1
ALPHA = (2 * DEPTH) ** 0.25
LN_EPS = 1e-5
RMS_EPS = 1e-6

S5_BLOCKS = 4
S5_BLOCK_CH = D_S5 // S5_BLOCKS
S5_BLOCK_STATES = 8 * S5_STATE
S5_CHUNK = 2
LRU_BLOCKS = 2
LRU_BLOCK_CH = D_LRU // LRU_BLOCKS

V7X_SCOPED_VMEM_BYTES = 56 * 1024 * 1024

MIXER_STEPS = 32
ATTN_ROWS = 512
MLP_ROWS = 1024
MLP_GROUP_ROWS = 512
KV_ROWS = 1024
MLP_CHUNK = 1024
VEC_ROWS = 16


def _layer_norm(x, g, b):
    mu = jnp.mean(x, axis=-1, keepdims=True)
    xc = x - mu
    var = jnp.mean(xc * xc, axis=-1, keepdims=True)
    return xc * lax.rsqrt(var + LN_EPS) * g + b


def _rms_norm(x, g):
    ms = jnp.mean(x * x, axis=-1, keepdims=True)
    return x * lax.rsqrt(ms + RMS_EPS) * g


def _gelu_tanh(x):
    c = 0.7978845608028654
    return 0.5 * x * (1.0 + jnp.tanh(c * (x + 0.044715 * (x * x * x))))


def _dot(a, b):
    return jnp.dot(a, b, preferred_element_type=F32)


def _const_spec(shape):
    zeros = (0,) * len(shape)
    return pl.BlockSpec(shape, lambda *_: zeros)


def _resident_spec(shape, block=None):
    zeros = (0,) * len(shape)
    index = zeros if block is None else block
    return pl.BlockSpec(shape, lambda *_: index, pipeline_mode=pl.Buffered(1))


def _row_chunks(rows, chunk=None):
    chunk = chunk or VEC_ROWS
    return [slice(r, r + chunk) for r in range(0, rows, chunk)]


def _cmul(a_re, a_im, b_re, b_im):
    return a_re * b_re - a_im * b_im, a_re * b_im + a_im * b_re


def _discretise(a_re, a_im, log_dt):
    dt = jnp.exp(log_dt)
    mag = jnp.exp(dt * a_re)
    abar_re = mag * jnp.cos(dt * a_im)
    abar_im = mag * jnp.sin(dt * a_im)
    den = a_re * a_re + a_im * a_im
    q_re = ((abar_re - 1.0) * a_re + abar_im * a_im) / den
    q_im = (abar_im * a_re - (abar_re - 1.0) * a_im) / den
    return abar_re, abar_im, q_re, q_im


def _powers(abar_re, abar_im):
    pows = [(jnp.ones_like(abar_re), jnp.zeros_like(abar_im))]
    for _ in range(S5_CHUNK):
        pows.append(_cmul(pows[-1][0], pows[-1][1], abar_re, abar_im))
    return pows


def _block_diag_tiles(stacked, group_rows, group_cols, n_groups):
    rows = n_groups * group_rows
    cols = n_groups * group_cols
    row_shift = group_rows.bit_length() - 1
    col_shift = group_cols.bit_length() - 1
    assert group_rows == 1 << row_shift and group_cols == 1 << col_shift
    src = lax.broadcasted_iota(jnp.int32, (group_cols, cols), 0)
    dst = lax.broadcasted_iota(jnp.int32, (group_cols, cols), 1)
    replicate = jnp.where((dst & (group_cols - 1)) == src, 1.0, 0.0).astype(BF16)
    tiled = _dot(stacked.astype(BF16), replicate)
    row_group = lax.broadcasted_iota(jnp.int32, (rows, cols), 0) >> row_shift
    col_group = lax.broadcasted_iota(jnp.int32, (rows, cols), 1) >> col_shift
    mask = row_group == col_group
    return [jnp.where(mask, tiled[i * rows:(i + 1) * rows], 0.0).astype(BF16)
            for i in range(stacked.shape[0] // rows)]


def _s5_prep_body(are_ref, aim_ref, ldt_ref, arec_ref, aimc_ref, ldtc_ref, bre_ref, bim_ref, cre_ref, cim_ref,
                  cret_ref, cimt_ref, wa_ref, wx_ref,
                  apow_re_ref, apow_im_ref, wic_ref, woc_ref, wg_ref):
    g8 = S5_GROUPS // S5_BLOCKS
    abar_re, abar_im, q_re, q_im = _discretise(are_ref[...], aim_ref[...], ldt_ref[...])
    pows = _powers(abar_re, abar_im)
    apow_re_ref[...] = pows[S5_CHUNK][0]
    apow_im_ref[...] = pows[S5_CHUNK][1]
    bbar_re, bbar_im = _cmul(q_re, q_im, bre_ref[...], bim_ref[...])
    c_re = cre_ref[...]
    c_im = cim_ref[...]
    abar_re_c, abar_im_c, _, _ = _discretise(arec_ref[...], aimc_ref[...], ldtc_ref[...])
    pows_c = _powers(abar_re_c, abar_im_c)
    ct_re = cret_ref[...]
    ct_im = cimt_ref[...]

    lane = lax.broadcasted_iota(jnp.int32, (S5_GROUPS, S5_GROUP, S5_GROUP), 2)
    within = 2 * S5_BLOCK_STATES
    wic_ref[:, :, within:] = jnp.zeros((S5_BLOCKS, S5_CHUNK * S5_BLOCK_CH, S5_CHUNK * S5_BLOCK_CH), BF16)
    for j in range(S5_CHUNK):
        bt = _cmul(*pows[S5_CHUNK - 1 - j], bbar_re, bbar_im)
        et = _cmul(*pows_c[j + 1], ct_re, ct_im)
        ce_re, ce_im = _cmul(*pows[j], c_re, c_im)
        lag = jnp.zeros(lane.shape, F32)
        for ho in range(S5_GROUP):
            col = jnp.sum(bbar_re * ce_re[:, ho:ho + 1, :] - bbar_im * ce_im[:, ho:ho + 1, :],
                          axis=-1, keepdims=True)
            lag = jnp.where(lane == ho, col, lag)
        rows_j = slice(j * S5_BLOCK_CH, (j + 1) * S5_BLOCK_CH)
        for part in range(2):
            cols = slice(part * S5_BLOCK_STATES, (part + 1) * S5_BLOCK_STATES)
            in_tiles = _block_diag_tiles(bt[part].reshape(S5_GROUPS * S5_GROUP, S5_STATE), S5_GROUP, S5_STATE, g8)
            out_tiles = _block_diag_tiles(et[part].reshape(S5_GROUPS * S5_STATE, S5_GROUP), S5_STATE, S5_GROUP, g8)
            for k in range(S5_BLOCKS):
                wic_ref[k, rows_j, cols] = in_tiles[k]
                woc_ref[k, cols, rows_j] = out_tiles[k] if part == 0 else -out_tiles[k]
        lag_tiles = _block_diag_tiles(lag.reshape(S5_GROUPS * S5_GROUP, S5_GROUP), S5_GROUP, S5_GROUP, g8)
        for k in range(S5_BLOCKS):
            for t_in in range(S5_CHUNK - j):
                t_out = t_in + j
                wic_ref[k, t_in * S5_BLOCK_CH:(t_in + 1) * S5_BLOCK_CH,
                        within + t_out * S5_BLOCK_CH:within + (t_out + 1) * S5_BLOCK_CH] = lag_tiles[k]

    heads = LRU_HEADS // LRU_BLOCKS
    for part, w_ref in enumerate((wa_ref, wx_ref)):
        gate_tiles = _block_diag_tiles(w_ref[...].reshape(D_LRU, LRU_HEAD_DIM), LRU_HEAD_DIM, LRU_HEAD_DIM, heads)
        for kb in range(LRU_BLOCKS):
            wg_ref[kb, :, part * LRU_BLOCK_CH:(part + 1) * LRU_BLOCK_CH] = gate_tiles[kb]


def _s5_prep(a_re, a_im, log_dt, b_re, b_im, c_re, c_im, w_a, w_x):
    g, p, h = b_re.shape
    f = jax.ShapeDtypeStruct
    chunk_cols = S5_CHUNK * S5_BLOCK_CH
    return pl.pallas_call(
        _s5_prep_body,
        out_shape=(f((g, 1, p), F32), f((g, 1, p), F32),
                   f((S5_BLOCKS, chunk_cols, 2 * S5_BLOCK_STATES + chunk_cols), BF16),
                   f((S5_BLOCKS, 2 * S5_BLOCK_STATES, chunk_cols), BF16),
                   f((LRU_BLOCKS, LRU_BLOCK_CH, 2 * LRU_BLOCK_CH), BF16)),
        compiler_params=pltpu.CompilerParams(vmem_limit_bytes=V7X_SCOPED_VMEM_BYTES),
        name="s5_prep",
    )(a_re.reshape(g, 1, p), a_im.reshape(g, 1, p), log_dt.reshape(g, 1, 1),
      a_re.reshape(g, p, 1), a_im.reshape(g, p, 1), log_dt.reshape(g, 1, 1),
      jnp.swapaxes(b_re, 1, 2), jnp.swapaxes(b_im, 1, 2), c_re, c_im,
      jnp.swapaxes(c_re, 1, 2), jnp.swapaxes(c_im, 1, 2), w_a, w_x)


def _mixer_norm(x_ref, lng_ref, lnb_ref, h0b_ref):
    for r in _row_chunks(x_ref.shape[0]):
        h0b_ref[r, :] = _layer_norm(x_ref[r, :], lng_ref[...], lnb_ref[...]).astype(BF16)


def _mixer_project_in(h0b_ref, win_ref, z_ref, uc_ref, batch):
    for piece in range(D_IN // D_S5):
        cols = slice(piece * D_S5, (piece + 1) * D_S5)
        z = _dot(h0b_ref[...], win_ref[:, cols])
        z_ref[:, cols] = z
        if piece == 0:
            zb = z.astype(BF16)
            for step in range(z.shape[0] // batch):
                c, t = divmod(step, S5_CHUNK)
                for k in range(S5_BLOCKS):
                    uc_ref[k, c * batch:(c + 1) * batch, t * S5_BLOCK_CH:(t + 1) * S5_BLOCK_CH] = (
                        zb[step * batch:(step + 1) * batch, k * S5_BLOCK_CH:(k + 1) * S5_BLOCK_CH])


def _mixer_conv(z_ref, cw_ref, cb_ref, d_ref, xl_ref, xc_ref, xcb_ref, gz_ref, du_ref, batch):
    rows = z_ref.shape[0]
    halo = (CONV_WIDTH - 1) * batch
    xl_ref[halo:halo + rows, :] = z_ref[:, D_S5:D_S5 + D_LRU]
    for r in _row_chunks(rows, 2 * VEC_ROWS):
        xc = cb_ref[...]
        for j in range(CONV_WIDTH):
            xc = xc + xl_ref[j * batch + r.start:j * batch + r.stop, :] * cw_ref[j:j + 1, :]
        xc_ref[r, :] = xc
        xcb_ref[r, :] = xc.astype(BF16)
        gz_ref[r, :] = _gelu_tanh(z_ref[r, D_S5 + D_LRU:D_IN])
        du_ref[r, :] = d_ref[...] * z_ref[r, 0:D_S5]
    xl_ref[0:halo, :] = xl_ref[rows:rows + halo, :]


def _mixer_step(xnext_ref, lng_ref, lnb_ref, win_ref, wic_ref, apr_ref, api_ref, woc_ref, d_ref,
                wglu_ref, bglu_ref, gs5_ref, cw_ref, cb_ref, wg_ref, ba_ref, bx_ref, lam_ref, glru_ref, wmix_ref,
                mix_ref,
                z_ref, uc_ref, h0b_ref, inc_ref, hp_ref, st_ref, xl_ref, xc_ref, xcb_ref, pre_ref, a_ref, hx_ref,
                hl_ref, yo_ref, y32_ref, yb_ref, gate_ref, gz_ref, du_ref, y_ref):
    rows = y_ref.shape[0]
    batch = st_ref.shape[1]
    steps = rows // batch
    chunks = steps // S5_CHUNK
    wide = _row_chunks(rows, 2 * VEC_ROWS)
    sre = slice(0, S5_BLOCK_STATES)
    sim = slice(S5_BLOCK_STATES, 2 * S5_BLOCK_STATES)

    for k in range(S5_BLOCKS):
        inc_ref[k] = _dot(uc_ref[k], wic_ref[k])
    for kb in range(LRU_BLOCKS):
        lo, hi_ = kb * LRU_BLOCK_CH, (kb + 1) * LRU_BLOCK_CH
        pre = _dot(xcb_ref[:, lo:hi_], wg_ref[kb])
        pre_ref[:, lo:hi_] = pre[:, 0:LRU_BLOCK_CH]
        pre_ref[:, D_LRU + lo:D_LRU + hi_] = pre[:, LRU_BLOCK_CH:2 * LRU_BLOCK_CH]
    _mixer_project_in(h0b_ref, win_ref, z_ref, uc_ref, batch)

    for k in range(S5_BLOCKS):
        ar = jnp.broadcast_to(apr_ref[k], (batch, S5_BLOCK_STATES))
        ai = jnp.broadcast_to(api_ref[k], (batch, S5_BLOCK_STATES))
        hr = st_ref[k, :, sre]
        hi = st_ref[k, :, sim]
        for c in range(chunks):
            r = slice(c * batch, (c + 1) * batch)
            hp_ref[k, r, sre] = hr.astype(BF16)
            hp_ref[k, r, sim] = hi.astype(BF16)
            hr, hi = (ar * hr - ai * hi + inc_ref[k, r, sre],
                      ar * hi + ai * hr + inc_ref[k, r, sim])
        st_ref[k, :, sre] = hr
        st_ref[k, :, sim] = hi
        yo_ref[k] = _dot(hp_ref[k], woc_ref[k])
        if k == 1:
            lam = lam_ref[...]
            softplus_neg_lam = jnp.maximum(-lam, 0.0) + jnp.log1p(jnp.exp(-jnp.abs(lam)))
            for r in wide:
                gate_a = jax.nn.sigmoid(pre_ref[r, 0:D_LRU] + ba_ref[...])
                gate_x = jax.nn.sigmoid(pre_ref[r, D_LRU:2 * D_LRU] + bx_ref[...])
                log_a = -LRU_C * gate_a * softplus_neg_lam
                a = jnp.exp(log_a)
                m2 = -jnp.tanh(log_a) * (a * a + 1.0)
                mult = jnp.where(m2 > 0.0, m2 * lax.rsqrt(m2), 0.0)
                a_ref[r, :] = a
                hx_ref[r, :] = mult * (gate_x * xc_ref[r, :])
    hl = hl_ref[...]
    for t in range(steps):
        r = slice(t * batch, (t + 1) * batch)
        hl = a_ref[r, :] * hl + hx_ref[r, :]
        hx_ref[r, :] = hl
    hl_ref[...] = hl

    within = 2 * S5_BLOCK_STATES
    for step in range(steps):
        c, t = divmod(step, S5_CHUNK)
        r = slice(step * batch, (step + 1) * batch)
        rc = slice(c * batch, (c + 1) * batch)
        cols = slice(t * S5_BLOCK_CH, (t + 1) * S5_BLOCK_CH)
        ys = jnp.concatenate(
            [yo_ref[k, rc, cols] + inc_ref[k, rc, within + cols.start:within + cols.stop]
             for k in range(S5_BLOCKS)], axis=1)
        y = _gelu_tanh(ys + du_ref[r, :])
        y32_ref[r, :] = y
        yb_ref[r, :] = y.astype(BF16)
    gate_ref[...] = _dot(yb_ref[...], wglu_ref[...])
    for r in wide:
        y = y32_ref[r, :] * jax.nn.sigmoid(gate_ref[r, :] + bglu_ref[...])
        y_ref[r, 0:D_S5] = _rms_norm(y, gs5_ref[...]).astype(y_ref.dtype)
        ylru = hx_ref[r, :] * gz_ref[r, :]
        y_ref[r, D_S5:D_MODEL] = _rms_norm(ylru, glru_ref[...]).astype(y_ref.dtype)
    mix_ref[...] = _dot(y_ref[...], wmix_ref[...]).astype(mix_ref.dtype)

    _mixer_conv(z_ref, cw_ref, cb_ref, d_ref, xl_ref, xc_ref, xcb_ref, gz_ref, du_ref, batch)
    _mixer_norm(xnext_ref, lng_ref, lnb_ref, h0b_ref)


def _cast_rows(src_ref, dst_ref, chunk=64):
    for r in _row_chunks(src_ref.shape[0], chunk):
        dst_ref[r, :] = src_ref[r, :].astype(dst_ref.dtype)


def _mixer_body(xfirst_ref, xsecond_ref, xnext_ref, st_in_ref, hl_in_ref, halo_in_ref, lng_ref, lnb_ref, win32_ref,
                *rest):
    back_consts = list(rest[:16])
    d_ref, cw_ref, cb_ref = back_consts[4], back_consts[8], back_consts[9]
    mix_ref, st_out_ref, hl_out_ref, halo_out_ref = rest[16:20]
    scratch = rest[20:-3]
    win_ref, wglu_ref, wmix_ref = rest[-3:]
    wglu32_ref, wmix32_ref = back_consts[5], back_consts[15]
    back_consts[5], back_consts[15] = wglu_ref, wmix_ref
    z_ref, uc_ref, h0b_ref, _, _, st_ref, xl_ref, xc_ref, xcb_ref = scratch[:9]
    hl_ref, gz_ref, du_ref = scratch[12], scratch[17], scratch[18]
    batch = st_ref.shape[1]
    halo = (CONV_WIDTH - 1) * batch

    @pl.when(pl.program_id(0) == 0)
    def _():
        _cast_rows(win32_ref, win_ref)
        _cast_rows(wglu32_ref, wglu_ref)
        _cast_rows(wmix32_ref, wmix_ref)
        st_ref[...] = st_in_ref[...]
        hl_ref[...] = hl_in_ref[...]
        xl_ref[0:halo, :] = halo_in_ref[...]
        _mixer_norm(xfirst_ref, lng_ref, lnb_ref, h0b_ref)
        _mixer_project_in(h0b_ref, win_ref, z_ref, uc_ref, batch)
        _mixer_conv(z_ref, cw_ref, cb_ref, d_ref, xl_ref, xc_ref, xcb_ref, gz_ref, du_ref, batch)
        _mixer_norm(xsecond_ref, lng_ref, lnb_ref, h0b_ref)

    _mixer_step(xnext_ref, lng_ref, lnb_ref, win_ref, *back_consts, mix_ref, *scratch)

    @pl.when(pl.program_id(0) == pl.num_programs(0) - 1)
    def _():
        st_out_ref[...] = st_ref[...]
        hl_out_ref[...] = hl_ref[...]
        halo_out_ref[...] = xl_ref[0:halo, :]


def _mixer_initial_state(batch):
    return (jnp.zeros((S5_BLOCKS, batch, 2 * S5_BLOCK_STATES), F32), jnp.zeros((batch, D_LRU), F32),
            jnp.zeros(((CONV_WIDTH - 1) * batch, D_LRU), F32))


def _mixer(x_tm, state, ln_g, ln_b, w_in, w_in_chunk, apow_re, apow_im, w_out_chunk, d, w_glu, b_glu, g_s5,
           conv_w, conv_b, wg, b_a, b_x, lam, g_lru, w_mix):
    batch = state[1].shape[0]
    n_rows = x_tm.shape[0]
    rows = MIXER_STEPS * batch
    n_blocks = n_rows // rows
    halo = (CONV_WIDTH - 1) * batch
    chunk_rows = rows // S5_CHUNK
    chunk_cols = S5_CHUNK * S5_BLOCK_CH
    consts = (ln_g, ln_b, w_in, w_in_chunk, apow_re, apow_im, w_out_chunk, d, w_glu, b_glu, g_s5,
              conv_w, conv_b, wg, b_a, b_x, lam, g_lru, w_mix)
    mix, *new_state = pl.pallas_call(
        _mixer_body,
        grid=(n_blocks,),
        in_specs=[_resident_spec((rows, D_MODEL), (0, 0)),
                  _resident_spec((rows, D_MODEL), (min(1, n_blocks - 1), 0)),
                  pl.BlockSpec((rows, D_MODEL), lambda i: (jnp.minimum(i + 2, n_blocks - 1), 0))]
                 + [_resident_spec(c.shape) for c in state + consts],
        out_specs=[pl.BlockSpec((rows, D_MODEL), lambda i: (i, 0))] + [_const_spec(c.shape) for c in state],
        out_shape=[jax.ShapeDtypeStruct((n_rows, D_MODEL), BF16)]
                  + [jax.ShapeDtypeStruct(c.shape, c.dtype) for c in state],
        scratch_shapes=[
            pltpu.VMEM((rows, D_IN), F32),
            pltpu.VMEM((S5_BLOCKS, chunk_rows, chunk_cols), BF16),
            pltpu.VMEM((rows, D_MODEL), BF16),
            pltpu.VMEM((S5_BLOCKS, chunk_rows, 2 * S5_BLOCK_STATES + chunk_cols), F32),
            pltpu.VMEM((S5_BLOCKS, chunk_rows, 2 * S5_BLOCK_STATES), BF16),
            pltpu.VMEM((S5_BLOCKS, batch, 2 * S5_BLOCK_STATES), F32),
            pltpu.VMEM((rows + halo, D_LRU), F32),
            pltpu.VMEM((rows, D_LRU), F32),
            pltpu.VMEM((rows, D_LRU), BF16),
            pltpu.VMEM((rows, 2 * D_LRU), F32),
            pltpu.VMEM((rows, D_LRU), F32),
            pltpu.VMEM((rows, D_LRU), F32),
            pltpu.VMEM((batch, D_LRU), F32),
            pltpu.VMEM((S5_BLOCKS, chunk_rows, chunk_cols), F32),
            pltpu.VMEM((rows, D_S5), F32),
            pltpu.VMEM((rows, D_S5), BF16),
            pltpu.VMEM((rows, D_S5), F32),
            pltpu.VMEM((rows, D_LRU), F32),
            pltpu.VMEM((rows, D_S5), F32),
            pltpu.VMEM((rows, D_MODEL), BF16),
            pltpu.VMEM(w_in.shape, BF16),
            pltpu.VMEM(w_glu.shape, BF16),
            pltpu.VMEM(w_mix.shape, BF16),
        ],
        compiler_params=pltpu.CompilerParams(
            dimension_semantics=("arbitrary",), vmem_limit_bytes=V7X_SCOPED_VMEM_BYTES),
        name="mixer",
    )(x_tm, x_tm, x_tm, *state, *consts)
    return mix, tuple(new_state)


def _kv_body(mem_ref, g_ref, b_ref, wk32_ref, wv32_ref, k_ref, v_ref, wk_ref, wv_ref):
    @pl.when(pl.program_id(0) == 0)
    def _():
        _cast_rows(wk32_ref, wk_ref)
        _cast_rows(wv32_ref, wv_ref)

    mn = _layer_norm(mem_ref[...], g_ref[...], b_ref[...]).astype(BF16)
    k_ref[...] = (_dot(mn, wk_ref[...]) * (CA_HEAD_DIM ** -0.5)).astype(k_ref.dtype)
    v_ref[...] = _dot(mn, wv_ref[...]).astype(v_ref.dtype)


def _kv(mem, g, b, w_k, w_v):
    bsz, mlen, _ = mem.shape
    n_rows = bsz * mlen
    blk = pl.BlockSpec((KV_ROWS, D_MODEL), lambda i: (i, 0))
    out = jax.ShapeDtypeStruct((n_rows, D_MODEL), BF16)
    k, v = pl.pallas_call(
        _kv_body,
        grid=(n_rows // KV_ROWS,),
        in_specs=[blk, _const_spec(g.shape), _const_spec(b.shape), _resident_spec(w_k.shape),
                  _resident_spec(w_v.shape)],
        out_specs=(blk, blk),
        out_shape=(out, out),
        scratch_shapes=[pltpu.VMEM(w_k.shape, BF16), pltpu.VMEM(w_v.shape, BF16)],
        compiler_params=pltpu.CompilerParams(
            dimension_semantics=("arbitrary",), vmem_limit_bytes=V7X_SCOPED_VMEM_BYTES),
        name="kv",
    )(mem.reshape(n_rows, D_MODEL), g, b, w_k, w_v)
    return k.reshape(mem.shape), v.reshape(mem.shape)


def _attn_norms(x_ref, mix, lng_ref, lnb_ref, g1_ref, b1_ref, h1_ref, h1b_ref, slot):
    for r in _row_chunks(x_ref.shape[0]):
        h0 = _layer_norm(x_ref[r, :], lng_ref[...], lnb_ref[...])
        h1 = _layer_norm(ALPHA * h0 + mix(r), g1_ref[...], b1_ref[...])
        h1_ref[slot, r, :] = h1
        h1b_ref[r, :] = h1.astype(BF16)


def _attn_body(blocks_per_batch, x0_ref, m0_ref, x1_ref, m1_ref, xnext_ref, mnext_early_ref, mnext_late_ref,
               k_ref, v_ref, wq32_ref, wo32_ref, lng_ref, lnb_ref, g1_ref, b1_ref, o_ref,
               h1_ref, q_ref, h1b_ref, s_ref, p_ref, att_ref, ca_ref, wq_ref, wo_ref):
    i = pl.program_id(0)
    slot = i % 2
    rows = o_ref.shape[0]
    norm_refs = (lng_ref, lnb_ref, g1_ref, b1_ref, h1_ref, h1b_ref)

    @pl.when(i == 0)
    def _():
        _cast_rows(wq32_ref, wq_ref)
        _cast_rows(wo32_ref, wo_ref)
        _attn_norms(x0_ref, lambda r: m0_ref[r, :], *norm_refs, 0)
        q_ref[...] = _dot(h1b_ref[...], wq_ref[...]).astype(BF16)
        _attn_norms(x1_ref, lambda r: m1_ref[r, :], *norm_refs, 1)

    head_slices = [slice(hd * CA_HEAD_DIM, (hd + 1) * CA_HEAD_DIM) for hd in range(CA_HEADS)]
    for hd, hs in enumerate(head_slices):
        s_ref[hd] = lax.dot_general(q_ref[:, hs], k_ref[:, hs], (((1,), (1,)), ((), ())),
                                    preferred_element_type=F32)
    q_ref[...] = _dot(h1b_ref[...], wq_ref[...]).astype(BF16)
    for r in _row_chunks(rows):
        for hd in range(CA_HEADS):
            s = s_ref[hd, r, :]
            e = jnp.exp(s - jnp.max(s, axis=-1, keepdims=True))
            p_ref[hd, r, :] = (e * (1.0 / jnp.sum(e, axis=-1, keepdims=True))).astype(BF16)
    for hd, hs in enumerate(head_slices):
        att_ref[:, hs] = _dot(p_ref[hd], v_ref[:, hs]).astype(BF16)
    ca_ref[...] = _dot(att_ref[...], wo_ref[...])
    for r in _row_chunks(rows):
        o_ref[r, :] = ALPHA * h1_ref[slot, r, :] + ca_ref[r, :]
    next_block = jnp.minimum(i + 2, pl.num_programs(0) - 1)
    is_early = next_block % blocks_per_batch < blocks_per_batch // 2
    _attn_norms(xnext_ref, lambda r: jnp.where(is_early, mnext_early_ref[r, :], mnext_late_ref[r, :]),
                *norm_refs, slot)


def _attn(x, mix_early, mix_late, k, v, w_q, w_o, ln_g, ln_b, g1, b1):
    bsz, seq, _ = x.shape
    mlen = k.shape[1]
    n_rows = bsz * seq
    n_blocks = n_rows // ATTN_ROWS
    blocks_per_batch = seq // ATTN_ROWS
    blocks_per_half = blocks_per_batch // 2
    assert blocks_per_half >= 2 and mix_early.shape == mix_late.shape == (bsz, seq // 2, D_MODEL)
    blk = (ATTN_ROWS, D_MODEL)
    first_blk = _resident_spec(blk, (0, 0))
    second_blk = _resident_spec(blk, (1, 0))
    next_blk = pl.BlockSpec(blk, lambda i: (jnp.minimum(i + 2, n_blocks - 1), 0))

    def next_half_blk(i):
        block = jnp.minimum(i + 2, n_blocks - 1)
        return (block // blocks_per_batch * blocks_per_half + block % blocks_per_half, 0)

    mem_blk = pl.BlockSpec((None, mlen, D_MODEL), lambda i: (i // blocks_per_batch, 0, 0))
    consts = (w_q, w_o, ln_g, ln_b, g1, b1)
    x2 = x.reshape(n_rows, D_MODEL)
    early2 = mix_early.reshape(n_rows // 2, D_MODEL)
    late2 = mix_late.reshape(n_rows // 2, D_MODEL)
    return pl.pallas_call(
        functools.partial(_attn_body, blocks_per_batch),
        grid=(n_blocks,),
        in_specs=[first_blk, first_blk, second_blk, second_blk, next_blk,
                  pl.BlockSpec(blk, next_half_blk), pl.BlockSpec(blk, next_half_blk), mem_blk, mem_blk]
                 + [_resident_spec(c.shape) for c in consts],
        out_specs=pl.BlockSpec(blk, lambda i: (i, 0)),
        out_shape=jax.ShapeDtypeStruct((n_rows, D_MODEL), F32),
        scratch_shapes=[pltpu.VMEM((2, ATTN_ROWS, D_MODEL), F32),
                        pltpu.VMEM((ATTN_ROWS, D_MODEL), BF16),
                        pltpu.VMEM((ATTN_ROWS, D_MODEL), BF16),
                        pltpu.VMEM((CA_HEADS, ATTN_ROWS, mlen), F32),
                        pltpu.VMEM((CA_HEADS, ATTN_ROWS, mlen), BF16),
                        pltpu.VMEM((ATTN_ROWS, D_MODEL), BF16),
                        pltpu.VMEM((ATTN_ROWS, D_MODEL), F32),
                        pltpu.VMEM(w_q.shape, BF16),
                        pltpu.VMEM(w_o.shape, BF16)],
        compiler_params=pltpu.CompilerParams(
            dimension_semantics=("arbitrary",), vmem_limit_bytes=V7X_SCOPED_VMEM_BYTES),
        name="attn",
    )(x2, early2, x2, early2, x2, early2, late2, k, v, *consts)


def _mlp_body(s_ref, w1_ref, w2_ref, g2_ref, b2_ref, g_ref, b_ref, o_ref):
    for r in _row_chunks(s_ref.shape[0], MLP_GROUP_ROWS):
        h = _layer_norm(s_ref[r, :], g2_ref[...], b2_ref[...])
        hb = h.astype(BF16)
        ff = jnp.zeros(h.shape, F32)
        for j in range(D_FF // MLP_CHUNK):
            lo, hi = j * MLP_CHUNK, (j + 1) * MLP_CHUNK
            t = jnp.maximum(_dot(hb, w1_ref[:, lo:hi]), 0.0)
            ff = ff + _dot((t * t).astype(BF16), w2_ref[lo:hi, :])
        o_ref[r, :] = _layer_norm(ALPHA * h + ff, g_ref[...], b_ref[...])


def _mlp(s, w1, w2, g2, b2, g, b):
    n_rows = s.shape[0]
    row_blk = pl.BlockSpec((MLP_ROWS, D_MODEL), lambda i: (i, 0))
    vecs = (g2, b2, g, b)
    return pl.pallas_call(
        _mlp_body,
        grid=(n_rows // MLP_ROWS,),
        in_specs=[row_blk, _resident_spec(w1.shape), _resident_spec(w2.shape)]
                 + [_const_spec(v.shape) for v in vecs],
        out_specs=row_blk,
        out_shape=jax.ShapeDtypeStruct(s.shape, F32),
        compiler_params=pltpu.CompilerParams(
            dimension_semantics=("arbitrary",), vmem_limit_bytes=V7X_SCOPED_VMEM_BYTES),
        name="mlp",
    )(s, w1, w2, *vecs)


def kernel(x, mem, ln_in_g, ln_in_b, w_in, s5_a_re, s5_a_im, s5_log_dt, s5_b_re, s5_b_im, s5_c_re, s5_c_im, s5_d, s5_w_glu, s5_b_glu, conv_w, conv_b, lru_w_a, lru_b_a, lru_w_x, lru_b_x, lru_lambda, g_s5, g_lru, w_mix_out, ln1_g, ln1_b, mem_ln_g, mem_ln_b, w_q, w_k, w_v, w_o, ln2_g, ln2_b, w_ff1, w_ff2, ln3_g, ln3_b):
    bsz, seq, d_model = x.shape
    assert d_model == D_MODEL and w_in.shape == (DEPTH, D_MODEL, D_IN)
    assert seq % (2 * MIXER_STEPS) == 0 and seq % (2 * ATTN_ROWS) == 0 and (bsz * seq) % MLP_ROWS == 0
    assert bsz % 16 == 0, "time-major rows of one step must fill whole bf16 tiles"
    row = lambda v: v.reshape(1, -1).astype(F32)
    l = 0

    apow_re, apow_im, w_in_chunk, w_out_chunk, wg = _s5_prep(
        s5_a_re[l], s5_a_im[l], s5_log_dt[l], s5_b_re[l], s5_b_im[l], s5_c_re[l], s5_c_im[l],
        lru_w_a[l], lru_w_x[l])
    apow_re = apow_re.reshape(S5_BLOCKS, 1, S5_BLOCK_STATES)
    apow_im = apow_im.reshape(S5_BLOCKS, 1, S5_BLOCK_STATES)

    half = seq // 2
    to_time_major = lambda a: jnp.swapaxes(a, 0, 1).reshape(half * bsz, D_MODEL)
    to_batch_major = lambda a: jnp.swapaxes(a.reshape(half, bsz, D_MODEL), 0, 1)
    mixer_consts = (row(ln_in_g), row(ln_in_b), w_in[l], w_in_chunk, apow_re, apow_im, w_out_chunk,
                    row(s5_d[l]), s5_w_glu[l], row(s5_b_glu[l]), row(g_s5[l]),
                    conv_w[l].astype(F32), row(conv_b[l]), wg, row(lru_b_a[l]), row(lru_b_x[l]),
                    row(lru_lambda[l]), row(g_lru[l]), w_mix_out[l])
    x_early, w_ff1_b, w_ff2_b = lax.optimization_barrier(
        (to_time_major(x[:, :half]), w_ff1[l].astype(BF16), w_ff2[l].astype(BF16)))
    mix_early_tm, state = _mixer(x_early, _mixer_initial_state(bsz), *mixer_consts)
    mix_late_tm, _ = _mixer(to_time_major(x[:, half:]), state, *mixer_consts)
    mix_late_tm, mem_l, w_k_l, w_v_l = lax.optimization_barrier((mix_late_tm, mem, w_k[l], w_v[l]))

    k, v = _kv(mem_l, row(mem_ln_g[l]), row(mem_ln_b[l]), w_k_l, w_v_l)
    pre2 = _attn(x, to_batch_major(mix_early_tm), to_batch_major(mix_late_tm), k, v, w_q[l], w_o[l],
                 row(ln_in_g), row(ln_in_b), row(ln1_g[l]), row(ln1_b[l]))
    out = _mlp(pre2, w_ff1_b, w_ff2_b, row(ln2_g[l]), row(ln2_b[l]), row(ln3_g[l]), row(ln3_b[l]))
    return out.reshape(bsz, seq, D_MODEL)
```

```python
import jax
import jax.numpy as jnp
from jax import lax
from jax.experimental import pallas as pl
from jax.experimental.pallas import tpu as pltpu

F32 = jnp.float32
BF16 = jnp.bfloat16

D_MODEL = 1024
D_S5 = 512
D_LRU = 512
D_IN = D_S5 + 2 * D_LRU
S5_GROUP = 16
S5_GROUPS = 32
S5_STATE = 64
LRU_HEADS = 8
LRU_HEAD_DIM = 64
CONV_WIDTH = 4
LRU_C = 8.0
D_FF = 4096
CA_HEADS = 4
CA_HEAD_DIM = 256
DEPTH = 1
ALPHA = (2 * DEPTH) ** 0.25
LN_EPS = 1e-5
RMS_EPS = 1e-6

S5_BLOCKS = 4
S5_BLOCK_CH = D_S5 // S5_BLOCKS
S5_BLOCK_STATES = 8 * S5_STATE
S5_CHUNK = 2
LRU_BLOCKS = 2
LRU_BLOCK_CH = D_LRU // LRU_BLOCKS

V7X_SCOPED_VMEM_BYTES = 56 * 1024 * 1024
V7X_LANES = 128

MIXER_STEPS = 32
ATTN_ROWS = 512
MLP_ROWS = 1024
MLP_GROUP_ROWS = 512
KV_ROWS = 1024
MLP_CHUNK = 1024
VEC_ROWS = 16


def _layer_norm(x, g, b):
    mu = jnp.mean(x, axis=-1, keepdims=True)
    xc = x - mu
    var = jnp.mean(xc * xc, axis=-1, keepdims=True)
    return xc * lax.rsqrt(var + LN_EPS) * g + b


def _rms_norm(x, g):
    ms = jnp.mean(x * x, axis=-1, keepdims=True)
    return x * lax.rsqrt(ms + RMS_EPS) * g


def _gelu_tanh(x):
    c = 0.7978845608028654
    return 0.5 * x * (1.0 + jnp.tanh(c * (x + 0.044715 * (x * x * x))))


def _dot(a, b):
    return jnp.dot(a, b, preferred_element_type=F32)


def _const_spec(shape):
    zeros = (0,) * len(shape)
    return pl.BlockSpec(shape, lambda *_: zeros)


def _resident_spec(shape, block=None):
    zeros = (0,) * len(shape)
    index = zeros if block is None else block
    return pl.BlockSpec(shape, lambda *_: index, pipeline_mode=pl.Buffered(1))


def _row_chunks(rows, chunk=None):
    chunk = chunk or VEC_ROWS
    return [slice(r, r + chunk) for r in range(0, rows, chunk)]


def _cmul(a_re, a_im, b_re, b_im):
    return a_re * b_re - a_im * b_im, a_re * b_im + a_im * b_re


def _discretise(a_re, a_im, log_dt):
    dt = jnp.exp(log_dt)
    mag = jnp.exp(dt * a_re)
    abar_re = mag * jnp.cos(dt * a_im)
    abar_im = mag * jnp.sin(dt * a_im)
    den = a_re * a_re + a_im * a_im
    q_re = ((abar_re - 1.0) * a_re + abar_im * a_im) / den
    q_im = (abar_im * a_re - (abar_re - 1.0) * a_im) / den
    return abar_re, abar_im, q_re, q_im


def _powers(abar_re, abar_im):
    pows = [(jnp.ones_like(abar_re), jnp.zeros_like(abar_im))]
    for _ in range(S5_CHUNK):
        pows.append(_cmul(pows[-1][0], pows[-1][1], abar_re, abar_im))
    return pows


def _block_diag_tiles(stacked, group_rows, group_cols, n_groups):
    rows = n_groups * group_rows
    cols = n_groups * group_cols
    row_shift = group_rows.bit_length() - 1
    col_shift = group_cols.bit_length() - 1
    assert group_rows == 1 << row_shift and group_cols == 1 << col_shift
    src = lax.broadcasted_iota(jnp.int32, (group_cols, cols), 0)
    dst = lax.broadcasted_iota(jnp.int32, (group_cols, cols), 1)
    replicate = jnp.where((dst & (group_cols - 1)) == src, 1.0, 0.0).astype(BF16)
    tiled = _dot(stacked.astype(BF16), replicate)
    row_group = lax.broadcasted_iota(jnp.int32, (rows, cols), 0) >> row_shift
    col_group = lax.broadcasted_iota(jnp.int32, (rows, cols), 1) >> col_shift
    mask = row_group == col_group
    return [jnp.where(mask, tiled[i * rows:(i + 1) * rows], 0.0).astype(BF16)
            for i in range(stacked.shape[0] // rows)]


def _s5_prep_body(are_ref, aim_ref, ldt_ref, arec_ref, aimc_ref, ldtc_ref, bre_ref, bim_ref, cre_ref, cim_ref,
                  cret_ref, cimt_ref, wa_ref, wx_ref,
                  apow_re_ref, apow_im_ref, wic_ref, woc_ref, wg_ref):
    g8 = S5_GROUPS // S5_BLOCKS
    abar_re, abar_im, q_re, q_im = _discretise(are_ref[...], aim_ref[...], ldt_ref[...])
    pows = _powers(abar_re, abar_im)
    apow_re_ref[...] = pows[S5_CHUNK][0]
    apow_im_ref[...] = pows[S5_CHUNK][1]
    bbar_re, bbar_im = _cmul(q_re, q_im, bre_ref[...], bim_ref[...])
    c_re = cre_ref[...]
    c_im = cim_ref[...]
    abar_re_c, abar_im_c, _, _ = _discretise(arec_ref[...], aimc_ref[...], ldtc_ref[...])
    pows_c = _powers(abar_re_c, abar_im_c)
    ct_re = cret_ref[...]
    ct_im = cimt_ref[...]

    lane = lax.broadcasted_iota(jnp.int32, (S5_GROUPS, S5_GROUP, S5_GROUP), 2)
    within = 2 * S5_BLOCK_STATES
    wic_ref[:, :, within:] = jnp.zeros((S5_BLOCKS, S5_CHUNK * S5_BLOCK_CH, S5_CHUNK * S5_BLOCK_CH), BF16)
    for j in range(S5_CHUNK):
        bt = _cmul(*pows[S5_CHUNK - 1 - j], bbar_re, bbar_im)
        et = _cmul(*pows_c[j + 1], ct_re, ct_im)
        ce_re, ce_im = _cmul(*pows[j], c_re, c_im)
        lag = jnp.zeros(lane.shape, F32)
        for ho in range(S5_GROUP):
            col = jnp.sum(bbar_re * ce_re[:, ho:ho + 1, :] - bbar_im * ce_im[:, ho:ho + 1, :],
                          axis=-1, keepdims=True)
            lag = jnp.where(lane == ho, col, lag)
        rows_j = slice(j * S5_BLOCK_CH, (j + 1) * S5_BLOCK_CH)
        for part in range(2):
            cols = slice(part * S5_BLOCK_STATES, (part + 1) * S5_BLOCK_STATES)
            in_tiles = _block_diag_tiles(bt[part].reshape(S5_GROUPS * S5_GROUP, S5_STATE), S5_GROUP, S5_STATE, g8)
            out_tiles = _block_diag_tiles(et[part].reshape(S5_GROUPS * S5_STATE, S5_GROUP), S5_STATE, S5_GROUP, g8)
            for k in range(S5_BLOCKS):
                wic_ref[k, rows_j, cols] = in_tiles[k]
                woc_ref[k, cols, rows_j] = out_tiles[k] if part == 0 else -out_tiles[k]
        lag_tiles = _block_diag_tiles(lag.reshape(S5_GROUPS * S5_GROUP, S5_GROUP), S5_GROUP, S5_GROUP, g8)
        for k in range(S5_BLOCKS):
            for t_in in range(S5_CHUNK - j):
                t_out = t_in + j
                wic_ref[k, t_in * S5_BLOCK_CH:(t_in + 1) * S5_BLOCK_CH,
                        within + t_out * S5_BLOCK_CH:within + (t_out + 1) * S5_BLOCK_CH] = lag_tiles[k]

    heads = LRU_HEADS // LRU_BLOCKS
    for part, w_ref in enumerate((wa_ref, wx_ref)):
        gate_tiles = _block_diag_tiles(w_ref[...].reshape(D_LRU, LRU_HEAD_DIM), LRU_HEAD_DIM, LRU_HEAD_DIM, heads)
        for kb in range(LRU_BLOCKS):
            wg_ref[kb, :, part * LRU_BLOCK_CH:(part + 1) * LRU_BLOCK_CH] = gate_tiles[kb]


def _s5_prep(a_re, a_im, log_dt, b_re, b_im, c_re, c_im, w_a, w_x):
    g, p, h = b_re.shape
    f = jax.ShapeDtypeStruct
    chunk_cols = S5_CHUNK * S5_BLOCK_CH
    return pl.pallas_call(
        _s5_prep_body,
        out_shape=(f((g, 1, p), F32), f((g, 1, p), F32),
                   f((S5_BLOCKS, chunk_cols, 2 * S5_BLOCK_STATES + chunk_cols), BF16),
                   f((S5_BLOCKS, 2 * S5_BLOCK_STATES, chunk_cols), BF16),
                   f((LRU_BLOCKS, LRU_BLOCK_CH, 2 * LRU_BLOCK_CH), BF16)),
        compiler_params=pltpu.CompilerParams(vmem_limit_bytes=V7X_SCOPED_VMEM_BYTES),
        name="s5_prep",
    )(a_re.reshape(g, 1, p), a_im.reshape(g, 1, p), log_dt.reshape(g, 1, 1),
      a_re.reshape(g, p, 1), a_im.reshape(g, p, 1), log_dt.reshape(g, 1, 1),
      jnp.swapaxes(b_re, 1, 2), jnp.swapaxes(b_im, 1, 2), c_re, c_im,
      jnp.swapaxes(c_re, 1, 2), jnp.swapaxes(c_im, 1, 2), w_a, w_x)


def _mixer_norm(x_ref, lng_ref, lnb_ref, h0b_ref):
    for r in _row_chunks(x_ref.shape[0]):
        h0b_ref[r, :] = _layer_norm(x_ref[r, :], lng_ref[...], lnb_ref[...]).astype(BF16)


def _mixer_project_in(h0b_ref, win_ref, z_ref, uc_ref, batch):
    for piece in range(D_IN // D_S5):
        cols = slice(piece * D_S5, (piece + 1) * D_S5)
        z = _dot(h0b_ref[...], win_ref[:, cols])
        z_ref[:, cols] = z
        if piece == 0:
            zb = z.astype(BF16)
            for step in range(z.shape[0] // batch):
                c, t = divmod(step, S5_CHUNK)
                for k in range(S5_BLOCKS):
                    uc_ref[k, c * batch:(c + 1) * batch, t * S5_BLOCK_CH:(t + 1) * S5_BLOCK_CH] = (
                        zb[step * batch:(step + 1) * batch, k * S5_BLOCK_CH:(k + 1) * S5_BLOCK_CH])


def _mixer_conv(z_ref, cw_ref, cb_ref, d_ref, xl_ref, xc_ref, xcb_ref, gz_ref, du_ref, batch):
    rows = z_ref.shape[0]
    halo = (CONV_WIDTH - 1) * batch
    xl_ref[halo:halo + rows, :] = z_ref[:, D_S5:D_S5 + D_LRU]
    for r in _row_chunks(rows, 2 * VEC_ROWS):
        xc = cb_ref[...]
        for j in range(CONV_WIDTH):
            xc = xc + xl_ref[j * batch + r.start:j * batch + r.stop, :] * cw_ref[j:j + 1, :]
        xc_ref[r, :] = xc
        xcb_ref[r, :] = xc.astype(BF16)
        gz_ref[r, :] = _gelu_tanh(z_ref[r, D_S5 + D_LRU:D_IN])
        du_ref[r, :] = d_ref[...] * z_ref[r, 0:D_S5]
    xl_ref[0:halo, :] = xl_ref[rows:rows + halo, :]


def _mixer_step(xnext_ref, lng_ref, lnb_ref, win_ref, wic_ref, apr_ref, api_ref, woc_ref, d_ref,
                wglu_ref, bglu_ref, gs5_ref, cw_ref, cb_ref, wg_ref, ba_ref, bx_ref, lam_ref, glru_ref, wmix_ref,
                mix_ref,
                z_ref, uc_ref, h0b_ref, inc_ref, hp_ref, st_ref, xl_ref, xc_ref, xcb_ref, pre_ref, a_ref, hx_ref,
                hl_ref, yo_ref, y32_ref, yb_ref, gate_ref, gz_ref, du_ref, y_ref):
    rows = y_ref.shape[0]
    batch = st_ref.shape[1]
    steps = rows // batch
    chunks = steps // S5_CHUNK
    wide = _row_chunks(rows, 2 * VEC_ROWS)
    sre = slice(0, S5_BLOCK_STATES)
    sim = slice(S5_BLOCK_STATES, 2 * S5_BLOCK_STATES)

    for k in range(S5_BLOCKS):
        inc_ref[k] = _dot(uc_ref[k], wic_ref[k])
    for kb in range(LRU_BLOCKS):
        lo, hi_ = kb * LRU_BLOCK_CH, (kb + 1) * LRU_BLOCK_CH
        pre = _dot(xcb_ref[:, lo:hi_], wg_ref[kb])
        pre_ref[:, lo:hi_] = pre[:, 0:LRU_BLOCK_CH]
        pre_ref[:, D_LRU + lo:D_LRU + hi_] = pre[:, LRU_BLOCK_CH:2 * LRU_BLOCK_CH]
    _mixer_project_in(h0b_ref, win_ref, z_ref, uc_ref, batch)

    for k in range(S5_BLOCKS):
        ar = jnp.broadcast_to(apr_ref[k], (batch, S5_BLOCK_STATES))
        ai = jnp.broadcast_to(api_ref[k], (batch, S5_BLOCK_STATES))
        hr = st_ref[k, :, sre]
        hi = st_ref[k, :, sim]
        for c in range(chunks):
            r = slice(c * batch, (c + 1) * batch)
            hp_ref[k, r, sre] = hr.astype(BF16)
            hp_ref[k, r, sim] = hi.astype(BF16)
            hr, hi = (ar * hr - ai * hi + inc_ref[k, r, sre],
                      ar * hi + ai * hr + inc_ref[k, r, sim])
        st_ref[k, :, sre] = hr
        st_ref[k, :, sim] = hi
        yo_ref[k] = _dot(hp_ref[k], woc_ref[k])
        if k == 1:
            lam = lam_ref[...]
            softplus_neg_lam = jnp.maximum(-lam, 0.0) + jnp.log1p(jnp.exp(-jnp.abs(lam)))
            for r in wide:
                gate_a = jax.nn.sigmoid(pre_ref[r, 0:D_LRU] + ba_ref[...])
                gate_x = jax.nn.sigmoid(pre_ref[r, D_LRU:2 * D_LRU] + bx_ref[...])
                log_a = -LRU_C * gate_a * softplus_neg_lam
                a = jnp.exp(log_a)
                m2 = -jnp.tanh(log_a) * (a * a + 1.0)
                mult = jnp.where(m2 > 0.0, m2 * lax.rsqrt(m2), 0.0)
                a_ref[r, :] = a
                hx_ref[r, :] = mult * (gate_x * xc_ref[r, :])
    hl = hl_ref[...]
    for t in range(steps):
        r = slice(t * batch, (t + 1) * batch)
        hl = a_ref[r, :] * hl + hx_ref[r, :]
        hx_ref[r, :] = hl
    hl_ref[...] = hl

    within = 2 * S5_BLOCK_STATES
    for step in range(steps):
        c, t = divmod(step, S5_CHUNK)
        r = slice(step * batch, (step + 1) * batch)
        rc = slice(c * batch, (c + 1) * batch)
        cols = slice(t * S5_BLOCK_CH, (t + 1) * S5_BLOCK_CH)
        ys = jnp.concatenate(
            [yo_ref[k, rc, cols] + inc_ref[k, rc, within + cols.start:within + cols.stop]
             for k in range(S5_BLOCKS)], axis=1)
        y = _gelu_tanh(ys + du_ref[r, :])
        y32_ref[r, :] = y
        yb_ref[r, :] = y.astype(BF16)
    gate_ref[...] = _dot(yb_ref[...], wglu_ref[...])
    for r in wide:
        y = y32_ref[r, :] * jax.nn.sigmoid(gate_ref[r, :] + bglu_ref[...])
        y_ref[r, 0:D_S5] = _rms_norm(y, gs5_ref[...]).astype(y_ref.dtype)
        ylru = hx_ref[r, :] * gz_ref[r, :]
        y_ref[r, D_S5:D_MODEL] = _rms_norm(ylru, glru_ref[...]).astype(y_ref.dtype)
    mix = _dot(y_ref[...], wmix_ref[...])
    for c in range(D_MODEL // V7X_LANES):
        piece = mix[:, c * V7X_LANES:(c + 1) * V7X_LANES].reshape(steps, batch, V7X_LANES)
        mix_ref[c] = pltpu.einshape("tbl->btl", piece).reshape(rows, V7X_LANES)

    _mixer_conv(z_ref, cw_ref, cb_ref, d_ref, xl_ref, xc_ref, xcb_ref, gz_ref, du_ref, batch)
    _mixer_norm(xnext_ref, lng_ref, lnb_ref, h0b_ref)


def _cast_rows(src_ref, dst_ref, chunk=64):
    for r in _row_chunks(src_ref.shape[0], chunk):
        dst_ref[r, :] = src_ref[r, :].astype(dst_ref.dtype)


def _mixer_body(xfirst_ref, xsecond_ref, xnext_ref, lng_ref, lnb_ref, win32_ref, *rest):
    back_consts = list(rest[:16])
    d_ref, cw_ref, cb_ref = back_consts[4], back_consts[8], back_consts[9]
    mix_ref = rest[16]
    scratch = rest[17:-3]
    win_ref, wglu_ref, wmix_ref = rest[-3:]
    wglu32_ref, wmix32_ref = back_consts[5], back_consts[15]
    back_consts[5], back_consts[15] = wglu_ref, wmix_ref
    z_ref, uc_ref, h0b_ref, _, _, st_ref, xl_ref, xc_ref, xcb_ref = scratch[:9]
    hl_ref, gz_ref, du_ref = scratch[12], scratch[17], scratch[18]
    batch = st_ref.shape[1]
    halo = (CONV_WIDTH - 1) * batch

    @pl.when(pl.program_id(0) == 0)
    def _():
        _cast_rows(win32_ref, win_ref)
        _cast_rows(wglu32_ref, wglu_ref)
        _cast_rows(wmix32_ref, wmix_ref)
        st_ref[...] = jnp.zeros_like(st_ref)
        hl_ref[...] = jnp.zeros_like(hl_ref)
        xl_ref[0:halo, :] = jnp.zeros((halo, D_LRU), F32)
        _mixer_norm(xfirst_ref, lng_ref, lnb_ref, h0b_ref)
        _mixer_project_in(h0b_ref, win_ref, z_ref, uc_ref, batch)
        _mixer_conv(z_ref, cw_ref, cb_ref, d_ref, xl_ref, xc_ref, xcb_ref, gz_ref, du_ref, batch)
        _mixer_norm(xsecond_ref, lng_ref, lnb_ref, h0b_ref)

    _mixer_step(xnext_ref, lng_ref, lnb_ref, win_ref, *back_consts, mix_ref, *scratch)


def _mixer(x_tm, batch, ln_g, ln_b, w_in, w_in_chunk, apow_re, apow_im, w_out_chunk, d, w_glu, b_glu, g_s5,
           conv_w, conv_b, wg, b_a, b_x, lam, g_lru, w_mix):
    n_rows = x_tm.shape[0]
    rows = MIXER_STEPS * batch
    n_blocks = n_rows // rows
    halo = (CONV_WIDTH - 1) * batch
    chunk_rows = rows // S5_CHUNK
    chunk_cols = S5_CHUNK * S5_BLOCK_CH
    slabs = D_MODEL // V7X_LANES
    consts = (ln_g, ln_b, w_in, w_in_chunk, apow_re, apow_im, w_out_chunk, d, w_glu, b_glu, g_s5,
              conv_w, conv_b, wg, b_a, b_x, lam, g_lru, w_mix)
    mix = pl.pallas_call(
        _mixer_body,
        grid=(n_blocks,),
        in_specs=[_resident_spec((rows, D_MODEL), (0, 0)),
                  _resident_spec((rows, D_MODEL), (min(1, n_blocks - 1), 0)),
                  pl.BlockSpec((rows, D_MODEL), lambda i: (jnp.minimum(i + 2, n_blocks - 1), 0))]
                 + [_resident_spec(c.shape) for c in consts],
        out_specs=pl.BlockSpec((slabs, None, rows, V7X_LANES), lambda i: (0, i, 0, 0)),
        out_shape=jax.ShapeDtypeStruct((slabs, n_blocks, rows, V7X_LANES), F32),
        scratch_shapes=[
            pltpu.VMEM((rows, D_IN), F32),
            pltpu.VMEM((S5_BLOCKS, chunk_rows, chunk_cols), BF16),
            pltpu.VMEM((rows, D_MODEL), BF16),
            pltpu.VMEM((S5_BLOCKS, chunk_rows, 2 * S5_BLOCK_STATES + chunk_cols), F32),
            pltpu.VMEM((S5_BLOCKS, chunk_rows, 2 * S5_BLOCK_STATES), BF16),
            pltpu.VMEM((S5_BLOCKS, batch, 2 * S5_BLOCK_STATES), F32),
            pltpu.VMEM((rows + halo, D_LRU), F32),
            pltpu.VMEM((rows, D_LRU), F32),
            pltpu.VMEM((rows, D_LRU), BF16),
            pltpu.VMEM((rows, 2 * D_LRU), F32),
            pltpu.VMEM((rows, D_LRU), F32),
            pltpu.VMEM((rows, D_LRU), F32),
            pltpu.VMEM((batch, D_LRU), F32),
            pltpu.VMEM((S5_BLOCKS, chunk_rows, chunk_cols), F32),
            pltpu.VMEM((rows, D_S5), F32),
            pltpu.VMEM((rows, D_S5), BF16),
            pltpu.VMEM((rows, D_S5), F32),
            pltpu.VMEM((rows, D_LRU), F32),
            pltpu.VMEM((rows, D_S5), F32),
            pltpu.VMEM((rows, D_MODEL), BF16),
            pltpu.VMEM(w_in.shape, BF16),
            pltpu.VMEM(w_glu.shape, BF16),
            pltpu.VMEM(w_mix.shape, BF16),
        ],
        compiler_params=pltpu.CompilerParams(
            dimension_semantics=("arbitrary",), vmem_limit_bytes=V7X_SCOPED_VMEM_BYTES),
        name="mixer",
    )(x_tm, x_tm, x_tm, *consts)
    return mix.reshape(slabs, n_blocks, batch, MIXER_STEPS, V7X_LANES)


def _kv_body(mem_ref, g_ref, b_ref, wk32_ref, wv32_ref, k_ref, v_ref, wk_ref, wv_ref):
    @pl.when(pl.program_id(0) == 0)
    def _():
        _cast_rows(wk32_ref, wk_ref)
        _cast_rows(wv32_ref, wv_ref)

    mn = _layer_norm(mem_ref[...], g_ref[...], b_ref[...]).astype(BF16)
    k_ref[...] = (_dot(mn, wk_ref[...]) * (CA_HEAD_DIM ** -0.5)).astype(k_ref.dtype)
    v_ref[...] = _dot(mn, wv_ref[...]).astype(v_ref.dtype)


def _kv(mem, g, b, w_k, w_v):
    bsz, mlen, _ = mem.shape
    n_rows = bsz * mlen
    blk = pl.BlockSpec((KV_ROWS, D_MODEL), lambda i: (i, 0))
    out = jax.ShapeDtypeStruct((n_rows, D_MODEL), BF16)
    k, v = pl.pallas_call(
        _kv_body,
        grid=(n_rows // KV_ROWS,),
        in_specs=[blk, _const_spec(g.shape), _const_spec(b.shape), _resident_spec(w_k.shape),
                  _resident_spec(w_v.shape)],
        out_specs=(blk, blk),
        out_shape=(out, out),
        scratch_shapes=[pltpu.VMEM(w_k.shape, BF16), pltpu.VMEM(w_v.shape, BF16)],
        compiler_params=pltpu.CompilerParams(
            dimension_semantics=("arbitrary",), vmem_limit_bytes=V7X_SCOPED_VMEM_BYTES),
        name="kv",
    )(mem.reshape(n_rows, D_MODEL), g, b, w_k, w_v)
    return k.reshape(mem.shape), v.reshape(mem.shape)


def _attn_norms(x_ref, mix_ref, lng_ref, lnb_ref, g1_ref, b1_ref, h1_ref, h1b_ref, slot):
    slabs, _, steps, _ = mix_ref.shape
    for r in _row_chunks(x_ref.shape[0]):
        blk, t = divmod(r.start, steps)
        mix = jnp.concatenate([mix_ref[c, blk, t:t + VEC_ROWS, :] for c in range(slabs)], axis=-1)
        h0 = _layer_norm(x_ref[r, :], lng_ref[...], lnb_ref[...])
        h1 = _layer_norm(ALPHA * h0 + mix, g1_ref[...], b1_ref[...])
        h1_ref[slot, r, :] = h1
        h1b_ref[r, :] = h1.astype(BF16)


def _attn_body(x0_ref, m0_ref, x1_ref, m1_ref, xnext_ref, mnext_ref, k_ref, v_ref, wq32_ref, wo32_ref,
               lng_ref, lnb_ref, g1_ref, b1_ref, o_ref,
               h1_ref, q_ref, h1b_ref, s_ref, p_ref, att_ref, ca_ref, wq_ref, wo_ref):
    i = pl.program_id(0)
    slot = i % 2
    rows = o_ref.shape[0]
    norm_refs = (lng_ref, lnb_ref, g1_ref, b1_ref, h1_ref, h1b_ref)

    @pl.when(i == 0)
    def _():
        _cast_rows(wq32_ref, wq_ref)
        _cast_rows(wo32_ref, wo_ref)
        _attn_norms(x0_ref, m0_ref, *norm_refs, 0)
        q_ref[...] = _dot(h1b_ref[...], wq_ref[...]).astype(BF16)
        _attn_norms(x1_ref, m1_ref, *norm_refs, 1)

    head_slices = [slice(hd * CA_HEAD_DIM, (hd + 1) * CA_HEAD_DIM) for hd in range(CA_HEADS)]
    for hd, hs in enumerate(head_slices):
        s_ref[hd] = lax.dot_general(q_ref[:, hs], k_ref[:, hs], (((1,), (1,)), ((), ())),
                                    preferred_element_type=F32)
    q_ref[...] = _dot(h1b_ref[...], wq_ref[...]).astype(BF16)
    for r in _row_chunks(rows):
        for hd in range(CA_HEADS):
            s = s_ref[hd, r, :]
            e = jnp.exp(s - jnp.max(s, axis=-1, keepdims=True))
            p_ref[hd, r, :] = (e * (1.0 / jnp.sum(e, axis=-1, keepdims=True))).astype(BF16)
    for hd, hs in enumerate(head_slices):
        att_ref[:, hs] = _dot(p_ref[hd], v_ref[:, hs]).astype(BF16)
    ca_ref[...] = _dot(att_ref[...], wo_ref[...])
    for r in _row_chunks(rows):
        o_ref[r, :] = ALPHA * h1_ref[slot, r, :] + ca_ref[r, :]
    _attn_norms(xnext_ref, mnext_ref, *norm_refs, slot)


def _attn(x, mix, k, v, w_q, w_o, ln_g, ln_b, g1, b1):
    bsz, seq, _ = x.shape
    mlen = k.shape[1]
    n_rows = bsz * seq
    n_blocks = n_rows // ATTN_ROWS
    blocks_per_batch = seq // ATTN_ROWS
    slabs, _, _, steps, lanes = mix.shape
    assert ATTN_ROWS % steps == 0 and steps % VEC_ROWS == 0
    blk = (ATTN_ROWS, D_MODEL)
    first_blk = _resident_spec(blk, (0, 0))
    second_blk = _resident_spec(blk, (min(1, n_blocks - 1), 0))
    next_blk = pl.BlockSpec(blk, lambda i: (jnp.minimum(i + 2, n_blocks - 1), 0))
    mix_blk = (slabs, ATTN_ROWS // steps, None, steps, lanes)

    def mix_index(block):
        return (0, block % blocks_per_batch, block // blocks_per_batch, 0, 0)

    first_mix = pl.BlockSpec(mix_blk, lambda i: mix_index(0), pipeline_mode=pl.Buffered(1))
    second_mix = pl.BlockSpec(mix_blk, lambda i: mix_index(min(1, n_blocks - 1)), pipeline_mode=pl.Buffered(1))
    next_mix = pl.BlockSpec(mix_blk, lambda i: mix_index(jnp.minimum(i + 2, n_blocks - 1)))
    mem_blk = pl.BlockSpec((None, mlen, D_MODEL), lambda i: (i // blocks_per_batch, 0, 0))
    consts = (w_q, w_o, ln_g, ln_b, g1, b1)
    x2 = x.reshape(n_rows, D_MODEL)
    return pl.pallas_call(
        _attn_body,
        grid=(n_blocks,),
        in_specs=[first_blk, first_mix, second_blk, second_mix, next_blk, next_mix, mem_blk, mem_blk]
                 + [_resident_spec(c.shape) for c in consts],
        out_specs=pl.BlockSpec(blk, lambda i: (i, 0)),
        out_shape=jax.ShapeDtypeStruct((n_rows, D_MODEL), F32),
        scratch_shapes=[pltpu.VMEM((2, ATTN_ROWS, D_MODEL), F32),
                        pltpu.VMEM((ATTN_ROWS, D_MODEL), BF16),
                        pltpu.VMEM((ATTN_ROWS, D_MODEL), BF16),
                        pltpu.VMEM((CA_HEADS, ATTN_ROWS, mlen), F32),
                        pltpu.VMEM((CA_HEADS, ATTN_ROWS, mlen), BF16),
                        pltpu.VMEM((ATTN_ROWS, D_MODEL), BF16),
                        pltpu.VMEM((ATTN_ROWS, D_MODEL), F32),
                        pltpu.VMEM(w_q.shape, BF16),
                        pltpu.VMEM(w_o.shape, BF16)],
        compiler_params=pltpu.CompilerParams(
            dimension_semantics=("arbitrary",), vmem_limit_bytes=V7X_SCOPED_VMEM_BYTES),
        name="attn",
    )(x2, mix, x2, mix, x2, mix, k, v, *consts)


def _mlp_body(s_ref, w1_ref, w2_ref, g2_ref, b2_ref, g_ref, b_ref, o_ref):
    for r in _row_chunks(s_ref.shape[0], MLP_GROUP_ROWS):
        h = _layer_norm(s_ref[r, :], g2_ref[...], b2_ref[...])
        hb = h.astype(BF16)
        ff = jnp.zeros(h.shape, F32)
        for j in range(D_FF // MLP_CHUNK):
            lo, hi = j * MLP_CHUNK, (j + 1) * MLP_CHUNK
            t = jnp.maximum(_dot(hb, w1_ref[:, lo:hi]), 0.0)
            ff = ff + _dot((t * t).astype(BF16), w2_ref[lo:hi, :])
        o_ref[r, :] = _layer_norm(ALPHA * h + ff, g_ref[...], b_ref[...])


def _mlp(s, w1, w2, g2, b2, g, b):
    n_rows = s.shape[0]
    row_blk = pl.BlockSpec((MLP_ROWS, D_MODEL), lambda i: (i, 0))
    vecs = (g2, b2, g, b)
    return pl.pallas_call(
        _mlp_body,
        grid=(n_rows // MLP_ROWS,),
        in_specs=[row_blk, _resident_spec(w1.shape), _resident_spec(w2.shape)]
                 + [_const_spec(v.shape) for v in vecs],
        out_specs=row_blk,
        out_shape=jax.ShapeDtypeStruct(s.shape, F32),
        compiler_params=pltpu.CompilerParams(
            dimension_semantics=("arbitrary",), vmem_limit_bytes=V7X_SCOPED_VMEM_BYTES),
        name="mlp",
    )(s, w1, w2, *vecs)


def kernel(x, mem, ln_in_g, ln_in_b, w_in, s5_a_re, s5_a_im, s5_log_dt, s5_b_re, s5_b_im, s5_c_re, s5_c_im, s5_d, s5_w_glu, s5_b_glu, conv_w, conv_b, lru_w_a, lru_b_a, lru_w_x, lru_b_x, lru_lambda, g_s5, g_lru, w_mix_out, ln1_g, ln1_b, mem_ln_g, mem_ln_b, w_q, w_k, w_v, w_o, ln2_g, ln2_b, w_ff1, w_ff2, ln3_g, ln3_b):
    bsz, seq, d_model = x.shape
    assert d_model == D_MODEL and w_in.shape == (DEPTH, D_MODEL, D_IN)
    assert seq % MIXER_STEPS == 0 and seq % ATTN_ROWS == 0 and (bsz * seq) % MLP_ROWS == 0
    assert bsz % 16 == 0, "time-major rows of one step must fill whole bf16 tiles"
    row = lambda v: v.reshape(1, -1).astype(F32)
    l = 0

    apow_re, apow_im, w_in_chunk, w_out_chunk, wg = _s5_prep(
        s5_a_re[l], s5_a_im[l], s5_log_dt[l], s5_b_re[l], s5_b_im[l], s5_c_re[l], s5_c_im[l],
        lru_w_a[l], lru_w_x[l])
    apow_re = apow_re.reshape(S5_BLOCKS, 1, S5_BLOCK_STATES)
    apow_im = apow_im.reshape(S5_BLOCKS, 1, S5_BLOCK_STATES)

    x_tm = jnp.swapaxes(x, 0, 1).reshape(seq * bsz, D_MODEL)
    k, v = _kv(mem, row(mem_ln_g[l]), row(mem_ln_b[l]), w_k[l], w_v[l])
    x_tm, k, v, w_ff1_b, w_ff2_b = lax.optimization_barrier(
        (x_tm, k, v, w_ff1[l].astype(BF16), w_ff2[l].astype(BF16)))
    mix = _mixer(x_tm, bsz, row(ln_in_g), row(ln_in_b), w_in[l], w_in_chunk, apow_re, apow_im,
                 w_out_chunk,
                 row(s5_d[l]), s5_w_glu[l], row(s5_b_glu[l]), row(g_s5[l]),
                 conv_w[l].astype(F32), row(conv_b[l]), wg, row(lru_b_a[l]), row(lru_b_x[l]),
                 row(lru_lambda[l]), row(g_lru[l]), w_mix_out[l])
    pre2 = _attn(x, mix, k, v, w_q[l], w_o[l], row(ln_in_g), row(ln_in_b), row(ln1_g[l]), row(ln1_b[l]))
    out = _mlp(pre2, w_ff1_b, w_ff2_b, row(ln2_g[l]), row(ln2_b[l]), row(ln3_g[l]), row(ln3_b[l]))
    return out.reshape(bsz, seq, D_MODEL)
```

```python
import jax
import jax.numpy as jnp
from jax import lax
from jax.experimental import pallas as pl
from jax.experimental.pallas import tpu as pltpu

F32 = jnp.float32
BF16 = jnp.bfloat16

D_MODEL = 1024
D_S5 = 512
D_LRU = 512
D_IN = D_S5 + 2 * D_LRU
S5_GROUP = 16
S5_GROUPS = 32
S5_STATE = 64
LRU_HEADS = 8
LRU_HEAD_DIM = 64
CONV_WIDTH = 4
LRU_C = 8.0
D_FF = 4096
CA_HEADS = 4
CA_HEAD_DIM = 256
DEPTH = 1
ALPHA = (2 * DEPTH) ** 0.25
LN_EPS = 1e-5
RMS_EPS = 1e-6

S5_BLOCKS = 4
S5_BLOCK_CH = D_S5 // S5_BLOCKS
S5_BLOCK_STATES = 8 * S5_STATE
S5_CHUNK = 2
LRU_BLOCKS = 2
LRU_BLOCK_CH = D_LRU // LRU_BLOCKS

V7X_SCOPED_VMEM_BYTES = 56 * 1024 * 1024

MIXER_STEPS = 32
ATTN_ROWS = 512
MLP_ROWS = 1024
MLP_GROUP_ROWS = 512
KV_ROWS = 1024
MLP_CHUNK = 1024
VEC_ROWS = 16


def _layer_norm(x, g, b):
    mu = jnp.mean(x, axis=-1, keepdims=True)
    xc = x - mu
    var = jnp.mean(xc * xc, axis=-1, keepdims=True)
    return xc * lax.rsqrt(var + LN_EPS) * g + b


def _rms_norm(x, g):
    ms = jnp.mean(x * x, axis=-1, keepdims=True)
    return x * lax.rsqrt(ms + RMS_EPS) * g


def _gelu_tanh(x):
    c = 0.7978845608028654
    return 0.5 * x * (1.0 + jnp.tanh(c * (x + 0.044715 * (x * x * x))))


def _dot(a, b):
    return jnp.dot(a, b, preferred_element_type=F32)


def _const_spec(shape):
    zeros = (0,) * len(shape)
    return pl.BlockSpec(shape, lambda *_: zeros)


def _resident_spec(shape, block=None):
    zeros = (0,) * len(shape)
    index = zeros if block is None else block
    return pl.BlockSpec(shape, lambda *_: index, pipeline_mode=pl.Buffered(1))


def _row_chunks(rows, chunk=None):
    chunk = chunk or VEC_ROWS
    return [slice(r, r + chunk) for r in range(0, rows, chunk)]


def _cmul(a_re, a_im, b_re, b_im):
    return a_re * b_re - a_im * b_im, a_re * b_im + a_im * b_re


def _discretise(a_re, a_im, log_dt):
    dt = jnp.exp(log_dt)
    mag = jnp.exp(dt * a_re)
    abar_re = mag * jnp.cos(dt * a_im)
    abar_im = mag * jnp.sin(dt * a_im)
    den = a_re * a_re + a_im * a_im
    q_re = ((abar_re - 1.0) * a_re + abar_im * a_im) / den
    q_im = (abar_im * a_re - (abar_re - 1.0) * a_im) / den
    return abar_re, abar_im, q_re, q_im


def _powers(abar_re, abar_im):
    pows = [(jnp.ones_like(abar_re), jnp.zeros_like(abar_im))]
    for _ in range(S5_CHUNK):
        pows.append(_cmul(pows[-1][0], pows[-1][1], abar_re, abar_im))
    return pows


def _block_diag_tiles(stacked, group_rows, group_cols, n_groups):
    rows = n_groups * group_rows
    cols = n_groups * group_cols
    row_shift = group_rows.bit_length() - 1
    col_shift = group_cols.bit_length() - 1
    assert group_rows == 1 << row_shift and group_cols == 1 << col_shift
    src = lax.broadcasted_iota(jnp.int32, (group_cols, cols), 0)
    dst = lax.broadcasted_iota(jnp.int32, (group_cols, cols), 1)
    replicate = jnp.where((dst & (group_cols - 1)) == src, 1.0, 0.0).astype(BF16)
    tiled = _dot(stacked.astype(BF16), replicate)
    row_group = lax.broadcasted_iota(jnp.int32, (rows, cols), 0) >> row_shift
    col_group = lax.broadcasted_iota(jnp.int32, (rows, cols), 1) >> col_shift
    mask = row_group == col_group
    return [jnp.where(mask, tiled[i * rows:(i + 1) * rows], 0.0).astype(BF16)
            for i in range(stacked.shape[0] // rows)]


def _s5_prep_body(are_ref, aim_ref, ldt_ref, arec_ref, aimc_ref, ldtc_ref, bre_ref, bim_ref, cre_ref, cim_ref,
                  cret_ref, cimt_ref, wa_ref, wx_ref,
                  apow_re_ref, apow_im_ref, wic_ref, woc_ref, wg_ref):
    g8 = S5_GROUPS // S5_BLOCKS
    abar_re, abar_im, q_re, q_im = _discretise(are_ref[...], aim_ref[...], ldt_ref[...])
    pows = _powers(abar_re, abar_im)
    apow_re_ref[...] = pows[S5_CHUNK][0]
    apow_im_ref[...] = pows[S5_CHUNK][1]
    bbar_re, bbar_im = _cmul(q_re, q_im, bre_ref[...], bim_ref[...])
    c_re = cre_ref[...]
    c_im = cim_ref[...]
    abar_re_c, abar_im_c, _, _ = _discretise(arec_ref[...], aimc_ref[...], ldtc_ref[...])
    pows_c = _powers(abar_re_c, abar_im_c)
    ct_re = cret_ref[...]
    ct_im = cimt_ref[...]

    lane = lax.broadcasted_iota(jnp.int32, (S5_GROUPS, S5_GROUP, S5_GROUP), 2)
    within = 2 * S5_BLOCK_STATES
    wic_ref[:, :, within:] = jnp.zeros((S5_BLOCKS, S5_CHUNK * S5_BLOCK_CH, S5_CHUNK * S5_BLOCK_CH), BF16)
    for j in range(S5_CHUNK):
        bt = _cmul(*pows[S5_CHUNK - 1 - j], bbar_re, bbar_im)
        et = _cmul(*pows_c[j + 1], ct_re, ct_im)
        ce_re, ce_im = _cmul(*pows[j], c_re, c_im)
        lag = jnp.zeros(lane.shape, F32)
        for ho in range(S5_GROUP):
            col = jnp.sum(bbar_re * ce_re[:, ho:ho + 1, :] - bbar_im * ce_im[:, ho:ho + 1, :],
                          axis=-1, keepdims=True)
            lag = jnp.where(lane == ho, col, lag)
        rows_j = slice(j * S5_BLOCK_CH, (j + 1) * S5_BLOCK_CH)
        for part in range(2):
            cols = slice(part * S5_BLOCK_STATES, (part + 1) * S5_BLOCK_STATES)
            in_tiles = _block_diag_tiles(bt[part].reshape(S5_GROUPS * S5_GROUP, S5_STATE), S5_GROUP, S5_STATE, g8)
            out_tiles = _block_diag_tiles(et[part].reshape(S5_GROUPS * S5_STATE, S5_GROUP), S5_STATE, S5_GROUP, g8)
            for k in range(S5_BLOCKS):
                wic_ref[k, rows_j, cols] = in_tiles[k]
                woc_ref[k, cols, rows_j] = out_tiles[k] if part == 0 else -out_tiles[k]
        lag_tiles = _block_diag_tiles(lag.reshape(S5_GROUPS * S5_GROUP, S5_GROUP), S5_GROUP, S5_GROUP, g8)
        for k in range(S5_BLOCKS):
            for t_in in range(S5_CHUNK - j):
                t_out = t_in + j
                wic_ref[k, t_in * S5_BLOCK_CH:(t_in + 1) * S5_BLOCK_CH,
                        within + t_out * S5_BLOCK_CH:within + (t_out + 1) * S5_BLOCK_CH] = lag_tiles[k]

    heads = LRU_HEADS // LRU_BLOCKS
    for part, w_ref in enumerate((wa_ref, wx_ref)):
        gate_tiles = _block_diag_tiles(w_ref[...].reshape(D_LRU, LRU_HEAD_DIM), LRU_HEAD_DIM, LRU_HEAD_DIM, heads)
        for kb in range(LRU_BLOCKS):
            wg_ref[kb, :, part * LRU_BLOCK_CH:(part + 1) * LRU_BLOCK_CH] = gate_tiles[kb]


def _s5_prep(a_re, a_im, log_dt, b_re, b_im, c_re, c_im, w_a, w_x):
    g, p, h = b_re.shape
    f = jax.ShapeDtypeStruct
    chunk_cols = S5_CHUNK * S5_BLOCK_CH
    return pl.pallas_call(
        _s5_prep_body,
        out_shape=(f((g, 1, p), F32), f((g, 1, p), F32),
                   f((S5_BLOCKS, chunk_cols, 2 * S5_BLOCK_STATES + chunk_cols), BF16),
                   f((S5_BLOCKS, 2 * S5_BLOCK_STATES, chunk_cols), BF16),
                   f((LRU_BLOCKS, LRU_BLOCK_CH, 2 * LRU_BLOCK_CH), BF16)),
        compiler_params=pltpu.CompilerParams(vmem_limit_bytes=V7X_SCOPED_VMEM_BYTES),
        name="s5_prep",
    )(a_re.reshape(g, 1, p), a_im.reshape(g, 1, p), log_dt.reshape(g, 1, 1),
      a_re.reshape(g, p, 1), a_im.reshape(g, p, 1), log_dt.reshape(g, 1, 1),
      jnp.swapaxes(b_re, 1, 2), jnp.swapaxes(b_im, 1, 2), c_re, c_im,
      jnp.swapaxes(c_re, 1, 2), jnp.swapaxes(c_im, 1, 2), w_a, w_x)


def _mixer_norm(x_ref, lng_ref, lnb_ref, h0b_ref):
    for r in _row_chunks(x_ref.shape[0]):
        h0b_ref[r, :] = _layer_norm(x_ref[r, :], lng_ref[...], lnb_ref[...]).astype(BF16)


def _mixer_project_in(h0b_ref, win_ref, z_ref, uc_ref, batch):
    for piece in range(D_IN // D_S5):
        cols = slice(piece * D_S5, (piece + 1) * D_S5)
        z = _dot(h0b_ref[...], win_ref[:, cols])
        z_ref[:, cols] = z
        if piece == 0:
            zb = z.astype(BF16)
            for step in range(z.shape[0] // batch):
                c, t = divmod(step, S5_CHUNK)
                for k in range(S5_BLOCKS):
                    uc_ref[k, c * batch:(c + 1) * batch, t * S5_BLOCK_CH:(t + 1) * S5_BLOCK_CH] = (
                        zb[step * batch:(step + 1) * batch, k * S5_BLOCK_CH:(k + 1) * S5_BLOCK_CH])


def _mixer_conv(z_ref, cw_ref, cb_ref, d_ref, xl_ref, xc_ref, xcb_ref, gz_ref, du_ref, batch):
    rows = z_ref.shape[0]
    halo = (CONV_WIDTH - 1) * batch
    xl_ref[halo:halo + rows, :] = z_ref[:, D_S5:D_S5 + D_LRU]
    for r in _row_chunks(rows, 2 * VEC_ROWS):
        xc = cb_ref[...]
        for j in range(CONV_WIDTH):
            xc = xc + xl_ref[j * batch + r.start:j * batch + r.stop, :] * cw_ref[j:j + 1, :]
        xc_ref[r, :] = xc
        xcb_ref[r, :] = xc.astype(BF16)
        gz_ref[r, :] = _gelu_tanh(z_ref[r, D_S5 + D_LRU:D_IN])
        du_ref[r, :] = d_ref[...] * z_ref[r, 0:D_S5]
    xl_ref[0:halo, :] = xl_ref[rows:rows + halo, :]


def _mixer_step(xnext_ref, lng_ref, lnb_ref, win_ref, wic_ref, apr_ref, api_ref, woc_ref, d_ref,
                wglu_ref, bglu_ref, gs5_ref, cw_ref, cb_ref, wg_ref, ba_ref, bx_ref, lam_ref, glru_ref, wmix_ref,
                mix_ref,
                z_ref, uc_ref, h0b_ref, inc_ref, hp_ref, st_ref, xl_ref, xc_ref, xcb_ref, pre_ref, a_ref, hx_ref,
                hl_ref, yo_ref, y32_ref, yb_ref, gate_ref, gz_ref, du_ref, y_ref):
    rows = y_ref.shape[0]
    batch = st_ref.shape[1]
    steps = rows // batch
    chunks = steps // S5_CHUNK
    wide = _row_chunks(rows, 2 * VEC_ROWS)
    sre = slice(0, S5_BLOCK_STATES)
    sim = slice(S5_BLOCK_STATES, 2 * S5_BLOCK_STATES)

    for k in range(S5_BLOCKS):
        inc_ref[k] = _dot(uc_ref[k], wic_ref[k])
    for kb in range(LRU_BLOCKS):
        lo, hi_ = kb * LRU_BLOCK_CH, (kb + 1) * LRU_BLOCK_CH
        pre = _dot(xcb_ref[:, lo:hi_], wg_ref[kb])
        pre_ref[:, lo:hi_] = pre[:, 0:LRU_BLOCK_CH]
        pre_ref[:, D_LRU + lo:D_LRU + hi_] = pre[:, LRU_BLOCK_CH:2 * LRU_BLOCK_CH]
    _mixer_project_in(h0b_ref, win_ref, z_ref, uc_ref, batch)

    for k in range(S5_BLOCKS):
        ar = jnp.broadcast_to(apr_ref[k], (batch, S5_BLOCK_STATES))
        ai = jnp.broadcast_to(api_ref[k], (batch, S5_BLOCK_STATES))
        hr = st_ref[k, :, sre]
        hi = st_ref[k, :, sim]
        for c in range(chunks):
            r = slice(c * batch, (c + 1) * batch)
            hp_ref[k, r, sre] = hr.astype(BF16)
            hp_ref[k, r, sim] = hi.astype(BF16)
            hr, hi = (ar * hr - ai * hi + inc_ref[k, r, sre],
                      ar * hi + ai * hr + inc_ref[k, r, sim])
        st_ref[k, :, sre] = hr
        st_ref[k, :, sim] = hi
        yo_ref[k] = _dot(hp_ref[k], woc_ref[k])
        if k == 1:
            lam = lam_ref[...]
            softplus_neg_lam = jnp.maximum(-lam, 0.0) + jnp.log1p(jnp.exp(-jnp.abs(lam)))
            for r in wide:
                gate_a = jax.nn.sigmoid(pre_ref[r, 0:D_LRU] + ba_ref[...])
                gate_x = jax.nn.sigmoid(pre_ref[r, D_LRU:2 * D_LRU] + bx_ref[...])
                log_a = -LRU_C * gate_a * softplus_neg_lam
                a = jnp.exp(log_a)
                m2 = -jnp.tanh(log_a) * (a * a + 1.0)
                mult = jnp.where(m2 > 0.0, m2 * lax.rsqrt(m2), 0.0)
                a_ref[r, :] = a
                hx_ref[r, :] = mult * (gate_x * xc_ref[r, :])
    hl = hl_ref[...]
    for t in range(steps):
        r = slice(t * batch, (t + 1) * batch)
        hl = a_ref[r, :] * hl + hx_ref[r, :]
        hx_ref[r, :] = hl
    hl_ref[...] = hl

    within = 2 * S5_BLOCK_STATES
    for step in range(steps):
        c, t = divmod(step, S5_CHUNK)
        r = slice(step * batch, (step + 1) * batch)
        rc = slice(c * batch, (c + 1) * batch)
        cols = slice(t * S5_BLOCK_CH, (t + 1) * S5_BLOCK_CH)
        ys = jnp.concatenate(
            [yo_ref[k, rc, cols] + inc_ref[k, rc, within + cols.start:within + cols.stop]
             for k in range(S5_BLOCKS)], axis=1)
        y = _gelu_tanh(ys + du_ref[r, :])
        y32_ref[r, :] = y
        yb_ref[r, :] = y.astype(BF16)
    gate_ref[...] = _dot(yb_ref[...], wglu_ref[...])
    for r in wide:
        y = y32_ref[r, :] * jax.nn.sigmoid(gate_ref[r, :] + bglu_ref[...])
        y_ref[r, 0:D_S5] = _rms_norm(y, gs5_ref[...]).astype(y_ref.dtype)
        ylru = hx_ref[r, :] * gz_ref[r, :]
        y_ref[r, D_S5:D_MODEL] = _rms_norm(ylru, glru_ref[...]).astype(y_ref.dtype)
    mix_ref[...] = _dot(y_ref[...], wmix_ref[...]).astype(mix_ref.dtype)

    _mixer_conv(z_ref, cw_ref, cb_ref, d_ref, xl_ref, xc_ref, xcb_ref, gz_ref, du_ref, batch)
    _mixer_norm(xnext_ref, lng_ref, lnb_ref, h0b_ref)


def _cast_rows(src_ref, dst_ref, chunk=64):
    for r in _row_chunks(src_ref.shape[0], chunk):
        dst_ref[r, :] = src_ref[r, :].astype(dst_ref.dtype)


def _mixer_body(xfirst_ref, xsecond_ref, xnext_ref, lng_ref, lnb_ref, win32_ref, *rest):
    back_consts = list(rest[:16])
    d_ref, cw_ref, cb_ref = back_consts[4], back_consts[8], back_consts[9]
    mix_ref = rest[16]
    scratch = rest[17:-3]
    win_ref, wglu_ref, wmix_ref = rest[-3:]
    wglu32_ref, wmix32_ref = back_consts[5], back_consts[15]
    back_consts[5], back_consts[15] = wglu_ref, wmix_ref
    z_ref, uc_ref, h0b_ref, _, _, st_ref, xl_ref, xc_ref, xcb_ref = scratch[:9]
    hl_ref, gz_ref, du_ref = scratch[12], scratch[17], scratch[18]
    batch = st_ref.shape[1]
    halo = (CONV_WIDTH - 1) * batch

    @pl.when(pl.program_id(0) == 0)
    def _():
        _cast_rows(win32_ref, win_ref)
        _cast_rows(wglu32_ref, wglu_ref)
        _cast_rows(wmix32_ref, wmix_ref)
        st_ref[...] = jnp.zeros_like(st_ref)
        hl_ref[...] = jnp.zeros_like(hl_ref)
        xl_ref[0:halo, :] = jnp.zeros((halo, D_LRU), F32)
        _mixer_norm(xfirst_ref, lng_ref, lnb_ref, h0b_ref)
        _mixer_project_in(h0b_ref, win_ref, z_ref, uc_ref, batch)
        _mixer_conv(z_ref, cw_ref, cb_ref, d_ref, xl_ref, xc_ref, xcb_ref, gz_ref, du_ref, batch)
        _mixer_norm(xsecond_ref, lng_ref, lnb_ref, h0b_ref)

    _mixer_step(xnext_ref, lng_ref, lnb_ref, win_ref, *back_consts, mix_ref, *scratch)


def _mixer(x_tm, batch, ln_g, ln_b, w_in, w_in_chunk, apow_re, apow_im, w_out_chunk, d, w_glu, b_glu, g_s5,
           conv_w, conv_b, wg, b_a, b_x, lam, g_lru, w_mix):
    n_rows = x_tm.shape[0]
    rows = MIXER_STEPS * batch
    n_blocks = n_rows // rows
    halo = (CONV_WIDTH - 1) * batch
    chunk_rows = rows // S5_CHUNK
    chunk_cols = S5_CHUNK * S5_BLOCK_CH
    consts = (ln_g, ln_b, w_in, w_in_chunk, apow_re, apow_im, w_out_chunk, d, w_glu, b_glu, g_s5,
              conv_w, conv_b, wg, b_a, b_x, lam, g_lru, w_mix)
    return pl.pallas_call(
        _mixer_body,
        grid=(n_blocks,),
        in_specs=[_resident_spec((rows, D_MODEL), (0, 0)),
                  _resident_spec((rows, D_MODEL), (min(1, n_blocks - 1), 0)),
                  pl.BlockSpec((rows, D_MODEL), lambda i: (jnp.minimum(i + 2, n_blocks - 1), 0))]
                 + [_resident_spec(c.shape) for c in consts],
        out_specs=pl.BlockSpec((rows, D_MODEL), lambda i: (i, 0)),
        out_shape=jax.ShapeDtypeStruct((n_rows, D_MODEL), BF16),
        scratch_shapes=[
            pltpu.VMEM((rows, D_IN), F32),
            pltpu.VMEM((S5_BLOCKS, chunk_rows, chunk_cols), BF16),
            pltpu.VMEM((rows, D_MODEL), BF16),
            pltpu.VMEM((S5_BLOCKS, chunk_rows, 2 * S5_BLOCK_STATES + chunk_cols), F32),
            pltpu.VMEM((S5_BLOCKS, chunk_rows, 2 * S5_BLOCK_STATES), BF16),
            pltpu.VMEM((S5_BLOCKS, batch, 2 * S5_BLOCK_STATES), F32),
            pltpu.VMEM((rows + halo, D_LRU), F32),
            pltpu.VMEM((rows, D_LRU), F32),
            pltpu.VMEM((rows, D_LRU), BF16),
            pltpu.VMEM((rows, 2 * D_LRU), F32),
            pltpu.VMEM((rows, D_LRU), F32),
            pltpu.VMEM((rows, D_LRU), F32),
            pltpu.VMEM((batch, D_LRU), F32),
            pltpu.VMEM((S5_BLOCKS, chunk_rows, chunk_cols), F32),
            pltpu.VMEM((rows, D_S5), F32),
            pltpu.VMEM((rows, D_S5), BF16),
            pltpu.VMEM((rows, D_S5), F32),
            pltpu.VMEM((rows, D_LRU), F32),
            pltpu.VMEM((rows, D_S5), F32),
            pltpu.VMEM((rows, D_MODEL), BF16),
            pltpu.VMEM(w_in.shape, BF16),
            pltpu.VMEM(w_glu.shape, BF16),
            pltpu.VMEM(w_mix.shape, BF16),
        ],
        compiler_params=pltpu.CompilerParams(
            dimension_semantics=("arbitrary",), vmem_limit_bytes=V7X_SCOPED_VMEM_BYTES),
        name="mixer",
    )(x_tm, x_tm, x_tm, *consts)


def _kv_body(mem_ref, g_ref, b_ref, wk32_ref, wv32_ref, k_ref, v_ref, wk_ref, wv_ref):
    @pl.when(pl.program_id(0) == 0)
    def _():
        _cast_rows(wk32_ref, wk_ref)
        _cast_rows(wv32_ref, wv_ref)

    mn = _layer_norm(mem_ref[...], g_ref[...], b_ref[...]).astype(BF16)
    k_ref[...] = (_dot(mn, wk_ref[...]) * (CA_HEAD_DIM ** -0.5)).astype(k_ref.dtype)
    v_ref[...] = _dot(mn, wv_ref[...]).astype(v_ref.dtype)


def _kv(mem, g, b, w_k, w_v):
    bsz, mlen, _ = mem.shape
    n_rows = bsz * mlen
    blk = pl.BlockSpec((KV_ROWS, D_MODEL), lambda i: (i, 0))
    out = jax.ShapeDtypeStruct((n_rows, D_MODEL), BF16)
    k, v = pl.pallas_call(
        _kv_body,
        grid=(n_rows // KV_ROWS,),
        in_specs=[blk, _const_spec(g.shape), _const_spec(b.shape), _resident_spec(w_k.shape),
                  _resident_spec(w_v.shape)],
        out_specs=(blk, blk),
        out_shape=(out, out),
        scratch_shapes=[pltpu.VMEM(w_k.shape, BF16), pltpu.VMEM(w_v.shape, BF16)],
        compiler_params=pltpu.CompilerParams(
            dimension_semantics=("arbitrary",), vmem_limit_bytes=V7X_SCOPED_VMEM_BYTES),
        name="kv",
    )(mem.reshape(n_rows, D_MODEL), g, b, w_k, w_v)
    return k.reshape(mem.shape), v.reshape(mem.shape)


def _attn_norms(x_ref, mix_ref, lng_ref, lnb_ref, g1_ref, b1_ref, h1_ref, h1b_ref, slot):
    for r in _row_chunks(x_ref.shape[0]):
        h0 = _layer_norm(x_ref[r, :], lng_ref[...], lnb_ref[...])
        h1 = _layer_norm(ALPHA * h0 + mix_ref[r, :], g1_ref[...], b1_ref[...])
        h1_ref[slot, r, :] = h1
        h1b_ref[r, :] = h1.astype(BF16)


def _attn_body(x0_ref, m0_ref, x1_ref, m1_ref, xnext_ref, mnext_ref, k_ref, v_ref, wq32_ref, wo32_ref,
               lng_ref, lnb_ref, g1_ref, b1_ref, o_ref,
               h1_ref, q_ref, h1b_ref, s_ref, p_ref, att_ref, ca_ref, wq_ref, wo_ref):
    i = pl.program_id(0)
    slot = i % 2
    rows = o_ref.shape[0]
    norm_refs = (lng_ref, lnb_ref, g1_ref, b1_ref, h1_ref, h1b_ref)

    @pl.when(i == 0)
    def _():
        _cast_rows(wq32_ref, wq_ref)
        _cast_rows(wo32_ref, wo_ref)
        _attn_norms(x0_ref, m0_ref, *norm_refs, 0)
        q_ref[...] = _dot(h1b_ref[...], wq_ref[...]).astype(BF16)
        _attn_norms(x1_ref, m1_ref, *norm_refs, 1)

    head_slices = [slice(hd * CA_HEAD_DIM, (hd + 1) * CA_HEAD_DIM) for hd in range(CA_HEADS)]
    for hd, hs in enumerate(head_slices):
        s_ref[hd] = lax.dot_general(q_ref[:, hs], k_ref[:, hs], (((1,), (1,)), ((), ())),
                                    preferred_element_type=F32)
    q_ref[...] = _dot(h1b_ref[...], wq_ref[...]).astype(BF16)
    for r in _row_chunks(rows):
        for hd in range(CA_HEADS):
            s = s_ref[hd, r, :]
            e = jnp.exp(s - jnp.max(s, axis=-1, keepdims=True))
            p_ref[hd, r, :] = (e * (1.0 / jnp.sum(e, axis=-1, keepdims=True))).astype(BF16)
    for hd, hs in enumerate(head_slices):
        att_ref[:, hs] = _dot(p_ref[hd], v_ref[:, hs]).astype(BF16)
    ca_ref[...] = _dot(att_ref[...], wo_ref[...])
    for r in _row_chunks(rows):
        o_ref[r, :] = ALPHA * h1_ref[slot, r, :] + ca_ref[r, :]
    _attn_norms(xnext_ref, mnext_ref, *norm_refs, slot)


def _attn(x, mix_tm, k, v, w_q, w_o, ln_g, ln_b, g1, b1):
    bsz, seq, _ = x.shape
    mlen = k.shape[1]
    n_rows = bsz * seq
    n_blocks = n_rows // ATTN_ROWS
    blocks_per_batch = seq // ATTN_ROWS
    blk = (ATTN_ROWS, D_MODEL)
    first_blk = _resident_spec(blk, (0, 0))
    second_blk = _resident_spec(blk, (min(1, n_blocks - 1), 0))
    next_blk = pl.BlockSpec(blk, lambda i: (jnp.minimum(i + 2, n_blocks - 1), 0))

    def mix_index(block):
        return (block % blocks_per_batch, block // blocks_per_batch)

    first_mix = _resident_spec(blk, mix_index(0))
    second_mix = _resident_spec(blk, mix_index(min(1, n_blocks - 1)))
    next_mix = pl.BlockSpec(blk, lambda i: mix_index(jnp.minimum(i + 2, n_blocks - 1)))
    mem_blk = pl.BlockSpec((None, mlen, D_MODEL), lambda i: (i // blocks_per_batch, 0, 0))
    consts = (w_q, w_o, ln_g, ln_b, g1, b1)
    x2 = x.reshape(n_rows, D_MODEL)
    mix = mix_tm.reshape(seq, bsz * D_MODEL)
    return pl.pallas_call(
        _attn_body,
        grid=(n_blocks,),
        in_specs=[first_blk, first_mix, second_blk, second_mix, next_blk, next_mix, mem_blk, mem_blk]
                 + [_resident_spec(c.shape) for c in consts],
        out_specs=pl.BlockSpec(blk, lambda i: (i, 0)),
        out_shape=jax.ShapeDtypeStruct((n_rows, D_MODEL), F32),
        scratch_shapes=[pltpu.VMEM((2, ATTN_ROWS, D_MODEL), F32),
                        pltpu.VMEM((ATTN_ROWS, D_MODEL), BF16),
                        pltpu.VMEM((ATTN_ROWS, D_MODEL), BF16),
                        pltpu.VMEM((CA_HEADS, ATTN_ROWS, mlen), F32),
                        pltpu.VMEM((CA_HEADS, ATTN_ROWS, mlen), BF16),
                        pltpu.VMEM((ATTN_ROWS, D_MODEL), BF16),
                        pltpu.VMEM((ATTN_ROWS, D_MODEL), F32),
                        pltpu.VMEM(w_q.shape, BF16),
                        pltpu.VMEM(w_o.shape, BF16)],
        compiler_params=pltpu.CompilerParams(
            dimension_semantics=("arbitrary",), vmem_limit_bytes=V7X_SCOPED_VMEM_BYTES),
        name="attn",
    )(x2, mix, x2, mix, x2, mix, k, v, *consts)


def _mlp_body(s_ref, w1_ref, w2_ref, g2_ref, b2_ref, g_ref, b_ref, o_ref):
    for r in _row_chunks(s_ref.shape[0], MLP_GROUP_ROWS):
        h = _layer_norm(s_ref[r, :], g2_ref[...], b2_ref[...])
        hb = h.astype(BF16)
        ff = jnp.zeros(h.shape, F32)
        for j in range(D_FF // MLP_CHUNK):
            lo, hi = j * MLP_CHUNK, (j + 1) * MLP_CHUNK
            t = jnp.maximum(_dot(hb, w1_ref[:, lo:hi]), 0.0)
            ff = ff + _dot((t * t).astype(BF16), w2_ref[lo:hi, :])
        o_ref[r, :] = _layer_norm(ALPHA * h + ff, g_ref[...], b_ref[...])


def _mlp(s, w1, w2, g2, b2, g, b):
    n_rows = s.shape[0]
    row_blk = pl.BlockSpec((MLP_ROWS, D_MODEL), lambda i: (i, 0))
    vecs = (g2, b2, g, b)
    return pl.pallas_call(
        _mlp_body,
        grid=(n_rows // MLP_ROWS,),
        in_specs=[row_blk, _resident_spec(w1.shape), _resident_spec(w2.shape)]
                 + [_const_spec(v.shape) for v in vecs],
        out_specs=row_blk,
        out_shape=jax.ShapeDtypeStruct(s.shape, F32),
        compiler_params=pltpu.CompilerParams(
            dimension_semantics=("arbitrary",), vmem_limit_bytes=V7X_SCOPED_VMEM_BYTES),
        name="mlp",
    )(s, w1, w2, *vecs)


def kernel(x, mem, ln_in_g, ln_in_b, w_in, s5_a_re, s5_a_im, s5_log_dt, s5_b_re, s5_b_im, s5_c_re, s5_c_im, s5_d, s5_w_glu, s5_b_glu, conv_w, conv_b, lru_w_a, lru_b_a, lru_w_x, lru_b_x, lru_lambda, g_s5, g_lru, w_mix_out, ln1_g, ln1_b, mem_ln_g, mem_ln_b, w_q, w_k, w_v, w_o, ln2_g, ln2_b, w_ff1, w_ff2, ln3_g, ln3_b):
    bsz, seq, d_model = x.shape
    assert d_model == D_MODEL and w_in.shape == (DEPTH, D_MODEL, D_IN)
    assert seq % MIXER_STEPS == 0 and seq % ATTN_ROWS == 0 and (bsz * seq) % MLP_ROWS == 0
    assert bsz % 16 == 0, "time-major rows of one step must fill whole bf16 tiles"
    row = lambda v: v.reshape(1, -1).astype(F32)
    l = 0

    apow_re, apow_im, w_in_chunk, w_out_chunk, wg = _s5_prep(
        s5_a_re[l], s5_a_im[l], s5_log_dt[l], s5_b_re[l], s5_b_im[l], s5_c_re[l], s5_c_im[l],
        lru_w_a[l], lru_w_x[l])
    apow_re = apow_re.reshape(S5_BLOCKS, 1, S5_BLOCK_STATES)
    apow_im = apow_im.reshape(S5_BLOCKS, 1, S5_BLOCK_STATES)

    x_tm = jnp.swapaxes(x, 0, 1).reshape(seq * bsz, D_MODEL)
    k, v = _kv(mem, row(mem_ln_g[l]), row(mem_ln_b[l]), w_k[l], w_v[l])
    x_tm, k, v, w_ff1_b, w_ff2_b = lax.optimization_barrier(
        (x_tm, k, v, w_ff1[l].astype(BF16), w_ff2[l].astype(BF16)))
    mix = _mixer(x_tm, bsz, row(ln_in_g), row(ln_in_b), w_in[l], w_in_chunk, apow_re, apow_im,
                 w_out_chunk,
                 row(s5_d[l]), s5_w_glu[l], row(s5_b_glu[l]), row(g_s5[l]),
                 conv_w[l].astype(F32), row(conv_b[l]), wg, row(lru_b_a[l]), row(lru_b_x[l]),
                 row(lru_lambda[l]), row(g_lru[l]), w_mix_out[l])
    pre2 = _attn(x, mix, k, v, w_q[l], w_o[l], row(ln_in_g), row(ln_in_b), row(ln1_g[l]), row(ln1_b[l]))
    out = _mlp(pre2, w_ff1_b, w_ff2_b, row(ln2_g[l]), row(ln2_b[l]), row(ln3_g[l]), row(ln3_b[l]))
    return out.reshape(bsz, seq, D_MODEL)
```

```python
import jax
import jax.numpy as jnp
from jax import lax
from jax.experimental import pallas as pl
from jax.experimental.pallas import tpu as pltpu

F32 = jnp.float32
BF16 = jnp.bfloat16

D_MODEL = 1024
D_S5 = 512
D_LRU = 512
D_IN = D_S5 + 2 * D_LRU
S5_GROUP = 16
S5_GROUPS = 32
S5_STATE = 64
LRU_HEADS = 8
LRU_HEAD_DIM = 64
CONV_WIDTH = 4
LRU_C = 8.0
D_FF = 4096
CA_HEADS = 4
CA_HEAD_DIM = 256
DEPTH = 1
ALPHA = (2 * DEPTH) ** 0.25
LN_EPS = 1e-5
RMS_EPS = 1e-6

S5_BLOCKS = 4
S5_BLOCK_CH = D_S5 // S5_BLOCKS
S5_BLOCK_STATES = 8 * S5_STATE
S5_CHUNK = 2
LRU_BLOCKS = 2
LRU_BLOCK_CH = D_LRU // LRU_BLOCKS

V7X_SCOPED_VMEM_BYTES = 56 * 1024 * 1024
V7X_LANES = 128

MIXER_STEPS = 32
ATTN_ROWS = 512
MLP_ROWS = 1024
MLP_GROUP_ROWS = 512
KV_ROWS = 1024
MLP_CHUNK = 1024
VEC_ROWS = 16


def _layer_norm(x, g, b):
    mu = jnp.mean(x, axis=-1, keepdims=True)
    xc = x - mu
    var = jnp.mean(xc * xc, axis=-1, keepdims=True)
    return xc * lax.rsqrt(var + LN_EPS) * g + b


def _rms_norm(x, g):
    ms = jnp.mean(x * x, axis=-1, keepdims=True)
    return x * lax.rsqrt(ms + RMS_EPS) * g


def _gelu_tanh(x):
    c = 0.7978845608028654
    return 0.5 * x * (1.0 + jnp.tanh(c * (x + 0.044715 * (x * x * x))))


def _dot(a, b):
    return jnp.dot(a, b, preferred_element_type=F32)


def _const_spec(shape):
    zeros = (0,) * len(shape)
    return pl.BlockSpec(shape, lambda *_: zeros)


def _resident_spec(shape, block=None):
    zeros = (0,) * len(shape)
    index = zeros if block is None else block
    return pl.BlockSpec(shape, lambda *_: index, pipeline_mode=pl.Buffered(1))


def _row_chunks(rows, chunk=None):
    chunk = chunk or VEC_ROWS
    return [slice(r, r + chunk) for r in range(0, rows, chunk)]


def _cmul(a_re, a_im, b_re, b_im):
    return a_re * b_re - a_im * b_im, a_re * b_im + a_im * b_re


def _discretise(a_re, a_im, log_dt):
    dt = jnp.exp(log_dt)
    mag = jnp.exp(dt * a_re)
    abar_re = mag * jnp.cos(dt * a_im)
    abar_im = mag * jnp.sin(dt * a_im)
    den = a_re * a_re + a_im * a_im
    q_re = ((abar_re - 1.0) * a_re + abar_im * a_im) / den
    q_im = (abar_im * a_re - (abar_re - 1.0) * a_im) / den
    return abar_re, abar_im, q_re, q_im


def _powers(abar_re, abar_im):
    pows = [(jnp.ones_like(abar_re), jnp.zeros_like(abar_im))]
    for _ in range(S5_CHUNK):
        pows.append(_cmul(pows[-1][0], pows[-1][1], abar_re, abar_im))
    return pows


def _block_diag_tiles(stacked, group_rows, group_cols, n_groups):
    rows = n_groups * group_rows
    cols = n_groups * group_cols
    row_shift = group_rows.bit_length() - 1
    col_shift = group_cols.bit_length() - 1
    assert group_rows == 1 << row_shift and group_cols == 1 << col_shift
    src = lax.broadcasted_iota(jnp.int32, (group_cols, cols), 0)
    dst = lax.broadcasted_iota(jnp.int32, (group_cols, cols), 1)
    replicate = jnp.where((dst & (group_cols - 1)) == src, 1.0, 0.0).astype(BF16)
    tiled = _dot(stacked.astype(BF16), replicate)
    row_group = lax.broadcasted_iota(jnp.int32, (rows, cols), 0) >> row_shift
    col_group = lax.broadcasted_iota(jnp.int32, (rows, cols), 1) >> col_shift
    mask = row_group == col_group
    return [jnp.where(mask, tiled[i * rows:(i + 1) * rows], 0.0).astype(BF16)
            for i in range(stacked.shape[0] // rows)]


def _s5_prep_body(are_ref, aim_ref, ldt_ref, arec_ref, aimc_ref, ldtc_ref, bre_ref, bim_ref, cre_ref, cim_ref,
                  cret_ref, cimt_ref, wa_ref, wx_ref,
                  apow_re_ref, apow_im_ref, wic_ref, woc_ref, wg_ref):
    g8 = S5_GROUPS // S5_BLOCKS
    abar_re, abar_im, q_re, q_im = _discretise(are_ref[...], aim_ref[...], ldt_ref[...])
    pows = _powers(abar_re, abar_im)
    apow_re_ref[...] = pows[S5_CHUNK][0]
    apow_im_ref[...] = pows[S5_CHUNK][1]
    bbar_re, bbar_im = _cmul(q_re, q_im, bre_ref[...], bim_ref[...])
    c_re = cre_ref[...]
    c_im = cim_ref[...]
    abar_re_c, abar_im_c, _, _ = _discretise(arec_ref[...], aimc_ref[...], ldtc_ref[...])
    pows_c = _powers(abar_re_c, abar_im_c)
    ct_re = cret_ref[...]
    ct_im = cimt_ref[...]

    lane = lax.broadcasted_iota(jnp.int32, (S5_GROUPS, S5_GROUP, S5_GROUP), 2)
    within = 2 * S5_BLOCK_STATES
    wic_ref[:, :, within:] = jnp.zeros((S5_BLOCKS, S5_CHUNK * S5_BLOCK_CH, S5_CHUNK * S5_BLOCK_CH), BF16)
    for j in range(S5_CHUNK):
        bt = _cmul(*pows[S5_CHUNK - 1 - j], bbar_re, bbar_im)
        et = _cmul(*pows_c[j + 1], ct_re, ct_im)
        ce_re, ce_im = _cmul(*pows[j], c_re, c_im)
        lag = jnp.zeros(lane.shape, F32)
        for ho in range(S5_GROUP):
            col = jnp.sum(bbar_re * ce_re[:, ho:ho + 1, :] - bbar_im * ce_im[:, ho:ho + 1, :],
                          axis=-1, keepdims=True)
            lag = jnp.where(lane == ho, col, lag)
        rows_j = slice(j * S5_BLOCK_CH, (j + 1) * S5_BLOCK_CH)
        for part in range(2):
            cols = slice(part * S5_BLOCK_STATES, (part + 1) * S5_BLOCK_STATES)
            in_tiles = _block_diag_tiles(bt[part].reshape(S5_GROUPS * S5_GROUP, S5_STATE), S5_GROUP, S5_STATE, g8)
            out_tiles = _block_diag_tiles(et[part].reshape(S5_GROUPS * S5_STATE, S5_GROUP), S5_STATE, S5_GROUP, g8)
            for k in range(S5_BLOCKS):
                wic_ref[k, rows_j, cols] = in_tiles[k]
                woc_ref[k, cols, rows_j] = out_tiles[k] if part == 0 else -out_tiles[k]
        lag_tiles = _block_diag_tiles(lag.reshape(S5_GROUPS * S5_GROUP, S5_GROUP), S5_GROUP, S5_GROUP, g8)
        for k in range(S5_BLOCKS):
            for t_in in range(S5_CHUNK - j):
                t_out = t_in + j
                wic_ref[k, t_in * S5_BLOCK_CH:(t_in + 1) * S5_BLOCK_CH,
                        within + t_out * S5_BLOCK_CH:within + (t_out + 1) * S5_BLOCK_CH] = lag_tiles[k]

    heads = LRU_HEADS // LRU_BLOCKS
    for part, w_ref in enumerate((wa_ref, wx_ref)):
        gate_tiles = _block_diag_tiles(w_ref[...].reshape(D_LRU, LRU_HEAD_DIM), LRU_HEAD_DIM, LRU_HEAD_DIM, heads)
        for kb in range(LRU_BLOCKS):
            wg_ref[kb, :, part * LRU_BLOCK_CH:(part + 1) * LRU_BLOCK_CH] = gate_tiles[kb]


def _s5_prep(a_re, a_im, log_dt, b_re, b_im, c_re, c_im, w_a, w_x):
    g, p, h = b_re.shape
    f = jax.ShapeDtypeStruct
    chunk_cols = S5_CHUNK * S5_BLOCK_CH
    return pl.pallas_call(
        _s5_prep_body,
        out_shape=(f((g, 1, p), F32), f((g, 1, p), F32),
                   f((S5_BLOCKS, chunk_cols, 2 * S5_BLOCK_STATES + chunk_cols), BF16),
                   f((S5_BLOCKS, 2 * S5_BLOCK_STATES, chunk_cols), BF16),
                   f((LRU_BLOCKS, LRU_BLOCK_CH, 2 * LRU_BLOCK_CH), BF16)),
        compiler_params=pltpu.CompilerParams(vmem_limit_bytes=V7X_SCOPED_VMEM_BYTES),
        name="s5_prep",
    )(a_re.reshape(g, 1, p), a_im.reshape(g, 1, p), log_dt.reshape(g, 1, 1),
      a_re.reshape(g, p, 1), a_im.reshape(g, p, 1), log_dt.reshape(g, 1, 1),
      jnp.swapaxes(b_re, 1, 2), jnp.swapaxes(b_im, 1, 2), c_re, c_im,
      jnp.swapaxes(c_re, 1, 2), jnp.swapaxes(c_im, 1, 2), w_a, w_x)


def _mixer_norm(x_ref, lng_ref, lnb_ref, h0b_ref):
    for r in _row_chunks(x_ref.shape[0]):
        h0b_ref[r, :] = _layer_norm(x_ref[r, :], lng_ref[...], lnb_ref[...]).astype(BF16)


def _mixer_project_in(h0b_ref, win_ref, z_ref, uc_ref, batch):
    for piece in range(D_IN // D_S5):
        cols = slice(piece * D_S5, (piece + 1) * D_S5)
        z = _dot(h0b_ref[...], win_ref[:, cols])
        z_ref[:, cols] = z
        if piece == 0:
            zb = z.astype(BF16)
            for step in range(z.shape[0] // batch):
                c, t = divmod(step, S5_CHUNK)
                for k in range(S5_BLOCKS):
                    uc_ref[k, c * batch:(c + 1) * batch, t * S5_BLOCK_CH:(t + 1) * S5_BLOCK_CH] = (
                        zb[step * batch:(step + 1) * batch, k * S5_BLOCK_CH:(k + 1) * S5_BLOCK_CH])


def _mixer_conv(z_ref, cw_ref, cb_ref, d_ref, xl_ref, xc_ref, xcb_ref, gz_ref, du_ref, batch):
    rows = z_ref.shape[0]
    halo = (CONV_WIDTH - 1) * batch
    xl_ref[halo:halo + rows, :] = z_ref[:, D_S5:D_S5 + D_LRU]
    for r in _row_chunks(rows, 2 * VEC_ROWS):
        xc = cb_ref[...]
        for j in range(CONV_WIDTH):
            xc = xc + xl_ref[j * batch + r.start:j * batch + r.stop, :] * cw_ref[j:j + 1, :]
        xc_ref[r, :] = xc
        xcb_ref[r, :] = xc.astype(BF16)
        gz_ref[r, :] = _gelu_tanh(z_ref[r, D_S5 + D_LRU:D_IN])
        du_ref[r, :] = d_ref[...] * z_ref[r, 0:D_S5]
    xl_ref[0:halo, :] = xl_ref[rows:rows + halo, :]


def _mixer_step(xnext_ref, lng_ref, lnb_ref, win_ref, wic_ref, apr_ref, api_ref, woc_ref, d_ref,
                wglu_ref, bglu_ref, gs5_ref, cw_ref, cb_ref, wg_ref, ba_ref, bx_ref, lam_ref, glru_ref, wmix_ref,
                mix_ref,
                z_ref, uc_ref, h0b_ref, inc_ref, hp_ref, st_ref, xl_ref, xc_ref, xcb_ref, pre_ref, a_ref, hx_ref,
                hl_ref, yo_ref, y32_ref, yb_ref, gate_ref, gz_ref, du_ref, y_ref):
    rows = y_ref.shape[0]
    batch = st_ref.shape[1]
    steps = rows // batch
    chunks = steps // S5_CHUNK
    wide = _row_chunks(rows, 2 * VEC_ROWS)
    sre = slice(0, S5_BLOCK_STATES)
    sim = slice(S5_BLOCK_STATES, 2 * S5_BLOCK_STATES)

    for k in range(S5_BLOCKS):
        inc_ref[k] = _dot(uc_ref[k], wic_ref[k])
    for kb in range(LRU_BLOCKS):
        lo, hi_ = kb * LRU_BLOCK_CH, (kb + 1) * LRU_BLOCK_CH
        pre = _dot(xcb_ref[:, lo:hi_], wg_ref[kb])
        pre_ref[:, lo:hi_] = pre[:, 0:LRU_BLOCK_CH]
        pre_ref[:, D_LRU + lo:D_LRU + hi_] = pre[:, LRU_BLOCK_CH:2 * LRU_BLOCK_CH]
    _mixer_project_in(h0b_ref, win_ref, z_ref, uc_ref, batch)

    for k in range(S5_BLOCKS):
        ar = jnp.broadcast_to(apr_ref[k], (batch, S5_BLOCK_STATES))
        ai = jnp.broadcast_to(api_ref[k], (batch, S5_BLOCK_STATES))
        hr = st_ref[k, :, sre]
        hi = st_ref[k, :, sim]
        for c in range(chunks):
            r = slice(c * batch, (c + 1) * batch)
            hp_ref[k, r, sre] = hr.astype(BF16)
            hp_ref[k, r, sim] = hi.astype(BF16)
            hr, hi = (ar * hr - ai * hi + inc_ref[k, r, sre],
                      ar * hi + ai * hr + inc_ref[k, r, sim])
        st_ref[k, :, sre] = hr
        st_ref[k, :, sim] = hi
        yo_ref[k] = _dot(hp_ref[k], woc_ref[k])
        if k == 1:
            lam = lam_ref[...]
            softplus_neg_lam = jnp.maximum(-lam, 0.0) + jnp.log1p(jnp.exp(-jnp.abs(lam)))
            for r in wide:
                gate_a = jax.nn.sigmoid(pre_ref[r, 0:D_LRU] + ba_ref[...])
                gate_x = jax.nn.sigmoid(pre_ref[r, D_LRU:2 * D_LRU] + bx_ref[...])
                log_a = -LRU_C * gate_a * softplus_neg_lam
                a = jnp.exp(log_a)
                m2 = -jnp.tanh(log_a) * (a * a + 1.0)
                mult = jnp.where(m2 > 0.0, m2 * lax.rsqrt(m2), 0.0)
                a_ref[r, :] = a
                hx_ref[r, :] = mult * (gate_x * xc_ref[r, :])
    hl = hl_ref[...]
    for t in range(steps):
        r = slice(t * batch, (t + 1) * batch)
        hl = a_ref[r, :] * hl + hx_ref[r, :]
        hx_ref[r, :] = hl
    hl_ref[...] = hl

    within = 2 * S5_BLOCK_STATES
    for step in range(steps):
        c, t = divmod(step, S5_CHUNK)
        r = slice(step * batch, (step + 1) * batch)
        rc = slice(c * batch, (c + 1) * batch)
        cols = slice(t * S5_BLOCK_CH, (t + 1) * S5_BLOCK_CH)
        ys = jnp.concatenate(
            [yo_ref[k, rc, cols] + inc_ref[k, rc, within + cols.start:within + cols.stop]
             for k in range(S5_BLOCKS)], axis=1)
        y = _gelu_tanh(ys + du_ref[r, :])
        y32_ref[r, :] = y
        yb_ref[r, :] = y.astype(BF16)
    gate_ref[...] = _dot(yb_ref[...], wglu_ref[...])
    for r in wide:
        y = y32_ref[r, :] * jax.nn.sigmoid(gate_ref[r, :] + bglu_ref[...])
        y_ref[r, 0:D_S5] = _rms_norm(y, gs5_ref[...]).astype(y_ref.dtype)
        ylru = hx_ref[r, :] * gz_ref[r, :]
        y_ref[r, D_S5:D_MODEL] = _rms_norm(ylru, glru_ref[...]).astype(y_ref.dtype)
    mix = _dot(y_ref[...], wmix_ref[...])
    for c in range(D_MODEL // V7X_LANES):
        piece = mix[:, c * V7X_LANES:(c + 1) * V7X_LANES].reshape(steps, batch, V7X_LANES)
        mix_ref[c] = jnp.swapaxes(piece, 0, 1).reshape(rows, V7X_LANES).astype(mix_ref.dtype)

    _mixer_conv(z_ref, cw_ref, cb_ref, d_ref, xl_ref, xc_ref, xcb_ref, gz_ref, du_ref, batch)
    _mixer_norm(xnext_ref, lng_ref, lnb_ref, h0b_ref)


def _cast_rows(src_ref, dst_ref, chunk=64):
    for r in _row_chunks(src_ref.shape[0], chunk):
        dst_ref[r, :] = src_ref[r, :].astype(dst_ref.dtype)


def _mixer_body(xfirst_ref, xsecond_ref, xnext_ref, lng_ref, lnb_ref, win32_ref, *rest):
    back_consts = list(rest[:16])
    d_ref, cw_ref, cb_ref = back_consts[4], back_consts[8], back_consts[9]
    mix_ref = rest[16]
    scratch = rest[17:-3]
    win_ref, wglu_ref, wmix_ref = rest[-3:]
    wglu32_ref, wmix32_ref = back_consts[5], back_consts[15]
    back_consts[5], back_consts[15] = wglu_ref, wmix_ref
    z_ref, uc_ref, h0b_ref, _, _, st_ref, xl_ref, xc_ref, xcb_ref = scratch[:9]
    hl_ref, gz_ref, du_ref = scratch[12], scratch[17], scratch[18]
    batch = st_ref.shape[1]
    halo = (CONV_WIDTH - 1) * batch

    @pl.when(pl.program_id(0) == 0)
    def _():
        _cast_rows(win32_ref, win_ref)
        _cast_rows(wglu32_ref, wglu_ref)
        _cast_rows(wmix32_ref, wmix_ref)
        st_ref[...] = jnp.zeros_like(st_ref)
        hl_ref[...] = jnp.zeros_like(hl_ref)
        xl_ref[0:halo, :] = jnp.zeros((halo, D_LRU), F32)
        _mixer_norm(xfirst_ref, lng_ref, lnb_ref, h0b_ref)
        _mixer_project_in(h0b_ref, win_ref, z_ref, uc_ref, batch)
        _mixer_conv(z_ref, cw_ref, cb_ref, d_ref, xl_ref, xc_ref, xcb_ref, gz_ref, du_ref, batch)
        _mixer_norm(xsecond_ref, lng_ref, lnb_ref, h0b_ref)

    _mixer_step(xnext_ref, lng_ref, lnb_ref, win_ref, *back_consts, mix_ref, *scratch)


def _mixer(x_tm, batch, ln_g, ln_b, w_in, w_in_chunk, apow_re, apow_im, w_out_chunk, d, w_glu, b_glu, g_s5,
           conv_w, conv_b, wg, b_a, b_x, lam, g_lru, w_mix):
    n_rows = x_tm.shape[0]
    rows = MIXER_STEPS * batch
    n_blocks = n_rows // rows
    halo = (CONV_WIDTH - 1) * batch
    chunk_rows = rows // S5_CHUNK
    chunk_cols = S5_CHUNK * S5_BLOCK_CH
    slabs = D_MODEL // V7X_LANES
    consts = (ln_g, ln_b, w_in, w_in_chunk, apow_re, apow_im, w_out_chunk, d, w_glu, b_glu, g_s5,
              conv_w, conv_b, wg, b_a, b_x, lam, g_lru, w_mix)
    mix = pl.pallas_call(
        _mixer_body,
        grid=(n_blocks,),
        in_specs=[_resident_spec((rows, D_MODEL), (0, 0)),
                  _resident_spec((rows, D_MODEL), (min(1, n_blocks - 1), 0)),
                  pl.BlockSpec((rows, D_MODEL), lambda i: (jnp.minimum(i + 2, n_blocks - 1), 0))]
                 + [_resident_spec(c.shape) for c in consts],
        out_specs=pl.BlockSpec((slabs, None, rows, V7X_LANES), lambda i: (0, i, 0, 0)),
        out_shape=jax.ShapeDtypeStruct((slabs, n_blocks, rows, V7X_LANES), BF16),
        scratch_shapes=[
            pltpu.VMEM((rows, D_IN), F32),
            pltpu.VMEM((S5_BLOCKS, chunk_rows, chunk_cols), BF16),
            pltpu.VMEM((rows, D_MODEL), BF16),
            pltpu.VMEM((S5_BLOCKS, chunk_rows, 2 * S5_BLOCK_STATES + chunk_cols), F32),
            pltpu.VMEM((S5_BLOCKS, chunk_rows, 2 * S5_BLOCK_STATES), BF16),
            pltpu.VMEM((S5_BLOCKS, batch, 2 * S5_BLOCK_STATES), F32),
            pltpu.VMEM((rows + halo, D_LRU), F32),
            pltpu.VMEM((rows, D_LRU), F32),
            pltpu.VMEM((rows, D_LRU), BF16),
            pltpu.VMEM((rows, 2 * D_LRU), F32),
            pltpu.VMEM((rows, D_LRU), F32),
            pltpu.VMEM((rows, D_LRU), F32),
            pltpu.VMEM((batch, D_LRU), F32),
            pltpu.VMEM((S5_BLOCKS, chunk_rows, chunk_cols), F32),
            pltpu.VMEM((rows, D_S5), F32),
            pltpu.VMEM((rows, D_S5), BF16),
            pltpu.VMEM((rows, D_S5), F32),
            pltpu.VMEM((rows, D_LRU), F32),
            pltpu.VMEM((rows, D_S5), F32),
            pltpu.VMEM((rows, D_MODEL), BF16),
            pltpu.VMEM(w_in.shape, BF16),
            pltpu.VMEM(w_glu.shape, BF16),
            pltpu.VMEM(w_mix.shape, BF16),
        ],
        compiler_params=pltpu.CompilerParams(
            dimension_semantics=("arbitrary",), vmem_limit_bytes=V7X_SCOPED_VMEM_BYTES),
        name="mixer",
    )(x_tm, x_tm, x_tm, *consts)
    return mix.reshape(slabs, n_blocks, batch, MIXER_STEPS, V7X_LANES)


def _kv_body(mem_ref, g_ref, b_ref, wk32_ref, wv32_ref, k_ref, v_ref, wk_ref, wv_ref):
    @pl.when(pl.program_id(0) == 0)
    def _():
        _cast_rows(wk32_ref, wk_ref)
        _cast_rows(wv32_ref, wv_ref)

    mn = _layer_norm(mem_ref[...], g_ref[...], b_ref[...]).astype(BF16)
    k_ref[...] = (_dot(mn, wk_ref[...]) * (CA_HEAD_DIM ** -0.5)).astype(k_ref.dtype)
    v_ref[...] = _dot(mn, wv_ref[...]).astype(v_ref.dtype)


def _kv(mem, g, b, w_k, w_v):
    bsz, mlen, _ = mem.shape
    n_rows = bsz * mlen
    blk = pl.BlockSpec((KV_ROWS, D_MODEL), lambda i: (i, 0))
    out = jax.ShapeDtypeStruct((n_rows, D_MODEL), BF16)
    k, v = pl.pallas_call(
        _kv_body,
        grid=(n_rows // KV_ROWS,),
        in_specs=[blk, _const_spec(g.shape), _const_spec(b.shape), _resident_spec(w_k.shape),
                  _resident_spec(w_v.shape)],
        out_specs=(blk, blk),
        out_shape=(out, out),
        scratch_shapes=[pltpu.VMEM(w_k.shape, BF16), pltpu.VMEM(w_v.shape, BF16)],
        compiler_params=pltpu.CompilerParams(
            dimension_semantics=("arbitrary",), vmem_limit_bytes=V7X_SCOPED_VMEM_BYTES),
        name="kv",
    )(mem.reshape(n_rows, D_MODEL), g, b, w_k, w_v)
    return k.reshape(mem.shape), v.reshape(mem.shape)


def _attn_norms(x_ref, mix_ref, lng_ref, lnb_ref, g1_ref, b1_ref, h1_ref, h1b_ref, slot):
    slabs, _, steps, _ = mix_ref.shape
    for r in _row_chunks(x_ref.shape[0]):
        blk, t = divmod(r.start, steps)
        mix = jnp.concatenate([mix_ref[c, blk, t:t + VEC_ROWS, :] for c in range(slabs)], axis=-1)
        h0 = _layer_norm(x_ref[r, :], lng_ref[...], lnb_ref[...])
        h1 = _layer_norm(ALPHA * h0 + mix, g1_ref[...], b1_ref[...])
        h1_ref[slot, r, :] = h1
        h1b_ref[r, :] = h1.astype(BF16)


def _attn_body(x0_ref, m0_ref, x1_ref, m1_ref, xnext_ref, mnext_ref, k_ref, v_ref, wq32_ref, wo32_ref,
               lng_ref, lnb_ref, g1_ref, b1_ref, o_ref,
               h1_ref, q_ref, h1b_ref, s_ref, p_ref, att_ref, ca_ref, wq_ref, wo_ref):
    i = pl.program_id(0)
    slot = i % 2
    rows = o_ref.shape[0]
    norm_refs = (lng_ref, lnb_ref, g1_ref, b1_ref, h1_ref, h1b_ref)

    @pl.when(i == 0)
    def _():
        _cast_rows(wq32_ref, wq_ref)
        _cast_rows(wo32_ref, wo_ref)
        _attn_norms(x0_ref, m0_ref, *norm_refs, 0)
        q_ref[...] = _dot(h1b_ref[...], wq_ref[...]).astype(BF16)
        _attn_norms(x1_ref, m1_ref, *norm_refs, 1)

    head_slices = [slice(hd * CA_HEAD_DIM, (hd + 1) * CA_HEAD_DIM) for hd in range(CA_HEADS)]
    for hd, hs in enumerate(head_slices):
        s_ref[hd] = lax.dot_general(q_ref[:, hs], k_ref[:, hs], (((1,), (1,)), ((), ())),
                                    preferred_element_type=F32)
    q_ref[...] = _dot(h1b_ref[...], wq_ref[...]).astype(BF16)
    for r in _row_chunks(rows):
        for hd in range(CA_HEADS):
            s = s_ref[hd, r, :]
            e = jnp.exp(s - jnp.max(s, axis=-1, keepdims=True))
            p_ref[hd, r, :] = (e * (1.0 / jnp.sum(e, axis=-1, keepdims=True))).astype(BF16)
    for hd, hs in enumerate(head_slices):
        att_ref[:, hs] = _dot(p_ref[hd], v_ref[:, hs]).astype(BF16)
    ca_ref[...] = _dot(att_ref[...], wo_ref[...])
    for r in _row_chunks(rows):
        o_ref[r, :] = ALPHA * h1_ref[slot, r, :] + ca_ref[r, :]
    _attn_norms(xnext_ref, mnext_ref, *norm_refs, slot)


def _attn(x, mix, k, v, w_q, w_o, ln_g, ln_b, g1, b1):
    bsz, seq, _ = x.shape
    mlen = k.shape[1]
    n_rows = bsz * seq
    n_blocks = n_rows // ATTN_ROWS
    blocks_per_batch = seq // ATTN_ROWS
    slabs, _, _, steps, lanes = mix.shape
    assert ATTN_ROWS % steps == 0 and steps % VEC_ROWS == 0
    blk = (ATTN_ROWS, D_MODEL)
    first_blk = _resident_spec(blk, (0, 0))
    second_blk = _resident_spec(blk, (min(1, n_blocks - 1), 0))
    next_blk = pl.BlockSpec(blk, lambda i: (jnp.minimum(i + 2, n_blocks - 1), 0))
    mix_blk = (slabs, ATTN_ROWS // steps, None, steps, lanes)

    def mix_index(block):
        return (0, block % blocks_per_batch, block // blocks_per_batch, 0, 0)

    first_mix = pl.BlockSpec(mix_blk, lambda i: mix_index(0), pipeline_mode=pl.Buffered(1))
    second_mix = pl.BlockSpec(mix_blk, lambda i: mix_index(min(1, n_blocks - 1)), pipeline_mode=pl.Buffered(1))
    next_mix = pl.BlockSpec(mix_blk, lambda i: mix_index(jnp.minimum(i + 2, n_blocks - 1)))
    mem_blk = pl.BlockSpec((None, mlen, D_MODEL), lambda i: (i // blocks_per_batch, 0, 0))
    consts = (w_q, w_o, ln_g, ln_b, g1, b1)
    x2 = x.reshape(n_rows, D_MODEL)
    return pl.pallas_call(
        _attn_body,
        grid=(n_blocks,),
        in_specs=[first_blk, first_mix, second_blk, second_mix, next_blk, next_mix, mem_blk, mem_blk]
                 + [_resident_spec(c.shape) for c in consts],
        out_specs=pl.BlockSpec(blk, lambda i: (i, 0)),
        out_shape=jax.ShapeDtypeStruct((n_rows, D_MODEL), F32),
        scratch_shapes=[pltpu.VMEM((2, ATTN_ROWS, D_MODEL), F32),
                        pltpu.VMEM((ATTN_ROWS, D_MODEL), BF16),
                        pltpu.VMEM((ATTN_ROWS, D_MODEL), BF16),
                        pltpu.VMEM((CA_HEADS, ATTN_ROWS, mlen), F32),
                        pltpu.VMEM((CA_HEADS, ATTN_ROWS, mlen), BF16),
                        pltpu.VMEM((ATTN_ROWS, D_MODEL), BF16),
                        pltpu.VMEM((ATTN_ROWS, D_MODEL), F32),
                        pltpu.VMEM(w_q.shape, BF16),
                        pltpu.VMEM(w_o.shape, BF16)],
        compiler_params=pltpu.CompilerParams(
            dimension_semantics=("arbitrary",), vmem_limit_bytes=V7X_SCOPED_VMEM_BYTES),
        name="attn",
    )(x2, mix, x2, mix, x2, mix, k, v, *consts)


def _mlp_body(s_ref, w1_ref, w2_ref, g2_ref, b2_ref, g_ref, b_ref, o_ref):
    for r in _row_chunks(s_ref.shape[0], MLP_GROUP_ROWS):
        h = _layer_norm(s_ref[r, :], g2_ref[...], b2_ref[...])
        hb = h.astype(BF16)
        ff = jnp.zeros(h.shape, F32)
        for j in range(D_FF // MLP_CHUNK):
            lo, hi = j * MLP_CHUNK, (j + 1) * MLP_CHUNK
            t = jnp.maximum(_dot(hb, w1_ref[:, lo:hi]), 0.0)
            ff = ff + _dot((t * t).astype(BF16), w2_ref[lo:hi, :])
        o_ref[r, :] = _layer_norm(ALPHA * h + ff, g_ref[...], b_ref[...])


def _mlp(s, w1, w2, g2, b2, g, b):
    n_rows = s.shape[0]
    row_blk = pl.BlockSpec((MLP_ROWS, D_MODEL), lambda i: (i, 0))
    vecs = (g2, b2, g, b)
    return pl.pallas_call(
        _mlp_body,
        grid=(n_rows // MLP_ROWS,),
        in_specs=[row_blk, _resident_spec(w1.shape), _resident_spec(w2.shape)]
                 + [_const_spec(v.shape) for v in vecs],
        out_specs=row_blk,
        out_shape=jax.ShapeDtypeStruct(s.shape, F32),
        compiler_params=pltpu.CompilerParams(
            dimension_semantics=("arbitrary",), vmem_limit_bytes=V7X_SCOPED_VMEM_BYTES),
        name="mlp",
    )(s, w1, w2, *vecs)


def kernel(x, mem, ln_in_g, ln_in_b, w_in, s5_a_re, s5_a_im, s5_log_dt, s5_b_re, s5_b_im, s5_c_re, s5_c_im, s5_d, s5_w_glu, s5_b_glu, conv_w, conv_b, lru_w_a, lru_b_a, lru_w_x, lru_b_x, lru_lambda, g_s5, g_lru, w_mix_out, ln1_g, ln1_b, mem_ln_g, mem_ln_b, w_q, w_k, w_v, w_o, ln2_g, ln2_b, w_ff1, w_ff2, ln3_g, ln3_b):
    bsz, seq, d_model = x.shape
    assert d_model == D_MODEL and w_in.shape == (DEPTH, D_MODEL, D_IN)
    assert seq % MIXER_STEPS == 0 and seq % ATTN_ROWS == 0 and (bsz * seq) % MLP_ROWS == 0
    assert bsz % 16 == 0, "time-major rows of one step must fill whole bf16 tiles"
    row = lambda v: v.reshape(1, -1).astype(F32)
    l = 0

    apow_re, apow_im, w_in_chunk, w_out_chunk, wg = _s5_prep(
        s5_a_re[l], s5_a_im[l], s5_log_dt[l], s5_b_re[l], s5_b_im[l], s5_c_re[l], s5_c_im[l],
        lru_w_a[l], lru_w_x[l])
    apow_re = apow_re.reshape(S5_BLOCKS, 1, S5_BLOCK_STATES)
    apow_im = apow_im.reshape(S5_BLOCKS, 1, S5_BLOCK_STATES)

    x_tm = jnp.swapaxes(x, 0, 1).reshape(seq * bsz, D_MODEL)
    k, v = _kv(mem, row(mem_ln_g[l]), row(mem_ln_b[l]), w_k[l], w_v[l])
    x_tm, k, v, w_ff1_b, w_ff2_b = lax.optimization_barrier(
        (x_tm, k, v, w_ff1[l].astype(BF16), w_ff2[l].astype(BF16)))
    mix = _mixer(x_tm, bsz, row(ln_in_g), row(ln_in_b), w_in[l], w_in_chunk, apow_re, apow_im,
                 w_out_chunk,
                 row(s5_d[l]), s5_w_glu[l], row(s5_b_glu[l]), row(g_s5[l]),
                 conv_w[l].astype(F32), row(conv_b[l]), wg, row(lru_b_a[l]), row(lru_b_x[l]),
                 row(lru_lambda[l]), row(g_lru[l]), w_mix_out[l])
    pre2 = _attn(x, mix, k, v, w_q[l], w_o[l], row(ln_in_g), row(ln_in_b), row(ln1_g[l]), row(ln1_b[l]))
    out = _mlp(pre2, w_ff1_b, w_ff2_b, row(ln2_g[l]), row(ln2_b[l]), row(ln3_g[l]), row(ln3_b[l]))
    return out.reshape(bsz, seq, D_MODEL)
```

```python
import jax
import jax.numpy as jnp
from jax import lax
from jax.experimental import pallas as pl
from jax.experimental.pallas import tpu as pltpu

F32 = jnp.float32
BF16 = jnp.bfloat16

D_MODEL = 1024
D_S5 = 512
D_LRU = 512
D_IN = D_S5 + 2 * D_LRU
S5_GROUP = 16
S5_GROUPS = 32
S5_STATE = 64
LRU_HEADS = 8
LRU_HEAD_DIM = 64
CONV_WIDTH = 4
LRU_C = 8.0
D_FF = 4096
CA_HEADS = 4
CA_HEAD_DIM = 256
DEPTH = 1
ALPHA = (2 * DEPTH) ** 0.25
LN_EPS = 1e-5
RMS_EPS = 1e-6

S5_BLOCKS = 4
S5_BLOCK_CH = D_S5 // S5_BLOCKS
S5_BLOCK_STATES = 8 * S5_STATE
S5_CHUNK = 2
LRU_BLOCKS = 2
LRU_BLOCK_CH = D_LRU // LRU_BLOCKS

V7X_SCOPED_VMEM_BYTES = 56 * 1024 * 1024
V7X_LANES = 128

MIXER_STEPS = 32
ATTN_ROWS = 512
MLP_ROWS = 1024
MLP_GROUP_ROWS = 512
KV_ROWS = 1024
MLP_CHUNK = 1024
VEC_ROWS = 16


def _layer_norm(x, g, b):
    mu = jnp.mean(x, axis=-1, keepdims=True)
    xc = x - mu
    var = jnp.mean(xc * xc, axis=-1, keepdims=True)
    return xc * lax.rsqrt(var + LN_EPS) * g + b


def _rms_norm(x, g):
    ms = jnp.mean(x * x, axis=-1, keepdims=True)
    return x * lax.rsqrt(ms + RMS_EPS) * g


def _gelu_tanh(x):
    c = 0.7978845608028654
    return 0.5 * x * (1.0 + jnp.tanh(c * (x + 0.044715 * (x * x * x))))


def _dot(a, b):
    return jnp.dot(a, b, preferred_element_type=F32)


def _const_spec(shape):
    zeros = (0,) * len(shape)
    return pl.BlockSpec(shape, lambda *_: zeros)


def _resident_spec(shape, block=None):
    zeros = (0,) * len(shape)
    index = zeros if block is None else block
    return pl.BlockSpec(shape, lambda *_: index, pipeline_mode=pl.Buffered(1))


def _row_chunks(rows, chunk=None):
    chunk = chunk or VEC_ROWS
    return [slice(r, r + chunk) for r in range(0, rows, chunk)]


def _cmul(a_re, a_im, b_re, b_im):
    return a_re * b_re - a_im * b_im, a_re * b_im + a_im * b_re


def _discretise(a_re, a_im, log_dt):
    dt = jnp.exp(log_dt)
    mag = jnp.exp(dt * a_re)
    abar_re = mag * jnp.cos(dt * a_im)
    abar_im = mag * jnp.sin(dt * a_im)
    den = a_re * a_re + a_im * a_im
    q_re = ((abar_re - 1.0) * a_re + abar_im * a_im) / den
    q_im = (abar_im * a_re - (abar_re - 1.0) * a_im) / den
    return abar_re, abar_im, q_re, q_im


def _powers(abar_re, abar_im):
    pows = [(jnp.ones_like(abar_re), jnp.zeros_like(abar_im))]
    for _ in range(S5_CHUNK):
        pows.append(_cmul(pows[-1][0], pows[-1][1], abar_re, abar_im))
    return pows


def _block_diag_tiles(stacked, group_rows, group_cols, n_groups):
    rows = n_groups * group_rows
    cols = n_groups * group_cols
    row_shift = group_rows.bit_length() - 1
    col_shift = group_cols.bit_length() - 1
    assert group_rows == 1 << row_shift and group_cols == 1 << col_shift
    src = lax.broadcasted_iota(jnp.int32, (group_cols, cols), 0)
    dst = lax.broadcasted_iota(jnp.int32, (group_cols, cols), 1)
    replicate = jnp.where((dst & (group_cols - 1)) == src, 1.0, 0.0).astype(BF16)
    tiled = _dot(stacked.astype(BF16), replicate)
    row_group = lax.broadcasted_iota(jnp.int32, (rows, cols), 0) >> row_shift
    col_group = lax.broadcasted_iota(jnp.int32, (rows, cols), 1) >> col_shift
    mask = row_group == col_group
    return [jnp.where(mask, tiled[i * rows:(i + 1) * rows], 0.0).astype(BF16)
            for i in range(stacked.shape[0] // rows)]


def _s5_prep_body(are_ref, aim_ref, ldt_ref, arec_ref, aimc_ref, ldtc_ref, bre_ref, bim_ref, cre_ref, cim_ref,
                  cret_ref, cimt_ref, wa_ref, wx_ref,
                  apow_re_ref, apow_im_ref, wic_ref, woc_ref, wg_ref):
    g8 = S5_GROUPS // S5_BLOCKS
    abar_re, abar_im, q_re, q_im = _discretise(are_ref[...], aim_ref[...], ldt_ref[...])
    pows = _powers(abar_re, abar_im)
    apow_re_ref[...] = pows[S5_CHUNK][0]
    apow_im_ref[...] = pows[S5_CHUNK][1]
    bbar_re, bbar_im = _cmul(q_re, q_im, bre_ref[...], bim_ref[...])
    c_re = cre_ref[...]
    c_im = cim_ref[...]
    abar_re_c, abar_im_c, _, _ = _discretise(arec_ref[...], aimc_ref[...], ldtc_ref[...])
    pows_c = _powers(abar_re_c, abar_im_c)
    ct_re = cret_ref[...]
    ct_im = cimt_ref[...]

    lane = lax.broadcasted_iota(jnp.int32, (S5_GROUPS, S5_GROUP, S5_GROUP), 2)
    within = 2 * S5_BLOCK_STATES
    wic_ref[:, :, within:] = jnp.zeros((S5_BLOCKS, S5_CHUNK * S5_BLOCK_CH, S5_CHUNK * S5_BLOCK_CH), BF16)
    for j in range(S5_CHUNK):
        bt = _cmul(*pows[S5_CHUNK - 1 - j], bbar_re, bbar_im)
        et = _cmul(*pows_c[j + 1], ct_re, ct_im)
        ce_re, ce_im = _cmul(*pows[j], c_re, c_im)
        lag = jnp.zeros(lane.shape, F32)
        for ho in range(S5_GROUP):
            col = jnp.sum(bbar_re * ce_re[:, ho:ho + 1, :] - bbar_im * ce_im[:, ho:ho + 1, :],
                          axis=-1, keepdims=True)
            lag = jnp.where(lane == ho, col, lag)
        rows_j = slice(j * S5_BLOCK_CH, (j + 1) * S5_BLOCK_CH)
        for part in range(2):
            cols = slice(part * S5_BLOCK_STATES, (part + 1) * S5_BLOCK_STATES)
            in_tiles = _block_diag_tiles(bt[part].reshape(S5_GROUPS * S5_GROUP, S5_STATE), S5_GROUP, S5_STATE, g8)
            out_tiles = _block_diag_tiles(et[part].reshape(S5_GROUPS * S5_STATE, S5_GROUP), S5_STATE, S5_GROUP, g8)
            for k in range(S5_BLOCKS):
                wic_ref[k, rows_j, cols] = in_tiles[k]
                woc_ref[k, cols, rows_j] = out_tiles[k] if part == 0 else -out_tiles[k]
        lag_tiles = _block_diag_tiles(lag.reshape(S5_GROUPS * S5_GROUP, S5_GROUP), S5_GROUP, S5_GROUP, g8)
        for k in range(S5_BLOCKS):
            for t_in in range(S5_CHUNK - j):
                t_out = t_in + j
                wic_ref[k, t_in * S5_BLOCK_CH:(t_in + 1) * S5_BLOCK_CH,
                        within + t_out * S5_BLOCK_CH:within + (t_out + 1) * S5_BLOCK_CH] = lag_tiles[k]

    heads = LRU_HEADS // LRU_BLOCKS
    for part, w_ref in enumerate((wa_ref, wx_ref)):
        gate_tiles = _block_diag_tiles(w_ref[...].reshape(D_LRU, LRU_HEAD_DIM), LRU_HEAD_DIM, LRU_HEAD_DIM, heads)
        for kb in range(LRU_BLOCKS):
            wg_ref[kb, :, part * LRU_BLOCK_CH:(part + 1) * LRU_BLOCK_CH] = gate_tiles[kb]


def _s5_prep(a_re, a_im, log_dt, b_re, b_im, c_re, c_im, w_a, w_x):
    g, p, h = b_re.shape
    f = jax.ShapeDtypeStruct
    chunk_cols = S5_CHUNK * S5_BLOCK_CH
    return pl.pallas_call(
        _s5_prep_body,
        out_shape=(f((g, 1, p), F32), f((g, 1, p), F32),
                   f((S5_BLOCKS, chunk_cols, 2 * S5_BLOCK_STATES + chunk_cols), BF16),
                   f((S5_BLOCKS, 2 * S5_BLOCK_STATES, chunk_cols), BF16),
                   f((LRU_BLOCKS, LRU_BLOCK_CH, 2 * LRU_BLOCK_CH), BF16)),
        compiler_params=pltpu.CompilerParams(vmem_limit_bytes=V7X_SCOPED_VMEM_BYTES),
        name="s5_prep",
    )(a_re.reshape(g, 1, p), a_im.reshape(g, 1, p), log_dt.reshape(g, 1, 1),
      a_re.reshape(g, p, 1), a_im.reshape(g, p, 1), log_dt.reshape(g, 1, 1),
      jnp.swapaxes(b_re, 1, 2), jnp.swapaxes(b_im, 1, 2), c_re, c_im,
      jnp.swapaxes(c_re, 1, 2), jnp.swapaxes(c_im, 1, 2), w_a, w_x)


def _mixer_norm(x_ref, lng_ref, lnb_ref, h0b_ref):
    for r in _row_chunks(x_ref.shape[0]):
        h0b_ref[r, :] = _layer_norm(x_ref[r, :], lng_ref[...], lnb_ref[...]).astype(BF16)


def _mixer_project_in(h0b_ref, win_ref, z_ref, uc_ref, batch):
    for piece in range(D_IN // D_S5):
        cols = slice(piece * D_S5, (piece + 1) * D_S5)
        z = _dot(h0b_ref[...], win_ref[:, cols])
        z_ref[:, cols] = z
        if piece == 0:
            zb = z.astype(BF16)
            for step in range(z.shape[0] // batch):
                c, t = divmod(step, S5_CHUNK)
                for k in range(S5_BLOCKS):
                    uc_ref[k, c * batch:(c + 1) * batch, t * S5_BLOCK_CH:(t + 1) * S5_BLOCK_CH] = (
                        zb[step * batch:(step + 1) * batch, k * S5_BLOCK_CH:(k + 1) * S5_BLOCK_CH])


def _mixer_conv(z_ref, cw_ref, cb_ref, d_ref, xl_ref, xc_ref, xcb_ref, gz_ref, du_ref, batch):
    rows = z_ref.shape[0]
    halo = (CONV_WIDTH - 1) * batch
    xl_ref[halo:halo + rows, :] = z_ref[:, D_S5:D_S5 + D_LRU]
    for r in _row_chunks(rows, 2 * VEC_ROWS):
        xc = cb_ref[...]
        for j in range(CONV_WIDTH):
            xc = xc + xl_ref[j * batch + r.start:j * batch + r.stop, :] * cw_ref[j:j + 1, :]
        xc_ref[r, :] = xc
        xcb_ref[r, :] = xc.astype(BF16)
        gz_ref[r, :] = _gelu_tanh(z_ref[r, D_S5 + D_LRU:D_IN])
        du_ref[r, :] = d_ref[...] * z_ref[r, 0:D_S5]
    xl_ref[0:halo, :] = xl_ref[rows:rows + halo, :]


def _mixer_step(xnext_ref, lng_ref, lnb_ref, win_ref, wic_ref, apr_ref, api_ref, woc_ref, d_ref,
                wglu_ref, bglu_ref, gs5_ref, cw_ref, cb_ref, wg_ref, ba_ref, bx_ref, lam_ref, glru_ref, wmix_ref,
                mix_ref,
                z_ref, uc_ref, h0b_ref, inc_ref, hp_ref, st_ref, xl_ref, xc_ref, xcb_ref, pre_ref, a_ref, hx_ref,
                hl_ref, yo_ref, y32_ref, yb_ref, gate_ref, gz_ref, du_ref, y_ref):
    rows = y_ref.shape[0]
    batch = st_ref.shape[1]
    steps = rows // batch
    chunks = steps // S5_CHUNK
    wide = _row_chunks(rows, 2 * VEC_ROWS)
    sre = slice(0, S5_BLOCK_STATES)
    sim = slice(S5_BLOCK_STATES, 2 * S5_BLOCK_STATES)

    for k in range(S5_BLOCKS):
        inc_ref[k] = _dot(uc_ref[k], wic_ref[k])
    for kb in range(LRU_BLOCKS):
        lo, hi_ = kb * LRU_BLOCK_CH, (kb + 1) * LRU_BLOCK_CH
        pre = _dot(xcb_ref[:, lo:hi_], wg_ref[kb])
        pre_ref[:, lo:hi_] = pre[:, 0:LRU_BLOCK_CH]
        pre_ref[:, D_LRU + lo:D_LRU + hi_] = pre[:, LRU_BLOCK_CH:2 * LRU_BLOCK_CH]
    _mixer_project_in(h0b_ref, win_ref, z_ref, uc_ref, batch)

    for k in range(S5_BLOCKS):
        ar = jnp.broadcast_to(apr_ref[k], (batch, S5_BLOCK_STATES))
        ai = jnp.broadcast_to(api_ref[k], (batch, S5_BLOCK_STATES))
        hr = st_ref[k, :, sre]
        hi = st_ref[k, :, sim]
        for c in range(chunks):
            r = slice(c * batch, (c + 1) * batch)
            hp_ref[k, r, sre] = hr.astype(BF16)
            hp_ref[k, r, sim] = hi.astype(BF16)
            hr, hi = (ar * hr - ai * hi + inc_ref[k, r, sre],
                      ar * hi + ai * hr + inc_ref[k, r, sim])
        st_ref[k, :, sre] = hr
        st_ref[k, :, sim] = hi
        yo_ref[k] = _dot(hp_ref[k], woc_ref[k])
        if k == 1:
            lam = lam_ref[...]
            softplus_neg_lam = jnp.maximum(-lam, 0.0) + jnp.log1p(jnp.exp(-jnp.abs(lam)))
            for r in wide:
                gate_a = jax.nn.sigmoid(pre_ref[r, 0:D_LRU] + ba_ref[...])
                gate_x = jax.nn.sigmoid(pre_ref[r, D_LRU:2 * D_LRU] + bx_ref[...])
                log_a = -LRU_C * gate_a * softplus_neg_lam
                a = jnp.exp(log_a)
                m2 = -jnp.tanh(log_a) * (a * a + 1.0)
                mult = jnp.where(m2 > 0.0, m2 * lax.rsqrt(m2), 0.0)
                a_ref[r, :] = a
                hx_ref[r, :] = mult * (gate_x * xc_ref[r, :])
    hl = hl_ref[...]
    for t in range(steps):
        r = slice(t * batch, (t + 1) * batch)
        hl = a_ref[r, :] * hl + hx_ref[r, :]
        hx_ref[r, :] = hl
    hl_ref[...] = hl

    within = 2 * S5_BLOCK_STATES
    for step in range(steps):
        c, t = divmod(step, S5_CHUNK)
        r = slice(step * batch, (step + 1) * batch)
        rc = slice(c * batch, (c + 1) * batch)
        cols = slice(t * S5_BLOCK_CH, (t + 1) * S5_BLOCK_CH)
        ys = jnp.concatenate(
            [yo_ref[k, rc, cols] + inc_ref[k, rc, within + cols.start:within + cols.stop]
             for k in range(S5_BLOCKS)], axis=1)
        y = _gelu_tanh(ys + du_ref[r, :])
        y32_ref[r, :] = y
        yb_ref[r, :] = y.astype(BF16)
    gate_ref[...] = _dot(yb_ref[...], wglu_ref[...])
    for r in wide:
        y = y32_ref[r, :] * jax.nn.sigmoid(gate_ref[r, :] + bglu_ref[...])
        y_ref[r, 0:D_S5] = _rms_norm(y, gs5_ref[...]).astype(y_ref.dtype)
        ylru = hx_ref[r, :] * gz_ref[r, :]
        y_ref[r, D_S5:D_MODEL] = _rms_norm(ylru, glru_ref[...]).astype(y_ref.dtype)
    mix = _dot(y_ref[...], wmix_ref[...])
    for c in range(D_MODEL // V7X_LANES):
        piece = mix[:, c * V7X_LANES:(c + 1) * V7X_LANES].reshape(steps, batch, V7X_LANES)
        mix_ref[c] = jnp.swapaxes(piece, 0, 1).reshape(rows, V7X_LANES).astype(mix_ref.dtype)

    _mixer_conv(z_ref, cw_ref, cb_ref, d_ref, xl_ref, xc_ref, xcb_ref, gz_ref, du_ref, batch)
    _mixer_norm(xnext_ref, lng_ref, lnb_ref, h0b_ref)


def _cast_rows(src_ref, dst_ref, chunk=64):
    for r in _row_chunks(src_ref.shape[0], chunk):
        dst_ref[r, :] = src_ref[r, :].astype(dst_ref.dtype)


def _mixer_body(xfirst_ref, xsecond_ref, xnext_ref, lng_ref, lnb_ref, win32_ref, *rest):
    back_consts = list(rest[:16])
    d_ref, cw_ref, cb_ref = back_consts[4], back_consts[8], back_consts[9]
    mix_ref = rest[16]
    scratch = rest[17:-3]
    win_ref, wglu_ref, wmix_ref = rest[-3:]
    wglu32_ref, wmix32_ref = back_consts[5], back_consts[15]
    back_consts[5], back_consts[15] = wglu_ref, wmix_ref
    z_ref, uc_ref, h0b_ref, _, _, st_ref, xl_ref, xc_ref, xcb_ref = scratch[:9]
    hl_ref, gz_ref, du_ref = scratch[12], scratch[17], scratch[18]
    batch = st_ref.shape[1]
    halo = (CONV_WIDTH - 1) * batch

    @pl.when(pl.program_id(0) == 0)
    def _():
        _cast_rows(win32_ref, win_ref)
        _cast_rows(wglu32_ref, wglu_ref)
        _cast_rows(wmix32_ref, wmix_ref)
        st_ref[...] = jnp.zeros_like(st_ref)
        hl_ref[...] = jnp.zeros_like(hl_ref)
        xl_ref[0:halo, :] = jnp.zeros((halo, D_LRU), F32)
        _mixer_norm(xfirst_ref, lng_ref, lnb_ref, h0b_ref)
        _mixer_project_in(h0b_ref, win_ref, z_ref, uc_ref, batch)
        _mixer_conv(z_ref, cw_ref, cb_ref, d_ref, xl_ref, xc_ref, xcb_ref, gz_ref, du_ref, batch)
        _mixer_norm(xsecond_ref, lng_ref, lnb_ref, h0b_ref)

    _mixer_step(xnext_ref, lng_ref, lnb_ref, win_ref, *back_consts, mix_ref, *scratch)


def _mixer(x_tm, batch, ln_g, ln_b, w_in, w_in_chunk, apow_re, apow_im, w_out_chunk, d, w_glu, b_glu, g_s5,
           conv_w, conv_b, wg, b_a, b_x, lam, g_lru, w_mix):
    n_rows = x_tm.shape[0]
    rows = MIXER_STEPS * batch
    n_blocks = n_rows // rows
    halo = (CONV_WIDTH - 1) * batch
    chunk_rows = rows // S5_CHUNK
    chunk_cols = S5_CHUNK * S5_BLOCK_CH
    slabs = D_MODEL // V7X_LANES
    consts = (ln_g, ln_b, w_in, w_in_chunk, apow_re, apow_im, w_out_chunk, d, w_glu, b_glu, g_s5,
              conv_w, conv_b, wg, b_a, b_x, lam, g_lru, w_mix)
    mix = pl.pallas_call(
        _mixer_body,
        grid=(n_blocks,),
        in_specs=[_resident_spec((rows, D_MODEL), (0, 0)),
                  _resident_spec((rows, D_MODEL), (min(1, n_blocks - 1), 0)),
                  pl.BlockSpec((rows, D_MODEL), lambda i: (jnp.minimum(i + 2, n_blocks - 1), 0))]
                 + [_resident_spec(c.shape) for c in consts],
        out_specs=pl.BlockSpec((slabs, None, rows, V7X_LANES), lambda i: (0, i, 0, 0)),
        out_shape=jax.ShapeDtypeStruct((slabs, n_blocks, rows, V7X_LANES), BF16),
        scratch_shapes=[
            pltpu.VMEM((rows, D_IN), F32),
            pltpu.VMEM((S5_BLOCKS, chunk_rows, chunk_cols), BF16),
            pltpu.VMEM((rows, D_MODEL), BF16),
            pltpu.VMEM((S5_BLOCKS, chunk_rows, 2 * S5_BLOCK_STATES + chunk_cols), F32),
            pltpu.VMEM((S5_BLOCKS, chunk_rows, 2 * S5_BLOCK_STATES), BF16),
            pltpu.VMEM((S5_BLOCKS, batch, 2 * S5_BLOCK_STATES), F32),
            pltpu.VMEM((rows + halo, D_LRU), F32),
            pltpu.VMEM((rows, D_LRU), F32),
            pltpu.VMEM((rows, D_LRU), BF16),
            pltpu.VMEM((rows, 2 * D_LRU), F32),
            pltpu.VMEM((rows, D_LRU), F32),
            pltpu.VMEM((rows, D_LRU), F32),
            pltpu.VMEM((batch, D_LRU), F32),
            pltpu.VMEM((S5_BLOCKS, chunk_rows, chunk_cols), F32),
            pltpu.VMEM((rows, D_S5), F32),
            pltpu.VMEM((rows, D_S5), BF16),
            pltpu.VMEM((rows, D_S5), F32),
            pltpu.VMEM((rows, D_LRU), F32),
            pltpu.VMEM((rows, D_S5), F32),
            pltpu.VMEM((rows, D_MODEL), BF16),
            pltpu.VMEM(w_in.shape, BF16),
            pltpu.VMEM(w_glu.shape, BF16),
            pltpu.VMEM(w_mix.shape, BF16),
        ],
        compiler_params=pltpu.CompilerParams(
            dimension_semantics=("arbitrary",), vmem_limit_bytes=V7X_SCOPED_VMEM_BYTES),
        name="mixer",
    )(x_tm, x_tm, x_tm, *consts)
    return mix.reshape(slabs, n_blocks, batch, MIXER_STEPS, V7X_LANES)


def _kv_body(mem_ref, g_ref, b_ref, wk32_ref, wv32_ref, k_ref, v_ref, wk_ref, wv_ref):
    @pl.when(pl.program_id(0) == 0)
    def _():
        _cast_rows(wk32_ref, wk_ref)
        _cast_rows(wv32_ref, wv_ref)

    mn = _layer_norm(mem_ref[...], g_ref[...], b_ref[...]).astype(BF16)
    k_ref[...] = (_dot(mn, wk_ref[...]) * (CA_HEAD_DIM ** -0.5)).astype(k_ref.dtype)
    v_ref[...] = _dot(mn, wv_ref[...]).astype(v_ref.dtype)


def _kv(mem, g, b, w_k, w_v):
    bsz, mlen, _ = mem.shape
    n_rows = bsz * mlen
    blk = pl.BlockSpec((KV_ROWS, D_MODEL), lambda i: (i, 0))
    out = jax.ShapeDtypeStruct((n_rows, D_MODEL), BF16)
    k, v = pl.pallas_call(
        _kv_body,
        grid=(n_rows // KV_ROWS,),
        in_specs=[blk, _const_spec(g.shape), _const_spec(b.shape), _resident_spec(w_k.shape),
                  _resident_spec(w_v.shape)],
        out_specs=(blk, blk),
        out_shape=(out, out),
        scratch_shapes=[pltpu.VMEM(w_k.shape, BF16), pltpu.VMEM(w_v.shape, BF16)],
        compiler_params=pltpu.CompilerParams(
            dimension_semantics=("arbitrary",), vmem_limit_bytes=V7X_SCOPED_VMEM_BYTES),
        name="kv",
    )(mem.reshape(n_rows, D_MODEL), g, b, w_k, w_v)
    return k.reshape(mem.shape), v.reshape(mem.shape)


def _attn_norms(x_ref, mix_ref, lng_ref, lnb_ref, g1_ref, b1_ref, h1_ref, h1b_ref, slot):
    slabs, _, steps, _ = mix_ref.shape
    for r in _row_chunks(x_ref.shape[0]):
        blk, t = divmod(r.start, steps)
        mix = jnp.concatenate([mix_ref[c, blk, t:t + VEC_ROWS, :] for c in range(slabs)], axis=-1)
        h0 = _layer_norm(x_ref[r, :], lng_ref[...], lnb_ref[...])
        h1 = _layer_norm(ALPHA * h0 + mix, g1_ref[...], b1_ref[...])
        h1_ref[slot, r, :] = h1
        h1b_ref[r, :] = h1.astype(BF16)


def _attn_body(x0_ref, m0_ref, x1_ref, m1_ref, xnext_ref, mnext_ref, k_ref, v_ref, wq32_ref, wo32_ref,
               lng_ref, lnb_ref, g1_ref, b1_ref, o_ref,
               h1_ref, q_ref, h1b_ref, s_ref, p_ref, att_ref, ca_ref, wq_ref, wo_ref):
    i = pl.program_id(0)
    slot = i % 2
    rows = o_ref.shape[0]
    norm_refs = (lng_ref, lnb_ref, g1_ref, b1_ref, h1_ref, h1b_ref)

    @pl.when(i == 0)
    def _():
        _cast_rows(wq32_ref, wq_ref)
        _cast_rows(wo32_ref, wo_ref)
        _attn_norms(x0_ref, m0_ref, *norm_refs, 0)
        q_ref[...] = _dot(h1b_ref[...], wq_ref[...]).astype(BF16)
        _attn_norms(x1_ref, m1_ref, *norm_refs, 1)

    head_slices = [slice(hd * CA_HEAD_DIM, (hd + 1) * CA_HEAD_DIM) for hd in range(CA_HEADS)]
    for hd, hs in enumerate(head_slices):
        s_ref[hd] = lax.dot_general(q_ref[:, hs], k_ref[:, hs], (((1,), (1,)), ((), ())),
                                    preferred_element_type=F32)
    q_ref[...] = _dot(h1b_ref[...], wq_ref[...]).astype(BF16)
    for r in _row_chunks(rows):
        for hd in range(CA_HEADS):
            s = s_ref[hd, r, :]
            e = jnp.exp(s - jnp.max(s, axis=-1, keepdims=True))
            p_ref[hd, r, :] = (e * (1.0 / jnp.sum(e, axis=-1, keepdims=True))).astype(BF16)
    for hd, hs in enumerate(head_slices):
        att_ref[:, hs] = _dot(p_ref[hd], v_ref[:, hs]).astype(BF16)
    ca_ref[...] = _dot(att_ref[...], wo_ref[...])
    for r in _row_chunks(rows):
        o_ref[r, :] = ALPHA * h1_ref[slot, r, :] + ca_ref[r, :]
    _attn_norms(xnext_ref, mnext_ref, *norm_refs, slot)


def _attn(x, mix, k, v, w_q, w_o, ln_g, ln_b, g1, b1):
    bsz, seq, _ = x.shape
    mlen = k.shape[1]
    n_rows = bsz * seq
    n_blocks = n_rows // ATTN_ROWS
    blocks_per_batch = seq // ATTN_ROWS
    slabs, _, _, steps, lanes = mix.shape
    assert ATTN_ROWS % steps == 0 and steps % VEC_ROWS == 0
    blk = (ATTN_ROWS, D_MODEL)
    first_blk = _resident_spec(blk, (0, 0))
    second_blk = _resident_spec(blk, (min(1, n_blocks - 1), 0))
    next_blk = pl.BlockSpec(blk, lambda i: (jnp.minimum(i + 2, n_blocks - 1), 0))
    mix_blk = (slabs, ATTN_ROWS // steps, None, steps, lanes)

    def mix_index(block):
        return (0, block % blocks_per_batch, block // blocks_per_batch, 0, 0)

    first_mix = pl.BlockSpec(mix_blk, lambda i: mix_index(0), pipeline_mode=pl.Buffered(1))
    second_mix = pl.BlockSpec(mix_blk, lambda i: mix_index(min(1, n_blocks - 1)), pipeline_mode=pl.Buffered(1))
    next_mix = pl.BlockSpec(mix_blk, lambda i: mix_index(jnp.minimum(i + 2, n_blocks - 1)))
    mem_blk = pl.BlockSpec((None, mlen, D_MODEL), lambda i: (i // blocks_per_batch, 0, 0))
    consts = (w_q, w_o, ln_g, ln_b, g1, b1)
    x2 = x.reshape(n_rows, D_MODEL)
    return pl.pallas_call(
        _attn_body,
        grid=(n_blocks,),
        in_specs=[first_blk, first_mix, second_blk, second_mix, next_blk, next_mix, mem_blk, mem_blk]
                 + [_resident_spec(c.shape) for c in consts],
        out_specs=pl.BlockSpec(blk, lambda i: (i, 0)),
        out_shape=jax.ShapeDtypeStruct((n_rows, D_MODEL), F32),
        scratch_shapes=[pltpu.VMEM((2, ATTN_ROWS, D_MODEL), F32),
                        pltpu.VMEM((ATTN_ROWS, D_MODEL), BF16),
                        pltpu.VMEM((ATTN_ROWS, D_MODEL), BF16),
                        pltpu.VMEM((CA_HEADS, ATTN_ROWS, mlen), F32),
                        pltpu.VMEM((CA_HEADS, ATTN_ROWS, mlen), BF16),
                        pltpu.VMEM((ATTN_ROWS, D_MODEL), BF16),
                        pltpu.VMEM((ATTN_ROWS, D_MODEL), F32),
                        pltpu.VMEM(w_q.shape, BF16),
                        pltpu.VMEM(w_o.shape, BF16)],
        compiler_params=pltpu.CompilerParams(
            dimension_semantics=("arbitrary",), vmem_limit_bytes=V7X_SCOPED_VMEM_BYTES),
        name="attn",
    )(x2, mix, x2, mix, x2, mix, k, v, *consts)


def _mlp_load_weights(w1_hbm, w2_hbm, w1_ref, w2_ref, stage_ref, sem):
    chunks = []
    for j in range(D_FF // MLP_CHUNK):
        cols = pl.ds(j * MLP_CHUNK, MLP_CHUNK)
        chunks.append((w1_hbm.at[:, cols], w1_ref.at[:, cols]))
        chunks.append((w2_hbm.at[cols, :], w2_ref.at[cols, :]))

    def copy(i):
        return pltpu.make_async_copy(chunks[i][0], stage_ref.at[i % 2], sem.at[i % 2])

    copy(0).start()
    for i, (_, dst) in enumerate(chunks):
        if i + 1 < len(chunks):
            copy(i + 1).start()
        copy(i).wait()
        _cast_rows(stage_ref.at[i % 2], dst)


def _mlp_body(s_ref, w1_hbm, w2_hbm, g2_ref, b2_ref, g_ref, b_ref, o_ref, w1_ref, w2_ref, stage_ref, sem):
    @pl.when(pl.program_id(0) == 0)
    def _():
        _mlp_load_weights(w1_hbm, w2_hbm, w1_ref, w2_ref, stage_ref, sem)

    for r in _row_chunks(s_ref.shape[0], MLP_GROUP_ROWS):
        h = _layer_norm(s_ref[r, :], g2_ref[...], b2_ref[...])
        hb = h.astype(BF16)
        ff = jnp.zeros(h.shape, F32)
        for j in range(D_FF // MLP_CHUNK):
            lo, hi = j * MLP_CHUNK, (j + 1) * MLP_CHUNK
            t = jnp.maximum(_dot(hb, w1_ref[:, lo:hi]), 0.0)
            ff = ff + _dot((t * t).astype(BF16), w2_ref[lo:hi, :])
        o_ref[r, :] = _layer_norm(ALPHA * h + ff, g_ref[...], b_ref[...])


def _mlp(s, w1, w2, g2, b2, g, b):
    n_rows = s.shape[0]
    assert w1.shape == (D_MODEL, D_FF) and w2.shape == (D_FF, D_MODEL) and MLP_CHUNK == D_MODEL
    row_blk = pl.BlockSpec((MLP_ROWS, D_MODEL), lambda i: (i, 0))
    in_hbm = pl.BlockSpec(memory_space=pl.ANY)
    vecs = (g2, b2, g, b)
    return pl.pallas_call(
        _mlp_body,
        grid=(n_rows // MLP_ROWS,),
        in_specs=[row_blk, in_hbm, in_hbm] + [_const_spec(v.shape) for v in vecs],
        out_specs=row_blk,
        out_shape=jax.ShapeDtypeStruct(s.shape, F32),
        scratch_shapes=[pltpu.VMEM(w1.shape, BF16), pltpu.VMEM(w2.shape, BF16),
                        pltpu.VMEM((2, MLP_CHUNK, D_MODEL), F32),
                        pltpu.SemaphoreType.DMA((2,))],
        compiler_params=pltpu.CompilerParams(
            dimension_semantics=("arbitrary",), vmem_limit_bytes=V7X_SCOPED_VMEM_BYTES),
        name="mlp",
    )(s, w1, w2, *vecs)


def kernel(x, mem, ln_in_g, ln_in_b, w_in, s5_a_re, s5_a_im, s5_log_dt, s5_b_re, s5_b_im, s5_c_re, s5_c_im, s5_d, s5_w_glu, s5_b_glu, conv_w, conv_b, lru_w_a, lru_b_a, lru_w_x, lru_b_x, lru_lambda, g_s5, g_lru, w_mix_out, ln1_g, ln1_b, mem_ln_g, mem_ln_b, w_q, w_k, w_v, w_o, ln2_g, ln2_b, w_ff1, w_ff2, ln3_g, ln3_b):
    bsz, seq, d_model = x.shape
    assert d_model == D_MODEL and w_in.shape == (DEPTH, D_MODEL, D_IN)
    assert seq % MIXER_STEPS == 0 and seq % ATTN_ROWS == 0 and (bsz * seq) % MLP_ROWS == 0
    assert bsz % 16 == 0, "time-major rows of one step must fill whole bf16 tiles"
    row = lambda v: v.reshape(1, -1).astype(F32)
    l = 0

    apow_re, apow_im, w_in_chunk, w_out_chunk, wg = _s5_prep(
        s5_a_re[l], s5_a_im[l], s5_log_dt[l], s5_b_re[l], s5_b_im[l], s5_c_re[l], s5_c_im[l],
        lru_w_a[l], lru_w_x[l])
    apow_re = apow_re.reshape(S5_BLOCKS, 1, S5_BLOCK_STATES)
    apow_im = apow_im.reshape(S5_BLOCKS, 1, S5_BLOCK_STATES)

    x_tm = jnp.swapaxes(x, 0, 1).reshape(seq * bsz, D_MODEL)
    k, v = _kv(mem, row(mem_ln_g[l]), row(mem_ln_b[l]), w_k[l], w_v[l])
    x_tm, k, v = lax.optimization_barrier((x_tm, k, v))
    mix = _mixer(x_tm, bsz, row(ln_in_g), row(ln_in_b), w_in[l], w_in_chunk, apow_re, apow_im,
                 w_out_chunk,
                 row(s5_d[l]), s5_w_glu[l], row(s5_b_glu[l]), row(g_s5[l]),
                 conv_w[l].astype(F32), row(conv_b[l]), wg, row(lru_b_a[l]), row(lru_b_x[l]),
                 row(lru_lambda[l]), row(g_lru[l]), w_mix_out[l])
    pre2 = _attn(x, mix, k, v, w_q[l], w_o[l], row(ln_in_g), row(ln_in_b), row(ln1_g[l]), row(ln1_b[l]))
    out = _mlp(pre2, w_ff1[l], w_ff2[l], row(ln2_g[l]), row(ln2_b[l]), row(ln3_g[l]), row(ln3_b[l]))
    return out.reshape(bsz, seq, D_MODEL)
```

```python
import jax
import jax.numpy as jnp
from jax import lax
from jax.experimental import pallas as pl
from jax.experimental.pallas import tpu as pltpu

F32 = jnp.float32
BF16 = jnp.bfloat16

D_MODEL = 1024
D_S5 = 512
D_LRU = 512
D_IN = D_S5 + 2 * D_LRU
S5_GROUP = 16
S5_GROUPS = 32
S5_STATE = 64
LRU_HEADS = 8
LRU_HEAD_DIM = 64
CONV_WIDTH = 4
LRU_C = 8.0
D_FF = 4096
CA_HEADS = 4
CA_HEAD_DIM = 256
DEPTH = 1
ALPHA = (2 * DEPTH) ** 0.25
LN_EPS = 1e-5
RMS_EPS = 1e-6

S5_BLOCKS = 4
S5_BLOCK_CH = D_S5 // S5_BLOCKS
S5_BLOCK_STATES = 8 * S5_STATE
S5_CHUNK = 2
LRU_BLOCKS = 2
LRU_BLOCK_CH = D_LRU // LRU_BLOCKS

V7X_SCOPED_VMEM_BYTES = 56 * 1024 * 1024
V7X_LANES = 128

MIXER_STEPS = 32
ATTN_ROWS = 512
MLP_ROWS = 1024
MLP_GROUP_ROWS = 512
KV_ROWS = 1024
MLP_CHUNK = 1024
VEC_ROWS = 16


def _layer_norm(x, g, b):
    mu = jnp.mean(x, axis=-1, keepdims=True)
    xc = x - mu
    var = jnp.mean(xc * xc, axis=-1, keepdims=True)
    return xc * lax.rsqrt(var + LN_EPS) * g + b


def _rms_norm(x, g):
    ms = jnp.mean(x * x, axis=-1, keepdims=True)
    return x * lax.rsqrt(ms + RMS_EPS) * g


def _gelu_tanh(x):
    c = 0.7978845608028654
    return 0.5 * x * (1.0 + jnp.tanh(c * (x + 0.044715 * (x * x * x))))


def _dot(a, b):
    return jnp.dot(a, b, preferred_element_type=F32)


def _const_spec(shape):
    zeros = (0,) * len(shape)
    return pl.BlockSpec(shape, lambda *_: zeros)


def _resident_spec(shape, block=None):
    zeros = (0,) * len(shape)
    index = zeros if block is None else block
    return pl.BlockSpec(shape, lambda *_: index, pipeline_mode=pl.Buffered(1))


def _row_chunks(rows, chunk=None):
    chunk = chunk or VEC_ROWS
    return [slice(r, r + chunk) for r in range(0, rows, chunk)]


def _cmul(a_re, a_im, b_re, b_im):
    return a_re * b_re - a_im * b_im, a_re * b_im + a_im * b_re


def _discretise(a_re, a_im, log_dt):
    dt = jnp.exp(log_dt)
    mag = jnp.exp(dt * a_re)
    abar_re = mag * jnp.cos(dt * a_im)
    abar_im = mag * jnp.sin(dt * a_im)
    den = a_re * a_re + a_im * a_im
    q_re = ((abar_re - 1.0) * a_re + abar_im * a_im) / den
    q_im = (abar_im * a_re - (abar_re - 1.0) * a_im) / den
    return abar_re, abar_im, q_re, q_im


def _powers(abar_re, abar_im):
    pows = [(jnp.ones_like(abar_re), jnp.zeros_like(abar_im))]
    for _ in range(S5_CHUNK):
        pows.append(_cmul(pows[-1][0], pows[-1][1], abar_re, abar_im))
    return pows


def _block_diag_tiles(stacked, group_rows, group_cols, n_groups):
    rows = n_groups * group_rows
    cols = n_groups * group_cols
    row_shift = group_rows.bit_length() - 1
    col_shift = group_cols.bit_length() - 1
    assert group_rows == 1 << row_shift and group_cols == 1 << col_shift
    src = lax.broadcasted_iota(jnp.int32, (group_cols, cols), 0)
    dst = lax.broadcasted_iota(jnp.int32, (group_cols, cols), 1)
    replicate = jnp.where((dst & (group_cols - 1)) == src, 1.0, 0.0).astype(BF16)
    tiled = _dot(stacked.astype(BF16), replicate)
    row_group = lax.broadcasted_iota(jnp.int32, (rows, cols), 0) >> row_shift
    col_group = lax.broadcasted_iota(jnp.int32, (rows, cols), 1) >> col_shift
    mask = row_group == col_group
    return [jnp.where(mask, tiled[i * rows:(i + 1) * rows], 0.0).astype(BF16)
            for i in range(stacked.shape[0] // rows)]


def _s5_prep_body(are_ref, aim_ref, ldt_ref, arec_ref, aimc_ref, ldtc_ref, bre_ref, bim_ref, cre_ref, cim_ref,
                  cret_ref, cimt_ref, wa_ref, wx_ref,
                  apow_re_ref, apow_im_ref, wic_ref, woc_ref, wg_ref):
    g8 = S5_GROUPS // S5_BLOCKS
    abar_re, abar_im, q_re, q_im = _discretise(are_ref[...], aim_ref[...], ldt_ref[...])
    pows = _powers(abar_re, abar_im)
    apow_re_ref[...] = pows[S5_CHUNK][0]
    apow_im_ref[...] = pows[S5_CHUNK][1]
    bbar_re, bbar_im = _cmul(q_re, q_im, bre_ref[...], bim_ref[...])
    c_re = cre_ref[...]
    c_im = cim_ref[...]
    abar_re_c, abar_im_c, _, _ = _discretise(arec_ref[...], aimc_ref[...], ldtc_ref[...])
    pows_c = _powers(abar_re_c, abar_im_c)
    ct_re = cret_ref[...]
    ct_im = cimt_ref[...]

    lane = lax.broadcasted_iota(jnp.int32, (S5_GROUPS, S5_GROUP, S5_GROUP), 2)
    within = 2 * S5_BLOCK_STATES
    wic_ref[:, :, within:] = jnp.zeros((S5_BLOCKS, S5_CHUNK * S5_BLOCK_CH, S5_CHUNK * S5_BLOCK_CH), BF16)
    for j in range(S5_CHUNK):
        bt = _cmul(*pows[S5_CHUNK - 1 - j], bbar_re, bbar_im)
        et = _cmul(*pows_c[j + 1], ct_re, ct_im)
        ce_re, ce_im = _cmul(*pows[j], c_re, c_im)
        lag = jnp.zeros(lane.shape, F32)
        for ho in range(S5_GROUP):
            col = jnp.sum(bbar_re * ce_re[:, ho:ho + 1, :] - bbar_im * ce_im[:, ho:ho + 1, :],
                          axis=-1, keepdims=True)
            lag = jnp.where(lane == ho, col, lag)
        rows_j = slice(j * S5_BLOCK_CH, (j + 1) * S5_BLOCK_CH)
        for part in range(2):
            cols = slice(part * S5_BLOCK_STATES, (part + 1) * S5_BLOCK_STATES)
            in_tiles = _block_diag_tiles(bt[part].reshape(S5_GROUPS * S5_GROUP, S5_STATE), S5_GROUP, S5_STATE, g8)
            out_tiles = _block_diag_tiles(et[part].reshape(S5_GROUPS * S5_STATE, S5_GROUP), S5_STATE, S5_GROUP, g8)
            for k in range(S5_BLOCKS):
                wic_ref[k, rows_j, cols] = in_tiles[k]
                woc_ref[k, cols, rows_j] = out_tiles[k] if part == 0 else -out_tiles[k]
        lag_tiles = _block_diag_tiles(lag.reshape(S5_GROUPS * S5_GROUP, S5_GROUP), S5_GROUP, S5_GROUP, g8)
        for k in range(S5_BLOCKS):
            for t_in in range(S5_CHUNK - j):
                t_out = t_in + j
                wic_ref[k, t_in * S5_BLOCK_CH:(t_in + 1) * S5_BLOCK_CH,
                        within + t_out * S5_BLOCK_CH:within + (t_out + 1) * S5_BLOCK_CH] = lag_tiles[k]

    heads = LRU_HEADS // LRU_BLOCKS
    for part, w_ref in enumerate((wa_ref, wx_ref)):
        gate_tiles = _block_diag_tiles(w_ref[...].reshape(D_LRU, LRU_HEAD_DIM), LRU_HEAD_DIM, LRU_HEAD_DIM, heads)
        for kb in range(LRU_BLOCKS):
            wg_ref[kb, :, part * LRU_BLOCK_CH:(part + 1) * LRU_BLOCK_CH] = gate_tiles[kb]


def _s5_prep(a_re, a_im, log_dt, b_re, b_im, c_re, c_im, w_a, w_x):
    g, p, h = b_re.shape
    f = jax.ShapeDtypeStruct
    chunk_cols = S5_CHUNK * S5_BLOCK_CH
    return pl.pallas_call(
        _s5_prep_body,
        out_shape=(f((g, 1, p), F32), f((g, 1, p), F32),
                   f((S5_BLOCKS, chunk_cols, 2 * S5_BLOCK_STATES + chunk_cols), BF16),
                   f((S5_BLOCKS, 2 * S5_BLOCK_STATES, chunk_cols), BF16),
                   f((LRU_BLOCKS, LRU_BLOCK_CH, 2 * LRU_BLOCK_CH), BF16)),
        compiler_params=pltpu.CompilerParams(vmem_limit_bytes=V7X_SCOPED_VMEM_BYTES),
        name="s5_prep",
    )(a_re.reshape(g, 1, p), a_im.reshape(g, 1, p), log_dt.reshape(g, 1, 1),
      a_re.reshape(g, p, 1), a_im.reshape(g, p, 1), log_dt.reshape(g, 1, 1),
      jnp.swapaxes(b_re, 1, 2), jnp.swapaxes(b_im, 1, 2), c_re, c_im,
      jnp.swapaxes(c_re, 1, 2), jnp.swapaxes(c_im, 1, 2), w_a, w_x)


def _mixer_norm(x_ref, lng_ref, lnb_ref, h0b_ref):
    for r in _row_chunks(x_ref.shape[0]):
        h0b_ref[r, :] = _layer_norm(x_ref[r, :], lng_ref[...], lnb_ref[...]).astype(BF16)


def _mixer_project_in(h0b_ref, win_ref, z_ref, uc_ref, batch):
    for piece in range(D_IN // D_S5):
        cols = slice(piece * D_S5, (piece + 1) * D_S5)
        z = _dot(h0b_ref[...], win_ref[:, cols])
        z_ref[:, cols] = z
        if piece == 0:
            zb = z.astype(BF16)
            for step in range(z.shape[0] // batch):
                c, t = divmod(step, S5_CHUNK)
                for k in range(S5_BLOCKS):
                    uc_ref[k, c * batch:(c + 1) * batch, t * S5_BLOCK_CH:(t + 1) * S5_BLOCK_CH] = (
                        zb[step * batch:(step + 1) * batch, k * S5_BLOCK_CH:(k + 1) * S5_BLOCK_CH])


def _mixer_conv(z_ref, cw_ref, cb_ref, d_ref, xl_ref, xc_ref, xcb_ref, gz_ref, du_ref, batch):
    rows = z_ref.shape[0]
    halo = (CONV_WIDTH - 1) * batch
    xl_ref[halo:halo + rows, :] = z_ref[:, D_S5:D_S5 + D_LRU]
    for r in _row_chunks(rows, 2 * VEC_ROWS):
        xc = cb_ref[...]
        for j in range(CONV_WIDTH):
            xc = xc + xl_ref[j * batch + r.start:j * batch + r.stop, :] * cw_ref[j:j + 1, :]
        xc_ref[r, :] = xc
        xcb_ref[r, :] = xc.astype(BF16)
        gz_ref[r, :] = _gelu_tanh(z_ref[r, D_S5 + D_LRU:D_IN])
        du_ref[r, :] = d_ref[...] * z_ref[r, 0:D_S5]
    xl_ref[0:halo, :] = xl_ref[rows:rows + halo, :]


def _mixer_step(xnext_ref, lng_ref, lnb_ref, win_ref, wic_ref, apr_ref, api_ref, woc_ref, d_ref,
                wglu_ref, bglu_ref, gs5_ref, cw_ref, cb_ref, wg_ref, ba_ref, bx_ref, lam_ref, glru_ref, wmix_ref,
                mix_ref,
                z_ref, uc_ref, h0b_ref, inc_ref, hp_ref, st_ref, xl_ref, xc_ref, xcb_ref, pre_ref, a_ref, hx_ref,
                hl_ref, yo_ref, y32_ref, yb_ref, gate_ref, gz_ref, du_ref, y_ref):
    rows = y_ref.shape[0]
    batch = st_ref.shape[1]
    steps = rows // batch
    chunks = steps // S5_CHUNK
    wide = _row_chunks(rows, 2 * VEC_ROWS)
    sre = slice(0, S5_BLOCK_STATES)
    sim = slice(S5_BLOCK_STATES, 2 * S5_BLOCK_STATES)

    for k in range(S5_BLOCKS):
        inc_ref[k] = _dot(uc_ref[k], wic_ref[k])
    for kb in range(LRU_BLOCKS):
        lo, hi_ = kb * LRU_BLOCK_CH, (kb + 1) * LRU_BLOCK_CH
        pre = _dot(xcb_ref[:, lo:hi_], wg_ref[kb])
        pre_ref[:, lo:hi_] = pre[:, 0:LRU_BLOCK_CH]
        pre_ref[:, D_LRU + lo:D_LRU + hi_] = pre[:, LRU_BLOCK_CH:2 * LRU_BLOCK_CH]
    _mixer_project_in(h0b_ref, win_ref, z_ref, uc_ref, batch)

    for k in range(S5_BLOCKS):
        ar = jnp.broadcast_to(apr_ref[k], (batch, S5_BLOCK_STATES))
        ai = jnp.broadcast_to(api_ref[k], (batch, S5_BLOCK_STATES))
        hr = st_ref[k, :, sre]
        hi = st_ref[k, :, sim]
        for c in range(chunks):
            r = slice(c * batch, (c + 1) * batch)
            hp_ref[k, r, sre] = hr.astype(BF16)
            hp_ref[k, r, sim] = hi.astype(BF16)
            hr, hi = (ar * hr - ai * hi + inc_ref[k, r, sre],
                      ar * hi + ai * hr + inc_ref[k, r, sim])
        st_ref[k, :, sre] = hr
        st_ref[k, :, sim] = hi
        yo_ref[k] = _dot(hp_ref[k], woc_ref[k])
        if k == 1:
            lam = lam_ref[...]
            softplus_neg_lam = jnp.maximum(-lam, 0.0) + jnp.log1p(jnp.exp(-jnp.abs(lam)))
            for r in wide:
                gate_a = jax.nn.sigmoid(pre_ref[r, 0:D_LRU] + ba_ref[...])
                gate_x = jax.nn.sigmoid(pre_ref[r, D_LRU:2 * D_LRU] + bx_ref[...])
                log_a = -LRU_C * gate_a * softplus_neg_lam
                a = jnp.exp(log_a)
                m2 = -jnp.tanh(log_a) * (a * a + 1.0)
                mult = jnp.where(m2 > 0.0, m2 * lax.rsqrt(m2), 0.0)
                a_ref[r, :] = a
                hx_ref[r, :] = mult * (gate_x * xc_ref[r, :])
    hl = hl_ref[...]
    for t in range(steps):
        r = slice(t * batch, (t + 1) * batch)
        hl = a_ref[r, :] * hl + hx_ref[r, :]
        hx_ref[r, :] = hl
    hl_ref[...] = hl

    within = 2 * S5_BLOCK_STATES
    for step in range(steps):
        c, t = divmod(step, S5_CHUNK)
        r = slice(step * batch, (step + 1) * batch)
        rc = slice(c * batch, (c + 1) * batch)
        cols = slice(t * S5_BLOCK_CH, (t + 1) * S5_BLOCK_CH)
        ys = jnp.concatenate(
            [yo_ref[k, rc, cols] + inc_ref[k, rc, within + cols.start:within + cols.stop]
             for k in range(S5_BLOCKS)], axis=1)
        y = _gelu_tanh(ys + du_ref[r, :])
        y32_ref[r, :] = y
        yb_ref[r, :] = y.astype(BF16)
    gate_ref[...] = _dot(yb_ref[...], wglu_ref[...])
    for r in wide:
        y = y32_ref[r, :] * jax.nn.sigmoid(gate_ref[r, :] + bglu_ref[...])
        y_ref[r, 0:D_S5] = _rms_norm(y, gs5_ref[...]).astype(y_ref.dtype)
        ylru = hx_ref[r, :] * gz_ref[r, :]
        y_ref[r, D_S5:D_MODEL] = _rms_norm(ylru, glru_ref[...]).astype(y_ref.dtype)
    mix = _dot(y_ref[...], wmix_ref[...])
    for c in range(D_MODEL // V7X_LANES):
        piece = mix[:, c * V7X_LANES:(c + 1) * V7X_LANES].reshape(steps, batch, V7X_LANES)
        mix_ref[c] = jnp.swapaxes(piece, 0, 1).reshape(rows, V7X_LANES).astype(mix_ref.dtype)

    _mixer_conv(z_ref, cw_ref, cb_ref, d_ref, xl_ref, xc_ref, xcb_ref, gz_ref, du_ref, batch)
    _mixer_norm(xnext_ref, lng_ref, lnb_ref, h0b_ref)


def _cast_rows(src_ref, dst_ref, chunk=64):
    for r in _row_chunks(src_ref.shape[0], chunk):
        dst_ref[r, :] = src_ref[r, :].astype(dst_ref.dtype)


def _mixer_body(xfirst_ref, xsecond_ref, xnext_ref, lng_ref, lnb_ref, win32_ref, *rest):
    back_consts = list(rest[:16])
    d_ref, cw_ref, cb_ref = back_consts[4], back_consts[8], back_consts[9]
    mix_ref = rest[16]
    scratch = rest[17:-3]
    win_ref, wglu_ref, wmix_ref = rest[-3:]
    wglu32_ref, wmix32_ref = back_consts[5], back_consts[15]
    back_consts[5], back_consts[15] = wglu_ref, wmix_ref
    z_ref, uc_ref, h0b_ref, _, _, st_ref, xl_ref, xc_ref, xcb_ref = scratch[:9]
    hl_ref, gz_ref, du_ref = scratch[12], scratch[17], scratch[18]
    batch = st_ref.shape[1]
    halo = (CONV_WIDTH - 1) * batch

    @pl.when(pl.program_id(0) == 0)
    def _():
        _cast_rows(win32_ref, win_ref)
        _cast_rows(wglu32_ref, wglu_ref)
        _cast_rows(wmix32_ref, wmix_ref)
        st_ref[...] = jnp.zeros_like(st_ref)
        hl_ref[...] = jnp.zeros_like(hl_ref)
        xl_ref[0:halo, :] = jnp.zeros((halo, D_LRU), F32)
        _mixer_norm(xfirst_ref, lng_ref, lnb_ref, h0b_ref)
        _mixer_project_in(h0b_ref, win_ref, z_ref, uc_ref, batch)
        _mixer_conv(z_ref, cw_ref, cb_ref, d_ref, xl_ref, xc_ref, xcb_ref, gz_ref, du_ref, batch)
        _mixer_norm(xsecond_ref, lng_ref, lnb_ref, h0b_ref)

    _mixer_step(xnext_ref, lng_ref, lnb_ref, win_ref, *back_consts, mix_ref, *scratch)


def _mixer(x_tm, batch, ln_g, ln_b, w_in, w_in_chunk, apow_re, apow_im, w_out_chunk, d, w_glu, b_glu, g_s5,
           conv_w, conv_b, wg, b_a, b_x, lam, g_lru, w_mix):
    n_rows = x_tm.shape[0]
    rows = MIXER_STEPS * batch
    n_blocks = n_rows // rows
    halo = (CONV_WIDTH - 1) * batch
    chunk_rows = rows // S5_CHUNK
    chunk_cols = S5_CHUNK * S5_BLOCK_CH
    slabs = D_MODEL // V7X_LANES
    consts = (ln_g, ln_b, w_in, w_in_chunk, apow_re, apow_im, w_out_chunk, d, w_glu, b_glu, g_s5,
              conv_w, conv_b, wg, b_a, b_x, lam, g_lru, w_mix)
    mix = pl.pallas_call(
        _mixer_body,
        grid=(n_blocks,),
        in_specs=[_resident_spec((rows, D_MODEL), (0, 0)),
                  _resident_spec((rows, D_MODEL), (min(1, n_blocks - 1), 0)),
                  pl.BlockSpec((rows, D_MODEL), lambda i: (jnp.minimum(i + 2, n_blocks - 1), 0))]
                 + [_resident_spec(c.shape) for c in consts],
        out_specs=pl.BlockSpec((slabs, None, rows, V7X_LANES), lambda i: (0, i, 0, 0)),
        out_shape=jax.ShapeDtypeStruct((slabs, n_blocks, rows, V7X_LANES), BF16),
        scratch_shapes=[
            pltpu.VMEM((rows, D_IN), F32),
            pltpu.VMEM((S5_BLOCKS, chunk_rows, chunk_cols), BF16),
            pltpu.VMEM((rows, D_MODEL), BF16),
            pltpu.VMEM((S5_BLOCKS, chunk_rows, 2 * S5_BLOCK_STATES + chunk_cols), F32),
            pltpu.VMEM((S5_BLOCKS, chunk_rows, 2 * S5_BLOCK_STATES), BF16),
            pltpu.VMEM((S5_BLOCKS, batch, 2 * S5_BLOCK_STATES), F32),
            pltpu.VMEM((rows + halo, D_LRU), F32),
            pltpu.VMEM((rows, D_LRU), F32),
            pltpu.VMEM((rows, D_LRU), BF16),
            pltpu.VMEM((rows, 2 * D_LRU), F32),
            pltpu.VMEM((rows, D_LRU), F32),
            pltpu.VMEM((rows, D_LRU), F32),
            pltpu.VMEM((batch, D_LRU), F32),
            pltpu.VMEM((S5_BLOCKS, chunk_rows, chunk_cols), F32),
            pltpu.VMEM((rows, D_S5), F32),
            pltpu.VMEM((rows, D_S5), BF16),
            pltpu.VMEM((rows, D_S5), F32),
            pltpu.VMEM((rows, D_LRU), F32),
            pltpu.VMEM((rows, D_S5), F32),
            pltpu.VMEM((rows, D_MODEL), BF16),
            pltpu.VMEM(w_in.shape, BF16),
            pltpu.VMEM(w_glu.shape, BF16),
            pltpu.VMEM(w_mix.shape, BF16),
        ],
        compiler_params=pltpu.CompilerParams(
            dimension_semantics=("arbitrary",), vmem_limit_bytes=V7X_SCOPED_VMEM_BYTES),
        name="mixer",
    )(x_tm, x_tm, x_tm, *consts)
    return mix.reshape(slabs, n_blocks, batch, MIXER_STEPS, V7X_LANES)


def _kv_body(mem_ref, g_ref, b_ref, wk32_ref, wv32_ref, k_ref, v_ref, wk_ref, wv_ref):
    @pl.when(pl.program_id(0) == 0)
    def _():
        _cast_rows(wk32_ref, wk_ref)
        _cast_rows(wv32_ref, wv_ref)

    mn = _layer_norm(mem_ref[...], g_ref[...], b_ref[...]).astype(BF16)
    k_ref[...] = (_dot(mn, wk_ref[...]) * (CA_HEAD_DIM ** -0.5)).astype(k_ref.dtype)
    v_ref[...] = _dot(mn, wv_ref[...]).astype(v_ref.dtype)


def _kv(mem, g, b, w_k, w_v):
    bsz, mlen, _ = mem.shape
    n_rows = bsz * mlen
    blk = pl.BlockSpec((KV_ROWS, D_MODEL), lambda i: (i, 0))
    out = jax.ShapeDtypeStruct((n_rows, D_MODEL), BF16)
    k, v = pl.pallas_call(
        _kv_body,
        grid=(n_rows // KV_ROWS,),
        in_specs=[blk, _const_spec(g.shape), _const_spec(b.shape), _resident_spec(w_k.shape),
                  _resident_spec(w_v.shape)],
        out_specs=(blk, blk),
        out_shape=(out, out),
        scratch_shapes=[pltpu.VMEM(w_k.shape, BF16), pltpu.VMEM(w_v.shape, BF16)],
        compiler_params=pltpu.CompilerParams(
            dimension_semantics=("arbitrary",), vmem_limit_bytes=V7X_SCOPED_VMEM_BYTES),
        name="kv",
    )(mem.reshape(n_rows, D_MODEL), g, b, w_k, w_v)
    return k.reshape(mem.shape), v.reshape(mem.shape)


def _attn_norms(x_ref, mix_ref, lng_ref, lnb_ref, g1_ref, b1_ref, h1_ref, h1b_ref, slot):
    slabs, _, steps, _ = mix_ref.shape
    for r in _row_chunks(x_ref.shape[0]):
        blk, t = divmod(r.start, steps)
        mix = jnp.concatenate([mix_ref[c, blk, t:t + VEC_ROWS, :] for c in range(slabs)], axis=-1)
        h0 = _layer_norm(x_ref[r, :], lng_ref[...], lnb_ref[...])
        h1 = _layer_norm(ALPHA * h0 + mix, g1_ref[...], b1_ref[...])
        h1_ref[slot, r, :] = h1
        h1b_ref[r, :] = h1.astype(BF16)


def _attn_body(x0_ref, m0_ref, x1_ref, m1_ref, xnext_ref, mnext_ref, k_ref, v_ref, wq32_ref, wo32_ref,
               lng_ref, lnb_ref, g1_ref, b1_ref, o_ref,
               h1_ref, q_ref, h1b_ref, s_ref, p_ref, att_ref, ca_ref, wq_ref, wo_ref):
    i = pl.program_id(0)
    slot = i % 2
    rows = o_ref.shape[0]
    norm_refs = (lng_ref, lnb_ref, g1_ref, b1_ref, h1_ref, h1b_ref)

    @pl.when(i == 0)
    def _():
        _cast_rows(wq32_ref, wq_ref)
        _cast_rows(wo32_ref, wo_ref)
        _attn_norms(x0_ref, m0_ref, *norm_refs, 0)
        q_ref[...] = _dot(h1b_ref[...], wq_ref[...]).astype(BF16)
        _attn_norms(x1_ref, m1_ref, *norm_refs, 1)

    head_slices = [slice(hd * CA_HEAD_DIM, (hd + 1) * CA_HEAD_DIM) for hd in range(CA_HEADS)]
    for hd, hs in enumerate(head_slices):
        s_ref[hd] = lax.dot_general(q_ref[:, hs], k_ref[:, hs], (((1,), (1,)), ((), ())),
                                    preferred_element_type=F32)
    q_ref[...] = _dot(h1b_ref[...], wq_ref[...]).astype(BF16)
    for r in _row_chunks(rows):
        for hd in range(CA_HEADS):
            s = s_ref[hd, r, :]
            e = jnp.exp(s - jnp.max(s, axis=-1, keepdims=True))
            p_ref[hd, r, :] = (e * (1.0 / jnp.sum(e, axis=-1, keepdims=True))).astype(BF16)
    for hd, hs in enumerate(head_slices):
        att_ref[:, hs] = _dot(p_ref[hd], v_ref[:, hs]).astype(BF16)
    ca_ref[...] = _dot(att_ref[...], wo_ref[...])
    for r in _row_chunks(rows):
        o_ref[r, :] = ALPHA * h1_ref[slot, r, :] + ca_ref[r, :]
    _attn_norms(xnext_ref, mnext_ref, *norm_refs, slot)


def _attn(x, mix, k, v, w_q, w_o, ln_g, ln_b, g1, b1):
    bsz, seq, _ = x.shape
    mlen = k.shape[1]
    n_rows = bsz * seq
    n_blocks = n_rows // ATTN_ROWS
    blocks_per_batch = seq // ATTN_ROWS
    slabs, _, _, steps, lanes = mix.shape
    assert ATTN_ROWS % steps == 0 and steps % VEC_ROWS == 0
    blk = (ATTN_ROWS, D_MODEL)
    first_blk = _resident_spec(blk, (0, 0))
    second_blk = _resident_spec(blk, (min(1, n_blocks - 1), 0))
    next_blk = pl.BlockSpec(blk, lambda i: (jnp.minimum(i + 2, n_blocks - 1), 0))
    mix_blk = (slabs, ATTN_ROWS // steps, None, steps, lanes)

    def mix_index(block):
        return (0, block % blocks_per_batch, block // blocks_per_batch, 0, 0)

    first_mix = pl.BlockSpec(mix_blk, lambda i: mix_index(0), pipeline_mode=pl.Buffered(1))
    second_mix = pl.BlockSpec(mix_blk, lambda i: mix_index(min(1, n_blocks - 1)), pipeline_mode=pl.Buffered(1))
    next_mix = pl.BlockSpec(mix_blk, lambda i: mix_index(jnp.minimum(i + 2, n_blocks - 1)))
    mem_blk = pl.BlockSpec((None, mlen, D_MODEL), lambda i: (i // blocks_per_batch, 0, 0))
    consts = (w_q, w_o, ln_g, ln_b, g1, b1)
    x2 = x.reshape(n_rows, D_MODEL)
    return pl.pallas_call(
        _attn_body,
        grid=(n_blocks,),
        in_specs=[first_blk, first_mix, second_blk, second_mix, next_blk, next_mix, mem_blk, mem_blk]
                 + [_resident_spec(c.shape) for c in consts],
        out_specs=pl.BlockSpec(blk, lambda i: (i, 0)),
        out_shape=jax.ShapeDtypeStruct((n_rows, D_MODEL), F32),
        scratch_shapes=[pltpu.VMEM((2, ATTN_ROWS, D_MODEL), F32),
                        pltpu.VMEM((ATTN_ROWS, D_MODEL), BF16),
                        pltpu.VMEM((ATTN_ROWS, D_MODEL), BF16),
                        pltpu.VMEM((CA_HEADS, ATTN_ROWS, mlen), F32),
                        pltpu.VMEM((CA_HEADS, ATTN_ROWS, mlen), BF16),
                        pltpu.VMEM((ATTN_ROWS, D_MODEL), BF16),
                        pltpu.VMEM((ATTN_ROWS, D_MODEL), F32),
                        pltpu.VMEM(w_q.shape, BF16),
                        pltpu.VMEM(w_o.shape, BF16)],
        compiler_params=pltpu.CompilerParams(
            dimension_semantics=("arbitrary",), vmem_limit_bytes=V7X_SCOPED_VMEM_BYTES),
        name="attn",
    )(x2, mix, x2, mix, x2, mix, k, v, *consts)


def _mlp_load_weights(w1_hbm, w2_hbm, w1_ref, w2_ref, stage_ref, sem):
    chunks = []
    for j in range(D_FF // MLP_CHUNK):
        cols = pl.ds(j * MLP_CHUNK, MLP_CHUNK)
        chunks.append((w1_hbm.at[:, cols], w1_ref.at[:, cols]))
        chunks.append((w2_hbm.at[cols, :], w2_ref.at[cols, :]))

    def copy(i):
        return pltpu.make_async_copy(chunks[i][0], stage_ref.at[i % 2], sem.at[i % 2])

    copy(0).start()
    for i, (_, dst) in enumerate(chunks):
        if i + 1 < len(chunks):
            copy(i + 1).start()
        copy(i).wait()
        _cast_rows(stage_ref.at[i % 2], dst)


def _mlp_norm(s_ref, g2_ref, b2_ref, h_ref, hb_ref):
    for r in _row_chunks(s_ref.shape[0]):
        h = _layer_norm(s_ref[r, :], g2_ref[...], b2_ref[...])
        h_ref[r, :] = h
        hb_ref[r, :] = h.astype(BF16)


def _mlp_body(s_hbm, snext_ref, w1_hbm, w2_hbm, g2_ref, b2_ref, g_ref, b_ref, o_ref,
              h_ref, hb_ref, w1_ref, w2_ref, stage_ref, sem):
    rows = o_ref.shape[0]

    @pl.when(pl.program_id(0) == 0)
    def _():
        _mlp_load_weights(w1_hbm, w2_hbm, w1_ref, w2_ref, stage_ref, sem)
        first = pltpu.make_async_copy(s_hbm.at[pl.ds(0, rows), :], stage_ref.at[0], sem.at[0])
        first.start()
        first.wait()
        _mlp_norm(stage_ref.at[0], g2_ref, b2_ref, h_ref, hb_ref)

    for r in _row_chunks(rows, MLP_GROUP_ROWS):
        acc = ALPHA * h_ref[r, :]
        for j in range(D_FF // MLP_CHUNK):
            lo, hi = j * MLP_CHUNK, (j + 1) * MLP_CHUNK
            t = jnp.maximum(_dot(hb_ref[r, :], w1_ref[:, lo:hi]), 0.0)
            acc = acc + _dot((t * t).astype(BF16), w2_ref[lo:hi, :])
        o_ref[r, :] = _layer_norm(acc, g_ref[...], b_ref[...])
    _mlp_norm(snext_ref, g2_ref, b2_ref, h_ref, hb_ref)


def _mlp(s, w1, w2, g2, b2, g, b):
    n_rows = s.shape[0]
    n_blocks = n_rows // MLP_ROWS
    assert w1.shape == (D_MODEL, D_FF) and w2.shape == (D_FF, D_MODEL)
    assert MLP_CHUNK == D_MODEL and MLP_ROWS == MLP_CHUNK, "a staging slot holds a weight chunk or a row block"
    row_blk = (MLP_ROWS, D_MODEL)
    in_hbm = pl.BlockSpec(memory_space=pl.ANY)
    vecs = (g2, b2, g, b)
    return pl.pallas_call(
        _mlp_body,
        grid=(n_blocks,),
        in_specs=[in_hbm, pl.BlockSpec(row_blk, lambda i: (jnp.minimum(i + 1, n_blocks - 1), 0)), in_hbm, in_hbm]
                 + [_const_spec(v.shape) for v in vecs],
        out_specs=pl.BlockSpec(row_blk, lambda i: (i, 0)),
        out_shape=jax.ShapeDtypeStruct(s.shape, F32),
        scratch_shapes=[pltpu.VMEM(row_blk, F32),
                        pltpu.VMEM(row_blk, BF16),
                        pltpu.VMEM(w1.shape, BF16), pltpu.VMEM(w2.shape, BF16),
                        pltpu.VMEM((2, MLP_CHUNK, D_MODEL), F32),
                        pltpu.SemaphoreType.DMA((2,))],
        compiler_params=pltpu.CompilerParams(
            dimension_semantics=("arbitrary",), vmem_limit_bytes=V7X_SCOPED_VMEM_BYTES),
        name="mlp",
    )(s, s, w1, w2, *vecs)


def kernel(x, mem, ln_in_g, ln_in_b, w_in, s5_a_re, s5_a_im, s5_log_dt, s5_b_re, s5_b_im, s5_c_re, s5_c_im, s5_d, s5_w_glu, s5_b_glu, conv_w, conv_b, lru_w_a, lru_b_a, lru_w_x, lru_b_x, lru_lambda, g_s5, g_lru, w_mix_out, ln1_g, ln1_b, mem_ln_g, mem_ln_b, w_q, w_k, w_v, w_o, ln2_g, ln2_b, w_ff1, w_ff2, ln3_g, ln3_b):
    bsz, seq, d_model = x.shape
    assert d_model == D_MODEL and w_in.shape == (DEPTH, D_MODEL, D_IN)
    assert seq % MIXER_STEPS == 0 and seq % ATTN_ROWS == 0 and (bsz * seq) % MLP_ROWS == 0
    assert bsz % 16 == 0, "time-major rows of one step must fill whole bf16 tiles"
    row = lambda v: v.reshape(1, -1).astype(F32)
    l = 0

    apow_re, apow_im, w_in_chunk, w_out_chunk, wg = _s5_prep(
        s5_a_re[l], s5_a_im[l], s5_log_dt[l], s5_b_re[l], s5_b_im[l], s5_c_re[l], s5_c_im[l],
        lru_w_a[l], lru_w_x[l])
    apow_re = apow_re.reshape(S5_BLOCKS, 1, S5_BLOCK_STATES)
    apow_im = apow_im.reshape(S5_BLOCKS, 1, S5_BLOCK_STATES)

    x_tm = jnp.swapaxes(x, 0, 1).reshape(seq * bsz, D_MODEL)
    k, v = _kv(mem, row(mem_ln_g[l]), row(mem_ln_b[l]), w_k[l], w_v[l])
    x_tm, k, v = lax.optimization_barrier((x_tm, k, v))
    mix = _mixer(x_tm, bsz, row(ln_in_g), row(ln_in_b), w_in[l], w_in_chunk, apow_re, apow_im,
                 w_out_chunk,
                 row(s5_d[l]), s5_w_glu[l], row(s5_b_glu[l]), row(g_s5[l]),
                 conv_w[l].astype(F32), row(conv_b[l]), wg, row(lru_b_a[l]), row(lru_b_x[l]),
                 row(lru_lambda[l]), row(g_lru[l]), w_mix_out[l])
    pre2 = _attn(x, mix, k, v, w_q[l], w_o[l], row(ln_in_g), row(ln_in_b), row(ln1_g[l]), row(ln1_b[l]))
    out = _mlp(pre2, w_ff1[l], w_ff2[l], row(ln2_g[l]), row(ln2_b[l]), row(ln3_g[l]), row(ln3_b[l]))
    return out.reshape(bsz, seq, D_MODEL)
```

```python
import jax
import jax.numpy as jnp
from jax import lax
from jax.experimental import pallas as pl
from jax.experimental.pallas import tpu as pltpu

F32 = jnp.float32
BF16 = jnp.bfloat16

D_MODEL = 1024
D_S5 = 512
D_LRU = 512
D_IN = D_S5 + 2 * D_LRU
S5_GROUP = 16
S5_GROUPS = 32
S5_STATE = 64
LRU_HEADS = 8
LRU_HEAD_DIM = 64
CONV_WIDTH = 4
LRU_C = 8.0
D_FF = 4096
CA_HEADS = 4
CA_HEAD_DIM = 256
DEPTH = 1
ALPHA = (2 * DEPTH) ** 0.25
LN_EPS = 1e-5
RMS_EPS = 1e-6

S5_BLOCKS = 4
S5_BLOCK_CH = D_S5 // S5_BLOCKS
S5_BLOCK_STATES = 8 * S5_STATE
S5_CHUNK = 2
LRU_BLOCKS = 2
LRU_BLOCK_CH = D_LRU // LRU_BLOCKS

V7X_SCOPED_VMEM_BYTES = 56 * 1024 * 1024
V7X_LANES = 128

MIXER_STEPS = 32
ATTN_ROWS = 512
MLP_ROWS = 1024
MLP_GROUP_ROWS = 512
KV_ROWS = 1024
LN_ROWS = 1024
MLP_CHUNK = 1024
VEC_ROWS = 16


def _layer_norm(x, g, b):
    mu = jnp.mean(x, axis=-1, keepdims=True)
    xc = x - mu
    var = jnp.mean(xc * xc, axis=-1, keepdims=True)
    return xc * lax.rsqrt(var + LN_EPS) * g + b


def _rms_norm(x, g):
    ms = jnp.mean(x * x, axis=-1, keepdims=True)
    return x * lax.rsqrt(ms + RMS_EPS) * g


def _gelu_tanh(x):
    c = 0.7978845608028654
    return 0.5 * x * (1.0 + jnp.tanh(c * (x + 0.044715 * (x * x * x))))


def _dot(a, b):
    return jnp.dot(a, b, preferred_element_type=F32)


def _const_spec(shape):
    zeros = (0,) * len(shape)
    return pl.BlockSpec(shape, lambda *_: zeros)


def _resident_spec(shape, block=None):
    zeros = (0,) * len(shape)
    index = zeros if block is None else block
    return pl.BlockSpec(shape, lambda *_: index, pipeline_mode=pl.Buffered(1))


def _row_chunks(rows, chunk=None):
    chunk = chunk or VEC_ROWS
    return [slice(r, r + chunk) for r in range(0, rows, chunk)]


def _cmul(a_re, a_im, b_re, b_im):
    return a_re * b_re - a_im * b_im, a_re * b_im + a_im * b_re


def _discretise(a_re, a_im, log_dt):
    dt = jnp.exp(log_dt)
    mag = jnp.exp(dt * a_re)
    abar_re = mag * jnp.cos(dt * a_im)
    abar_im = mag * jnp.sin(dt * a_im)
    den = a_re * a_re + a_im * a_im
    q_re = ((abar_re - 1.0) * a_re + abar_im * a_im) / den
    q_im = (abar_im * a_re - (abar_re - 1.0) * a_im) / den
    return abar_re, abar_im, q_re, q_im


def _powers(abar_re, abar_im):
    pows = [(jnp.ones_like(abar_re), jnp.zeros_like(abar_im))]
    for _ in range(S5_CHUNK):
        pows.append(_cmul(pows[-1][0], pows[-1][1], abar_re, abar_im))
    return pows


def _block_diag_tiles(stacked, group_rows, group_cols, n_groups):
    rows = n_groups * group_rows
    cols = n_groups * group_cols
    row_shift = group_rows.bit_length() - 1
    col_shift = group_cols.bit_length() - 1
    assert group_rows == 1 << row_shift and group_cols == 1 << col_shift
    src = lax.broadcasted_iota(jnp.int32, (group_cols, cols), 0)
    dst = lax.broadcasted_iota(jnp.int32, (group_cols, cols), 1)
    replicate = jnp.where((dst & (group_cols - 1)) == src, 1.0, 0.0).astype(BF16)
    tiled = _dot(stacked.astype(BF16), replicate)
    row_group = lax.broadcasted_iota(jnp.int32, (rows, cols), 0) >> row_shift
    col_group = lax.broadcasted_iota(jnp.int32, (rows, cols), 1) >> col_shift
    mask = row_group == col_group
    return [jnp.where(mask, tiled[i * rows:(i + 1) * rows], 0.0).astype(BF16)
            for i in range(stacked.shape[0] // rows)]


def _s5_prep_body(are_ref, aim_ref, ldt_ref, arec_ref, aimc_ref, ldtc_ref, bre_ref, bim_ref, cre_ref, cim_ref,
                  cret_ref, cimt_ref, wa_ref, wx_ref,
                  apow_re_ref, apow_im_ref, wic_ref, woc_ref, wg_ref):
    g8 = S5_GROUPS // S5_BLOCKS
    abar_re, abar_im, q_re, q_im = _discretise(are_ref[...], aim_ref[...], ldt_ref[...])
    pows = _powers(abar_re, abar_im)
    apow_re_ref[...] = pows[S5_CHUNK][0]
    apow_im_ref[...] = pows[S5_CHUNK][1]
    bbar_re, bbar_im = _cmul(q_re, q_im, bre_ref[...], bim_ref[...])
    c_re = cre_ref[...]
    c_im = cim_ref[...]
    abar_re_c, abar_im_c, _, _ = _discretise(arec_ref[...], aimc_ref[...], ldtc_ref[...])
    pows_c = _powers(abar_re_c, abar_im_c)
    ct_re = cret_ref[...]
    ct_im = cimt_ref[...]

    lane = lax.broadcasted_iota(jnp.int32, (S5_GROUPS, S5_GROUP, S5_GROUP), 2)
    within = 2 * S5_BLOCK_STATES
    wic_ref[:, :, within:] = jnp.zeros((S5_BLOCKS, S5_CHUNK * S5_BLOCK_CH, S5_CHUNK * S5_BLOCK_CH), BF16)
    for j in range(S5_CHUNK):
        bt = _cmul(*pows[S5_CHUNK - 1 - j], bbar_re, bbar_im)
        et = _cmul(*pows_c[j + 1], ct_re, ct_im)
        ce_re, ce_im = _cmul(*pows[j], c_re, c_im)
        lag = jnp.zeros(lane.shape, F32)
        for ho in range(S5_GROUP):
            col = jnp.sum(bbar_re * ce_re[:, ho:ho + 1, :] - bbar_im * ce_im[:, ho:ho + 1, :],
                          axis=-1, keepdims=True)
            lag = jnp.where(lane == ho, col, lag)
        rows_j = slice(j * S5_BLOCK_CH, (j + 1) * S5_BLOCK_CH)
        for part in range(2):
            cols = slice(part * S5_BLOCK_STATES, (part + 1) * S5_BLOCK_STATES)
            in_tiles = _block_diag_tiles(bt[part].reshape(S5_GROUPS * S5_GROUP, S5_STATE), S5_GROUP, S5_STATE, g8)
            out_tiles = _block_diag_tiles(et[part].reshape(S5_GROUPS * S5_STATE, S5_GROUP), S5_STATE, S5_GROUP, g8)
            for k in range(S5_BLOCKS):
                wic_ref[k, rows_j, cols] = in_tiles[k]
                woc_ref[k, cols, rows_j] = out_tiles[k] if part == 0 else -out_tiles[k]
        lag_tiles = _block_diag_tiles(lag.reshape(S5_GROUPS * S5_GROUP, S5_GROUP), S5_GROUP, S5_GROUP, g8)
        for k in range(S5_BLOCKS):
            for t_in in range(S5_CHUNK - j):
                t_out = t_in + j
                wic_ref[k, t_in * S5_BLOCK_CH:(t_in + 1) * S5_BLOCK_CH,
                        within + t_out * S5_BLOCK_CH:within + (t_out + 1) * S5_BLOCK_CH] = lag_tiles[k]

    heads = LRU_HEADS // LRU_BLOCKS
    for part, w_ref in enumerate((wa_ref, wx_ref)):
        gate_tiles = _block_diag_tiles(w_ref[...].reshape(D_LRU, LRU_HEAD_DIM), LRU_HEAD_DIM, LRU_HEAD_DIM, heads)
        for kb in range(LRU_BLOCKS):
            wg_ref[kb, :, part * LRU_BLOCK_CH:(part + 1) * LRU_BLOCK_CH] = gate_tiles[kb]


def _s5_prep(a_re, a_im, log_dt, b_re, b_im, c_re, c_im, w_a, w_x):
    g, p, h = b_re.shape
    f = jax.ShapeDtypeStruct
    chunk_cols = S5_CHUNK * S5_BLOCK_CH
    return pl.pallas_call(
        _s5_prep_body,
        out_shape=(f((g, 1, p), F32), f((g, 1, p), F32),
                   f((S5_BLOCKS, chunk_cols, 2 * S5_BLOCK_STATES + chunk_cols), BF16),
                   f((S5_BLOCKS, 2 * S5_BLOCK_STATES, chunk_cols), BF16),
                   f((LRU_BLOCKS, LRU_BLOCK_CH, 2 * LRU_BLOCK_CH), BF16)),
        compiler_params=pltpu.CompilerParams(vmem_limit_bytes=V7X_SCOPED_VMEM_BYTES),
        name="s5_prep",
    )(a_re.reshape(g, 1, p), a_im.reshape(g, 1, p), log_dt.reshape(g, 1, 1),
      a_re.reshape(g, p, 1), a_im.reshape(g, p, 1), log_dt.reshape(g, 1, 1),
      jnp.swapaxes(b_re, 1, 2), jnp.swapaxes(b_im, 1, 2), c_re, c_im,
      jnp.swapaxes(c_re, 1, 2), jnp.swapaxes(c_im, 1, 2), w_a, w_x)


def _ln_in_body(x_ref, g_ref, b_ref, o_ref):
    for r in _row_chunks(x_ref.shape[0]):
        o_ref[r, :] = _layer_norm(x_ref[r, :], g_ref[...], b_ref[...]).astype(o_ref.dtype)


def _ln_in(x2, g, b):
    blk = pl.BlockSpec((LN_ROWS, D_MODEL), lambda i: (i, 0))
    return pl.pallas_call(
        _ln_in_body,
        grid=(x2.shape[0] // LN_ROWS,),
        in_specs=[blk, _const_spec(g.shape), _const_spec(b.shape)],
        out_specs=blk,
        out_shape=jax.ShapeDtypeStruct(x2.shape, BF16),
        compiler_params=pltpu.CompilerParams(
            dimension_semantics=("arbitrary",), vmem_limit_bytes=V7X_SCOPED_VMEM_BYTES),
        name="ln_in",
    )(x2, g, b)


def _mixer_project_in(h0b_ref, win_ref, z_ref, uc_ref, batch):
    for piece in range(D_IN // D_S5):
        cols = slice(piece * D_S5, (piece + 1) * D_S5)
        z = _dot(h0b_ref[...], win_ref[:, cols])
        z_ref[:, cols] = z
        if piece == 0:
            zb = z.astype(BF16)
            for step in range(z.shape[0] // batch):
                c, t = divmod(step, S5_CHUNK)
                for k in range(S5_BLOCKS):
                    uc_ref[k, c * batch:(c + 1) * batch, t * S5_BLOCK_CH:(t + 1) * S5_BLOCK_CH] = (
                        zb[step * batch:(step + 1) * batch, k * S5_BLOCK_CH:(k + 1) * S5_BLOCK_CH])


def _mixer_conv(z_ref, cw_ref, cb_ref, d_ref, xl_ref, xc_ref, xcb_ref, gz_ref, du_ref, batch):
    rows = z_ref.shape[0]
    halo = (CONV_WIDTH - 1) * batch
    xl_ref[halo:halo + rows, :] = z_ref[:, D_S5:D_S5 + D_LRU]
    for r in _row_chunks(rows, 2 * VEC_ROWS):
        xc = cb_ref[...]
        for j in range(CONV_WIDTH):
            xc = xc + xl_ref[j * batch + r.start:j * batch + r.stop, :] * cw_ref[j:j + 1, :]
        xc_ref[r, :] = xc
        xcb_ref[r, :] = xc.astype(BF16)
        gz_ref[r, :] = _gelu_tanh(z_ref[r, D_S5 + D_LRU:D_IN])
        du_ref[r, :] = d_ref[...] * z_ref[r, 0:D_S5]
    xl_ref[0:halo, :] = xl_ref[rows:rows + halo, :]


def _mixer_step(win_ref, wic_ref, apr_ref, api_ref, woc_ref, d_ref,
                wglu_ref, bglu_ref, gs5_ref, cw_ref, cb_ref, wg_ref, ba_ref, bx_ref, lam_ref, glru_ref, wmix_ref,
                mix_ref,
                z_ref, uc_ref, h0b_ref, inc_ref, hp_ref, st_ref, xl_ref, xc_ref, xcb_ref, pre_ref, a_ref, hx_ref,
                hl_ref, yo_ref, y32_ref, yb_ref, gate_ref, gz_ref, du_ref, y_ref):
    rows = y_ref.shape[0]
    batch = st_ref.shape[1]
    steps = rows // batch
    chunks = steps // S5_CHUNK
    wide = _row_chunks(rows, 2 * VEC_ROWS)
    sre = slice(0, S5_BLOCK_STATES)
    sim = slice(S5_BLOCK_STATES, 2 * S5_BLOCK_STATES)

    for k in range(S5_BLOCKS):
        inc_ref[k] = _dot(uc_ref[k], wic_ref[k])
    for kb in range(LRU_BLOCKS):
        lo, hi_ = kb * LRU_BLOCK_CH, (kb + 1) * LRU_BLOCK_CH
        pre = _dot(xcb_ref[:, lo:hi_], wg_ref[kb])
        pre_ref[:, lo:hi_] = pre[:, 0:LRU_BLOCK_CH]
        pre_ref[:, D_LRU + lo:D_LRU + hi_] = pre[:, LRU_BLOCK_CH:2 * LRU_BLOCK_CH]
    _mixer_project_in(h0b_ref, win_ref, z_ref, uc_ref, batch)

    for k in range(S5_BLOCKS):
        ar = jnp.broadcast_to(apr_ref[k], (batch, S5_BLOCK_STATES))
        ai = jnp.broadcast_to(api_ref[k], (batch, S5_BLOCK_STATES))
        hr = st_ref[k, :, sre]
        hi = st_ref[k, :, sim]
        for c in range(chunks):
            r = slice(c * batch, (c + 1) * batch)
            hp_ref[k, r, sre] = hr.astype(BF16)
            hp_ref[k, r, sim] = hi.astype(BF16)
            hr, hi = (ar * hr - ai * hi + inc_ref[k, r, sre],
                      ar * hi + ai * hr + inc_ref[k, r, sim])
        st_ref[k, :, sre] = hr
        st_ref[k, :, sim] = hi
        yo_ref[k] = _dot(hp_ref[k], woc_ref[k])
        if k == 1:
            lam = lam_ref[...]
            softplus_neg_lam = jnp.maximum(-lam, 0.0) + jnp.log1p(jnp.exp(-jnp.abs(lam)))
            for r in wide:
                gate_a = jax.nn.sigmoid(pre_ref[r, 0:D_LRU] + ba_ref[...])
                gate_x = jax.nn.sigmoid(pre_ref[r, D_LRU:2 * D_LRU] + bx_ref[...])
                log_a = -LRU_C * gate_a * softplus_neg_lam
                a = jnp.exp(log_a)
                m2 = -jnp.tanh(log_a) * (a * a + 1.0)
                mult = jnp.where(m2 > 0.0, m2 * lax.rsqrt(m2), 0.0)
                a_ref[r, :] = a
                hx_ref[r, :] = mult * (gate_x * xc_ref[r, :])
    hl = hl_ref[...]
    for t in range(steps):
        r = slice(t * batch, (t + 1) * batch)
        hl = a_ref[r, :] * hl + hx_ref[r, :]
        hx_ref[r, :] = hl
    hl_ref[...] = hl

    within = 2 * S5_BLOCK_STATES
    for step in range(steps):
        c, t = divmod(step, S5_CHUNK)
        r = slice(step * batch, (step + 1) * batch)
        rc = slice(c * batch, (c + 1) * batch)
        cols = slice(t * S5_BLOCK_CH, (t + 1) * S5_BLOCK_CH)
        ys = jnp.concatenate(
            [yo_ref[k, rc, cols] + inc_ref[k, rc, within + cols.start:within + cols.stop]
             for k in range(S5_BLOCKS)], axis=1)
        y = _gelu_tanh(ys + du_ref[r, :])
        y32_ref[r, :] = y
        yb_ref[r, :] = y.astype(BF16)
    gate_ref[...] = _dot(yb_ref[...], wglu_ref[...])
    for r in wide:
        y = y32_ref[r, :] * jax.nn.sigmoid(gate_ref[r, :] + bglu_ref[...])
        y_ref[r, 0:D_S5] = _rms_norm(y, gs5_ref[...]).astype(y_ref.dtype)
        ylru = hx_ref[r, :] * gz_ref[r, :]
        y_ref[r, D_S5:D_MODEL] = _rms_norm(ylru, glru_ref[...]).astype(y_ref.dtype)
    mix = _dot(y_ref[...], wmix_ref[...])
    for c in range(D_MODEL // V7X_LANES):
        piece = mix[:, c * V7X_LANES:(c + 1) * V7X_LANES].reshape(steps, batch, V7X_LANES)
        mix_ref[c] = jnp.swapaxes(piece, 0, 1).reshape(rows, V7X_LANES).astype(mix_ref.dtype)

    _mixer_conv(z_ref, cw_ref, cb_ref, d_ref, xl_ref, xc_ref, xcb_ref, gz_ref, du_ref, batch)


def _cast_rows(src_ref, dst_ref, chunk=64):
    for r in _row_chunks(src_ref.shape[0], chunk):
        dst_ref[r, :] = src_ref[r, :].astype(dst_ref.dtype)


def _mixer_body(hfirst_ref, hnext_ref, win32_ref, *rest):
    back_consts = list(rest[:16])
    d_ref, cw_ref, cb_ref = back_consts[4], back_consts[8], back_consts[9]
    mix_ref = rest[16]
    scratch = rest[17:-3]
    scratch = scratch[:2] + (hnext_ref,) + scratch[2:]
    win_ref, wglu_ref, wmix_ref = rest[-3:]
    wglu32_ref, wmix32_ref = back_consts[5], back_consts[15]
    back_consts[5], back_consts[15] = wglu_ref, wmix_ref
    z_ref, uc_ref, _, _, _, st_ref, xl_ref, xc_ref, xcb_ref = scratch[:9]
    hl_ref, gz_ref, du_ref = scratch[12], scratch[17], scratch[18]
    batch = st_ref.shape[1]
    halo = (CONV_WIDTH - 1) * batch

    @pl.when(pl.program_id(0) == 0)
    def _():
        _cast_rows(win32_ref, win_ref)
        _cast_rows(wglu32_ref, wglu_ref)
        _cast_rows(wmix32_ref, wmix_ref)
        st_ref[...] = jnp.zeros_like(st_ref)
        hl_ref[...] = jnp.zeros_like(hl_ref)
        xl_ref[0:halo, :] = jnp.zeros((halo, D_LRU), F32)
        _mixer_project_in(hfirst_ref, win_ref, z_ref, uc_ref, batch)
        _mixer_conv(z_ref, cw_ref, cb_ref, d_ref, xl_ref, xc_ref, xcb_ref, gz_ref, du_ref, batch)

    _mixer_step(win_ref, *back_consts, mix_ref, *scratch)


def _mixer(h0_tm, batch, w_in, w_in_chunk, apow_re, apow_im, w_out_chunk, d, w_glu, b_glu, g_s5,
           conv_w, conv_b, wg, b_a, b_x, lam, g_lru, w_mix):
    n_rows = h0_tm.shape[0]
    rows = MIXER_STEPS * batch
    n_blocks = n_rows // rows
    halo = (CONV_WIDTH - 1) * batch
    chunk_rows = rows // S5_CHUNK
    chunk_cols = S5_CHUNK * S5_BLOCK_CH
    slabs = D_MODEL // V7X_LANES
    consts = (w_in, w_in_chunk, apow_re, apow_im, w_out_chunk, d, w_glu, b_glu, g_s5,
              conv_w, conv_b, wg, b_a, b_x, lam, g_lru, w_mix)
    mix = pl.pallas_call(
        _mixer_body,
        grid=(n_blocks,),
        in_specs=[_resident_spec((rows, D_MODEL), (0, 0)),
                  pl.BlockSpec((rows, D_MODEL), lambda i: (jnp.minimum(i + 1, n_blocks - 1), 0))]
                 + [_resident_spec(c.shape) for c in consts],
        out_specs=pl.BlockSpec((slabs, None, rows, V7X_LANES), lambda i: (0, i, 0, 0)),
        out_shape=jax.ShapeDtypeStruct((slabs, n_blocks, rows, V7X_LANES), BF16),
        scratch_shapes=[
            pltpu.VMEM((rows, D_IN), F32),
            pltpu.VMEM((S5_BLOCKS, chunk_rows, chunk_cols), BF16),
            pltpu.VMEM((S5_BLOCKS, chunk_rows, 2 * S5_BLOCK_STATES + chunk_cols), F32),
            pltpu.VMEM((S5_BLOCKS, chunk_rows, 2 * S5_BLOCK_STATES), BF16),
            pltpu.VMEM((S5_BLOCKS, batch, 2 * S5_BLOCK_STATES), F32),
            pltpu.VMEM((rows + halo, D_LRU), F32),
            pltpu.VMEM((rows, D_LRU), F32),
            pltpu.VMEM((rows, D_LRU), BF16),
            pltpu.VMEM((rows, 2 * D_LRU), F32),
            pltpu.VMEM((rows, D_LRU), F32),
            pltpu.VMEM((rows, D_LRU), F32),
            pltpu.VMEM((batch, D_LRU), F32),
            pltpu.VMEM((S5_BLOCKS, chunk_rows, chunk_cols), F32),
            pltpu.VMEM((rows, D_S5), F32),
            pltpu.VMEM((rows, D_S5), BF16),
            pltpu.VMEM((rows, D_S5), F32),
            pltpu.VMEM((rows, D_LRU), F32),
            pltpu.VMEM((rows, D_S5), F32),
            pltpu.VMEM((rows, D_MODEL), BF16),
            pltpu.VMEM(w_in.shape, BF16),
            pltpu.VMEM(w_glu.shape, BF16),
            pltpu.VMEM(w_mix.shape, BF16),
        ],
        compiler_params=pltpu.CompilerParams(
            dimension_semantics=("arbitrary",), vmem_limit_bytes=V7X_SCOPED_VMEM_BYTES),
        name="mixer",
    )(h0_tm, h0_tm, *consts)
    return mix.reshape(slabs, n_blocks, batch, MIXER_STEPS, V7X_LANES)


def _kv_body(mem_ref, g_ref, b_ref, wk32_ref, wv32_ref, k_ref, v_ref, wk_ref, wv_ref):
    @pl.when(pl.program_id(0) == 0)
    def _():
        _cast_rows(wk32_ref, wk_ref)
        _cast_rows(wv32_ref, wv_ref)

    mn = _layer_norm(mem_ref[...], g_ref[...], b_ref[...]).astype(BF16)
    k_ref[...] = (_dot(mn, wk_ref[...]) * (CA_HEAD_DIM ** -0.5)).astype(k_ref.dtype)
    v_ref[...] = _dot(mn, wv_ref[...]).astype(v_ref.dtype)


def _kv(mem, g, b, w_k, w_v):
    bsz, mlen, _ = mem.shape
    n_rows = bsz * mlen
    blk = pl.BlockSpec((KV_ROWS, D_MODEL), lambda i: (i, 0))
    out = jax.ShapeDtypeStruct((n_rows, D_MODEL), BF16)
    k, v = pl.pallas_call(
        _kv_body,
        grid=(n_rows // KV_ROWS,),
        in_specs=[blk, _const_spec(g.shape), _const_spec(b.shape), _resident_spec(w_k.shape),
                  _resident_spec(w_v.shape)],
        out_specs=(blk, blk),
        out_shape=(out, out),
        scratch_shapes=[pltpu.VMEM(w_k.shape, BF16), pltpu.VMEM(w_v.shape, BF16)],
        compiler_params=pltpu.CompilerParams(
            dimension_semantics=("arbitrary",), vmem_limit_bytes=V7X_SCOPED_VMEM_BYTES),
        name="kv",
    )(mem.reshape(n_rows, D_MODEL), g, b, w_k, w_v)
    return k.reshape(mem.shape), v.reshape(mem.shape)


def _attn_norms(x_ref, mix_ref, lng_ref, lnb_ref, g1_ref, b1_ref, h1_ref, h1b_ref, slot):
    slabs, _, steps, _ = mix_ref.shape
    for r in _row_chunks(x_ref.shape[0]):
        blk, t = divmod(r.start, steps)
        mix = jnp.concatenate([mix_ref[c, blk, t:t + VEC_ROWS, :] for c in range(slabs)], axis=-1)
        h0 = _layer_norm(x_ref[r, :], lng_ref[...], lnb_ref[...])
        h1 = _layer_norm(ALPHA * h0 + mix, g1_ref[...], b1_ref[...])
        h1_ref[slot, r, :] = h1
        h1b_ref[r, :] = h1.astype(BF16)


def _attn_body(x0_ref, m0_ref, x1_ref, m1_ref, xnext_ref, mnext_ref, k_ref, v_ref, wq32_ref, wo32_ref,
               lng_ref, lnb_ref, g1_ref, b1_ref, o_ref,
               h1_ref, q_ref, h1b_ref, s_ref, p_ref, att_ref, ca_ref, wq_ref, wo_ref):
    i = pl.program_id(0)
    slot = i % 2
    rows = o_ref.shape[0]
    norm_refs = (lng_ref, lnb_ref, g1_ref, b1_ref, h1_ref, h1b_ref)

    @pl.when(i == 0)
    def _():
        _cast_rows(wq32_ref, wq_ref)
        _cast_rows(wo32_ref, wo_ref)
        _attn_norms(x0_ref, m0_ref, *norm_refs, 0)
        q_ref[...] = _dot(h1b_ref[...], wq_ref[...]).astype(BF16)
        _attn_norms(x1_ref, m1_ref, *norm_refs, 1)

    head_slices = [slice(hd * CA_HEAD_DIM, (hd + 1) * CA_HEAD_DIM) for hd in range(CA_HEADS)]
    for hd, hs in enumerate(head_slices):
        s_ref[hd] = lax.dot_general(q_ref[:, hs], k_ref[:, hs], (((1,), (1,)), ((), ())),
                                    preferred_element_type=F32)
    q_ref[...] = _dot(h1b_ref[...], wq_ref[...]).astype(BF16)
    for r in _row_chunks(rows):
        for hd in range(CA_HEADS):
            s = s_ref[hd, r, :]
            e = jnp.exp(s - jnp.max(s, axis=-1, keepdims=True))
            p_ref[hd, r, :] = (e * (1.0 / jnp.sum(e, axis=-1, keepdims=True))).astype(BF16)
    for hd, hs in enumerate(head_slices):
        att_ref[:, hs] = _dot(p_ref[hd], v_ref[:, hs]).astype(BF16)
    ca_ref[...] = _dot(att_ref[...], wo_ref[...])
    for r in _row_chunks(rows):
        o_ref[r, :] = ALPHA * h1_ref[slot, r, :] + ca_ref[r, :]
    _attn_norms(xnext_ref, mnext_ref, *norm_refs, slot)


def _attn(x, mix, k, v, w_q, w_o, ln_g, ln_b, g1, b1):
    bsz, seq, _ = x.shape
    mlen = k.shape[1]
    n_rows = bsz * seq
    n_blocks = n_rows // ATTN_ROWS
    blocks_per_batch = seq // ATTN_ROWS
    slabs, _, _, steps, lanes = mix.shape
    assert ATTN_ROWS % steps == 0 and steps % VEC_ROWS == 0
    blk = (ATTN_ROWS, D_MODEL)
    first_blk = _resident_spec(blk, (0, 0))
    second_blk = _resident_spec(blk, (min(1, n_blocks - 1), 0))
    next_blk = pl.BlockSpec(blk, lambda i: (jnp.minimum(i + 2, n_blocks - 1), 0))
    mix_blk = (slabs, ATTN_ROWS // steps, None, steps, lanes)

    def mix_index(block):
        return (0, block % blocks_per_batch, block // blocks_per_batch, 0, 0)

    first_mix = pl.BlockSpec(mix_blk, lambda i: mix_index(0), pipeline_mode=pl.Buffered(1))
    second_mix = pl.BlockSpec(mix_blk, lambda i: mix_index(min(1, n_blocks - 1)), pipeline_mode=pl.Buffered(1))
    next_mix = pl.BlockSpec(mix_blk, lambda i: mix_index(jnp.minimum(i + 2, n_blocks - 1)))
    mem_blk = pl.BlockSpec((None, mlen, D_MODEL), lambda i: (i // blocks_per_batch, 0, 0))
    consts = (w_q, w_o, ln_g, ln_b, g1, b1)
    x2 = x.reshape(n_rows, D_MODEL)
    return pl.pallas_call(
        _attn_body,
        grid=(n_blocks,),
        in_specs=[first_blk, first_mix, second_blk, second_mix, next_blk, next_mix, mem_blk, mem_blk]
                 + [_resident_spec(c.shape) for c in consts],
        out_specs=pl.BlockSpec(blk, lambda i: (i, 0)),
        out_shape=jax.ShapeDtypeStruct((n_rows, D_MODEL), F32),
        scratch_shapes=[pltpu.VMEM((2, ATTN_ROWS, D_MODEL), F32),
                        pltpu.VMEM((ATTN_ROWS, D_MODEL), BF16),
                        pltpu.VMEM((ATTN_ROWS, D_MODEL), BF16),
                        pltpu.VMEM((CA_HEADS, ATTN_ROWS, mlen), F32),
                        pltpu.VMEM((CA_HEADS, ATTN_ROWS, mlen), BF16),
                        pltpu.VMEM((ATTN_ROWS, D_MODEL), BF16),
                        pltpu.VMEM((ATTN_ROWS, D_MODEL), F32),
                        pltpu.VMEM(w_q.shape, BF16),
                        pltpu.VMEM(w_o.shape, BF16)],
        compiler_params=pltpu.CompilerParams(
            dimension_semantics=("arbitrary",), vmem_limit_bytes=V7X_SCOPED_VMEM_BYTES),
        name="attn",
    )(x2, mix, x2, mix, x2, mix, k, v, *consts)


def _mlp_load_weights(w1_hbm, w2_hbm, w1_ref, w2_ref, stage_ref, sem):
    chunks = []
    for j in range(D_FF // MLP_CHUNK):
        cols = pl.ds(j * MLP_CHUNK, MLP_CHUNK)
        chunks.append((w1_hbm.at[:, cols], w1_ref.at[:, cols]))
        chunks.append((w2_hbm.at[cols, :], w2_ref.at[cols, :]))

    def copy(i):
        return pltpu.make_async_copy(chunks[i][0], stage_ref.at[i % 2], sem.at[i % 2])

    copy(0).start()
    for i, (_, dst) in enumerate(chunks):
        if i + 1 < len(chunks):
            copy(i + 1).start()
        copy(i).wait()
        _cast_rows(stage_ref.at[i % 2], dst)


def _mlp_body(s_ref, w1_hbm, w2_hbm, g2_ref, b2_ref, g_ref, b_ref, o_ref, w1_ref, w2_ref, stage_ref, sem):
    @pl.when(pl.program_id(0) == 0)
    def _():
        _mlp_load_weights(w1_hbm, w2_hbm, w1_ref, w2_ref, stage_ref, sem)

    for r in _row_chunks(s_ref.shape[0], MLP_GROUP_ROWS):
        h = _layer_norm(s_ref[r, :], g2_ref[...], b2_ref[...])
        hb = h.astype(BF16)
        ff = jnp.zeros(h.shape, F32)
        for j in range(D_FF // MLP_CHUNK):
            lo, hi = j * MLP_CHUNK, (j + 1) * MLP_CHUNK
            t = jnp.maximum(_dot(hb, w1_ref[:, lo:hi]), 0.0)
            ff = ff + _dot((t * t).astype(BF16), w2_ref[lo:hi, :])
        o_ref[r, :] = _layer_norm(ALPHA * h + ff, g_ref[...], b_ref[...])


def _mlp(s, w1, w2, g2, b2, g, b):
    n_rows = s.shape[0]
    assert w1.shape == (D_MODEL, D_FF) and w2.shape == (D_FF, D_MODEL) and MLP_CHUNK == D_MODEL
    row_blk = pl.BlockSpec((MLP_ROWS, D_MODEL), lambda i: (i, 0))
    in_hbm = pl.BlockSpec(memory_space=pl.ANY)
    vecs = (g2, b2, g, b)
    return pl.pallas_call(
        _mlp_body,
        grid=(n_rows // MLP_ROWS,),
        in_specs=[row_blk, in_hbm, in_hbm] + [_const_spec(v.shape) for v in vecs],
        out_specs=row_blk,
        out_shape=jax.ShapeDtypeStruct(s.shape, F32),
        scratch_shapes=[pltpu.VMEM(w1.shape, BF16), pltpu.VMEM(w2.shape, BF16),
                        pltpu.VMEM((2, MLP_CHUNK, D_MODEL), F32),
                        pltpu.SemaphoreType.DMA((2,))],
        compiler_params=pltpu.CompilerParams(
            dimension_semantics=("arbitrary",), vmem_limit_bytes=V7X_SCOPED_VMEM_BYTES),
        name="mlp",
    )(s, w1, w2, *vecs)


def kernel(x, mem, ln_in_g, ln_in_b, w_in, s5_a_re, s5_a_im, s5_log_dt, s5_b_re, s5_b_im, s5_c_re, s5_c_im, s5_d, s5_w_glu, s5_b_glu, conv_w, conv_b, lru_w_a, lru_b_a, lru_w_x, lru_b_x, lru_lambda, g_s5, g_lru, w_mix_out, ln1_g, ln1_b, mem_ln_g, mem_ln_b, w_q, w_k, w_v, w_o, ln2_g, ln2_b, w_ff1, w_ff2, ln3_g, ln3_b):
    bsz, seq, d_model = x.shape
    assert d_model == D_MODEL and w_in.shape == (DEPTH, D_MODEL, D_IN)
    assert seq % MIXER_STEPS == 0 and seq % ATTN_ROWS == 0 and (bsz * seq) % MLP_ROWS == 0
    assert bsz % 16 == 0, "time-major rows of one step must fill whole bf16 tiles"
    row = lambda v: v.reshape(1, -1).astype(F32)
    l = 0

    apow_re, apow_im, w_in_chunk, w_out_chunk, wg = _s5_prep(
        s5_a_re[l], s5_a_im[l], s5_log_dt[l], s5_b_re[l], s5_b_im[l], s5_c_re[l], s5_c_im[l],
        lru_w_a[l], lru_w_x[l])
    apow_re = apow_re.reshape(S5_BLOCKS, 1, S5_BLOCK_STATES)
    apow_im = apow_im.reshape(S5_BLOCKS, 1, S5_BLOCK_STATES)

    h0 = _ln_in(x.reshape(bsz * seq, D_MODEL), row(ln_in_g), row(ln_in_b)).reshape(bsz, seq, D_MODEL)
    h0_tm = jnp.swapaxes(h0, 0, 1).reshape(seq * bsz, D_MODEL)
    k, v = _kv(mem, row(mem_ln_g[l]), row(mem_ln_b[l]), w_k[l], w_v[l])
    h0_tm, k, v = lax.optimization_barrier((h0_tm, k, v))
    mix = _mixer(h0_tm, bsz, w_in[l], w_in_chunk, apow_re, apow_im,
                 w_out_chunk,
                 row(s5_d[l]), s5_w_glu[l], row(s5_b_glu[l]), row(g_s5[l]),
                 conv_w[l].astype(F32), row(conv_b[l]), wg, row(lru_b_a[l]), row(lru_b_x[l]),
                 row(lru_lambda[l]), row(g_lru[l]), w_mix_out[l])
    pre2 = _attn(x, mix, k, v, w_q[l], w_o[l], row(ln_in_g), row(ln_in_b), row(ln1_g[l]), row(ln1_b[l]))
    out = _mlp(pre2, w_ff1[l], w_ff2[l], row(ln2_g[l]), row(ln2_b[l]), row(ln3_g[l]), row(ln3_b[l]))
    return out.reshape(bsz, seq, D_MODEL)
```

```python
import jax
import jax.numpy as jnp
from jax import lax
from jax.experimental import pallas as pl
from jax.experimental.pallas import tpu as pltpu

F32 = jnp.float32
BF16 = jnp.bfloat16

D_MODEL = 1024
D_S5 = 512
D_LRU = 512
D_IN = D_S5 + 2 * D_LRU
S5_GROUP = 16
S5_GROUPS = 32
S5_STATE = 64
LRU_HEADS = 8
LRU_HEAD_DIM = 64
CONV_WIDTH = 4
LRU_C = 8.0
D_FF = 4096
CA_HEADS = 4
CA_HEAD_DIM = 256
DEPTH = 1
ALPHA = (2 * DEPTH) ** 0.25
LN_EPS = 1e-5
RMS_EPS = 1e-6

S5_BLOCKS = 4
S5_BLOCK_CH = D_S5 // S5_BLOCKS
S5_BLOCK_STATES = 8 * S5_STATE
S5_CHUNK = 2
LRU_BLOCKS = 2
LRU_BLOCK_CH = D_LRU // LRU_BLOCKS

V7X_SCOPED_VMEM_BYTES = 56 * 1024 * 1024
V7X_LANES = 128

MIXER_STEPS = 32
ATTN_ROWS = 512
MLP_ROWS = 1024
MLP_GROUP_ROWS = 512
KV_ROWS = 1024
MLP_CHUNK = 1024
VEC_ROWS = 16


def _layer_norm(x, g, b):
    mu = jnp.mean(x, axis=-1, keepdims=True)
    xc = x - mu
    var = jnp.mean(xc * xc, axis=-1, keepdims=True)
    return xc * lax.rsqrt(var + LN_EPS) * g + b


def _rms_norm(x, g):
    ms = jnp.mean(x * x, axis=-1, keepdims=True)
    return x * lax.rsqrt(ms + RMS_EPS) * g


def _gelu_tanh(x):
    c = 0.7978845608028654
    return 0.5 * x * (1.0 + jnp.tanh(c * (x + 0.044715 * (x * x * x))))


def _dot(a, b):
    return jnp.dot(a, b, preferred_element_type=F32)


def _const_spec(shape):
    zeros = (0,) * len(shape)
    return pl.BlockSpec(shape, lambda *_: zeros)


def _resident_spec(shape, block=None):
    zeros = (0,) * len(shape)
    index = zeros if block is None else block
    return pl.BlockSpec(shape, lambda *_: index, pipeline_mode=pl.Buffered(1))


def _row_chunks(rows, chunk=None):
    chunk = chunk or VEC_ROWS
    return [slice(r, r + chunk) for r in range(0, rows, chunk)]


def _cmul(a_re, a_im, b_re, b_im):
    return a_re * b_re - a_im * b_im, a_re * b_im + a_im * b_re


def _discretise(a_re, a_im, log_dt):
    dt = jnp.exp(log_dt)
    mag = jnp.exp(dt * a_re)
    abar_re = mag * jnp.cos(dt * a_im)
    abar_im = mag * jnp.sin(dt * a_im)
    den = a_re * a_re + a_im * a_im
    q_re = ((abar_re - 1.0) * a_re + abar_im * a_im) / den
    q_im = (abar_im * a_re - (abar_re - 1.0) * a_im) / den
    return abar_re, abar_im, q_re, q_im


def _powers(abar_re, abar_im):
    pows = [(jnp.ones_like(abar_re), jnp.zeros_like(abar_im))]
    for _ in range(S5_CHUNK):
        pows.append(_cmul(pows[-1][0], pows[-1][1], abar_re, abar_im))
    return pows


def _block_diag_tiles(stacked, group_rows, group_cols, n_groups):
    rows = n_groups * group_rows
    cols = n_groups * group_cols
    row_shift = group_rows.bit_length() - 1
    col_shift = group_cols.bit_length() - 1
    assert group_rows == 1 << row_shift and group_cols == 1 << col_shift
    src = lax.broadcasted_iota(jnp.int32, (group_cols, cols), 0)
    dst = lax.broadcasted_iota(jnp.int32, (group_cols, cols), 1)
    replicate = jnp.where((dst & (group_cols - 1)) == src, 1.0, 0.0).astype(BF16)
    tiled = _dot(stacked.astype(BF16), replicate)
    row_group = lax.broadcasted_iota(jnp.int32, (rows, cols), 0) >> row_shift
    col_group = lax.broadcasted_iota(jnp.int32, (rows, cols), 1) >> col_shift
    mask = row_group == col_group
    return [jnp.where(mask, tiled[i * rows:(i + 1) * rows], 0.0).astype(BF16)
            for i in range(stacked.shape[0] // rows)]


def _s5_prep_body(are_ref, aim_ref, ldt_ref, arec_ref, aimc_ref, ldtc_ref, bre_ref, bim_ref, cre_ref, cim_ref,
                  cret_ref, cimt_ref, wa_ref, wx_ref,
                  apow_re_ref, apow_im_ref, wic_ref, woc_ref, wg_ref):
    g8 = S5_GROUPS // S5_BLOCKS
    abar_re, abar_im, q_re, q_im = _discretise(are_ref[...], aim_ref[...], ldt_ref[...])
    pows = _powers(abar_re, abar_im)
    apow_re_ref[...] = pows[S5_CHUNK][0]
    apow_im_ref[...] = pows[S5_CHUNK][1]
    bbar_re, bbar_im = _cmul(q_re, q_im, bre_ref[...], bim_ref[...])
    c_re = cre_ref[...]
    c_im = cim_ref[...]
    abar_re_c, abar_im_c, _, _ = _discretise(arec_ref[...], aimc_ref[...], ldtc_ref[...])
    pows_c = _powers(abar_re_c, abar_im_c)
    ct_re = cret_ref[...]
    ct_im = cimt_ref[...]

    lane = lax.broadcasted_iota(jnp.int32, (S5_GROUPS, S5_GROUP, S5_GROUP), 2)
    within = 2 * S5_BLOCK_STATES
    wic_ref[:, :, within:] = jnp.zeros((S5_BLOCKS, S5_CHUNK * S5_BLOCK_CH, S5_CHUNK * S5_BLOCK_CH), BF16)
    for j in range(S5_CHUNK):
        bt = _cmul(*pows[S5_CHUNK - 1 - j], bbar_re, bbar_im)
        et = _cmul(*pows_c[j + 1], ct_re, ct_im)
        ce_re, ce_im = _cmul(*pows[j], c_re, c_im)
        lag = jnp.zeros(lane.shape, F32)
        for ho in range(S5_GROUP):
            col = jnp.sum(bbar_re * ce_re[:, ho:ho + 1, :] - bbar_im * ce_im[:, ho:ho + 1, :],
                          axis=-1, keepdims=True)
            lag = jnp.where(lane == ho, col, lag)
        rows_j = slice(j * S5_BLOCK_CH, (j + 1) * S5_BLOCK_CH)
        for part in range(2):
            cols = slice(part * S5_BLOCK_STATES, (part + 1) * S5_BLOCK_STATES)
            in_tiles = _block_diag_tiles(bt[part].reshape(S5_GROUPS * S5_GROUP, S5_STATE), S5_GROUP, S5_STATE, g8)
            out_tiles = _block_diag_tiles(et[part].reshape(S5_GROUPS * S5_STATE, S5_GROUP), S5_STATE, S5_GROUP, g8)
            for k in range(S5_BLOCKS):
                wic_ref[k, rows_j, cols] = in_tiles[k]
                woc_ref[k, cols, rows_j] = out_tiles[k] if part == 0 else -out_tiles[k]
        lag_tiles = _block_diag_tiles(lag.reshape(S5_GROUPS * S5_GROUP, S5_GROUP), S5_GROUP, S5_GROUP, g8)
        for k in range(S5_BLOCKS):
            for t_in in range(S5_CHUNK - j):
                t_out = t_in + j
                wic_ref[k, t_in * S5_BLOCK_CH:(t_in + 1) * S5_BLOCK_CH,
                        within + t_out * S5_BLOCK_CH:within + (t_out + 1) * S5_BLOCK_CH] = lag_tiles[k]

    heads = LRU_HEADS // LRU_BLOCKS
    for part, w_ref in enumerate((wa_ref, wx_ref)):
        gate_tiles = _block_diag_tiles(w_ref[...].reshape(D_LRU, LRU_HEAD_DIM), LRU_HEAD_DIM, LRU_HEAD_DIM, heads)
        for kb in range(LRU_BLOCKS):
            wg_ref[kb, :, part * LRU_BLOCK_CH:(part + 1) * LRU_BLOCK_CH] = gate_tiles[kb]


def _s5_prep(a_re, a_im, log_dt, b_re, b_im, c_re, c_im, w_a, w_x):
    g, p, h = b_re.shape
    f = jax.ShapeDtypeStruct
    chunk_cols = S5_CHUNK * S5_BLOCK_CH
    return pl.pallas_call(
        _s5_prep_body,
        out_shape=(f((g, 1, p), F32), f((g, 1, p), F32),
                   f((S5_BLOCKS, chunk_cols, 2 * S5_BLOCK_STATES + chunk_cols), BF16),
                   f((S5_BLOCKS, 2 * S5_BLOCK_STATES, chunk_cols), BF16),
                   f((LRU_BLOCKS, LRU_BLOCK_CH, 2 * LRU_BLOCK_CH), BF16)),
        compiler_params=pltpu.CompilerParams(vmem_limit_bytes=V7X_SCOPED_VMEM_BYTES),
        name="s5_prep",
    )(a_re.reshape(g, 1, p), a_im.reshape(g, 1, p), log_dt.reshape(g, 1, 1),
      a_re.reshape(g, p, 1), a_im.reshape(g, p, 1), log_dt.reshape(g, 1, 1),
      jnp.swapaxes(b_re, 1, 2), jnp.swapaxes(b_im, 1, 2), c_re, c_im,
      jnp.swapaxes(c_re, 1, 2), jnp.swapaxes(c_im, 1, 2), w_a, w_x)


def _mixer_norm(x_ref, lng_ref, lnb_ref, hn_ref, h0b_ref):
    batch, steps, _ = x_ref.shape
    slabs = hn_ref.shape[0]
    for b in range(batch):
        for t0 in range(0, steps, VEC_ROWS):
            h = _layer_norm(x_ref[b, t0:t0 + VEC_ROWS, :], lng_ref[...], lnb_ref[...])
            for c in range(slabs):
                hn_ref[c, b * steps + t0:b * steps + t0 + VEC_ROWS, :] = h[:, c * V7X_LANES:(c + 1) * V7X_LANES]
    for t in range(steps):
        h = jnp.concatenate([hn_ref[c, pl.ds(t, batch, stride=steps), :] for c in range(slabs)], axis=-1)
        h0b_ref[t * batch:(t + 1) * batch, :] = h.astype(BF16)


def _mixer_project_in(h0b_ref, win_ref, z_ref, uc_ref, batch):
    for piece in range(D_IN // D_S5):
        cols = slice(piece * D_S5, (piece + 1) * D_S5)
        z = _dot(h0b_ref[...], win_ref[:, cols])
        z_ref[:, cols] = z
        if piece == 0:
            zb = z.astype(BF16)
            for step in range(z.shape[0] // batch):
                c, t = divmod(step, S5_CHUNK)
                for k in range(S5_BLOCKS):
                    uc_ref[k, c * batch:(c + 1) * batch, t * S5_BLOCK_CH:(t + 1) * S5_BLOCK_CH] = (
                        zb[step * batch:(step + 1) * batch, k * S5_BLOCK_CH:(k + 1) * S5_BLOCK_CH])


def _mixer_conv(z_ref, cw_ref, cb_ref, d_ref, xl_ref, xc_ref, xcb_ref, gz_ref, du_ref, batch):
    rows = z_ref.shape[0]
    halo = (CONV_WIDTH - 1) * batch
    xl_ref[halo:halo + rows, :] = z_ref[:, D_S5:D_S5 + D_LRU]
    for r in _row_chunks(rows, 2 * VEC_ROWS):
        xc = cb_ref[...]
        for j in range(CONV_WIDTH):
            xc = xc + xl_ref[j * batch + r.start:j * batch + r.stop, :] * cw_ref[j:j + 1, :]
        xc_ref[r, :] = xc
        xcb_ref[r, :] = xc.astype(BF16)
        gz_ref[r, :] = _gelu_tanh(z_ref[r, D_S5 + D_LRU:D_IN])
        du_ref[r, :] = d_ref[...] * z_ref[r, 0:D_S5]
    xl_ref[0:halo, :] = xl_ref[rows:rows + halo, :]


def _mixer_step(xnext_ref, lng_ref, lnb_ref, win_ref, wic_ref, apr_ref, api_ref, woc_ref, d_ref,
                wglu_ref, bglu_ref, gs5_ref, cw_ref, cb_ref, wg_ref, ba_ref, bx_ref, lam_ref, glru_ref, wmix_ref,
                mix_ref,
                z_ref, uc_ref, h0b_ref, inc_ref, hp_ref, st_ref, xl_ref, xc_ref, xcb_ref, pre_ref, a_ref, hx_ref,
                hl_ref, yo_ref, y32_ref, yb_ref, gate_ref, gz_ref, du_ref, y_ref, hn_ref):
    rows = y_ref.shape[0]
    batch = st_ref.shape[1]
    steps = rows // batch
    chunks = steps // S5_CHUNK
    wide = _row_chunks(rows, 2 * VEC_ROWS)
    sre = slice(0, S5_BLOCK_STATES)
    sim = slice(S5_BLOCK_STATES, 2 * S5_BLOCK_STATES)

    for k in range(S5_BLOCKS):
        inc_ref[k] = _dot(uc_ref[k], wic_ref[k])
    for kb in range(LRU_BLOCKS):
        lo, hi_ = kb * LRU_BLOCK_CH, (kb + 1) * LRU_BLOCK_CH
        pre = _dot(xcb_ref[:, lo:hi_], wg_ref[kb])
        pre_ref[:, lo:hi_] = pre[:, 0:LRU_BLOCK_CH]
        pre_ref[:, D_LRU + lo:D_LRU + hi_] = pre[:, LRU_BLOCK_CH:2 * LRU_BLOCK_CH]
    _mixer_project_in(h0b_ref, win_ref, z_ref, uc_ref, batch)

    for k in range(S5_BLOCKS):
        ar = jnp.broadcast_to(apr_ref[k], (batch, S5_BLOCK_STATES))
        ai = jnp.broadcast_to(api_ref[k], (batch, S5_BLOCK_STATES))
        hr = st_ref[k, :, sre]
        hi = st_ref[k, :, sim]
        for c in range(chunks):
            r = slice(c * batch, (c + 1) * batch)
            hp_ref[k, r, sre] = hr.astype(BF16)
            hp_ref[k, r, sim] = hi.astype(BF16)
            hr, hi = (ar * hr - ai * hi + inc_ref[k, r, sre],
                      ar * hi + ai * hr + inc_ref[k, r, sim])
        st_ref[k, :, sre] = hr
        st_ref[k, :, sim] = hi
        yo_ref[k] = _dot(hp_ref[k], woc_ref[k])
        if k == 1:
            lam = lam_ref[...]
            softplus_neg_lam = jnp.maximum(-lam, 0.0) + jnp.log1p(jnp.exp(-jnp.abs(lam)))
            for r in wide:
                gate_a = jax.nn.sigmoid(pre_ref[r, 0:D_LRU] + ba_ref[...])
                gate_x = jax.nn.sigmoid(pre_ref[r, D_LRU:2 * D_LRU] + bx_ref[...])
                log_a = -LRU_C * gate_a * softplus_neg_lam
                a = jnp.exp(log_a)
                m2 = -jnp.tanh(log_a) * (a * a + 1.0)
                mult = jnp.where(m2 > 0.0, m2 * lax.rsqrt(m2), 0.0)
                a_ref[r, :] = a
                hx_ref[r, :] = mult * (gate_x * xc_ref[r, :])
    hl = hl_ref[...]
    for t in range(steps):
        r = slice(t * batch, (t + 1) * batch)
        hl = a_ref[r, :] * hl + hx_ref[r, :]
        hx_ref[r, :] = hl
    hl_ref[...] = hl

    within = 2 * S5_BLOCK_STATES
    for step in range(steps):
        c, t = divmod(step, S5_CHUNK)
        r = slice(step * batch, (step + 1) * batch)
        rc = slice(c * batch, (c + 1) * batch)
        cols = slice(t * S5_BLOCK_CH, (t + 1) * S5_BLOCK_CH)
        ys = jnp.concatenate(
            [yo_ref[k, rc, cols] + inc_ref[k, rc, within + cols.start:within + cols.stop]
             for k in range(S5_BLOCKS)], axis=1)
        y = _gelu_tanh(ys + du_ref[r, :])
        y32_ref[r, :] = y
        yb_ref[r, :] = y.astype(BF16)
    gate_ref[...] = _dot(yb_ref[...], wglu_ref[...])
    for r in wide:
        y = y32_ref[r, :] * jax.nn.sigmoid(gate_ref[r, :] + bglu_ref[...])
        y_ref[r, 0:D_S5] = _rms_norm(y, gs5_ref[...]).astype(y_ref.dtype)
        ylru = hx_ref[r, :] * gz_ref[r, :]
        y_ref[r, D_S5:D_MODEL] = _rms_norm(ylru, glru_ref[...]).astype(y_ref.dtype)
    mix = _dot(y_ref[...], wmix_ref[...])
    for c in range(D_MODEL // V7X_LANES):
        piece = mix[:, c * V7X_LANES:(c + 1) * V7X_LANES].reshape(steps, batch, V7X_LANES)
        mix_ref[c] = jnp.swapaxes(piece, 0, 1).reshape(rows, V7X_LANES).astype(mix_ref.dtype)

    _mixer_conv(z_ref, cw_ref, cb_ref, d_ref, xl_ref, xc_ref, xcb_ref, gz_ref, du_ref, batch)
    _mixer_norm(xnext_ref, lng_ref, lnb_ref, hn_ref, h0b_ref)


def _cast_rows(src_ref, dst_ref, chunk=64):
    for r in _row_chunks(src_ref.shape[0], chunk):
        dst_ref[r, :] = src_ref[r, :].astype(dst_ref.dtype)


def _mixer_body(xfirst_ref, xsecond_ref, xnext_ref, lng_ref, lnb_ref, win32_ref, *rest):
    back_consts = list(rest[:16])
    d_ref, cw_ref, cb_ref = back_consts[4], back_consts[8], back_consts[9]
    mix_ref = rest[16]
    scratch = rest[17:-3]
    win_ref, wglu_ref, wmix_ref = rest[-3:]
    wglu32_ref, wmix32_ref = back_consts[5], back_consts[15]
    back_consts[5], back_consts[15] = wglu_ref, wmix_ref
    z_ref, uc_ref, h0b_ref, _, _, st_ref, xl_ref, xc_ref, xcb_ref = scratch[:9]
    hl_ref, gz_ref, du_ref, hn_ref = scratch[12], scratch[17], scratch[18], scratch[20]
    batch = st_ref.shape[1]
    halo = (CONV_WIDTH - 1) * batch

    @pl.when(pl.program_id(0) == 0)
    def _():
        _cast_rows(win32_ref, win_ref)
        _cast_rows(wglu32_ref, wglu_ref)
        _cast_rows(wmix32_ref, wmix_ref)
        st_ref[...] = jnp.zeros_like(st_ref)
        hl_ref[...] = jnp.zeros_like(hl_ref)
        xl_ref[0:halo, :] = jnp.zeros((halo, D_LRU), F32)
        _mixer_norm(xfirst_ref, lng_ref, lnb_ref, hn_ref, h0b_ref)
        _mixer_project_in(h0b_ref, win_ref, z_ref, uc_ref, batch)
        _mixer_conv(z_ref, cw_ref, cb_ref, d_ref, xl_ref, xc_ref, xcb_ref, gz_ref, du_ref, batch)
        _mixer_norm(xsecond_ref, lng_ref, lnb_ref, hn_ref, h0b_ref)

    _mixer_step(xnext_ref, lng_ref, lnb_ref, win_ref, *back_consts, mix_ref, *scratch)


def _mixer(x, ln_g, ln_b, w_in, w_in_chunk, apow_re, apow_im, w_out_chunk, d, w_glu, b_glu, g_s5,
           conv_w, conv_b, wg, b_a, b_x, lam, g_lru, w_mix):
    batch, seq, _ = x.shape
    rows = MIXER_STEPS * batch
    n_blocks = seq // MIXER_STEPS
    x_blk = (batch, MIXER_STEPS, D_MODEL)
    halo = (CONV_WIDTH - 1) * batch
    chunk_rows = rows // S5_CHUNK
    chunk_cols = S5_CHUNK * S5_BLOCK_CH
    slabs = D_MODEL // V7X_LANES
    consts = (ln_g, ln_b, w_in, w_in_chunk, apow_re, apow_im, w_out_chunk, d, w_glu, b_glu, g_s5,
              conv_w, conv_b, wg, b_a, b_x, lam, g_lru, w_mix)
    mix = pl.pallas_call(
        _mixer_body,
        grid=(n_blocks,),
        in_specs=[_resident_spec(x_blk, (0, 0, 0)),
                  _resident_spec(x_blk, (0, min(1, n_blocks - 1), 0)),
                  pl.BlockSpec(x_blk, lambda i: (0, jnp.minimum(i + 2, n_blocks - 1), 0))]
                 + [_resident_spec(c.shape) for c in consts],
        out_specs=pl.BlockSpec((slabs, None, rows, V7X_LANES), lambda i: (0, i, 0, 0)),
        out_shape=jax.ShapeDtypeStruct((slabs, n_blocks, rows, V7X_LANES), BF16),
        scratch_shapes=[
            pltpu.VMEM((rows, D_IN), F32),
            pltpu.VMEM((S5_BLOCKS, chunk_rows, chunk_cols), BF16),
            pltpu.VMEM((rows, D_MODEL), BF16),
            pltpu.VMEM((S5_BLOCKS, chunk_rows, 2 * S5_BLOCK_STATES + chunk_cols), F32),
            pltpu.VMEM((S5_BLOCKS, chunk_rows, 2 * S5_BLOCK_STATES), BF16),
            pltpu.VMEM((S5_BLOCKS, batch, 2 * S5_BLOCK_STATES), F32),
            pltpu.VMEM((rows + halo, D_LRU), F32),
            pltpu.VMEM((rows, D_LRU), F32),
            pltpu.VMEM((rows, D_LRU), BF16),
            pltpu.VMEM((rows, 2 * D_LRU), F32),
            pltpu.VMEM((rows, D_LRU), F32),
            pltpu.VMEM((rows, D_LRU), F32),
            pltpu.VMEM((batch, D_LRU), F32),
            pltpu.VMEM((S5_BLOCKS, chunk_rows, chunk_cols), F32),
            pltpu.VMEM((rows, D_S5), F32),
            pltpu.VMEM((rows, D_S5), BF16),
            pltpu.VMEM((rows, D_S5), F32),
            pltpu.VMEM((rows, D_LRU), F32),
            pltpu.VMEM((rows, D_S5), F32),
            pltpu.VMEM((rows, D_MODEL), BF16),
            pltpu.VMEM((slabs, rows, V7X_LANES), F32),
            pltpu.VMEM(w_in.shape, BF16),
            pltpu.VMEM(w_glu.shape, BF16),
            pltpu.VMEM(w_mix.shape, BF16),
        ],
        compiler_params=pltpu.CompilerParams(
            dimension_semantics=("arbitrary",), vmem_limit_bytes=V7X_SCOPED_VMEM_BYTES),
        name="mixer",
    )(x, x, x, *consts)
    return mix.reshape(slabs, n_blocks, batch, MIXER_STEPS, V7X_LANES)


def _kv_body(mem_ref, g_ref, b_ref, wk32_ref, wv32_ref, k_ref, v_ref, wk_ref, wv_ref):
    @pl.when(pl.program_id(0) == 0)
    def _():
        _cast_rows(wk32_ref, wk_ref)
        _cast_rows(wv32_ref, wv_ref)

    mn = _layer_norm(mem_ref[...], g_ref[...], b_ref[...]).astype(BF16)
    k_ref[...] = (_dot(mn, wk_ref[...]) * (CA_HEAD_DIM ** -0.5)).astype(k_ref.dtype)
    v_ref[...] = _dot(mn, wv_ref[...]).astype(v_ref.dtype)


def _kv(mem, g, b, w_k, w_v):
    bsz, mlen, _ = mem.shape
    n_rows = bsz * mlen
    blk = pl.BlockSpec((KV_ROWS, D_MODEL), lambda i: (i, 0))
    out = jax.ShapeDtypeStruct((n_rows, D_MODEL), BF16)
    k, v = pl.pallas_call(
        _kv_body,
        grid=(n_rows // KV_ROWS,),
        in_specs=[blk, _const_spec(g.shape), _const_spec(b.shape), _resident_spec(w_k.shape),
                  _resident_spec(w_v.shape)],
        out_specs=(blk, blk),
        out_shape=(out, out),
        scratch_shapes=[pltpu.VMEM(w_k.shape, BF16), pltpu.VMEM(w_v.shape, BF16)],
        compiler_params=pltpu.CompilerParams(
            dimension_semantics=("arbitrary",), vmem_limit_bytes=V7X_SCOPED_VMEM_BYTES),
        name="kv",
    )(mem.reshape(n_rows, D_MODEL), g, b, w_k, w_v)
    return k.reshape(mem.shape), v.reshape(mem.shape)


def _attn_norms(x_ref, mix_ref, lng_ref, lnb_ref, g1_ref, b1_ref, h1_ref, h1b_ref, slot):
    slabs, _, steps, _ = mix_ref.shape
    for r in _row_chunks(x_ref.shape[0]):
        blk, t = divmod(r.start, steps)
        mix = jnp.concatenate([mix_ref[c, blk, t:t + VEC_ROWS, :] for c in range(slabs)], axis=-1)
        h0 = _layer_norm(x_ref[r, :], lng_ref[...], lnb_ref[...])
        h1 = _layer_norm(ALPHA * h0 + mix, g1_ref[...], b1_ref[...])
        h1_ref[slot, r, :] = h1
        h1b_ref[r, :] = h1.astype(BF16)


def _attn_body(x0_ref, m0_ref, x1_ref, m1_ref, xnext_ref, mnext_ref, k_ref, v_ref, wq32_ref, wo32_ref,
               lng_ref, lnb_ref, g1_ref, b1_ref, o_ref,
               h1_ref, q_ref, h1b_ref, s_ref, p_ref, att_ref, ca_ref, wq_ref, wo_ref):
    i = pl.program_id(0)
    slot = i % 2
    rows = o_ref.shape[0]
    norm_refs = (lng_ref, lnb_ref, g1_ref, b1_ref, h1_ref, h1b_ref)

    @pl.when(i == 0)
    def _():
        _cast_rows(wq32_ref, wq_ref)
        _cast_rows(wo32_ref, wo_ref)
        _attn_norms(x0_ref, m0_ref, *norm_refs, 0)
        q_ref[...] = _dot(h1b_ref[...], wq_ref[...]).astype(BF16)
        _attn_norms(x1_ref, m1_ref, *norm_refs, 1)

    head_slices = [slice(hd * CA_HEAD_DIM, (hd + 1) * CA_HEAD_DIM) for hd in range(CA_HEADS)]
    for hd, hs in enumerate(head_slices):
        s_ref[hd] = lax.dot_general(q_ref[:, hs], k_ref[:, hs], (((1,), (1,)), ((), ())),
                                    preferred_element_type=F32)
    q_ref[...] = _dot(h1b_ref[...], wq_ref[...]).astype(BF16)
    for r in _row_chunks(rows):
        for hd in range(CA_HEADS):
            s = s_ref[hd, r, :]
            e = jnp.exp(s - jnp.max(s, axis=-1, keepdims=True))
            p_ref[hd, r, :] = (e * (1.0 / jnp.sum(e, axis=-1, keepdims=True))).astype(BF16)
    for hd, hs in enumerate(head_slices):
        att_ref[:, hs] = _dot(p_ref[hd], v_ref[:, hs]).astype(BF16)
    ca_ref[...] = _dot(att_ref[...], wo_ref[...])
    for r in _row_chunks(rows):
        o_ref[r, :] = ALPHA * h1_ref[slot, r, :] + ca_ref[r, :]
    _attn_norms(xnext_ref, mnext_ref, *norm_refs, slot)


def _attn(x, mix, k, v, w_q, w_o, ln_g, ln_b, g1, b1):
    bsz, seq, _ = x.shape
    mlen = k.shape[1]
    n_rows = bsz * seq
    n_blocks = n_rows // ATTN_ROWS
    blocks_per_batch = seq // ATTN_ROWS
    slabs, _, _, steps, lanes = mix.shape
    assert ATTN_ROWS % steps == 0 and steps % VEC_ROWS == 0
    blk = (ATTN_ROWS, D_MODEL)
    first_blk = _resident_spec(blk, (0, 0))
    second_blk = _resident_spec(blk, (min(1, n_blocks - 1), 0))
    next_blk = pl.BlockSpec(blk, lambda i: (jnp.minimum(i + 2, n_blocks - 1), 0))
    mix_blk = (slabs, ATTN_ROWS // steps, None, steps, lanes)

    def mix_index(block):
        return (0, block % blocks_per_batch, block // blocks_per_batch, 0, 0)

    first_mix = pl.BlockSpec(mix_blk, lambda i: mix_index(0), pipeline_mode=pl.Buffered(1))
    second_mix = pl.BlockSpec(mix_blk, lambda i: mix_index(min(1, n_blocks - 1)), pipeline_mode=pl.Buffered(1))
    next_mix = pl.BlockSpec(mix_blk, lambda i: mix_index(jnp.minimum(i + 2, n_blocks - 1)))
    mem_blk = pl.BlockSpec((None, mlen, D_MODEL), lambda i: (i // blocks_per_batch, 0, 0))
    consts = (w_q, w_o, ln_g, ln_b, g1, b1)
    x2 = x.reshape(n_rows, D_MODEL)
    return pl.pallas_call(
        _attn_body,
        grid=(n_blocks,),
        in_specs=[first_blk, first_mix, second_blk, second_mix, next_blk, next_mix, mem_blk, mem_blk]
                 + [_resident_spec(c.shape) for c in consts],
        out_specs=pl.BlockSpec(blk, lambda i: (i, 0)),
        out_shape=jax.ShapeDtypeStruct((n_rows, D_MODEL), F32),
        scratch_shapes=[pltpu.VMEM((2, ATTN_ROWS, D_MODEL), F32),
                        pltpu.VMEM((ATTN_ROWS, D_MODEL), BF16),
                        pltpu.VMEM((ATTN_ROWS, D_MODEL), BF16),
                        pltpu.VMEM((CA_HEADS, ATTN_ROWS, mlen), F32),
                        pltpu.VMEM((CA_HEADS, ATTN_ROWS, mlen), BF16),
                        pltpu.VMEM((ATTN_ROWS, D_MODEL), BF16),
                        pltpu.VMEM((ATTN_ROWS, D_MODEL), F32),
                        pltpu.VMEM(w_q.shape, BF16),
                        pltpu.VMEM(w_o.shape, BF16)],
        compiler_params=pltpu.CompilerParams(
            dimension_semantics=("arbitrary",), vmem_limit_bytes=V7X_SCOPED_VMEM_BYTES),
        name="attn",
    )(x2, mix, x2, mix, x2, mix, k, v, *consts)


def _mlp_load_weights(w1_hbm, w2_hbm, w1_ref, w2_ref, stage_ref, sem):
    chunks = []
    for j in range(D_FF // MLP_CHUNK):
        cols = pl.ds(j * MLP_CHUNK, MLP_CHUNK)
        chunks.append((w1_hbm.at[:, cols], w1_ref.at[:, cols]))
        chunks.append((w2_hbm.at[cols, :], w2_ref.at[cols, :]))

    def copy(i):
        return pltpu.make_async_copy(chunks[i][0], stage_ref.at[i % 2], sem.at[i % 2])

    copy(0).start()
    for i, (_, dst) in enumerate(chunks):
        if i + 1 < len(chunks):
            copy(i + 1).start()
        copy(i).wait()
        _cast_rows(stage_ref.at[i % 2], dst)


def _mlp_body(s_ref, w1_hbm, w2_hbm, g2_ref, b2_ref, g_ref, b_ref, o_ref, w1_ref, w2_ref, stage_ref, sem):
    @pl.when(pl.program_id(0) == 0)
    def _():
        _mlp_load_weights(w1_hbm, w2_hbm, w1_ref, w2_ref, stage_ref, sem)

    for r in _row_chunks(s_ref.shape[0], MLP_GROUP_ROWS):
        h = _layer_norm(s_ref[r, :], g2_ref[...], b2_ref[...])
        hb = h.astype(BF16)
        ff = jnp.zeros(h.shape, F32)
        for j in range(D_FF // MLP_CHUNK):
            lo, hi = j * MLP_CHUNK, (j + 1) * MLP_CHUNK
            t = jnp.maximum(_dot(hb, w1_ref[:, lo:hi]), 0.0)
            ff = ff + _dot((t * t).astype(BF16), w2_ref[lo:hi, :])
        o_ref[r, :] = _layer_norm(ALPHA * h + ff, g_ref[...], b_ref[...])


def _mlp(s, w1, w2, g2, b2, g, b):
    n_rows = s.shape[0]
    assert w1.shape == (D_MODEL, D_FF) and w2.shape == (D_FF, D_MODEL) and MLP_CHUNK == D_MODEL
    row_blk = pl.BlockSpec((MLP_ROWS, D_MODEL), lambda i: (i, 0))
    in_hbm = pl.BlockSpec(memory_space=pl.ANY)
    vecs = (g2, b2, g, b)
    return pl.pallas_call(
        _mlp_body,
        grid=(n_rows // MLP_ROWS,),
        in_specs=[row_blk, in_hbm, in_hbm] + [_const_spec(v.shape) for v in vecs],
        out_specs=row_blk,
        out_shape=jax.ShapeDtypeStruct(s.shape, F32),
        scratch_shapes=[pltpu.VMEM(w1.shape, BF16), pltpu.VMEM(w2.shape, BF16),
                        pltpu.VMEM((2, MLP_CHUNK, D_MODEL), F32),
                        pltpu.SemaphoreType.DMA((2,))],
        compiler_params=pltpu.CompilerParams(
            dimension_semantics=("arbitrary",), vmem_limit_bytes=V7X_SCOPED_VMEM_BYTES),
        name="mlp",
    )(s, w1, w2, *vecs)


def kernel(x, mem, ln_in_g, ln_in_b, w_in, s5_a_re, s5_a_im, s5_log_dt, s5_b_re, s5_b_im, s5_c_re, s5_c_im, s5_d, s5_w_glu, s5_b_glu, conv_w, conv_b, lru_w_a, lru_b_a, lru_w_x, lru_b_x, lru_lambda, g_s5, g_lru, w_mix_out, ln1_g, ln1_b, mem_ln_g, mem_ln_b, w_q, w_k, w_v, w_o, ln2_g, ln2_b, w_ff1, w_ff2, ln3_g, ln3_b):
    bsz, seq, d_model = x.shape
    assert d_model == D_MODEL and w_in.shape == (DEPTH, D_MODEL, D_IN)
    assert seq % MIXER_STEPS == 0 and seq % ATTN_ROWS == 0 and (bsz * seq) % MLP_ROWS == 0
    assert bsz % 16 == 0, "time-major rows of one step must fill whole bf16 tiles"
    row = lambda v: v.reshape(1, -1).astype(F32)
    l = 0

    apow_re, apow_im, w_in_chunk, w_out_chunk, wg = _s5_prep(
        s5_a_re[l], s5_a_im[l], s5_log_dt[l], s5_b_re[l], s5_b_im[l], s5_c_re[l], s5_c_im[l],
        lru_w_a[l], lru_w_x[l])
    apow_re = apow_re.reshape(S5_BLOCKS, 1, S5_BLOCK_STATES)
    apow_im = apow_im.reshape(S5_BLOCKS, 1, S5_BLOCK_STATES)

    k, v = _kv(mem, row(mem_ln_g[l]), row(mem_ln_b[l]), w_k[l], w_v[l])
    mix = _mixer(x, row(ln_in_g), row(ln_in_b), w_in[l], w_in_chunk, apow_re, apow_im,
                 w_out_chunk,
                 row(s5_d[l]), s5_w_glu[l], row(s5_b_glu[l]), row(g_s5[l]),
                 conv_w[l].astype(F32), row(conv_b[l]), wg, row(lru_b_a[l]), row(lru_b_x[l]),
                 row(lru_lambda[l]), row(g_lru[l]), w_mix_out[l])
    pre2 = _attn(x, mix, k, v, w_q[l], w_o[l], row(ln_in_g), row(ln_in_b), row(ln1_g[l]), row(ln1_b[l]))
    out = _mlp(pre2, w_ff1[l], w_ff2[l], row(ln2_g[l]), row(ln2_b[l]), row(ln3_g[l]), row(ln3_b[l]))
    return out.reshape(bsz, seq, D_MODEL)
```

```python
import jax
import jax.numpy as jnp
from jax import lax
from jax.experimental import pallas as pl
from jax.experimental.pallas import tpu as pltpu

F32 = jnp.float32
BF16 = jnp.bfloat16

D_MODEL = 1024
D_S5 = 512
D_LRU = 512
D_IN = D_S5 + 2 * D_LRU
S5_GROUP = 16
S5_GROUPS = 32
S5_STATE = 64
LRU_HEADS = 8
LRU_HEAD_DIM = 64
CONV_WIDTH = 4
LRU_C = 8.0
D_FF = 4096
CA_HEADS = 4
CA_HEAD_DIM = 256
DEPTH = 1
ALPHA = (2 * DEPTH) ** 0.25
LN_EPS = 1e-5
RMS_EPS = 1e-6

S5_BLOCKS = 4
S5_BLOCK_CH = D_S5 // S5_BLOCKS
S5_BLOCK_STATES = 8 * S5_STATE
S5_CHUNK = 2
LRU_BLOCKS = 2
LRU_BLOCK_CH = D_LRU // LRU_BLOCKS

V7X_SCOPED_VMEM_BYTES = 56 * 1024 * 1024
V7X_LANES = 128

MIXER_STEPS = 32
ATTN_ROWS = 512
MLP_ROWS = 1024
MLP_GROUP_ROWS = 512
KV_ROWS = 1024
MLP_CHUNK = 1024
VEC_ROWS = 16


def _layer_norm(x, g, b):
    mu = jnp.mean(x, axis=-1, keepdims=True)
    xc = x - mu
    var = jnp.mean(xc * xc, axis=-1, keepdims=True)
    return xc * lax.rsqrt(var + LN_EPS) * g + b


def _rms_norm(x, g):
    ms = jnp.mean(x * x, axis=-1, keepdims=True)
    return x * lax.rsqrt(ms + RMS_EPS) * g


def _gelu_tanh(x):
    c = 0.7978845608028654
    return 0.5 * x * (1.0 + jnp.tanh(c * (x + 0.044715 * (x * x * x))))


def _dot(a, b):
    return jnp.dot(a, b, preferred_element_type=F32)


def _const_spec(shape):
    zeros = (0,) * len(shape)
    return pl.BlockSpec(shape, lambda *_: zeros)


def _resident_spec(shape, block=None):
    zeros = (0,) * len(shape)
    index = zeros if block is None else block
    return pl.BlockSpec(shape, lambda *_: index, pipeline_mode=pl.Buffered(1))


def _row_chunks(rows, chunk=None):
    chunk = chunk or VEC_ROWS
    return [slice(r, r + chunk) for r in range(0, rows, chunk)]


def _cmul(a_re, a_im, b_re, b_im):
    return a_re * b_re - a_im * b_im, a_re * b_im + a_im * b_re


def _discretise(a_re, a_im, log_dt):
    dt = jnp.exp(log_dt)
    mag = jnp.exp(dt * a_re)
    abar_re = mag * jnp.cos(dt * a_im)
    abar_im = mag * jnp.sin(dt * a_im)
    den = a_re * a_re + a_im * a_im
    q_re = ((abar_re - 1.0) * a_re + abar_im * a_im) / den
    q_im = (abar_im * a_re - (abar_re - 1.0) * a_im) / den
    return abar_re, abar_im, q_re, q_im


def _powers(abar_re, abar_im):
    pows = [(jnp.ones_like(abar_re), jnp.zeros_like(abar_im))]
    for _ in range(S5_CHUNK):
        pows.append(_cmul(pows[-1][0], pows[-1][1], abar_re, abar_im))
    return pows


def _block_diag_tiles(stacked, group_rows, group_cols, n_groups):
    rows = n_groups * group_rows
    cols = n_groups * group_cols
    row_shift = group_rows.bit_length() - 1
    col_shift = group_cols.bit_length() - 1
    assert group_rows == 1 << row_shift and group_cols == 1 << col_shift
    src = lax.broadcasted_iota(jnp.int32, (group_cols, cols), 0)
    dst = lax.broadcasted_iota(jnp.int32, (group_cols, cols), 1)
    replicate = jnp.where((dst & (group_cols - 1)) == src, 1.0, 0.0).astype(BF16)
    tiled = _dot(stacked.astype(BF16), replicate)
    row_group = lax.broadcasted_iota(jnp.int32, (rows, cols), 0) >> row_shift
    col_group = lax.broadcasted_iota(jnp.int32, (rows, cols), 1) >> col_shift
    mask = row_group == col_group
    return [jnp.where(mask, tiled[i * rows:(i + 1) * rows], 0.0).astype(BF16)
            for i in range(stacked.shape[0] // rows)]


def _s5_prep_body(are_ref, aim_ref, ldt_ref, arec_ref, aimc_ref, ldtc_ref, bre_ref, bim_ref, cre_ref, cim_ref,
                  cret_ref, cimt_ref, wa_ref, wx_ref,
                  apow_re_ref, apow_im_ref, wic_ref, woc_ref, wg_ref):
    g8 = S5_GROUPS // S5_BLOCKS
    abar_re, abar_im, q_re, q_im = _discretise(are_ref[...], aim_ref[...], ldt_ref[...])
    pows = _powers(abar_re, abar_im)
    apow_re_ref[...] = pows[S5_CHUNK][0]
    apow_im_ref[...] = pows[S5_CHUNK][1]
    bbar_re, bbar_im = _cmul(q_re, q_im, bre_ref[...], bim_ref[...])
    c_re = cre_ref[...]
    c_im = cim_ref[...]
    abar_re_c, abar_im_c, _, _ = _discretise(arec_ref[...], aimc_ref[...], ldtc_ref[...])
    pows_c = _powers(abar_re_c, abar_im_c)
    ct_re = cret_ref[...]
    ct_im = cimt_ref[...]

    lane = lax.broadcasted_iota(jnp.int32, (S5_GROUPS, S5_GROUP, S5_GROUP), 2)
    within = 2 * S5_BLOCK_STATES
    wic_ref[:, :, within:] = jnp.zeros((S5_BLOCKS, S5_CHUNK * S5_BLOCK_CH, S5_CHUNK * S5_BLOCK_CH), BF16)
    for j in range(S5_CHUNK):
        bt = _cmul(*pows[S5_CHUNK - 1 - j], bbar_re, bbar_im)
        et = _cmul(*pows_c[j + 1], ct_re, ct_im)
        ce_re, ce_im = _cmul(*pows[j], c_re, c_im)
        lag = jnp.zeros(lane.shape, F32)
        for ho in range(S5_GROUP):
            col = jnp.sum(bbar_re * ce_re[:, ho:ho + 1, :] - bbar_im * ce_im[:, ho:ho + 1, :],
                          axis=-1, keepdims=True)
            lag = jnp.where(lane == ho, col, lag)
        rows_j = slice(j * S5_BLOCK_CH, (j + 1) * S5_BLOCK_CH)
        for part in range(2):
            cols = slice(part * S5_BLOCK_STATES, (part + 1) * S5_BLOCK_STATES)
            in_tiles = _block_diag_tiles(bt[part].reshape(S5_GROUPS * S5_GROUP, S5_STATE), S5_GROUP, S5_STATE, g8)
            out_tiles = _block_diag_tiles(et[part].reshape(S5_GROUPS * S5_STATE, S5_GROUP), S5_STATE, S5_GROUP, g8)
            for k in range(S5_BLOCKS):
                wic_ref[k, rows_j, cols] = in_tiles[k]
                woc_ref[k, cols, rows_j] = out_tiles[k] if part == 0 else -out_tiles[k]
        lag_tiles = _block_diag_tiles(lag.reshape(S5_GROUPS * S5_GROUP, S5_GROUP), S5_GROUP, S5_GROUP, g8)
        for k in range(S5_BLOCKS):
            for t_in in range(S5_CHUNK - j):
                t_out = t_in + j
                wic_ref[k, t_in * S5_BLOCK_CH:(t_in + 1) * S5_BLOCK_CH,
                        within + t_out * S5_BLOCK_CH:within + (t_out + 1) * S5_BLOCK_CH] = lag_tiles[k]

    heads = LRU_HEADS // LRU_BLOCKS
    for part, w_ref in enumerate((wa_ref, wx_ref)):
        gate_tiles = _block_diag_tiles(w_ref[...].reshape(D_LRU, LRU_HEAD_DIM), LRU_HEAD_DIM, LRU_HEAD_DIM, heads)
        for kb in range(LRU_BLOCKS):
            wg_ref[kb, :, part * LRU_BLOCK_CH:(part + 1) * LRU_BLOCK_CH] = gate_tiles[kb]


def _s5_prep(a_re, a_im, log_dt, b_re, b_im, c_re, c_im, w_a, w_x):
    g, p, h = b_re.shape
    f = jax.ShapeDtypeStruct
    chunk_cols = S5_CHUNK * S5_BLOCK_CH
    return pl.pallas_call(
        _s5_prep_body,
        out_shape=(f((g, 1, p), F32), f((g, 1, p), F32),
                   f((S5_BLOCKS, chunk_cols, 2 * S5_BLOCK_STATES + chunk_cols), BF16),
                   f((S5_BLOCKS, 2 * S5_BLOCK_STATES, chunk_cols), BF16),
                   f((LRU_BLOCKS, LRU_BLOCK_CH, 2 * LRU_BLOCK_CH), BF16)),
        compiler_params=pltpu.CompilerParams(vmem_limit_bytes=V7X_SCOPED_VMEM_BYTES),
        name="s5_prep",
    )(a_re.reshape(g, 1, p), a_im.reshape(g, 1, p), log_dt.reshape(g, 1, 1),
      a_re.reshape(g, p, 1), a_im.reshape(g, p, 1), log_dt.reshape(g, 1, 1),
      jnp.swapaxes(b_re, 1, 2), jnp.swapaxes(b_im, 1, 2), c_re, c_im,
      jnp.swapaxes(c_re, 1, 2), jnp.swapaxes(c_im, 1, 2), w_a, w_x)


def _mixer_norm(x_ref, lng_ref, lnb_ref, h0b_ref):
    batch, steps, d_model = x_ref.shape
    for t0 in range(0, steps, VEC_ROWS):
        h = [_layer_norm(x_ref[b, t0:t0 + VEC_ROWS, :], lng_ref[...], lnb_ref[...]) for b in range(batch)]
        for c in range(d_model // V7X_LANES):
            cols = slice(c * V7X_LANES, (c + 1) * V7X_LANES)
            piece = jnp.concatenate([hb[:, cols] for hb in h], axis=0).reshape(batch, VEC_ROWS, V7X_LANES)
            h0b_ref[t0 * batch:(t0 + VEC_ROWS) * batch, cols] = (
                jnp.swapaxes(piece, 0, 1).reshape(VEC_ROWS * batch, V7X_LANES).astype(BF16))


def _mixer_project_in(h0b_ref, win_ref, z_ref, uc_ref, batch):
    for piece in range(D_IN // D_S5):
        cols = slice(piece * D_S5, (piece + 1) * D_S5)
        z = _dot(h0b_ref[...], win_ref[:, cols])
        z_ref[:, cols] = z
        if piece == 0:
            zb = z.astype(BF16)
            for step in range(z.shape[0] // batch):
                c, t = divmod(step, S5_CHUNK)
                for k in range(S5_BLOCKS):
                    uc_ref[k, c * batch:(c + 1) * batch, t * S5_BLOCK_CH:(t + 1) * S5_BLOCK_CH] = (
                        zb[step * batch:(step + 1) * batch, k * S5_BLOCK_CH:(k + 1) * S5_BLOCK_CH])


def _mixer_conv(z_ref, cw_ref, cb_ref, d_ref, xl_ref, xc_ref, xcb_ref, gz_ref, du_ref, batch):
    rows = z_ref.shape[0]
    halo = (CONV_WIDTH - 1) * batch
    xl_ref[halo:halo + rows, :] = z_ref[:, D_S5:D_S5 + D_LRU]
    for r in _row_chunks(rows, 2 * VEC_ROWS):
        xc = cb_ref[...]
        for j in range(CONV_WIDTH):
            xc = xc + xl_ref[j * batch + r.start:j * batch + r.stop, :] * cw_ref[j:j + 1, :]
        xc_ref[r, :] = xc
        xcb_ref[r, :] = xc.astype(BF16)
        gz_ref[r, :] = _gelu_tanh(z_ref[r, D_S5 + D_LRU:D_IN])
        du_ref[r, :] = d_ref[...] * z_ref[r, 0:D_S5]
    xl_ref[0:halo, :] = xl_ref[rows:rows + halo, :]


def _mixer_step(xnext_ref, lng_ref, lnb_ref, win_ref, wic_ref, apr_ref, api_ref, woc_ref, d_ref,
                wglu_ref, bglu_ref, gs5_ref, cw_ref, cb_ref, wg_ref, ba_ref, bx_ref, lam_ref, glru_ref, wmix_ref,
                mix_ref,
                z_ref, uc_ref, h0b_ref, inc_ref, hp_ref, st_ref, xl_ref, xc_ref, xcb_ref, pre_ref, a_ref, hx_ref,
                hl_ref, yo_ref, y32_ref, yb_ref, gate_ref, gz_ref, du_ref, y_ref):
    rows = y_ref.shape[0]
    batch = st_ref.shape[1]
    steps = rows // batch
    chunks = steps // S5_CHUNK
    wide = _row_chunks(rows, 2 * VEC_ROWS)
    sre = slice(0, S5_BLOCK_STATES)
    sim = slice(S5_BLOCK_STATES, 2 * S5_BLOCK_STATES)

    for k in range(S5_BLOCKS):
        inc_ref[k] = _dot(uc_ref[k], wic_ref[k])
    for kb in range(LRU_BLOCKS):
        lo, hi_ = kb * LRU_BLOCK_CH, (kb + 1) * LRU_BLOCK_CH
        pre = _dot(xcb_ref[:, lo:hi_], wg_ref[kb])
        pre_ref[:, lo:hi_] = pre[:, 0:LRU_BLOCK_CH]
        pre_ref[:, D_LRU + lo:D_LRU + hi_] = pre[:, LRU_BLOCK_CH:2 * LRU_BLOCK_CH]
    _mixer_project_in(h0b_ref, win_ref, z_ref, uc_ref, batch)

    for k in range(S5_BLOCKS):
        ar = jnp.broadcast_to(apr_ref[k], (batch, S5_BLOCK_STATES))
        ai = jnp.broadcast_to(api_ref[k], (batch, S5_BLOCK_STATES))
        hr = st_ref[k, :, sre]
        hi = st_ref[k, :, sim]
        for c in range(chunks):
            r = slice(c * batch, (c + 1) * batch)
            hp_ref[k, r, sre] = hr.astype(BF16)
            hp_ref[k, r, sim] = hi.astype(BF16)
            hr, hi = (ar * hr - ai * hi + inc_ref[k, r, sre],
                      ar * hi + ai * hr + inc_ref[k, r, sim])
        st_ref[k, :, sre] = hr
        st_ref[k, :, sim] = hi
        yo_ref[k] = _dot(hp_ref[k], woc_ref[k])
        if k == 1:
            lam = lam_ref[...]
            softplus_neg_lam = jnp.maximum(-lam, 0.0) + jnp.log1p(jnp.exp(-jnp.abs(lam)))
            for r in wide:
                gate_a = jax.nn.sigmoid(pre_ref[r, 0:D_LRU] + ba_ref[...])
                gate_x = jax.nn.sigmoid(pre_ref[r, D_LRU:2 * D_LRU] + bx_ref[...])
                log_a = -LRU_C * gate_a * softplus_neg_lam
                a = jnp.exp(log_a)
                m2 = -jnp.tanh(log_a) * (a * a + 1.0)
                mult = jnp.where(m2 > 0.0, m2 * lax.rsqrt(m2), 0.0)
                a_ref[r, :] = a
                hx_ref[r, :] = mult * (gate_x * xc_ref[r, :])
    hl = hl_ref[...]
    for t in range(steps):
        r = slice(t * batch, (t + 1) * batch)
        hl = a_ref[r, :] * hl + hx_ref[r, :]
        hx_ref[r, :] = hl
    hl_ref[...] = hl

    within = 2 * S5_BLOCK_STATES
    for step in range(steps):
        c, t = divmod(step, S5_CHUNK)
        r = slice(step * batch, (step + 1) * batch)
        rc = slice(c * batch, (c + 1) * batch)
        cols = slice(t * S5_BLOCK_CH, (t + 1) * S5_BLOCK_CH)
        ys = jnp.concatenate(
            [yo_ref[k, rc, cols] + inc_ref[k, rc, within + cols.start:within + cols.stop]
             for k in range(S5_BLOCKS)], axis=1)
        y = _gelu_tanh(ys + du_ref[r, :])
        y32_ref[r, :] = y
        yb_ref[r, :] = y.astype(BF16)
    gate_ref[...] = _dot(yb_ref[...], wglu_ref[...])
    for r in wide:
        y = y32_ref[r, :] * jax.nn.sigmoid(gate_ref[r, :] + bglu_ref[...])
        y_ref[r, 0:D_S5] = _rms_norm(y, gs5_ref[...]).astype(y_ref.dtype)
        ylru = hx_ref[r, :] * gz_ref[r, :]
        y_ref[r, D_S5:D_MODEL] = _rms_norm(ylru, glru_ref[...]).astype(y_ref.dtype)
    mix = _dot(y_ref[...], wmix_ref[...])
    for c in range(D_MODEL // V7X_LANES):
        piece = mix[:, c * V7X_LANES:(c + 1) * V7X_LANES].reshape(steps, batch, V7X_LANES)
        mix_ref[c] = jnp.swapaxes(piece, 0, 1).reshape(rows, V7X_LANES).astype(mix_ref.dtype)

    _mixer_conv(z_ref, cw_ref, cb_ref, d_ref, xl_ref, xc_ref, xcb_ref, gz_ref, du_ref, batch)
    _mixer_norm(xnext_ref, lng_ref, lnb_ref, h0b_ref)


def _cast_rows(src_ref, dst_ref, chunk=64):
    for r in _row_chunks(src_ref.shape[0], chunk):
        dst_ref[r, :] = src_ref[r, :].astype(dst_ref.dtype)


def _mixer_body(xfirst_ref, xsecond_ref, xnext_ref, lng_ref, lnb_ref, win32_ref, *rest):
    back_consts = list(rest[:16])
    d_ref, cw_ref, cb_ref = back_consts[4], back_consts[8], back_consts[9]
    mix_ref = rest[16]
    scratch = rest[17:-3]
    win_ref, wglu_ref, wmix_ref = rest[-3:]
    wglu32_ref, wmix32_ref = back_consts[5], back_consts[15]
    back_consts[5], back_consts[15] = wglu_ref, wmix_ref
    z_ref, uc_ref, h0b_ref, _, _, st_ref, xl_ref, xc_ref, xcb_ref = scratch[:9]
    hl_ref, gz_ref, du_ref = scratch[12], scratch[17], scratch[18]
    batch = st_ref.shape[1]
    halo = (CONV_WIDTH - 1) * batch

    @pl.when(pl.program_id(0) == 0)
    def _():
        _cast_rows(win32_ref, win_ref)
        _cast_rows(wglu32_ref, wglu_ref)
        _cast_rows(wmix32_ref, wmix_ref)
        st_ref[...] = jnp.zeros_like(st_ref)
        hl_ref[...] = jnp.zeros_like(hl_ref)
        xl_ref[0:halo, :] = jnp.zeros((halo, D_LRU), F32)
        _mixer_norm(xfirst_ref, lng_ref, lnb_ref, h0b_ref)
        _mixer_project_in(h0b_ref, win_ref, z_ref, uc_ref, batch)
        _mixer_conv(z_ref, cw_ref, cb_ref, d_ref, xl_ref, xc_ref, xcb_ref, gz_ref, du_ref, batch)
        _mixer_norm(xsecond_ref, lng_ref, lnb_ref, h0b_ref)

    _mixer_step(xnext_ref, lng_ref, lnb_ref, win_ref, *back_consts, mix_ref, *scratch)


def _mixer(x, ln_g, ln_b, w_in, w_in_chunk, apow_re, apow_im, w_out_chunk, d, w_glu, b_glu, g_s5,
           conv_w, conv_b, wg, b_a, b_x, lam, g_lru, w_mix):
    batch, seq, _ = x.shape
    rows = MIXER_STEPS * batch
    n_blocks = seq // MIXER_STEPS
    x_blk = (batch, MIXER_STEPS, D_MODEL)
    halo = (CONV_WIDTH - 1) * batch
    chunk_rows = rows // S5_CHUNK
    chunk_cols = S5_CHUNK * S5_BLOCK_CH
    slabs = D_MODEL // V7X_LANES
    consts = (ln_g, ln_b, w_in, w_in_chunk, apow_re, apow_im, w_out_chunk, d, w_glu, b_glu, g_s5,
              conv_w, conv_b, wg, b_a, b_x, lam, g_lru, w_mix)
    mix = pl.pallas_call(
        _mixer_body,
        grid=(n_blocks,),
        in_specs=[_resident_spec(x_blk, (0, 0, 0)),
                  _resident_spec(x_blk, (0, min(1, n_blocks - 1), 0)),
                  pl.BlockSpec(x_blk, lambda i: (0, jnp.minimum(i + 2, n_blocks - 1), 0))]
                 + [_resident_spec(c.shape) for c in consts],
        out_specs=pl.BlockSpec((slabs, None, rows, V7X_LANES), lambda i: (0, i, 0, 0)),
        out_shape=jax.ShapeDtypeStruct((slabs, n_blocks, rows, V7X_LANES), BF16),
        scratch_shapes=[
            pltpu.VMEM((rows, D_IN), F32),
            pltpu.VMEM((S5_BLOCKS, chunk_rows, chunk_cols), BF16),
            pltpu.VMEM((rows, D_MODEL), BF16),
            pltpu.VMEM((S5_BLOCKS, chunk_rows, 2 * S5_BLOCK_STATES + chunk_cols), F32),
            pltpu.VMEM((S5_BLOCKS, chunk_rows, 2 * S5_BLOCK_STATES), BF16),
            pltpu.VMEM((S5_BLOCKS, batch, 2 * S5_BLOCK_STATES), F32),
            pltpu.VMEM((rows + halo, D_LRU), F32),
            pltpu.VMEM((rows, D_LRU), F32),
            pltpu.VMEM((rows, D_LRU), BF16),
            pltpu.VMEM((rows, 2 * D_LRU), F32),
            pltpu.VMEM((rows, D_LRU), F32),
            pltpu.VMEM((rows, D_LRU), F32),
            pltpu.VMEM((batch, D_LRU), F32),
            pltpu.VMEM((S5_BLOCKS, chunk_rows, chunk_cols), F32),
            pltpu.VMEM((rows, D_S5), F32),
            pltpu.VMEM((rows, D_S5), BF16),
            pltpu.VMEM((rows, D_S5), F32),
            pltpu.VMEM((rows, D_LRU), F32),
            pltpu.VMEM((rows, D_S5), F32),
            pltpu.VMEM((rows, D_MODEL), BF16),
            pltpu.VMEM(w_in.shape, BF16),
            pltpu.VMEM(w_glu.shape, BF16),
            pltpu.VMEM(w_mix.shape, BF16),
        ],
        compiler_params=pltpu.CompilerParams(
            dimension_semantics=("arbitrary",), vmem_limit_bytes=V7X_SCOPED_VMEM_BYTES),
        name="mixer",
    )(x, x, x, *consts)
    return mix.reshape(slabs, n_blocks, batch, MIXER_STEPS, V7X_LANES)


def _kv_body(mem_ref, g_ref, b_ref, wk32_ref, wv32_ref, k_ref, v_ref, wk_ref, wv_ref):
    @pl.when(pl.program_id(0) == 0)
    def _():
        _cast_rows(wk32_ref, wk_ref)
        _cast_rows(wv32_ref, wv_ref)

    mn = _layer_norm(mem_ref[...], g_ref[...], b_ref[...]).astype(BF16)
    k_ref[...] = (_dot(mn, wk_ref[...]) * (CA_HEAD_DIM ** -0.5)).astype(k_ref.dtype)
    v_ref[...] = _dot(mn, wv_ref[...]).astype(v_ref.dtype)


def _kv(mem, g, b, w_k, w_v):
    bsz, mlen, _ = mem.shape
    n_rows = bsz * mlen
    blk = pl.BlockSpec((KV_ROWS, D_MODEL), lambda i: (i, 0))
    out = jax.ShapeDtypeStruct((n_rows, D_MODEL), BF16)
    k, v = pl.pallas_call(
        _kv_body,
        grid=(n_rows // KV_ROWS,),
        in_specs=[blk, _const_spec(g.shape), _const_spec(b.shape), _resident_spec(w_k.shape),
                  _resident_spec(w_v.shape)],
        out_specs=(blk, blk),
        out_shape=(out, out),
        scratch_shapes=[pltpu.VMEM(w_k.shape, BF16), pltpu.VMEM(w_v.shape, BF16)],
        compiler_params=pltpu.CompilerParams(
            dimension_semantics=("arbitrary",), vmem_limit_bytes=V7X_SCOPED_VMEM_BYTES),
        name="kv",
    )(mem.reshape(n_rows, D_MODEL), g, b, w_k, w_v)
    return k.reshape(mem.shape), v.reshape(mem.shape)


def _attn_norms(x_ref, mix_ref, lng_ref, lnb_ref, g1_ref, b1_ref, h1_ref, h1b_ref, slot):
    slabs, _, steps, _ = mix_ref.shape
    for r in _row_chunks(x_ref.shape[0]):
        blk, t = divmod(r.start, steps)
        mix = jnp.concatenate([mix_ref[c, blk, t:t + VEC_ROWS, :] for c in range(slabs)], axis=-1)
        h0 = _layer_norm(x_ref[r, :], lng_ref[...], lnb_ref[...])
        h1 = _layer_norm(ALPHA * h0 + mix, g1_ref[...], b1_ref[...])
        h1_ref[slot, r, :] = h1
        h1b_ref[r, :] = h1.astype(BF16)


def _attn_body(x0_ref, m0_ref, x1_ref, m1_ref, xnext_ref, mnext_ref, k_ref, v_ref, wq32_ref, wo32_ref,
               lng_ref, lnb_ref, g1_ref, b1_ref, o_ref,
               h1_ref, q_ref, h1b_ref, s_ref, p_ref, att_ref, ca_ref, wq_ref, wo_ref):
    i = pl.program_id(0)
    slot = i % 2
    rows = o_ref.shape[0]
    norm_refs = (lng_ref, lnb_ref, g1_ref, b1_ref, h1_ref, h1b_ref)

    @pl.when(i == 0)
    def _():
        _cast_rows(wq32_ref, wq_ref)
        _cast_rows(wo32_ref, wo_ref)
        _attn_norms(x0_ref, m0_ref, *norm_refs, 0)
        q_ref[...] = _dot(h1b_ref[...], wq_ref[...]).astype(BF16)
        _attn_norms(x1_ref, m1_ref, *norm_refs, 1)

    head_slices = [slice(hd * CA_HEAD_DIM, (hd + 1) * CA_HEAD_DIM) for hd in range(CA_HEADS)]
    for hd, hs in enumerate(head_slices):
        s_ref[hd] = lax.dot_general(q_ref[:, hs], k_ref[:, hs], (((1,), (1,)), ((), ())),
                                    preferred_element_type=F32)
    q_ref[...] = _dot(h1b_ref[...], wq_ref[...]).astype(BF16)
    for r in _row_chunks(rows):
        for hd in range(CA_HEADS):
            s = s_ref[hd, r, :]
            e = jnp.exp(s - jnp.max(s, axis=-1, keepdims=True))
            p_ref[hd, r, :] = (e * (1.0 / jnp.sum(e, axis=-1, keepdims=True))).astype(BF16)
    for hd, hs in enumerate(head_slices):
        att_ref[:, hs] = _dot(p_ref[hd], v_ref[:, hs]).astype(BF16)
    ca_ref[...] = _dot(att_ref[...], wo_ref[...])
    for r in _row_chunks(rows):
        o_ref[r, :] = ALPHA * h1_ref[slot, r, :] + ca_ref[r, :]
    _attn_norms(xnext_ref, mnext_ref, *norm_refs, slot)


def _attn(x, mix, k, v, w_q, w_o, ln_g, ln_b, g1, b1):
    bsz, seq, _ = x.shape
    mlen = k.shape[1]
    n_rows = bsz * seq
    n_blocks = n_rows // ATTN_ROWS
    blocks_per_batch = seq // ATTN_ROWS
    slabs, _, _, steps, lanes = mix.shape
    assert ATTN_ROWS % steps == 0 and steps % VEC_ROWS == 0
    blk = (ATTN_ROWS, D_MODEL)
    first_blk = _resident_spec(blk, (0, 0))
    second_blk = _resident_spec(blk, (min(1, n_blocks - 1), 0))
    next_blk = pl.BlockSpec(blk, lambda i: (jnp.minimum(i + 2, n_blocks - 1), 0))
    mix_blk = (slabs, ATTN_ROWS // steps, None, steps, lanes)

    def mix_index(block):
        return (0, block % blocks_per_batch, block // blocks_per_batch, 0, 0)

    first_mix = pl.BlockSpec(mix_blk, lambda i: mix_index(0), pipeline_mode=pl.Buffered(1))
    second_mix = pl.BlockSpec(mix_blk, lambda i: mix_index(min(1, n_blocks - 1)), pipeline_mode=pl.Buffered(1))
    next_mix = pl.BlockSpec(mix_blk, lambda i: mix_index(jnp.minimum(i + 2, n_blocks - 1)))
    mem_blk = pl.BlockSpec((None, mlen, D_MODEL), lambda i: (i // blocks_per_batch, 0, 0))
    consts = (w_q, w_o, ln_g, ln_b, g1, b1)
    x2 = x.reshape(n_rows, D_MODEL)
    return pl.pallas_call(
        _attn_body,
        grid=(n_blocks,),
        in_specs=[first_blk, first_mix, second_blk, second_mix, next_blk, next_mix, mem_blk, mem_blk]
                 + [_resident_spec(c.shape) for c in consts],
        out_specs=pl.BlockSpec(blk, lambda i: (i, 0)),
        out_shape=jax.ShapeDtypeStruct((n_rows, D_MODEL), F32),
        scratch_shapes=[pltpu.VMEM((2, ATTN_ROWS, D_MODEL), F32),
                        pltpu.VMEM((ATTN_ROWS, D_MODEL), BF16),
                        pltpu.VMEM((ATTN_ROWS, D_MODEL), BF16),
                        pltpu.VMEM((CA_HEADS, ATTN_ROWS, mlen), F32),
                        pltpu.VMEM((CA_HEADS, ATTN_ROWS, mlen), BF16),
                        pltpu.VMEM((ATTN_ROWS, D_MODEL), BF16),
                        pltpu.VMEM((ATTN_ROWS, D_MODEL), F32),
                        pltpu.VMEM(w_q.shape, BF16),
                        pltpu.VMEM(w_o.shape, BF16)],
        compiler_params=pltpu.CompilerParams(
            dimension_semantics=("arbitrary",), vmem_limit_bytes=V7X_SCOPED_VMEM_BYTES),
        name="attn",
    )(x2, mix, x2, mix, x2, mix, k, v, *consts)


def _mlp_load_weights(w1_hbm, w2_hbm, w1_ref, w2_ref, stage_ref, sem):
    chunks = []
    for j in range(D_FF // MLP_CHUNK):
        cols = pl.ds(j * MLP_CHUNK, MLP_CHUNK)
        chunks.append((w1_hbm.at[:, cols], w1_ref.at[:, cols]))
        chunks.append((w2_hbm.at[cols, :], w2_ref.at[cols, :]))

    def copy(i):
        return pltpu.make_async_copy(chunks[i][0], stage_ref.at[i % 2], sem.at[i % 2])

    copy(0).start()
    for i, (_, dst) in enumerate(chunks):
        if i + 1 < len(chunks):
            copy(i + 1).start()
        copy(i).wait()
        _cast_rows(stage_ref.at[i % 2], dst)


def _mlp_body(s_ref, w1_hbm, w2_hbm, g2_ref, b2_ref, g_ref, b_ref, o_ref, w1_ref, w2_ref, stage_ref, sem):
    @pl.when(pl.program_id(0) == 0)
    def _():
        _mlp_load_weights(w1_hbm, w2_hbm, w1_ref, w2_ref, stage_ref, sem)

    for r in _row_chunks(s_ref.shape[0], MLP_GROUP_ROWS):
        h = _layer_norm(s_ref[r, :], g2_ref[...], b2_ref[...])
        hb = h.astype(BF16)
        ff = jnp.zeros(h.shape, F32)
        for j in range(D_FF // MLP_CHUNK):
            lo, hi = j * MLP_CHUNK, (j + 1) * MLP_CHUNK
            t = jnp.maximum(_dot(hb, w1_ref[:, lo:hi]), 0.0)
            ff = ff + _dot((t * t).astype(BF16), w2_ref[lo:hi, :])
        o_ref[r, :] = _layer_norm(ALPHA * h + ff, g_ref[...], b_ref[...])


def _mlp(s, w1, w2, g2, b2, g, b):
    n_rows = s.shape[0]
    assert w1.shape == (D_MODEL, D_FF) and w2.shape == (D_FF, D_MODEL) and MLP_CHUNK == D_MODEL
    row_blk = pl.BlockSpec((MLP_ROWS, D_MODEL), lambda i: (i, 0))
    in_hbm = pl.BlockSpec(memory_space=pl.ANY)
    vecs = (g2, b2, g, b)
    return pl.pallas_call(
        _mlp_body,
        grid=(n_rows // MLP_ROWS,),
        in_specs=[row_blk, in_hbm, in_hbm] + [_const_spec(v.shape) for v in vecs],
        out_specs=row_blk,
        out_shape=jax.ShapeDtypeStruct(s.shape, F32),
        scratch_shapes=[pltpu.VMEM(w1.shape, BF16), pltpu.VMEM(w2.shape, BF16),
                        pltpu.VMEM((2, MLP_CHUNK, D_MODEL), F32),
                        pltpu.SemaphoreType.DMA((2,))],
        compiler_params=pltpu.CompilerParams(
            dimension_semantics=("arbitrary",), vmem_limit_bytes=V7X_SCOPED_VMEM_BYTES),
        name="mlp",
    )(s, w1, w2, *vecs)


def kernel(x, mem, ln_in_g, ln_in_b, w_in, s5_a_re, s5_a_im, s5_log_dt, s5_b_re, s5_b_im, s5_c_re, s5_c_im, s5_d, s5_w_glu, s5_b_glu, conv_w, conv_b, lru_w_a, lru_b_a, lru_w_x, lru_b_x, lru_lambda, g_s5, g_lru, w_mix_out, ln1_g, ln1_b, mem_ln_g, mem_ln_b, w_q, w_k, w_v, w_o, ln2_g, ln2_b, w_ff1, w_ff2, ln3_g, ln3_b):
    bsz, seq, d_model = x.shape
    assert d_model == D_MODEL and w_in.shape == (DEPTH, D_MODEL, D_IN)
    assert seq % MIXER_STEPS == 0 and seq % ATTN_ROWS == 0 and (bsz * seq) % MLP_ROWS == 0
    assert bsz % 16 == 0, "time-major rows of one step must fill whole bf16 tiles"
    row = lambda v: v.reshape(1, -1).astype(F32)
    l = 0

    apow_re, apow_im, w_in_chunk, w_out_chunk, wg = _s5_prep(
        s5_a_re[l], s5_a_im[l], s5_log_dt[l], s5_b_re[l], s5_b_im[l], s5_c_re[l], s5_c_im[l],
        lru_w_a[l], lru_w_x[l])
    apow_re = apow_re.reshape(S5_BLOCKS, 1, S5_BLOCK_STATES)
    apow_im = apow_im.reshape(S5_BLOCKS, 1, S5_BLOCK_STATES)

    k, v = _kv(mem, row(mem_ln_g[l]), row(mem_ln_b[l]), w_k[l], w_v[l])
    mix = _mixer(x, row(ln_in_g), row(ln_in_b), w_in[l], w_in_chunk, apow_re, apow_im,
                 w_out_chunk,
                 row(s5_d[l]), s5_w_glu[l], row(s5_b_glu[l]), row(g_s5[l]),
                 conv_w[l].astype(F32), row(conv_b[l]), wg, row(lru_b_a[l]), row(lru_b_x[l]),
                 row(lru_lambda[l]), row(g_lru[l]), w_mix_out[l])
    pre2 = _attn(x, mix, k, v, w_q[l], w_o[l], row(ln_in_g), row(ln_in_b), row(ln1_g[l]), row(ln1_b[l]))
    out = _mlp(pre2, w_ff1[l], w_ff2[l], row(ln2_g[l]), row(ln2_b[l]), row(ln3_g[l]), row(ln3_b[l]))
    return out.reshape(bsz, seq, D_MODEL)
```
